```python
import math
import jax
import jax.numpy as jnp
from jax import lax
import numpy as np


D_MODEL = 1024
BATCH = 8
SEQ = 2048
DEPTH = 2

CTX_LEN = 256
GRID_W = 64
N_EVEN = (DEPTH + 1) // 2
N_ODD = DEPTH // 2
N_ADA = 9
D_FF = 2816

HY_WIDTH = 512
RG_WIDTH = 512
EV_IN = 3 * HY_WIDTH + 2 * RG_WIDTH
EV_MIX = HY_WIDTH + RG_WIDTH
HY_SHORT = 3
HY_EMB = 33
HY_BANDS = (HY_EMB - 1) // 2
HY_FILT_HIDDEN = 64
HY_TARGET = 1e-2
HY_DECAY_PCT_SHORT = 0.3
HY_DECAY_PCT_LONG = 1.5
HY_MAX_DECAY = math.log(HY_TARGET) / HY_DECAY_PCT_SHORT
HY_MIN_DECAY = math.log(HY_TARGET) / HY_DECAY_PCT_LONG
HY_SHIFT = 0.05
RG_CONV = 4
RG_PAD_L = 2
RG_PAD_R = 1
RG_BLOCKS = 8
RG_BLOCK_DIM = RG_WIDTH // RG_BLOCKS
RG_C = 8.0

MLA_HEADS = 16
MLA_Q_LORA = 768
MLA_KV_LORA = 256
MLA_NOPE = 64
MLA_ROPE = 32
MLA_V = 64
MLA_IN = MLA_Q_LORA + MLA_KV_LORA + MLA_ROPE
MLA_SCALE = (MLA_NOPE + MLA_ROPE) ** -0.5
ROPE_AXIS_PAIRS = MLA_ROPE // 4
ROPE_BASE = 10000.0
Q_BLOCK = 128

ALPHA = (2.0 * DEPTH) ** 0.25
BETA = (8.0 * DEPTH) ** -0.25
LN_EPS = 1e-6
RMS_EPS = 1e-6
F32 = jnp.float32

kernel_name = 'hybrid_hyena_rglru_mla_prefix_dit'


def layernorm(x, g, b):
    xf = x.astype(F32)
    mu = jnp.mean(xf, axis=-1, keepdims=True)
    var = jnp.mean(jnp.square(xf - mu), axis=-1, keepdims=True)
    return ((xf - mu) * lax.rsqrt(var + LN_EPS) * g + b).astype(x.dtype)


def rmsnorm(x, g):
    xf = x.astype(F32)
    y = xf * lax.rsqrt(jnp.mean(jnp.square(xf), axis=-1, keepdims=True) + RMS_EPS)
    return (y * g).astype(x.dtype)


def modulate(x, shift, scale):
    return x * (1 + scale) + shift


def adaln(cond, w, b):
    m = jax.nn.silu(cond) @ w + b
    return m.reshape(cond.shape[0], N_ADA, D_MODEL).transpose(1, 0, 2)[:, :, None, :]


def half_ffn(x, shift, scale, gate, w_in, w_out, g, b):
    h = modulate(x, shift, scale)
    a, u = jnp.split(h @ w_in, 2, axis=-1)
    y = (jax.nn.silu(a) * u) @ w_out
    return layernorm(ALPHA * x + 0.5 * gate * y, g, b)


def dwconv(x, w, b, pad_l, pad_r):
    L = x.shape[1]
    xp = jnp.pad(x, ((0, 0), (pad_l, pad_r), (0, 0)))
    y = xp[:, 0:L] * w[0]
    for k in range(1, w.shape[0]):
        y = y + xp[:, k:k + L] * w[k]
    return y + b


def hyena_filters(L, w1, b1, w2, b2, w3, sin_freq):
    pos = jnp.arange(L, dtype=F32)
    t = pos / L
    bands = jnp.linspace(1e-4, HY_BANDS - 1, HY_BANDS, dtype=F32)
    ang = (2.0 * math.pi * pos / L)[:, None] * bands[None, :]
    feats = jnp.concatenate([t[:, None], jnp.cos(ang), -jnp.sin(ang)], axis=-1)
    h = jnp.sin(sin_freq[0] * (feats @ w1 + b1))
    h = jnp.sin(sin_freq[1] * (h @ w2 + b2))
    h = (h @ w3).astype(F32).reshape(L, 2, HY_WIDTH)
    deltas = jnp.abs(jnp.linspace(HY_MIN_DECAY, HY_MAX_DECAY, HY_WIDTH, dtype=F32))
    window = jnp.exp(-t[:, None] * deltas[None, :]) + HY_SHIFT
    h = h * window[:, None, :]
    return h[:, 0], h[:, 1]


def bidir_fftconv(u, h_fwd, h_bwd, bias):
    L, C = h_fwd.shape
    k = jnp.concatenate([h_fwd, jnp.zeros((1, C), F32), h_bwd[1:][::-1]], axis=0)
    U = jnp.fft.rfft(u.astype(F32), n=2 * L, axis=1)
    K = jnp.fft.rfft(k, n=2 * L, axis=0)
    y = jnp.fft.irfft(U * K[None], n=2 * L, axis=1)[:, :L]
    return (y + u.astype(F32) * bias).astype(u.dtype)


def hyena(u, conv_w, conv_b, f_w1, f_b1, f_w2, f_b2, f_w3, sin_freq, bias):
    u = dwconv(u, conv_w, conv_b, 1, 1)
    x0, x1, v = jnp.split(u, 3, axis=-1)
    h_fwd, h_bwd = hyena_filters(u.shape[1], f_w1, f_b1, f_w2, f_b2, f_w3, sin_freq)
    return x0 * bidir_fftconv(v * x1, h_fwd, h_bwd, bias)


def rglru_coeffs(xc, a_w, a_b, x_w, x_b, lam):
    B, L, _ = xc.shape
    xb = xc.reshape(B, L, RG_BLOCKS, RG_BLOCK_DIM)
    r = jax.nn.sigmoid(jnp.einsum('blnd,nde->blne', xb, a_w).reshape(B, L, RG_WIDTH) + a_b)
    i = jax.nn.sigmoid(jnp.einsum('blnd,nde->blne', xb, x_w).reshape(B, L, RG_WIDTH) + x_b)
    log_a = -RG_C * r.astype(F32) * jax.nn.softplus(-lam.astype(F32))
    a = jnp.exp(log_a)
    b = jnp.sqrt(-jnp.expm1(2.0 * log_a)) * (i * xc).astype(F32)
    return a, b


def linear_scan(a, b, h0, reverse):
    if h0 is not None:
        first = -1 if reverse else 0
        b = b.at[:, first].add(a[:, first] * h0)

    def combine(e1, e2):
        a1, b1 = e1
        a2, b2 = e2
        return a1 * a2, a2 * b1 + b2

    _, h = lax.associative_scan(combine, (a, b), reverse=reverse, axis=1)
    return h


def even_mixer(h_lat, h_ctx, w_in, w_out, conv_w, conv_b, f_w1, f_b1, f_w2, f_b2, f_w3,
               sin_freq, hy_b, rc_w, rc_b, a_w, a_b, x_w, x_b, lam, need_ctx):
    def split(u):
        return (u[..., :3 * HY_WIDTH], u[..., 3 * HY_WIDTH:3 * HY_WIDTH + RG_WIDTH],
                u[..., 3 * HY_WIDTH + RG_WIDTH:])

    hy_l, rx_l, gate_l = split(h_lat @ w_in)
    hy_c, rx_c, gate_c = split(h_ctx @ w_in)
    xc_l = dwconv(rx_l, rc_w, rc_b, RG_PAD_L, RG_PAD_R)
    xc_c = dwconv(rx_c, rc_w, rc_b, RG_PAD_L, RG_PAD_R)
    rnn_l, rnn_c = [], []
    for d, reverse in enumerate((False, True)):
        a_c, b_c = rglru_coeffs(xc_c, a_w[d], a_b[d], x_w[d], x_b[d], lam[d])
        hc = linear_scan(a_c, b_c, None, reverse)
        h_final = hc[:, 0] if reverse else hc[:, -1]
        a_l, b_l = rglru_coeffs(xc_l, a_w[d], a_b[d], x_w[d], x_b[d], lam[d])
        rnn_l.append(linear_scan(a_l, b_l, h_final, reverse))
        rnn_c.append(hc)

    def merge(hy_u, rnn, gate):
        y_hy = hyena(hy_u, conv_w, conv_b, f_w1, f_b1, f_w2, f_b2, f_w3, sin_freq, hy_b)
        y_rg = (rnn[0] + rnn[1]).astype(gate.dtype) * jax.nn.gelu(gate)
        return jnp.concatenate([y_hy, y_rg], axis=-1) @ w_out

    out_lat = merge(hy_l, rnn_l, gate_l)
    out_ctx = merge(hy_c, rnn_c, gate_c) if need_ctx else None
    return out_lat, out_ctx


def mla_queries(q_c, q_norm, w_q_up):
    B, L, _ = q_c.shape
    q = (rmsnorm(q_c, q_norm) @ w_q_up).reshape(B, L, MLA_HEADS, MLA_NOPE + MLA_ROPE)
    return q[..., :MLA_NOPE], q[..., MLA_NOPE:]


def mla_kv(kv_c, kv_norm, w_kv_up):
    B, L, _ = kv_c.shape
    kv = (rmsnorm(kv_c, kv_norm) @ w_kv_up).reshape(B, L, MLA_HEADS, MLA_NOPE + MLA_V)
    return kv[..., :MLA_NOPE], kv[..., MLA_NOPE:]


def apply_rope(x, cos, sin):
    xs = x.reshape(x.shape[:-1] + (2, 2, ROPE_AXIS_PAIRS))
    x1, x2 = xs[..., 0, :], xs[..., 1, :]
    out = jnp.stack([x1 * cos - x2 * sin, x2 * cos + x1 * sin], axis=-2)
    return out.reshape(x.shape).astype(x.dtype)


def blocked_attention(q_nope, q_rope, k_nope, k_rope, v):
    B, Lq, H, _ = q_nope.shape
    nblk = Lq // Q_BLOCK

    def to_blocks(t):
        return t.reshape(B, nblk, Q_BLOCK, H, t.shape[-1]).swapaxes(0, 1)

    def one_block(qb):
        qn, qr = qb
        s = jnp.einsum('bqhd,bkhd->bhqk', qn, k_nope) + jnp.einsum('bqhr,bkr->bhqk', qr, k_rope)
        p = jax.nn.softmax(s.astype(F32) * MLA_SCALE, axis=-1)
        return jnp.einsum('bhqk,bkhd->bqhd', p.astype(v.dtype), v)

    o = lax.map(one_block, (to_blocks(q_nope), to_blocks(q_rope)))
    return o.swapaxes(0, 1).reshape(B, Lq, H, v.shape[-1])


def odd_mixer(h_lat, h_ctx, w_in, q_norm, kv_norm, w_q_up, w_kv_up, w_out, cos, sin, need_ctx):
    B, L, _ = h_lat.shape
    p_l = h_lat @ w_in
    p_c = h_ctx @ w_in
    q_cl, kv_cl, kr_l = p_l[..., :MLA_Q_LORA], p_l[..., MLA_Q_LORA:MLA_Q_LORA + MLA_KV_LORA], p_l[..., MLA_Q_LORA + MLA_KV_LORA:]
    q_cc, kv_cc, kr_c = p_c[..., :MLA_Q_LORA], p_c[..., MLA_Q_LORA:MLA_Q_LORA + MLA_KV_LORA], p_c[..., MLA_Q_LORA + MLA_KV_LORA:]
    qn_l, qr_l = mla_queries(q_cl, q_norm, w_q_up)
    kn_l, v_l = mla_kv(kv_cl, kv_norm, w_kv_up)
    kn_c, v_c = mla_kv(kv_cc, kv_norm, w_kv_up)
    qr_l = apply_rope(qr_l, cos[:, None], sin[:, None])
    kr_l = apply_rope(kr_l, cos, sin)
    o_l = blocked_attention(qn_l, qr_l,
                            jnp.concatenate([kn_l, kn_c], axis=1),
                            jnp.concatenate([kr_l, kr_c], axis=1),
                            jnp.concatenate([v_l, v_c], axis=1))
    out_lat = o_l.reshape(B, L, MLA_HEADS * MLA_V) @ w_out
    out_ctx = None
    if need_ctx:
        qn_c, qr_c = mla_queries(q_cc, q_norm, w_q_up)
        o_c = blocked_attention(qn_c, qr_c, kn_c, kr_c, v_c)
        out_ctx = o_c.reshape(B, h_ctx.shape[1], MLA_HEADS * MLA_V) @ w_out
    return out_lat, out_ctx


def setup_inputs(seed: int = 0) -> dict:
    key = jax.random.key(seed)
    keys = iter(jax.random.split(key, 48))

    def rnd(shape, std):
        return std * jax.random.normal(next(keys), shape, F32)

    a_base = jax.random.uniform(next(keys), (N_EVEN, 2, RG_WIDTH), F32, 0.9, 0.999) ** (1.0 / RG_C)
    rg_lambda = jnp.log(a_base) - jnp.log1p(-a_base)
    return {
        'x': rnd((BATCH, SEQ, D_MODEL), 1.0),
        'c': rnd((BATCH, D_MODEL), 1.0),
        'ctx': rnd((BATCH, CTX_LEN, D_MODEL), 1.0),
        'c_ctx': rnd((D_MODEL,), 1.0),
        'ada_w': rnd((DEPTH, D_MODEL, N_ADA * D_MODEL), 0.5 * D_MODEL ** -0.5),
        'ada_b': rnd((DEPTH, N_ADA * D_MODEL), 0.01),
        'ln_g': 1.0 + rnd((DEPTH, 3, D_MODEL), 0.02),
        'ln_b': rnd((DEPTH, 3, D_MODEL), 0.02),
        'ffn_w_in': rnd((DEPTH, 2, D_MODEL, 2 * D_FF), D_MODEL ** -0.5),
        'ffn_w_out': rnd((DEPTH, 2, D_FF, D_MODEL), BETA * D_FF ** -0.5),
        'ev_w_in': rnd((N_EVEN, D_MODEL, EV_IN), D_MODEL ** -0.5),
        'ev_w_out': rnd((N_EVEN, EV_MIX, D_MODEL), BETA * EV_MIX ** -0.5),
        'hy_conv_w': rnd((N_EVEN, HY_SHORT, 3 * HY_WIDTH), HY_SHORT ** -0.5),
        'hy_conv_b': rnd((N_EVEN, 3 * HY_WIDTH), 0.02),
        'hy_filt_w1': rnd((N_EVEN, HY_EMB, HY_FILT_HIDDEN), HY_EMB ** -0.5),
        'hy_filt_b1': rnd((N_EVEN, HY_FILT_HIDDEN), 0.1),
        'hy_filt_w2': rnd((N_EVEN, HY_FILT_HIDDEN, HY_FILT_HIDDEN), HY_FILT_HIDDEN ** -0.5),
        'hy_filt_b2': rnd((N_EVEN, HY_FILT_HIDDEN), 0.1),
        'hy_filt_w3': rnd((N_EVEN, HY_FILT_HIDDEN, 2 * HY_WIDTH), 0.1 * HY_FILT_HIDDEN ** -0.5),
        'hy_sin_freq': 1.0 + rnd((N_EVEN, 2, HY_FILT_HIDDEN), 0.02),
        'hy_bias': rnd((N_EVEN, HY_WIDTH), 0.5),
        'rg_conv_w': rnd((N_EVEN, RG_CONV, RG_WIDTH), RG_CONV ** -0.5),
        'rg_conv_b': rnd((N_EVEN, RG_WIDTH), 0.02),
        'rg_a_w': rnd((N_EVEN, 2, RG_BLOCKS, RG_BLOCK_DIM, RG_BLOCK_DIM), RG_BLOCK_DIM ** -0.5),
        'rg_a_b': rnd((N_EVEN, 2, RG_WIDTH), 0.02),
        'rg_x_w': rnd((N_EVEN, 2, RG_BLOCKS, RG_BLOCK_DIM, RG_BLOCK_DIM), RG_BLOCK_DIM ** -0.5),
        'rg_x_b': rnd((N_EVEN, 2, RG_WIDTH), 0.02),
        'rg_lambda': rg_lambda,
        'mla_w_in': rnd((N_ODD, D_MODEL, MLA_IN), D_MODEL ** -0.5),
        'mla_q_norm': 1.0 + rnd((N_ODD, MLA_Q_LORA), 0.02),
        'mla_kv_norm': 1.0 + rnd((N_ODD, MLA_KV_LORA), 0.02),
        'mla_w_q_up': rnd((N_ODD, MLA_Q_LORA, MLA_HEADS * (MLA_NOPE + MLA_ROPE)), MLA_Q_LORA ** -0.5),
        'mla_w_kv_up': rnd((N_ODD, MLA_KV_LORA, MLA_HEADS * (MLA_NOPE + MLA_V)), MLA_KV_LORA ** -0.5),
        'mla_w_out': rnd((N_ODD, MLA_HEADS * MLA_V, D_MODEL), BETA * (MLA_HEADS * MLA_V) ** -0.5),
    }


def reference(x, c, ctx, c_ctx, ada_w, ada_b, ln_g, ln_b, ffn_w_in, ffn_w_out, ev_w_in, ev_w_out,
              hy_conv_w, hy_conv_b, hy_filt_w1, hy_filt_b1, hy_filt_w2, hy_filt_b2, hy_filt_w3,
              hy_sin_freq, hy_bias, rg_conv_w, rg_conv_b, rg_a_w, rg_a_b, rg_x_w, rg_x_b, rg_lambda,
              mla_w_in, mla_q_norm, mla_kv_norm, mla_w_q_up, mla_w_kv_up, mla_w_out):
    L = x.shape[1]
    ROWS = L // GRID_W
    rows = jnp.repeat(jnp.arange(ROWS, dtype=F32), GRID_W)
    cols = jnp.tile(jnp.arange(GRID_W, dtype=F32), ROWS)
    inv_freq = ROPE_BASE ** (-jnp.arange(ROPE_AXIS_PAIRS, dtype=F32) / ROPE_AXIS_PAIRS)
    ang = jnp.stack([rows[:, None] * inv_freq, cols[:, None] * inv_freq], axis=1)
    cos, sin = jnp.cos(ang), jnp.sin(ang)

    x_lat, x_ctx = x, ctx
    for l in range(DEPTH):
        last = l == DEPTH - 1
        m_lat = adaln(c, ada_w[l], ada_b[l])
        m_ctx = adaln(c_ctx[None], ada_w[l], ada_b[l])
        x_lat = half_ffn(x_lat, m_lat[0], m_lat[1], m_lat[2], ffn_w_in[l, 0], ffn_w_out[l, 0], ln_g[l, 0], ln_b[l, 0])
        x_ctx = half_ffn(x_ctx, m_ctx[0], m_ctx[1], m_ctx[2], ffn_w_in[l, 0], ffn_w_out[l, 0], ln_g[l, 0], ln_b[l, 0])
        h_lat = modulate(x_lat, m_lat[3], m_lat[4])
        h_ctx = modulate(x_ctx, m_ctx[3], m_ctx[4])
        if l % 2 == 0:
            e = l // 2
            y_lat, y_ctx = even_mixer(h_lat, h_ctx, ev_w_in[e], ev_w_out[e], hy_conv_w[e], hy_conv_b[e],
                                      hy_filt_w1[e], hy_filt_b1[e], hy_filt_w2[e], hy_filt_b2[e], hy_filt_w3[e],
                                      hy_sin_freq[e], hy_bias[e], rg_conv_w[e], rg_conv_b[e], rg_a_w[e], rg_a_b[e],
                                      rg_x_w[e], rg_x_b[e], rg_lambda[e], not last)
        else:
            o = l // 2
            y_lat, y_ctx = odd_mixer(h_lat, h_ctx, mla_w_in[o], mla_q_norm[o], mla_kv_norm[o], mla_w_q_up[o],
                                     mla_w_kv_up[o], mla_w_out[o], cos, sin, not last)
        x_lat = layernorm(ALPHA * x_lat + m_lat[5] * y_lat, ln_g[l, 1], ln_b[l, 1])
        if not last:
            x_ctx = layernorm(ALPHA * x_ctx + m_ctx[5] * y_ctx, ln_g[l, 1], ln_b[l, 1])
            x_ctx = half_ffn(x_ctx, m_ctx[6], m_ctx[7], m_ctx[8], ffn_w_in[l, 1], ffn_w_out[l, 1], ln_g[l, 2], ln_b[l, 2])
        x_lat = half_ffn(x_lat, m_lat[6], m_lat[7], m_lat[8], ffn_w_in[l, 1], ffn_w_out[l, 1], ln_g[l, 2], ln_b[l, 2])
    return x_lat
```

```python
import functools
import math

import jax
import jax.numpy as jnp
from jax import lax
from jax.experimental import pallas as pl
from jax.experimental.pallas import tpu as pltpu

F32 = jnp.float32
BF16 = jnp.bfloat16

D_MODEL = 1024
BATCH = 8
SEQ = 2048
DEPTH = 2
CTX_LEN = 256
GRID_W = 64
N_ADA = 9
D_FF = 2816

HY_WIDTH = 512
RG_WIDTH = 512
EV_IN = 3 * HY_WIDTH + 2 * RG_WIDTH
HY_EMB = 33
HY_BANDS = (HY_EMB - 1) // 2
HY_FILT_HIDDEN = 64
HY_TARGET = 1e-2
HY_MAX_DECAY = math.log(HY_TARGET) / 0.3
HY_MIN_DECAY = math.log(HY_TARGET) / 1.5
HY_SHIFT = 0.05
RG_BLOCKS = 8
RG_BLOCK_DIM = RG_WIDTH // RG_BLOCKS
RG_C = 8.0

MLA_HEADS = 16
MLA_Q_LORA = 768
MLA_KV_LORA = 256
MLA_NOPE = 64
MLA_ROPE = 32
MLA_V = 64
MLA_SCALE = (MLA_NOPE + MLA_ROPE) ** -0.5
ROPE_AXIS_PAIRS = MLA_ROPE // 4
ROPE_BASE = 10000.0

ALPHA = (2.0 * DEPTH) ** 0.25
LN_EPS = 1e-6
RMS_EPS = 1e-6

T_LAT = BATCH * SEQ
T_CTX = BATCH * CTX_LEN
T_ALL = T_LAT + T_CTX
N_GROUPS = BATCH + 1

LANES = 128
SUBLANES = 8
VMEM_LIMIT = 56 * 1024 * 1024

TM_FFN = 512
FF_CHUNK = 256
N_FF_CHUNKS = D_FF // FF_CHUNK
TM_PROJ = 512
HY_CT = 256
HY_FREQ_TILE = 512
RG_CT = 128
RG_TC = 32
RG_PAD_LO = 2
RG_PAD_HI = 6
HEAD_SLOT = 128
TQ = 512
ADA_TN = 1152


def _dot(a, b):
    return jnp.dot(a, b, preferred_element_type=F32)


def _dot_nt(a, b):
    return lax.dot_general(a, b, (((1,), (1,)), ((), ())), preferred_element_type=F32)


def _resident(shape):
    nd = len(shape)
    return pl.BlockSpec(shape, lambda *_: (0,) * nd, pipeline_mode=pl.Buffered(1))


def _group_of_tile(i, tm):
    return jnp.where(i < T_LAT // tm, i // (SEQ // tm), BATCH)


def _params(*sem):
    return pltpu.CompilerParams(dimension_semantics=sem, vmem_limit_bytes=VMEM_LIMIT)


def _layernorm(z, g, b):
    mu = jnp.mean(z, axis=-1, keepdims=True)
    zc = z - mu
    var = jnp.mean(zc * zc, axis=-1, keepdims=True)
    return zc * lax.rsqrt(var + LN_EPS) * g + b


def _adaln_kernel(c_ref, w_ref, b_ref, o_ref):
    c = c_ref[...]
    s = (c * jax.nn.sigmoid(c)).astype(BF16)
    o_ref[0] = _dot(s, w_ref[0].astype(BF16)) + b_ref[0]


def _adaln(cond, ada_w, ada_b):
    n_out = N_ADA * D_MODEL
    rows = cond.shape[0]
    return pl.pallas_call(
        _adaln_kernel,
        grid=(DEPTH, n_out // ADA_TN),
        in_specs=[
            pl.BlockSpec((rows, D_MODEL), lambda l, j: (0, 0)),
            pl.BlockSpec((1, D_MODEL, ADA_TN), lambda l, j: (l, 0, j)),
            pl.BlockSpec((1, 1, ADA_TN), lambda l, j: (l, 0, j)),
        ],
        out_specs=pl.BlockSpec((1, rows, ADA_TN), lambda l, j: (l, 0, j)),
        out_shape=jax.ShapeDtypeStruct((DEPTH, rows, n_out), F32),
        compiler_params=_params("arbitrary", "arbitrary"),
        name="adaln",
    )(cond, ada_w, ada_b.reshape(DEPTH, 1, n_out))


def _ffn_kernel(x_ref, mod_ref, wa_ref, wu_ref, wo_ref, g_ref, b_ref, o_ref, act_ref):
    x = x_ref[...]
    shift, scale, gate = mod_ref[0, 0], mod_ref[0, 1], mod_ref[0, 2]
    h = (x * (1.0 + scale) + shift).astype(BF16)
    for j in range(N_FF_CHUNKS):
        a = _dot(h, wa_ref[j])
        u = _dot(h, wu_ref[j])
        act_ref[:, j * FF_CHUNK:(j + 1) * FF_CHUNK] = (a * jax.nn.sigmoid(a) * u).astype(BF16)
    y = _dot(act_ref[...], wo_ref[...])
    o_ref[...] = _layernorm(ALPHA * x + (0.5 * gate) * y, g_ref[...], b_ref[...])


def _ffn(x, mod, w_in, w_out, g, b, n_rows):
    wa = w_in[:, :D_FF].reshape(D_MODEL, N_FF_CHUNKS, FF_CHUNK).transpose(1, 0, 2).astype(BF16)
    wu = w_in[:, D_FF:].reshape(D_MODEL, N_FF_CHUNKS, FF_CHUNK).transpose(1, 0, 2).astype(BF16)
    wo = w_out.astype(BF16)
    tm = TM_FFN
    return pl.pallas_call(
        _ffn_kernel,
        grid=(n_rows // tm,),
        in_specs=[
            pl.BlockSpec((tm, D_MODEL), lambda i: (i, 0)),
            pl.BlockSpec((1, 3, 1, D_MODEL), lambda i: (_group_of_tile(i, tm), 0, 0, 0)),
            _resident((N_FF_CHUNKS, D_MODEL, FF_CHUNK)),
            _resident((N_FF_CHUNKS, D_MODEL, FF_CHUNK)),
            _resident((D_FF, D_MODEL)),
            _resident((1, D_MODEL)),
            _resident((1, D_MODEL)),
        ],
        out_specs=pl.BlockSpec((tm, D_MODEL), lambda i: (i, 0)),
        out_shape=jax.ShapeDtypeStruct((n_rows, D_MODEL), F32),
        scratch_shapes=[pltpu.VMEM((tm, D_FF), BF16)],
        compiler_params=_params("arbitrary"),
        name="half_ffn",
    )(x, mod, wa, wu, wo, g.reshape(1, D_MODEL), b.reshape(1, D_MODEL))


def _evproj_kernel(x_ref, mod_ref, w_ref, hy_ref, rx_ref, gt_ref):
    shift, scale = mod_ref[0, 0], mod_ref[0, 1]
    h = (x_ref[...] * (1.0 + scale) + shift).astype(BF16)
    n_hy = 3 * HY_WIDTH
    hy_ref[...] = _dot(h, w_ref[:, :n_hy])
    rx_ref[...] = _dot(h, w_ref[:, n_hy:n_hy + RG_WIDTH])
    gt_ref[...] = _dot(h, w_ref[:, n_hy + RG_WIDTH:])


def _evproj(x, mod, w_in):
    tm = TM_PROJ
    row = lambda i: (i, 0)
    return pl.pallas_call(
        _evproj_kernel,
        grid=(T_ALL // tm,),
        in_specs=[
            pl.BlockSpec((tm, D_MODEL), row),
            pl.BlockSpec((1, 2, 1, D_MODEL), lambda i: (_group_of_tile(i, tm), 0, 0, 0)),
            _resident((D_MODEL, EV_IN)),
        ],
        out_specs=[
            pl.BlockSpec((tm, 3 * HY_WIDTH), row),
            pl.BlockSpec((tm, RG_WIDTH), row),
            pl.BlockSpec((tm, RG_WIDTH), row),
        ],
        out_shape=[
            jax.ShapeDtypeStruct((T_ALL, 3 * HY_WIDTH), F32),
            jax.ShapeDtypeStruct((T_ALL, RG_WIDTH), F32),
            jax.ShapeDtypeStruct((T_ALL, RG_WIDTH), F32),
        ],
        compiler_params=_params("arbitrary"),
        name="even_in_proj",
    )(x, mod, w_in.astype(BF16))


def _rg_kernel(rxl_ref, rxc_ref, cw_ref, cb_ref, w_ref, bias_ref, lam_ref, hl_ref, hc_ref,
               af_ref, bf_ref, ab_ref, bb_ref):
    neg_lam = -lam_ref[...]
    softplus = jnp.maximum(neg_lam, 0.0) + jnp.log1p(jnp.exp(-jnp.abs(neg_lam)))
    cb = cb_ref[...]

    def coeffs(src_ref, t0, d, a_ref, b_ref):
        xc = cb + src_ref[pl.ds(t0, RG_TC)] * cw_ref[0:1, :]
        for k in range(1, 4):
            xc = xc + src_ref[pl.ds(t0 + k, RG_TC)] * cw_ref[k:k + 1, :]
        x2 = xc.reshape(RG_TC * SUBLANES, RG_CT)
        xb = x2.astype(BF16)
        r = jax.nn.sigmoid(_dot(xb, w_ref[0, 2 * d]) + bias_ref[2 * d:2 * d + 1, :])
        i = jax.nn.sigmoid(_dot(xb, w_ref[0, 2 * d + 1]) + bias_ref[2 * d + 1:2 * d + 2, :])
        log_a = (-RG_C) * r * softplus[d:d + 1, :]
        a = jnp.exp(log_a)
        b = jnp.sqrt(-jnp.tanh(log_a) * (a * a + 1.0)) * (i * x2)
        a_ref[...] = a.reshape(RG_TC, SUBLANES, RG_CT)
        b_ref[...] = b.reshape(RG_TC, SUBLANES, RG_CT)

    def sweep(src_ref, out_ref, length, h_fwd, h_bwd):
        n_chunks = length // RG_TC

        def body(j, carry):
            hf, hb = carry
            tf = j * RG_TC
            tb = (n_chunks - 1 - j) * RG_TC
            coeffs(src_ref, tf, 0, af_ref, bf_ref)
            coeffs(src_ref, tb, 1, ab_ref, bb_ref)
            for s in range(RG_TC):
                hf = af_ref[s] * hf + bf_ref[s]
                out_ref[tf + s] += hf
                sb = RG_TC - 1 - s
                hb = ab_ref[sb] * hb + bb_ref[sb]
                out_ref[tb + sb] += hb
            return hf, hb

        return lax.fori_loop(0, n_chunks, body, (h_fwd, h_bwd))

    hl_ref[...] = jnp.zeros_like(hl_ref)
    hc_ref[...] = jnp.zeros_like(hc_ref)
    zero = jnp.zeros((SUBLANES, RG_CT), F32)
    hf, hb = sweep(rxc_ref, hc_ref, CTX_LEN, zero, zero)
    sweep(rxl_ref, hl_ref, SEQ, hf, hb)


def _block_diag_tiles(w):
    per = RG_CT // RG_BLOCK_DIM
    w = w.reshape(RG_WIDTH // RG_CT, per, RG_BLOCK_DIM, RG_BLOCK_DIM)
    eye = jnp.eye(per, dtype=w.dtype)
    return jnp.einsum('cpde,pq->cpdqe', w, eye).reshape(RG_WIDTH // RG_CT, RG_CT, RG_CT)


def _rglru(rx, conv_w, conv_b, a_w, a_b, x_w, x_b, lam):
    def time_major(v, length):
        v = v.reshape(BATCH, length, RG_WIDTH).transpose(1, 0, 2)
        return jnp.pad(v, ((RG_PAD_LO, RG_PAD_HI), (0, 0), (0, 0)))

    rxl = time_major(rx[:T_LAT], SEQ)
    rxc = time_major(rx[T_LAT:], CTX_LEN)
    w = jnp.stack([_block_diag_tiles(a_w[0]), _block_diag_tiles(x_w[0]),
                   _block_diag_tiles(a_w[1]), _block_diag_tiles(x_w[1])], axis=1).astype(BF16)
    bias = jnp.stack([a_b[0], x_b[0], a_b[1], x_b[1]], axis=0)
    n_ct = RG_WIDTH // RG_CT
    chan = lambda c: (0, 0, c)
    lane = lambda c: (0, c)
    coef = pltpu.VMEM((RG_TC, SUBLANES, RG_CT), F32)
    hl, hc = pl.pallas_call(
        _rg_kernel,
        grid=(n_ct,),
        in_specs=[
            pl.BlockSpec((SEQ + RG_PAD_LO + RG_PAD_HI, BATCH, RG_CT), chan),
            pl.BlockSpec((CTX_LEN + RG_PAD_LO + RG_PAD_HI, BATCH, RG_CT), chan),
            pl.BlockSpec((4, RG_CT), lane),
            pl.BlockSpec((1, RG_CT), lane),
            pl.BlockSpec((1, 4, RG_CT, RG_CT), lambda c: (c, 0, 0, 0)),
            pl.BlockSpec((4, RG_CT), lane),
            pl.BlockSpec((2, RG_CT), lane),
        ],
        out_specs=[
            pl.BlockSpec((SEQ, BATCH, RG_CT), chan),
            pl.BlockSpec((CTX_LEN, BATCH, RG_CT), chan),
        ],
        out_shape=[
            jax.ShapeDtypeStruct((SEQ, BATCH, RG_WIDTH), F32),
            jax.ShapeDtypeStruct((CTX_LEN, BATCH, RG_WIDTH), F32),
        ],
        scratch_shapes=[coef, coef, coef, coef],
        compiler_params=_params("arbitrary"),
        name="rglru_scan",
    )(rxl, rxc, conv_w, conv_b.reshape(1, RG_WIDTH), w, bias, lam)
    return jnp.concatenate([hl.transpose(1, 0, 2).reshape(T_LAT, RG_WIDTH),
                            hc.transpose(1, 0, 2).reshape(T_CTX, RG_WIDTH)], axis=0)


def _dft_tables(length):
    k = jnp.arange(length, dtype=jnp.int32)
    ks = (k[:, None] * k[None, :]) % (2 * length)
    ang = ks.astype(F32) * (math.pi / length)
    return jnp.cos(ang).astype(BF16), jnp.sin(ang).astype(BF16)


def _filter_features(length):
    pos = jnp.arange(length, dtype=F32)
    t = pos / length
    bands = jnp.linspace(1e-4, HY_BANDS - 1, HY_BANDS, dtype=F32)
    ang = (2.0 * math.pi * pos / length)[:, None] * bands[None, :]
    feats = jnp.concatenate([t[:, None], jnp.cos(ang), -jnp.sin(ang)], axis=-1)
    feats = jnp.pad(feats, ((0, 0), (0, LANES - HY_EMB)))
    deltas = jnp.abs(jnp.linspace(HY_MIN_DECAY, HY_MAX_DECAY, HY_WIDTH, dtype=F32))
    window = jnp.exp(-t[:, None] * deltas[None, :]) + HY_SHIFT
    return feats, window


def _filter_kernel(feat_ref, w1_ref, b1_ref, w2_ref, b2_ref, w3f_ref, w3b_ref, sf_ref, win_ref,
                   fc_ref, fs_ref, kc_ref, ks_ref, kn_ref, *, length):
    n = 2 * length
    h = jnp.sin(sf_ref[0:1, :] * (_dot(feat_ref[...].astype(BF16), w1_ref[...]) + b1_ref[...]))
    h = jnp.sin(sf_ref[1:2, :] * (_dot(h.astype(BF16), w2_ref[...]) + b2_ref[...]))
    hb16 = h.astype(BF16)
    win = win_ref[...]
    row = lax.broadcasted_iota(jnp.int32, (length, HY_CT), 0)
    h_fwd = _dot(hb16, w3f_ref[...]) * win
    h_bwd0 = jnp.where(row == 0, 0.0, _dot(hb16, w3b_ref[...]) * win)
    h_sum = h_fwd + h_bwd0
    h_dif = h_fwd - h_bwd0
    weight = jnp.where(row == 0, 1.0 / n, 2.0 / n)
    kc_ref[...] = _dot(fc_ref[...], h_sum.astype(BF16)) * weight
    ks_ref[...] = _dot(fs_ref[...], h_dif.astype(BF16)) * weight
    sign = jnp.where((row & 1) == 0, 1.0, -1.0)
    kn_ref[...] = jnp.sum(h_sum * sign, axis=0, keepdims=True) * (1.0 / n)


def _hyena_filter_spectrum(length, fc, fs, w1, b1, w2, b2, w3, sin_freq):
    feats, window = _filter_features(length)
    hp = LANES - HY_FILT_HIDDEN
    w1p = jnp.pad(w1, ((0, LANES - HY_EMB), (0, hp))).astype(BF16)
    w2p = jnp.pad(w2, ((0, hp), (0, hp))).astype(BF16)
    w3p = jnp.pad(w3, ((0, hp), (0, 0))).astype(BF16)
    b1p = jnp.pad(b1, (0, hp)).reshape(1, LANES)
    b2p = jnp.pad(b2, (0, hp)).reshape(1, LANES)
    sfp = jnp.pad(sin_freq, ((0, 0), (0, hp)))
    spec = jax.ShapeDtypeStruct((length, HY_WIDTH), F32)
    chan = lambda c: (0, c)
    return pl.pallas_call(
        functools.partial(_filter_kernel, length=length),
        grid=(HY_WIDTH // HY_CT,),
        in_specs=[
            _resident((length, LANES)),
            _resident((LANES, LANES)),
            _resident((1, LANES)),
            _resident((LANES, LANES)),
            _resident((1, LANES)),
            pl.BlockSpec((LANES, HY_CT), chan),
            pl.BlockSpec((LANES, HY_CT), chan),
            _resident((2, LANES)),
            pl.BlockSpec((length, HY_CT), chan),
            _resident((length, length)),
            _resident((length, length)),
        ],
        out_specs=[
            pl.BlockSpec((length, HY_CT), chan),
            pl.BlockSpec((length, HY_CT), chan),
            pl.BlockSpec((1, HY_CT), chan),
        ],
        out_shape=[spec, spec, jax.ShapeDtypeStruct((1, HY_WIDTH), F32)],
        compiler_params=_params("arbitrary"),
        name=f"hyena_filter_{length}",
    )(feats, w1p, b1p, w2p, b2p, w3p[:, :HY_WIDTH], w3p[:, HY_WIDTH:], sfp, window, fc, fs)


def _hyena_kernel(x0_ref, x1_ref, v_ref, cw_ref, cb_ref, fc_ref, fs_ref, kc_ref, ks_ref, kn_ref,
                  bias_ref, *rest, length):
    o_ref = rest[-1]
    row = lax.broadcasted_iota(jnp.int32, (length, HY_CT), 0)

    def short_conv(ref, part):
        x = ref[...]
        prev = jnp.where(row == 0, 0.0, pltpu.roll(x, 1, 0))
        nxt = jnp.where(row == length - 1, 0.0, pltpu.roll(x, length - 1, 0))
        return (prev * cw_ref[part, 0:1, :] + x * cw_ref[part, 1:2, :] + nxt * cw_ref[part, 2:3, :]
                + cb_ref[part:part + 1, :])

    x0 = short_conv(x0_ref, 0)
    x1 = short_conv(x1_ref, 1)
    v = short_conv(v_ref, 2)
    vx = v * x1
    vb = vx.astype(BF16)
    ft = min(length, HY_FREQ_TILE)
    y = None
    for f0 in range(0, length, ft):
        spec_c = _dot(fc_ref[f0:f0 + ft, :], vb)
        spec_s = _dot(fs_ref[f0:f0 + ft, :], vb)
        kc = kc_ref[f0:f0 + ft, :]
        ks = ks_ref[f0:f0 + ft, :]
        p = (spec_c * kc - spec_s * ks).astype(BF16)
        q = (spec_c * ks + spec_s * kc).astype(BF16)
        part = _dot(fc_ref[:, f0:f0 + ft], p) + _dot(fs_ref[:, f0:f0 + ft], q)
        y = part if y is None else y + part
    sign = jnp.where((row & 1) == 0, 1.0, -1.0)
    nyquist = jnp.sum(vx * sign, axis=0, keepdims=True) * kn_ref[...]
    o_ref[...] = x0 * (y + sign * nyquist + vx * bias_ref[...])


def _hyena(hy, length, row_block0, tables, spectrum, conv_w, conv_b, bias, prev_out=None):
    fc, fs = tables
    kc, ks, kn = spectrum
    n_ct = HY_WIDTH // HY_CT
    cw = conv_w.reshape(3, 3, HY_WIDTH).transpose(1, 0, 2)
    cb = conv_b.reshape(3, HY_WIDTH)

    def part_spec(part):
        return pl.BlockSpec((length, HY_CT), lambda c, b: (row_block0 + b, part * n_ct + c))

    chan = lambda c, b: (0, c)
    in_specs = [
        part_spec(0), part_spec(1), part_spec(2),
        pl.BlockSpec((3, 3, HY_CT), lambda c, b: (0, 0, c)),
        pl.BlockSpec((3, HY_CT), chan),
        _resident((length, length)),
        _resident((length, length)),
        pl.BlockSpec((length, HY_CT), chan, pipeline_mode=pl.Buffered(1)),
        pl.BlockSpec((length, HY_CT), chan, pipeline_mode=pl.Buffered(1)),
        pl.BlockSpec((1, HY_CT), chan),
        pl.BlockSpec((1, HY_CT), chan),
    ]
    args = [hy, hy, hy, cw, cb, fc, fs, kc, ks, kn, bias.reshape(1, HY_WIDTH)]
    aliases = {}
    if prev_out is not None:
        in_specs.append(pl.BlockSpec(memory_space=pl.ANY))
        args.append(prev_out)
        aliases = {len(args) - 1: 0}
    return pl.pallas_call(
        functools.partial(_hyena_kernel, length=length),
        grid=(n_ct, BATCH),
        in_specs=in_specs,
        out_specs=pl.BlockSpec((length, HY_CT), lambda c, b: (row_block0 + b, c)),
        out_shape=jax.ShapeDtypeStruct((T_ALL, HY_WIDTH), F32),
        input_output_aliases=aliases,
        compiler_params=_params("arbitrary", "arbitrary"),
        name=f"hyena_conv_{length}",
    )(*args)


def _evout_kernel(x_ref, yh_ref, hs_ref, gt_ref, mod_ref, w_ref, g_ref, b_ref, o_ref):
    x = x_ref[...]
    y_rg = hs_ref[...] * jax.nn.gelu(gt_ref[...], approximate=True)
    y = (_dot(yh_ref[...].astype(BF16), w_ref[:HY_WIDTH, :])
         + _dot(y_rg.astype(BF16), w_ref[HY_WIDTH:, :]))
    o_ref[...] = _layernorm(ALPHA * x + mod_ref[0, 0] * y, g_ref[...], b_ref[...])


def _evout(x, y_hy, h_sum, gate_in, mod, w_out, g, b):
    tm = TM_PROJ
    row = lambda i: (i, 0)
    return pl.pallas_call(
        _evout_kernel,
        grid=(T_ALL // tm,),
        in_specs=[
            pl.BlockSpec((tm, D_MODEL), row),
            pl.BlockSpec((tm, HY_WIDTH), row),
            pl.BlockSpec((tm, RG_WIDTH), row),
            pl.BlockSpec((tm, RG_WIDTH), row),
            pl.BlockSpec((1, 1, 1, D_MODEL), lambda i: (_group_of_tile(i, tm), 0, 0, 0)),
            _resident((HY_WIDTH + RG_WIDTH, D_MODEL)),
            _resident((1, D_MODEL)),
            _resident((1, D_MODEL)),
        ],
        out_specs=pl.BlockSpec((tm, D_MODEL), row),
        out_shape=jax.ShapeDtypeStruct((T_ALL, D_MODEL), F32),
        compiler_params=_params("arbitrary"),
        name="even_out_proj",
    )(x, y_hy, h_sum, gate_in, mod, w_out.astype(BF16), g.reshape(1, D_MODEL), b.reshape(1, D_MODEL))


def _rope_tables():
    rows = jnp.repeat(jnp.arange(SEQ // GRID_W, dtype=F32), GRID_W)
    cols = jnp.tile(jnp.arange(GRID_W, dtype=F32), SEQ // GRID_W)
    inv_freq = ROPE_BASE ** (-jnp.arange(ROPE_AXIS_PAIRS, dtype=F32) / ROPE_AXIS_PAIRS)
    ang_r = rows[:, None] * inv_freq
    ang_c = cols[:, None] * inv_freq
    cos32 = jnp.concatenate([jnp.cos(ang_r)] * 2 + [jnp.cos(ang_c)] * 2, axis=1)
    sin32 = jnp.concatenate([-jnp.sin(ang_r), jnp.sin(ang_r), -jnp.sin(ang_c), jnp.sin(ang_c)], axis=1)
    ones = jnp.ones((SEQ, MLA_NOPE), F32)
    zeros = jnp.zeros((SEQ, MLA_NOPE), F32)
    pad = jnp.zeros((SEQ, HEAD_SLOT - MLA_NOPE - MLA_ROPE), F32)
    q_cos = jnp.concatenate([ones, cos32, pad], axis=1) * MLA_SCALE
    q_sin = jnp.concatenate([zeros, sin32, pad], axis=1) * MLA_SCALE
    tail = jnp.zeros((SEQ, HEAD_SLOT - MLA_ROPE), F32)
    k_cos = jnp.concatenate([cos32, tail], axis=1)
    k_sin = jnp.concatenate([sin32, tail], axis=1)
    ident = jnp.zeros((TM_PROJ, HEAD_SLOT), F32).at[:, :MLA_ROPE].set(1.0)
    zero = jnp.zeros((TM_PROJ, HEAD_SLOT), F32)
    return (jnp.concatenate([q_cos, zero], 0), jnp.concatenate([q_sin, zero], 0),
            jnp.concatenate([k_cos, ident], 0), jnp.concatenate([k_sin, zero], 0))


def _rope_partner(n):
    idx = jnp.arange(n)
    return jnp.where((idx % 16) < ROPE_AXIS_PAIRS, idx + ROPE_AXIS_PAIRS, idx - ROPE_AXIS_PAIRS)


def _mla_weights(w_in, w_q_up, w_kv_up):
    w_q = w_in[:, :MLA_Q_LORA]
    w_kv = w_in[:, MLA_Q_LORA:MLA_Q_LORA + MLA_KV_LORA]
    w_kr = w_in[:, MLA_Q_LORA + MLA_KV_LORA:]
    part = _rope_partner(MLA_ROPE)
    lane_pad = ((0, 0), (0, HEAD_SLOT - MLA_ROPE))
    w_kr_p = jnp.pad(w_kr, lane_pad)
    w_kr_partner = jnp.pad(w_kr[:, part], lane_pad)

    qh = w_q_up.reshape(MLA_Q_LORA, MLA_HEADS, MLA_NOPE + MLA_ROPE)
    slot_pad = ((0, 0), (0, 0), (0, HEAD_SLOT - MLA_NOPE - MLA_ROPE))
    wq_slot = jnp.pad(qh, slot_pad).reshape(MLA_Q_LORA, MLA_HEADS * HEAD_SLOT)
    q_partner = jnp.concatenate([jnp.zeros_like(qh[..., :MLA_NOPE]), qh[..., MLA_NOPE:][..., part]], axis=-1)
    wq_partner = jnp.pad(q_partner, slot_pad).reshape(MLA_Q_LORA, MLA_HEADS * HEAD_SLOT)

    kvh = w_kv_up.reshape(MLA_KV_LORA, MLA_HEADS, MLA_NOPE + MLA_V)
    wk_slot = jnp.pad(kvh[..., :MLA_NOPE], ((0, 0), (0, 0), (0, HEAD_SLOT - MLA_NOPE)))
    wk_slot = wk_slot.reshape(MLA_KV_LORA, MLA_HEADS * HEAD_SLOT)
    wv = kvh[..., MLA_NOPE:].reshape(MLA_KV_LORA, MLA_HEADS * MLA_V)

    lane = jnp.arange(MLA_HEADS * HEAD_SLOT) % HEAD_SLOT
    place = (jnp.arange(HEAD_SLOT)[:, None] + MLA_NOPE == lane[None, :]) & (jnp.arange(HEAD_SLOT)[:, None] < MLA_ROPE)
    return [w.astype(BF16) for w in (w_q, w_kv, w_kr_p, w_kr_partner, wq_slot, wq_partner, wk_slot, wv,
                                     place.astype(F32))]


def _mlaproj_kernel(x_ref, mod_ref, wq_ref, wkv_ref, wkr_ref, wkrp_ref, wqs_ref, wqp_ref, wks_ref, wv_ref,
                    place_ref, qg_ref, kvg_ref, qcos_ref, qsin_ref, kcos_ref, ksin_ref,
                    q_ref, k_ref, v_ref):
    shift, scale = mod_ref[0, 0], mod_ref[0, 1]
    h = (x_ref[...] * (1.0 + scale) + shift).astype(BF16)

    def rmsnorm(y, g):
        return (y * lax.rsqrt(jnp.mean(y * y, axis=-1, keepdims=True) + RMS_EPS) * g).astype(BF16)

    qn = rmsnorm(_dot(h, wq_ref[...]), qg_ref[...])
    kvn = rmsnorm(_dot(h, wkv_ref[...]), kvg_ref[...])
    k_rope = _dot(h, wkr_ref[...]) * kcos_ref[...] + _dot(h, wkrp_ref[...]) * ksin_ref[...]
    k_ref[...] = (_dot(kvn, wks_ref[...]) + _dot(k_rope.astype(BF16), place_ref[...])).astype(BF16)
    v_ref[...] = _dot(kvn, wv_ref[...]).astype(BF16)
    q_all = _dot(qn, wqs_ref[...])
    q_partner = _dot(qn, wqp_ref[...])
    q_cos = qcos_ref[...]
    q_sin = qsin_ref[...]
    for hd in range(MLA_HEADS):
        sl = slice(hd * HEAD_SLOT, (hd + 1) * HEAD_SLOT)
        q_ref[:, sl] = (q_all[:, sl] * q_cos + q_partner[:, sl] * q_sin).astype(BF16)


def _mlaproj(x, mod, w_in, q_norm, kv_norm, w_q_up, w_kv_up):
    tm = TM_PROJ
    weights = _mla_weights(w_in, w_q_up, w_kv_up)
    tables = _rope_tables()
    row = lambda i: (i, 0)
    tab = lambda i: (jnp.where(i < T_LAT // tm, i % (SEQ // tm), SEQ // tm), 0)
    wide = MLA_HEADS * HEAD_SLOT
    return pl.pallas_call(
        _mlaproj_kernel,
        grid=(T_ALL // tm,),
        in_specs=[
            pl.BlockSpec((tm, D_MODEL), row),
            pl.BlockSpec((1, 2, 1, D_MODEL), lambda i: (_group_of_tile(i, tm), 0, 0, 0)),
            *[_resident(w.shape) for w in weights],
            _resident((1, MLA_Q_LORA)),
            _resident((1, MLA_KV_LORA)),
            *[pl.BlockSpec((tm, HEAD_SLOT), tab) for _ in tables],
        ],
        out_specs=[
            pl.BlockSpec((tm, wide), row),
            pl.BlockSpec((tm, wide), row),
            pl.BlockSpec((tm, MLA_HEADS * MLA_V), row),
        ],
        out_shape=[
            jax.ShapeDtypeStruct((T_ALL, wide), BF16),
            jax.ShapeDtypeStruct((T_ALL, wide), BF16),
            jax.ShapeDtypeStruct((T_ALL, MLA_HEADS * MLA_V), BF16),
        ],
        compiler_params=_params("arbitrary"),
        name="mla_proj",
    )(x, mod, *weights, q_norm.reshape(1, MLA_Q_LORA), kv_norm.reshape(1, MLA_KV_LORA), *tables)


def _attn_kernel(q_ref, kl_ref, kc_ref, vl_ref, vc_ref, o_ref):
    lane = lax.broadcasted_iota(jnp.int32, (TQ, 2 * MLA_V), 1)
    outs = []
    for hd in range(2):
        sl = slice(hd * HEAD_SLOT, (hd + 1) * HEAD_SLOT)
        q = q_ref[:, sl]
        s_lat = _dot_nt(q, kl_ref[:, sl])
        s_ctx = _dot_nt(q, kc_ref[:, sl])
        m = jnp.maximum(jnp.max(s_lat, axis=-1, keepdims=True), jnp.max(s_ctx, axis=-1, keepdims=True))
        p_lat = jnp.exp(s_lat - m)
        p_ctx = jnp.exp(s_ctx - m)
        denom = jnp.sum(p_lat, axis=-1, keepdims=True) + jnp.sum(p_ctx, axis=-1, keepdims=True)
        o = _dot(p_lat.astype(BF16), vl_ref[...]) + _dot(p_ctx.astype(BF16), vc_ref[...])
        outs.append(o / denom)
    o_ref[...] = jnp.where(lane < MLA_V, outs[0], outs[1]).astype(BF16)


def _attention(q, k, v):
    n_q = SEQ // TQ
    n_pairs = MLA_HEADS // 2
    ctx_block0 = T_LAT // CTX_LEN
    return pl.pallas_call(
        _attn_kernel,
        grid=(BATCH, n_pairs, n_q),
        in_specs=[
            pl.BlockSpec((TQ, 2 * HEAD_SLOT), lambda b, h, i: (b * n_q + i, h)),
            pl.BlockSpec((SEQ, 2 * HEAD_SLOT), lambda b, h, i: (b, h)),
            pl.BlockSpec((CTX_LEN, 2 * HEAD_SLOT), lambda b, h, i: (ctx_block0 + b, h)),
            pl.BlockSpec((SEQ, 2 * MLA_V), lambda b, h, i: (b, h)),
            pl.BlockSpec((CTX_LEN, 2 * MLA_V), lambda b, h, i: (ctx_block0 + b, h)),
        ],
        out_specs=pl.BlockSpec((TQ, 2 * MLA_V), lambda b, h, i: (b * n_q + i, h)),
        out_shape=jax.ShapeDtypeStruct((T_LAT, MLA_HEADS * MLA_V), BF16),
        compiler_params=_params("arbitrary", "arbitrary", "arbitrary"),
        name="mla_attention",
    )(q, k, k, v, v)


def _mlaout_kernel(x_ref, o_in_ref, mod_ref, w_ref, g_ref, b_ref, o_ref):
    y = _dot(o_in_ref[...], w_ref[...])
    o_ref[...] = _layernorm(ALPHA * x_ref[...] + mod_ref[0, 0] * y, g_ref[...], b_ref[...])


def _mlaout(x, o, mod, w_out, g, b):
    tm = TM_PROJ
    row = lambda i: (i, 0)
    return pl.pallas_call(
        _mlaout_kernel,
        grid=(T_LAT // tm,),
        in_specs=[
            pl.BlockSpec((tm, D_MODEL), row),
            pl.BlockSpec((tm, MLA_HEADS * MLA_V), row),
            pl.BlockSpec((1, 1, 1, D_MODEL), lambda i: (_group_of_tile(i, tm), 0, 0, 0)),
            _resident((MLA_HEADS * MLA_V, D_MODEL)),
            _resident((1, D_MODEL)),
            _resident((1, D_MODEL)),
        ],
        out_specs=pl.BlockSpec((tm, D_MODEL), row),
        out_shape=jax.ShapeDtypeStruct((T_LAT, D_MODEL), F32),
        compiler_params=_params("arbitrary"),
        name="mla_out_proj",
    )(x, o, mod, w_out.astype(BF16), g.reshape(1, D_MODEL), b.reshape(1, D_MODEL))


def kernel(x, c, ctx, c_ctx, ada_w, ada_b, ln_g, ln_b, ffn_w_in, ffn_w_out, ev_w_in, ev_w_out, hy_conv_w, hy_conv_b, hy_filt_w1, hy_filt_b1, hy_filt_w2, hy_filt_b2, hy_filt_w3, hy_sin_freq, hy_bias, rg_conv_w, rg_conv_b, rg_a_w, rg_a_b, rg_x_w, rg_x_b, rg_lambda, mla_w_in, mla_q_norm, mla_kv_norm, mla_w_q_up, mla_w_kv_up, mla_w_out):
    assert x.shape == (BATCH, SEQ, D_MODEL) and ctx.shape == (BATCH, CTX_LEN, D_MODEL)
    xs = jnp.concatenate([x.reshape(T_LAT, D_MODEL), ctx.reshape(T_CTX, D_MODEL)], axis=0)

    cond = jnp.concatenate([c, c_ctx[None], jnp.zeros((2 * SUBLANES - N_GROUPS, D_MODEL), F32)], axis=0)
    mods = _adaln(cond, ada_w, ada_b)[:, :N_GROUPS].reshape(DEPTH, N_GROUPS, N_ADA, 1, D_MODEL)

    m = mods[0]
    xs = _ffn(xs, m[:, 0:3], ffn_w_in[0, 0], ffn_w_out[0, 0], ln_g[0, 0], ln_b[0, 0], T_ALL)
    hy, rx, gate_in = _evproj(xs, m[:, 3:5], ev_w_in[0])
    h_sum = _rglru(rx, rg_conv_w[0], rg_conv_b[0], rg_a_w[0], rg_a_b[0], rg_x_w[0], rg_x_b[0], rg_lambda[0])
    filt = (hy_filt_w1[0], hy_filt_b1[0], hy_filt_w2[0], hy_filt_b2[0], hy_filt_w3[0], hy_sin_freq[0])
    tab_l = _dft_tables(SEQ)
    tab_c = _dft_tables(CTX_LEN)
    spec_l = _hyena_filter_spectrum(SEQ, *tab_l, *filt)
    spec_c = _hyena_filter_spectrum(CTX_LEN, *tab_c, *filt)
    y_hy = _hyena(hy, SEQ, 0, tab_l, spec_l, hy_conv_w[0], hy_conv_b[0], hy_bias[0])
    y_hy = _hyena(hy, CTX_LEN, T_LAT // CTX_LEN, tab_c, spec_c, hy_conv_w[0], hy_conv_b[0], hy_bias[0],
                  prev_out=y_hy)
    xs = _evout(xs, y_hy, h_sum, gate_in, m[:, 5:6], ev_w_out[0], ln_g[0, 1], ln_b[0, 1])
    xs = _ffn(xs, m[:, 6:9], ffn_w_in[0, 1], ffn_w_out[0, 1], ln_g[0, 2], ln_b[0, 2], T_ALL)

    m = mods[1]
    xs = _ffn(xs, m[:, 0:3], ffn_w_in[1, 0], ffn_w_out[1, 0], ln_g[1, 0], ln_b[1, 0], T_ALL)
    q, k, v = _mlaproj(xs, m[:, 3:5], mla_w_in[0], mla_q_norm[0], mla_kv_norm[0], mla_w_q_up[0], mla_w_kv_up[0])
    o = _attention(q, k, v)
    xl = _mlaout(xs, o, m[:, 5:6], mla_w_out[0], ln_g[1, 1], ln_b[1, 1])
    xl = _ffn(xl, m[:, 6:9], ffn_w_in[1, 1], ffn_w_out[1, 1], ln_g[1, 2], ln_b[1, 2], T_LAT)
    return xl.reshape(BATCH, SEQ, D_MODEL)
```

```python
import functools
import math

import jax
import jax.numpy as jnp
from jax import lax
from jax.experimental import pallas as pl
from jax.experimental.pallas import tpu as pltpu

F32 = jnp.float32
BF16 = jnp.bfloat16

D_MODEL = 1024
BATCH = 8
SEQ = 2048
DEPTH = 2
CTX_LEN = 256
GRID_W = 64
N_ADA = 9
D_FF = 2816

HY_WIDTH = 512
RG_WIDTH = 512
EV_IN = 3 * HY_WIDTH + 2 * RG_WIDTH
HY_EMB = 33
HY_BANDS = (HY_EMB - 1) // 2
HY_FILT_HIDDEN = 64
HY_TARGET = 1e-2
HY_MAX_DECAY = math.log(HY_TARGET) / 0.3
HY_MIN_DECAY = math.log(HY_TARGET) / 1.5
HY_SHIFT = 0.05
RG_BLOCKS = 8
RG_BLOCK_DIM = RG_WIDTH // RG_BLOCKS
RG_C = 8.0

MLA_HEADS = 16
MLA_Q_LORA = 768
MLA_KV_LORA = 256
MLA_NOPE = 64
MLA_ROPE = 32
MLA_V = 64
MLA_SCALE = (MLA_NOPE + MLA_ROPE) ** -0.5
ROPE_AXIS_PAIRS = MLA_ROPE // 4
ROPE_BASE = 10000.0

ALPHA = (2.0 * DEPTH) ** 0.25
LN_EPS = 1e-6
RMS_EPS = 1e-6

T_LAT = BATCH * SEQ
T_CTX = BATCH * CTX_LEN
T_ALL = T_LAT + T_CTX
N_GROUPS = BATCH + 1

LANES = 128
SUBLANES = 8
VMEM_LIMIT = 56 * 1024 * 1024

TM_FFN = 512
FF_CHUNK = 256
N_FF_CHUNKS = D_FF // FF_CHUNK
TM_PROJ = 512
HY_CT = 256
HY_FREQ_TILE = 512
RG_CT = 128
RG_TC = 32
RG_PAD_LO = 2
RG_PAD_HI = 6
HEAD_SLOT = 128
TQ = 512
ATTN_HEADS = 4
Q_SCALE = MLA_SCALE * math.log2(math.e)
ADA_TN = 1152


def _dot(a, b):
    return jnp.dot(a, b, preferred_element_type=F32)


def _dot_nt(a, b):
    return lax.dot_general(a, b, (((1,), (1,)), ((), ())), preferred_element_type=F32)


def _resident(shape):
    nd = len(shape)
    return pl.BlockSpec(shape, lambda *_: (0,) * nd, pipeline_mode=pl.Buffered(1))


def _group_of_tile(i, tm):
    return jnp.where(i < T_LAT // tm, i // (SEQ // tm), BATCH)


def _params(*sem):
    return pltpu.CompilerParams(dimension_semantics=sem, vmem_limit_bytes=VMEM_LIMIT)


def _layernorm(z, g, b):
    mu = jnp.mean(z, axis=-1, keepdims=True)
    zc = z - mu
    var = jnp.mean(zc * zc, axis=-1, keepdims=True)
    return zc * lax.rsqrt(var + LN_EPS) * g + b


def _adaln_kernel(c_ref, w_ref, b_ref, o_ref):
    c = c_ref[...]
    s = (c * jax.nn.sigmoid(c)).astype(BF16)
    o_ref[0] = _dot(s, w_ref[0].astype(BF16)) + b_ref[0]


def _adaln(cond, ada_w, ada_b):
    n_out = N_ADA * D_MODEL
    rows = cond.shape[0]
    return pl.pallas_call(
        _adaln_kernel,
        grid=(DEPTH, n_out // ADA_TN),
        in_specs=[
            pl.BlockSpec((rows, D_MODEL), lambda l, j: (0, 0)),
            pl.BlockSpec((1, D_MODEL, ADA_TN), lambda l, j: (l, 0, j)),
            pl.BlockSpec((1, 1, ADA_TN), lambda l, j: (l, 0, j)),
        ],
        out_specs=pl.BlockSpec((1, rows, ADA_TN), lambda l, j: (l, 0, j)),
        out_shape=jax.ShapeDtypeStruct((DEPTH, rows, n_out), F32),
        compiler_params=_params("arbitrary", "arbitrary"),
        name="adaln",
    )(cond, ada_w, ada_b.reshape(DEPTH, 1, n_out))


def _ffn_kernel(x_ref, mod_ref, wa_ref, wu_ref, wo_ref, g_ref, b_ref, o_ref, act_ref):
    x = x_ref[...]
    shift, scale, gate = mod_ref[0, 0], mod_ref[0, 1], mod_ref[0, 2]
    h = (x * (1.0 + scale) + shift).astype(BF16)
    for j in range(N_FF_CHUNKS):
        a = _dot(h, wa_ref[j])
        u = _dot(h, wu_ref[j])
        act_ref[:, j * FF_CHUNK:(j + 1) * FF_CHUNK] = (a * jax.nn.sigmoid(a) * u).astype(BF16)
    y = _dot(act_ref[...], wo_ref[...])
    o_ref[...] = _layernorm(ALPHA * x + (0.5 * gate) * y, g_ref[...], b_ref[...])


def _ffn(x, mod, w_in, w_out, g, b, n_rows):
    wa = w_in[:, :D_FF].reshape(D_MODEL, N_FF_CHUNKS, FF_CHUNK).transpose(1, 0, 2).astype(BF16)
    wu = w_in[:, D_FF:].reshape(D_MODEL, N_FF_CHUNKS, FF_CHUNK).transpose(1, 0, 2).astype(BF16)
    wo = w_out.astype(BF16)
    tm = TM_FFN
    return pl.pallas_call(
        _ffn_kernel,
        grid=(n_rows // tm,),
        in_specs=[
            pl.BlockSpec((tm, D_MODEL), lambda i: (i, 0)),
            pl.BlockSpec((1, 3, 1, D_MODEL), lambda i: (_group_of_tile(i, tm), 0, 0, 0)),
            _resident((N_FF_CHUNKS, D_MODEL, FF_CHUNK)),
            _resident((N_FF_CHUNKS, D_MODEL, FF_CHUNK)),
            _resident((D_FF, D_MODEL)),
            _resident((1, D_MODEL)),
            _resident((1, D_MODEL)),
        ],
        out_specs=pl.BlockSpec((tm, D_MODEL), lambda i: (i, 0)),
        out_shape=jax.ShapeDtypeStruct((n_rows, D_MODEL), F32),
        scratch_shapes=[pltpu.VMEM((tm, D_FF), BF16)],
        compiler_params=_params("arbitrary"),
        name="half_ffn",
    )(x, mod, wa, wu, wo, g.reshape(1, D_MODEL), b.reshape(1, D_MODEL))


def _evproj_kernel(x_ref, mod_ref, w_ref, hy_ref, rx_ref, gt_ref):
    shift, scale = mod_ref[0, 0], mod_ref[0, 1]
    h = (x_ref[...] * (1.0 + scale) + shift).astype(BF16)
    n_hy = 3 * HY_WIDTH
    hy_ref[...] = _dot(h, w_ref[:, :n_hy])
    rx_ref[...] = _dot(h, w_ref[:, n_hy:n_hy + RG_WIDTH])
    gt_ref[...] = _dot(h, w_ref[:, n_hy + RG_WIDTH:])


def _evproj(x, mod, w_in):
    tm = TM_PROJ
    row = lambda i: (i, 0)
    return pl.pallas_call(
        _evproj_kernel,
        grid=(T_ALL // tm,),
        in_specs=[
            pl.BlockSpec((tm, D_MODEL), row),
            pl.BlockSpec((1, 2, 1, D_MODEL), lambda i: (_group_of_tile(i, tm), 0, 0, 0)),
            _resident((D_MODEL, EV_IN)),
        ],
        out_specs=[
            pl.BlockSpec((tm, 3 * HY_WIDTH), row),
            pl.BlockSpec((tm, RG_WIDTH), row),
            pl.BlockSpec((tm, RG_WIDTH), row),
        ],
        out_shape=[
            jax.ShapeDtypeStruct((T_ALL, 3 * HY_WIDTH), F32),
            jax.ShapeDtypeStruct((T_ALL, RG_WIDTH), F32),
            jax.ShapeDtypeStruct((T_ALL, RG_WIDTH), F32),
        ],
        compiler_params=_params("arbitrary"),
        name="even_in_proj",
    )(x, mod, w_in.astype(BF16))


def _rg_kernel(rxl_ref, rxc_ref, cw_ref, cb_ref, w_ref, bias_ref, lam_ref, hl_ref, hc_ref,
               af_ref, bf_ref, ab_ref, bb_ref):
    neg_lam = -lam_ref[...]
    softplus = jnp.maximum(neg_lam, 0.0) + jnp.log1p(jnp.exp(-jnp.abs(neg_lam)))
    cb = cb_ref[...]

    def coeffs(src_ref, t0, d, a_ref, b_ref):
        xc = cb + src_ref[pl.ds(t0, RG_TC)] * cw_ref[0:1, :]
        for k in range(1, 4):
            xc = xc + src_ref[pl.ds(t0 + k, RG_TC)] * cw_ref[k:k + 1, :]
        x2 = xc.reshape(RG_TC * SUBLANES, RG_CT)
        xb = x2.astype(BF16)
        r = jax.nn.sigmoid(_dot(xb, w_ref[0, 2 * d]) + bias_ref[2 * d:2 * d + 1, :])
        i = jax.nn.sigmoid(_dot(xb, w_ref[0, 2 * d + 1]) + bias_ref[2 * d + 1:2 * d + 2, :])
        log_a = (-RG_C) * r * softplus[d:d + 1, :]
        a = jnp.exp(log_a)
        b = jnp.sqrt(-jnp.tanh(log_a) * (a * a + 1.0)) * (i * x2)
        a_ref[...] = a.reshape(RG_TC, SUBLANES, RG_CT)
        b_ref[...] = b.reshape(RG_TC, SUBLANES, RG_CT)

    def sweep(src_ref, out_ref, length, h_fwd, h_bwd):
        n_chunks = length // RG_TC

        def body(j, carry):
            hf, hb = carry
            tf = j * RG_TC
            tb = (n_chunks - 1 - j) * RG_TC
            coeffs(src_ref, tf, 0, af_ref, bf_ref)
            coeffs(src_ref, tb, 1, ab_ref, bb_ref)
            for s in range(RG_TC):
                hf = af_ref[s] * hf + bf_ref[s]
                out_ref[tf + s] += hf
                sb = RG_TC - 1 - s
                hb = ab_ref[sb] * hb + bb_ref[sb]
                out_ref[tb + sb] += hb
            return hf, hb

        return lax.fori_loop(0, n_chunks, body, (h_fwd, h_bwd))

    hl_ref[...] = jnp.zeros_like(hl_ref)
    hc_ref[...] = jnp.zeros_like(hc_ref)
    zero = jnp.zeros((SUBLANES, RG_CT), F32)
    hf, hb = sweep(rxc_ref, hc_ref, CTX_LEN, zero, zero)
    sweep(rxl_ref, hl_ref, SEQ, hf, hb)


def _block_diag_tiles(w):
    per = RG_CT // RG_BLOCK_DIM
    w = w.reshape(RG_WIDTH // RG_CT, per, RG_BLOCK_DIM, RG_BLOCK_DIM)
    eye = jnp.eye(per, dtype=w.dtype)
    return jnp.einsum('cpde,pq->cpdqe', w, eye).reshape(RG_WIDTH // RG_CT, RG_CT, RG_CT)


def _rglru(rx, conv_w, conv_b, a_w, a_b, x_w, x_b, lam):
    def time_major(v, length):
        v = v.reshape(BATCH, length, RG_WIDTH).transpose(1, 0, 2)
        return jnp.pad(v, ((RG_PAD_LO, RG_PAD_HI), (0, 0), (0, 0)))

    rxl = time_major(rx[:T_LAT], SEQ)
    rxc = time_major(rx[T_LAT:], CTX_LEN)
    w = jnp.stack([_block_diag_tiles(a_w[0]), _block_diag_tiles(x_w[0]),
                   _block_diag_tiles(a_w[1]), _block_diag_tiles(x_w[1])], axis=1).astype(BF16)
    bias = jnp.stack([a_b[0], x_b[0], a_b[1], x_b[1]], axis=0)
    n_ct = RG_WIDTH // RG_CT
    chan = lambda c: (0, 0, c)
    lane = lambda c: (0, c)
    coef = pltpu.VMEM((RG_TC, SUBLANES, RG_CT), F32)
    hl, hc = pl.pallas_call(
        _rg_kernel,
        grid=(n_ct,),
        in_specs=[
            pl.BlockSpec((SEQ + RG_PAD_LO + RG_PAD_HI, BATCH, RG_CT), chan),
            pl.BlockSpec((CTX_LEN + RG_PAD_LO + RG_PAD_HI, BATCH, RG_CT), chan),
            pl.BlockSpec((4, RG_CT), lane),
            pl.BlockSpec((1, RG_CT), lane),
            pl.BlockSpec((1, 4, RG_CT, RG_CT), lambda c: (c, 0, 0, 0)),
            pl.BlockSpec((4, RG_CT), lane),
            pl.BlockSpec((2, RG_CT), lane),
        ],
        out_specs=[
            pl.BlockSpec((SEQ, BATCH, RG_CT), chan),
            pl.BlockSpec((CTX_LEN, BATCH, RG_CT), chan),
        ],
        out_shape=[
            jax.ShapeDtypeStruct((SEQ, BATCH, RG_WIDTH), F32),
            jax.ShapeDtypeStruct((CTX_LEN, BATCH, RG_WIDTH), F32),
        ],
        scratch_shapes=[coef, coef, coef, coef],
        compiler_params=_params("arbitrary"),
        name="rglru_scan",
    )(rxl, rxc, conv_w, conv_b.reshape(1, RG_WIDTH), w, bias, lam)
    return jnp.concatenate([hl.transpose(1, 0, 2).reshape(T_LAT, RG_WIDTH),
                            hc.transpose(1, 0, 2).reshape(T_CTX, RG_WIDTH)], axis=0)


def _dft_tables(length):
    step = 1 << (length.bit_length() // 2)
    s = jnp.arange(length, dtype=jnp.int32)

    def thin(kvec):
        ang = ((kvec[:, None] * s[None, :]) % (2 * length)).astype(F32) * (math.pi / length)
        return jnp.cos(ang)[:, None, :], jnp.sin(ang)[:, None, :]

    c_hi, s_hi = thin(jnp.arange(0, length, step, dtype=jnp.int32))
    c_lo, s_lo = thin(jnp.arange(step, dtype=jnp.int32))
    c_lo, s_lo = c_lo.reshape(1, step, length), s_lo.reshape(1, step, length)
    fc = (c_hi * c_lo - s_hi * s_lo).reshape(length, length)
    fs = (s_hi * c_lo + c_hi * s_lo).reshape(length, length)
    return fc.astype(BF16), fs.astype(BF16)


def _filter_features(length):
    pos = jnp.arange(length, dtype=F32)
    t = pos / length
    bands = jnp.linspace(1e-4, HY_BANDS - 1, HY_BANDS, dtype=F32)
    ang = (2.0 * math.pi * pos / length)[:, None] * bands[None, :]
    feats = jnp.concatenate([t[:, None], jnp.cos(ang), -jnp.sin(ang)], axis=-1)
    feats = jnp.pad(feats, ((0, 0), (0, LANES - HY_EMB)))
    deltas = jnp.abs(jnp.linspace(HY_MIN_DECAY, HY_MAX_DECAY, HY_WIDTH, dtype=F32))
    window = jnp.exp(-t[:, None] * deltas[None, :]) + HY_SHIFT
    return feats, window


def _filter_kernel(feat_ref, w1_ref, b1_ref, w2_ref, b2_ref, w3f_ref, w3b_ref, sf_ref, win_ref,
                   fc_ref, fs_ref, kc_ref, ks_ref, kn_ref, *, length):
    n = 2 * length
    h = jnp.sin(sf_ref[0:1, :] * (_dot(feat_ref[...].astype(BF16), w1_ref[...]) + b1_ref[...]))
    h = jnp.sin(sf_ref[1:2, :] * (_dot(h.astype(BF16), w2_ref[...]) + b2_ref[...]))
    hb16 = h.astype(BF16)
    win = win_ref[...]
    row = lax.broadcasted_iota(jnp.int32, (length, HY_CT), 0)
    h_fwd = _dot(hb16, w3f_ref[...]) * win
    h_bwd0 = jnp.where(row == 0, 0.0, _dot(hb16, w3b_ref[...]) * win)
    h_sum = h_fwd + h_bwd0
    h_dif = h_fwd - h_bwd0
    weight = jnp.where(row == 0, 1.0 / n, 2.0 / n)
    kc_ref[...] = _dot(fc_ref[...], h_sum.astype(BF16)) * weight
    ks_ref[...] = _dot(fs_ref[...], h_dif.astype(BF16)) * weight
    sign = jnp.where((row & 1) == 0, 1.0, -1.0)
    kn_ref[...] = jnp.sum(h_sum * sign, axis=0, keepdims=True) * (1.0 / n)


def _hyena_filter_spectrum(length, fc, fs, w1, b1, w2, b2, w3, sin_freq):
    feats, window = _filter_features(length)
    hp = LANES - HY_FILT_HIDDEN
    w1p = jnp.pad(w1, ((0, LANES - HY_EMB), (0, hp))).astype(BF16)
    w2p = jnp.pad(w2, ((0, hp), (0, hp))).astype(BF16)
    w3p = jnp.pad(w3, ((0, hp), (0, 0))).astype(BF16)
    b1p = jnp.pad(b1, (0, hp)).reshape(1, LANES)
    b2p = jnp.pad(b2, (0, hp)).reshape(1, LANES)
    sfp = jnp.pad(sin_freq, ((0, 0), (0, hp)))
    spec = jax.ShapeDtypeStruct((length, HY_WIDTH), F32)
    chan = lambda c: (0, c)
    return pl.pallas_call(
        functools.partial(_filter_kernel, length=length),
        grid=(HY_WIDTH // HY_CT,),
        in_specs=[
            _resident((length, LANES)),
            _resident((LANES, LANES)),
            _resident((1, LANES)),
            _resident((LANES, LANES)),
            _resident((1, LANES)),
            pl.BlockSpec((LANES, HY_CT), chan),
            pl.BlockSpec((LANES, HY_CT), chan),
            _resident((2, LANES)),
            pl.BlockSpec((length, HY_CT), chan),
            _resident((length, length)),
            _resident((length, length)),
        ],
        out_specs=[
            pl.BlockSpec((length, HY_CT), chan),
            pl.BlockSpec((length, HY_CT), chan),
            pl.BlockSpec((1, HY_CT), chan),
        ],
        out_shape=[spec, spec, jax.ShapeDtypeStruct((1, HY_WIDTH), F32)],
        compiler_params=_params("arbitrary"),
        name=f"hyena_filter_{length}",
    )(feats, w1p, b1p, w2p, b2p, w3p[:, :HY_WIDTH], w3p[:, HY_WIDTH:], sfp, window, fc, fs)


def _hyena_kernel(x0_ref, x1_ref, v_ref, cw_ref, cb_ref, fc_ref, fs_ref, kc_ref, ks_ref, kn_ref,
                  bias_ref, *rest, length):
    o_ref = rest[-1]
    row = lax.broadcasted_iota(jnp.int32, (length, HY_CT), 0)

    def short_conv(ref, part):
        x = ref[...]
        prev = jnp.where(row == 0, 0.0, pltpu.roll(x, 1, 0))
        nxt = jnp.where(row == length - 1, 0.0, pltpu.roll(x, length - 1, 0))
        return (prev * cw_ref[part, 0:1, :] + x * cw_ref[part, 1:2, :] + nxt * cw_ref[part, 2:3, :]
                + cb_ref[part:part + 1, :])

    x0 = short_conv(x0_ref, 0)
    x1 = short_conv(x1_ref, 1)
    v = short_conv(v_ref, 2)
    vx = v * x1
    vb = vx.astype(BF16)
    ft = min(length, HY_FREQ_TILE)
    y = None
    for f0 in range(0, length, ft):
        spec_c = _dot(fc_ref[f0:f0 + ft, :], vb)
        spec_s = _dot(fs_ref[f0:f0 + ft, :], vb)
        kc = kc_ref[f0:f0 + ft, :]
        ks = ks_ref[f0:f0 + ft, :]
        p = (spec_c * kc - spec_s * ks).astype(BF16)
        q = (spec_c * ks + spec_s * kc).astype(BF16)
        part = _dot(fc_ref[:, f0:f0 + ft], p) + _dot(fs_ref[:, f0:f0 + ft], q)
        y = part if y is None else y + part
    sign = jnp.where((row & 1) == 0, 1.0, -1.0)
    nyquist = jnp.sum(vx * sign, axis=0, keepdims=True) * kn_ref[...]
    o_ref[...] = x0 * (y + sign * nyquist + vx * bias_ref[...])


def _hyena(hy, length, row_block0, tables, spectrum, conv_w, conv_b, bias, prev_out=None):
    fc, fs = tables
    kc, ks, kn = spectrum
    n_ct = HY_WIDTH // HY_CT
    cw = conv_w.reshape(3, 3, HY_WIDTH).transpose(1, 0, 2)
    cb = conv_b.reshape(3, HY_WIDTH)

    def part_spec(part):
        return pl.BlockSpec((length, HY_CT), lambda c, b: (row_block0 + b, part * n_ct + c))

    chan = lambda c, b: (0, c)
    in_specs = [
        part_spec(0), part_spec(1), part_spec(2),
        pl.BlockSpec((3, 3, HY_CT), lambda c, b: (0, 0, c)),
        pl.BlockSpec((3, HY_CT), chan),
        _resident((length, length)),
        _resident((length, length)),
        pl.BlockSpec((length, HY_CT), chan, pipeline_mode=pl.Buffered(1)),
        pl.BlockSpec((length, HY_CT), chan, pipeline_mode=pl.Buffered(1)),
        pl.BlockSpec((1, HY_CT), chan),
        pl.BlockSpec((1, HY_CT), chan),
    ]
    args = [hy, hy, hy, cw, cb, fc, fs, kc, ks, kn, bias.reshape(1, HY_WIDTH)]
    aliases = {}
    if prev_out is not None:
        in_specs.append(pl.BlockSpec(memory_space=pl.ANY))
        args.append(prev_out)
        aliases = {len(args) - 1: 0}
    return pl.pallas_call(
        functools.partial(_hyena_kernel, length=length),
        grid=(n_ct, BATCH),
        in_specs=in_specs,
        out_specs=pl.BlockSpec((length, HY_CT), lambda c, b: (row_block0 + b, c)),
        out_shape=jax.ShapeDtypeStruct((T_ALL, HY_WIDTH), F32),
        input_output_aliases=aliases,
        compiler_params=_params("arbitrary", "arbitrary"),
        name=f"hyena_conv_{length}",
    )(*args)


def _evout_kernel(x_ref, yh_ref, hs_ref, gt_ref, mod_ref, w_ref, g_ref, b_ref, o_ref):
    x = x_ref[...]
    y_rg = hs_ref[...] * jax.nn.gelu(gt_ref[...], approximate=True)
    y = (_dot(yh_ref[...].astype(BF16), w_ref[:HY_WIDTH, :])
         + _dot(y_rg.astype(BF16), w_ref[HY_WIDTH:, :]))
    o_ref[...] = _layernorm(ALPHA * x + mod_ref[0, 0] * y, g_ref[...], b_ref[...])


def _evout(x, y_hy, h_sum, gate_in, mod, w_out, g, b):
    tm = TM_PROJ
    row = lambda i: (i, 0)
    return pl.pallas_call(
        _evout_kernel,
        grid=(T_ALL // tm,),
        in_specs=[
            pl.BlockSpec((tm, D_MODEL), row),
            pl.BlockSpec((tm, HY_WIDTH), row),
            pl.BlockSpec((tm, RG_WIDTH), row),
            pl.BlockSpec((tm, RG_WIDTH), row),
            pl.BlockSpec((1, 1, 1, D_MODEL), lambda i: (_group_of_tile(i, tm), 0, 0, 0)),
            _resident((HY_WIDTH + RG_WIDTH, D_MODEL)),
            _resident((1, D_MODEL)),
            _resident((1, D_MODEL)),
        ],
        out_specs=pl.BlockSpec((tm, D_MODEL), row),
        out_shape=jax.ShapeDtypeStruct((T_ALL, D_MODEL), F32),
        compiler_params=_params("arbitrary"),
        name="even_out_proj",
    )(x, y_hy, h_sum, gate_in, mod, w_out.astype(BF16), g.reshape(1, D_MODEL), b.reshape(1, D_MODEL))


def _rope_tables():
    rows = jnp.repeat(jnp.arange(SEQ // GRID_W, dtype=F32), GRID_W)
    cols = jnp.tile(jnp.arange(GRID_W, dtype=F32), SEQ // GRID_W)
    inv_freq = ROPE_BASE ** (-jnp.arange(ROPE_AXIS_PAIRS, dtype=F32) / ROPE_AXIS_PAIRS)
    ang_r = rows[:, None] * inv_freq
    ang_c = cols[:, None] * inv_freq
    cos32 = jnp.concatenate([jnp.cos(ang_r)] * 2 + [jnp.cos(ang_c)] * 2, axis=1)
    sin32 = jnp.concatenate([-jnp.sin(ang_r), jnp.sin(ang_r), -jnp.sin(ang_c), jnp.sin(ang_c)], axis=1)
    ones = jnp.ones((SEQ, MLA_NOPE), F32)
    zeros = jnp.zeros((SEQ, MLA_NOPE), F32)
    pad = jnp.zeros((SEQ, HEAD_SLOT - MLA_NOPE - MLA_ROPE), F32)
    q_cos = jnp.concatenate([ones, cos32, pad], axis=1) * Q_SCALE
    q_sin = jnp.concatenate([zeros, sin32, pad], axis=1) * Q_SCALE
    k_cos = jnp.concatenate([zeros, cos32, pad], axis=1)
    k_sin = jnp.concatenate([zeros, sin32, pad], axis=1)
    ident = jnp.zeros((TM_PROJ, HEAD_SLOT), F32).at[:, MLA_NOPE:MLA_NOPE + MLA_ROPE].set(1.0)
    zero = jnp.zeros((TM_PROJ, HEAD_SLOT), F32)
    return (jnp.concatenate([q_cos, zero], 0), jnp.concatenate([q_sin, zero], 0),
            jnp.concatenate([k_cos, ident], 0), jnp.concatenate([k_sin, zero], 0))


def _rope_partner(n):
    idx = jnp.arange(n)
    return jnp.where((idx % 16) < ROPE_AXIS_PAIRS, idx + ROPE_AXIS_PAIRS, idx - ROPE_AXIS_PAIRS)


def _mla_weights(w_in, w_q_up, w_kv_up):
    w_q = w_in[:, :MLA_Q_LORA]
    w_kv = w_in[:, MLA_Q_LORA:MLA_Q_LORA + MLA_KV_LORA]
    w_kr = w_in[:, MLA_Q_LORA + MLA_KV_LORA:]
    part = _rope_partner(MLA_ROPE)
    lane_pad = ((0, 0), (MLA_NOPE, HEAD_SLOT - MLA_NOPE - MLA_ROPE))
    w_kr2 = jnp.concatenate([jnp.pad(w_kr, lane_pad), jnp.pad(w_kr[:, part], lane_pad)], axis=1)

    qh = w_q_up.reshape(MLA_Q_LORA, MLA_HEADS, MLA_NOPE + MLA_ROPE)
    slot_pad = ((0, 0), (0, 0), (0, HEAD_SLOT - MLA_NOPE - MLA_ROPE))
    wq_slot = jnp.pad(qh, slot_pad).reshape(MLA_Q_LORA, MLA_HEADS * HEAD_SLOT)
    q_partner = jnp.concatenate([jnp.zeros_like(qh[..., :MLA_NOPE]), qh[..., MLA_NOPE:][..., part]], axis=-1)
    wq_partner = jnp.pad(q_partner, slot_pad).reshape(MLA_Q_LORA, MLA_HEADS * HEAD_SLOT)

    kvh = w_kv_up.reshape(MLA_KV_LORA, MLA_HEADS, MLA_NOPE + MLA_V)
    wk_slot = jnp.pad(kvh[..., :MLA_NOPE], ((0, 0), (0, 0), (0, HEAD_SLOT - MLA_NOPE)))
    wk_slot = wk_slot.reshape(MLA_KV_LORA, MLA_HEADS * HEAD_SLOT)
    wv_slot = jnp.pad(kvh[..., MLA_NOPE:], ((0, 0), (0, 0), (0, HEAD_SLOT - MLA_V)))
    wv_slot = wv_slot.reshape(MLA_KV_LORA, MLA_HEADS * HEAD_SLOT)
    return [w.astype(BF16) for w in (w_q, w_kv, w_kr2, wq_slot, wq_partner, wk_slot, wv_slot)]


def _mlaproj_kernel(x_ref, mod_ref, wq_ref, wkv_ref, wkr_ref, wqs_ref, wqp_ref, wks_ref, wv_ref,
                    vone_ref, qg_ref, kvg_ref, qcos_ref, qsin_ref, kcos_ref, ksin_ref,
                    q_ref, k_ref, v_ref):
    shift, scale = mod_ref[0, 0], mod_ref[0, 1]
    h = (x_ref[...] * (1.0 + scale) + shift).astype(BF16)

    def rmsnorm(y, g):
        return (y * lax.rsqrt(jnp.mean(y * y, axis=-1, keepdims=True) + RMS_EPS) * g).astype(BF16)

    qn = rmsnorm(_dot(h, wq_ref[...]), qg_ref[...])
    kvn = rmsnorm(_dot(h, wkv_ref[...]), kvg_ref[...])
    kr2 = _dot(h, wkr_ref[...])
    k_rope = kr2[:, :HEAD_SLOT] * kcos_ref[...] + kr2[:, HEAD_SLOT:] * ksin_ref[...]
    k_nope = _dot(kvn, wks_ref[...])
    v_ref[...] = (_dot(kvn, wv_ref[...]) + vone_ref[...]).astype(BF16)
    q_all = _dot(qn, wqs_ref[...])
    q_partner = _dot(qn, wqp_ref[...])
    q_cos = qcos_ref[...]
    q_sin = qsin_ref[...]
    for hd in range(MLA_HEADS):
        sl = slice(hd * HEAD_SLOT, (hd + 1) * HEAD_SLOT)
        q_ref[:, sl] = (q_all[:, sl] * q_cos + q_partner[:, sl] * q_sin).astype(BF16)
        k_ref[:, sl] = (k_nope[:, sl] + k_rope).astype(BF16)


def _mlaproj(x, mod, w_in, q_norm, kv_norm, w_q_up, w_kv_up):
    tm = TM_PROJ
    weights = _mla_weights(w_in, w_q_up, w_kv_up)
    tables = _rope_tables()
    row = lambda i: (i, 0)
    tab = lambda i: (jnp.where(i < T_LAT // tm, i % (SEQ // tm), SEQ // tm), 0)
    wide = MLA_HEADS * HEAD_SLOT
    v_one = (jnp.arange(wide) % HEAD_SLOT == MLA_V).astype(F32).reshape(1, wide)
    return pl.pallas_call(
        _mlaproj_kernel,
        grid=(T_ALL // tm,),
        in_specs=[
            pl.BlockSpec((tm, D_MODEL), row),
            pl.BlockSpec((1, 2, 1, D_MODEL), lambda i: (_group_of_tile(i, tm), 0, 0, 0)),
            *[_resident(w.shape) for w in weights],
            _resident((1, wide)),
            _resident((1, MLA_Q_LORA)),
            _resident((1, MLA_KV_LORA)),
            *[pl.BlockSpec((tm, HEAD_SLOT), tab) for _ in tables],
        ],
        out_specs=[pl.BlockSpec((tm, wide), row)] * 3,
        out_shape=[jax.ShapeDtypeStruct((T_ALL, wide), BF16)] * 3,
        compiler_params=_params("arbitrary"),
        name="mla_proj",
    )(x, mod, *weights, v_one, q_norm.reshape(1, MLA_Q_LORA), kv_norm.reshape(1, MLA_KV_LORA), *tables)


def _attn_kernel(q_ref, kl_ref, kc_ref, vl_ref, vc_ref, o_ref):
    lane = lax.broadcasted_iota(jnp.int32, (TQ, HEAD_SLOT), 1)
    outs = []
    for hd in range(ATTN_HEADS):
        sl = slice(hd * HEAD_SLOT, (hd + 1) * HEAD_SLOT)
        q = q_ref[:, sl]
        s_lat = _dot_nt(q, kl_ref[:, sl])
        s_ctx = _dot_nt(q, kc_ref[:, sl])
        m = jnp.maximum(jnp.max(s_lat, axis=-1, keepdims=True), jnp.max(s_ctx, axis=-1, keepdims=True))
        p_lat = jnp.exp2(s_lat - m).astype(BF16)
        p_ctx = jnp.exp2(s_ctx - m).astype(BF16)
        o = _dot(p_lat, vl_ref[:, sl]) + _dot(p_ctx, vc_ref[:, sl])
        outs.append(o / o[:, MLA_V:MLA_V + 1])
    for pair in range(ATTN_HEADS // 2):
        packed = jnp.where(lane < MLA_V, outs[2 * pair], pltpu.roll(outs[2 * pair + 1], MLA_V, 1))
        o_ref[:, pair * HEAD_SLOT:(pair + 1) * HEAD_SLOT] = packed.astype(BF16)


def _attention(q, k, v):
    n_q = SEQ // TQ
    n_groups = MLA_HEADS // ATTN_HEADS
    ctx_block0 = T_LAT // CTX_LEN
    wide = ATTN_HEADS * HEAD_SLOT
    lat = lambda b, h, i: (b, h)
    ctx = lambda b, h, i: (ctx_block0 + b, h)
    return pl.pallas_call(
        _attn_kernel,
        grid=(BATCH, n_groups, n_q),
        in_specs=[
            pl.BlockSpec((TQ, wide), lambda b, h, i: (b * n_q + i, h)),
            pl.BlockSpec((SEQ, wide), lat),
            pl.BlockSpec((CTX_LEN, wide), ctx),
            pl.BlockSpec((SEQ, wide), lat),
            pl.BlockSpec((CTX_LEN, wide), ctx),
        ],
        out_specs=pl.BlockSpec((TQ, ATTN_HEADS * MLA_V), lambda b, h, i: (b * n_q + i, h)),
        out_shape=jax.ShapeDtypeStruct((T_LAT, MLA_HEADS * MLA_V), BF16),
        compiler_params=_params("arbitrary", "arbitrary", "arbitrary"),
        name="mla_attention",
    )(q, k, k, v, v)


def _mlaout_kernel(x_ref, o_in_ref, mod_ref, w_ref, g_ref, b_ref, o_ref):
    y = _dot(o_in_ref[...], w_ref[...])
    o_ref[...] = _layernorm(ALPHA * x_ref[...] + mod_ref[0, 0] * y, g_ref[...], b_ref[...])


def _mlaout(x, o, mod, w_out, g, b):
    tm = TM_PROJ
    row = lambda i: (i, 0)
    return pl.pallas_call(
        _mlaout_kernel,
        grid=(T_LAT // tm,),
        in_specs=[
            pl.BlockSpec((tm, D_MODEL), row),
            pl.BlockSpec((tm, MLA_HEADS * MLA_V), row),
            pl.BlockSpec((1, 1, 1, D_MODEL), lambda i: (_group_of_tile(i, tm), 0, 0, 0)),
            _resident((MLA_HEADS * MLA_V, D_MODEL)),
            _resident((1, D_MODEL)),
            _resident((1, D_MODEL)),
        ],
        out_specs=pl.BlockSpec((tm, D_MODEL), row),
        out_shape=jax.ShapeDtypeStruct((T_LAT, D_MODEL), F32),
        compiler_params=_params("arbitrary"),
        name="mla_out_proj",
    )(x, o, mod, w_out.astype(BF16), g.reshape(1, D_MODEL), b.reshape(1, D_MODEL))


def kernel(x, c, ctx, c_ctx, ada_w, ada_b, ln_g, ln_b, ffn_w_in, ffn_w_out, ev_w_in, ev_w_out, hy_conv_w, hy_conv_b, hy_filt_w1, hy_filt_b1, hy_filt_w2, hy_filt_b2, hy_filt_w3, hy_sin_freq, hy_bias, rg_conv_w, rg_conv_b, rg_a_w, rg_a_b, rg_x_w, rg_x_b, rg_lambda, mla_w_in, mla_q_norm, mla_kv_norm, mla_w_q_up, mla_w_kv_up, mla_w_out):
    assert x.shape == (BATCH, SEQ, D_MODEL) and ctx.shape == (BATCH, CTX_LEN, D_MODEL)
    xs = jnp.concatenate([x.reshape(T_LAT, D_MODEL), ctx.reshape(T_CTX, D_MODEL)], axis=0)

    cond = jnp.concatenate([c, c_ctx[None], jnp.zeros((2 * SUBLANES - N_GROUPS, D_MODEL), F32)], axis=0)
    mods = _adaln(cond, ada_w, ada_b)[:, :N_GROUPS].reshape(DEPTH, N_GROUPS, N_ADA, 1, D_MODEL)

    m = mods[0]
    xs = _ffn(xs, m[:, 0:3], ffn_w_in[0, 0], ffn_w_out[0, 0], ln_g[0, 0], ln_b[0, 0], T_ALL)
    hy, rx, gate_in = _evproj(xs, m[:, 3:5], ev_w_in[0])
    h_sum = _rglru(rx, rg_conv_w[0], rg_conv_b[0], rg_a_w[0], rg_a_b[0], rg_x_w[0], rg_x_b[0], rg_lambda[0])
    filt = (hy_filt_w1[0], hy_filt_b1[0], hy_filt_w2[0], hy_filt_b2[0], hy_filt_w3[0], hy_sin_freq[0])
    tab_l = _dft_tables(SEQ)
    tab_c = _dft_tables(CTX_LEN)
    spec_l = _hyena_filter_spectrum(SEQ, *tab_l, *filt)
    spec_c = _hyena_filter_spectrum(CTX_LEN, *tab_c, *filt)
    y_hy = _hyena(hy, SEQ, 0, tab_l, spec_l, hy_conv_w[0], hy_conv_b[0], hy_bias[0])
    y_hy = _hyena(hy, CTX_LEN, T_LAT // CTX_LEN, tab_c, spec_c, hy_conv_w[0], hy_conv_b[0], hy_bias[0],
                  prev_out=y_hy)
    xs = _evout(xs, y_hy, h_sum, gate_in, m[:, 5:6], ev_w_out[0], ln_g[0, 1], ln_b[0, 1])
    xs = _ffn(xs, m[:, 6:9], ffn_w_in[0, 1], ffn_w_out[0, 1], ln_g[0, 2], ln_b[0, 2], T_ALL)

    m = mods[1]
    xs = _ffn(xs, m[:, 0:3], ffn_w_in[1, 0], ffn_w_out[1, 0], ln_g[1, 0], ln_b[1, 0], T_ALL)
    q, k, v = _mlaproj(xs, m[:, 3:5], mla_w_in[0], mla_q_norm[0], mla_kv_norm[0], mla_w_q_up[0], mla_w_kv_up[0])
    o = _attention(q, k, v)
    xl = _mlaout(xs, o, m[:, 5:6], mla_w_out[0], ln_g[1, 1], ln_b[1, 1])
    xl = _ffn(xl, m[:, 6:9], ffn_w_in[1, 1], ffn_w_out[1, 1], ln_g[1, 2], ln_b[1, 2], T_LAT)
    return xl.reshape(BATCH, SEQ, D_MODEL)
```

```python
import functools
import math

import jax
import jax.numpy as jnp
from jax import lax
from jax.experimental import pallas as pl
from jax.experimental.pallas import tpu as pltpu

F32 = jnp.float32
BF16 = jnp.bfloat16

D_MODEL = 1024
BATCH = 8
SEQ = 2048
DEPTH = 2
CTX_LEN = 256
GRID_W = 64
N_ADA = 9
D_FF = 2816

HY_WIDTH = 512
RG_WIDTH = 512
EV_IN = 3 * HY_WIDTH + 2 * RG_WIDTH
HY_EMB = 33
HY_BANDS = (HY_EMB - 1) // 2
HY_FILT_HIDDEN = 64
HY_TARGET = 1e-2
HY_MAX_DECAY = math.log(HY_TARGET) / 0.3
HY_MIN_DECAY = math.log(HY_TARGET) / 1.5
HY_SHIFT = 0.05
RG_BLOCKS = 8
RG_BLOCK_DIM = RG_WIDTH // RG_BLOCKS
RG_C = 8.0

MLA_HEADS = 16
MLA_Q_LORA = 768
MLA_KV_LORA = 256
MLA_NOPE = 64
MLA_ROPE = 32
MLA_V = 64
MLA_SCALE = (MLA_NOPE + MLA_ROPE) ** -0.5
ROPE_AXIS_PAIRS = MLA_ROPE // 4
ROPE_BASE = 10000.0

ALPHA = (2.0 * DEPTH) ** 0.25
LN_EPS = 1e-6
RMS_EPS = 1e-6

T_LAT = BATCH * SEQ
T_CTX = BATCH * CTX_LEN
T_ALL = T_LAT + T_CTX
N_GROUPS = BATCH + 1

LANES = 128
SUBLANES = 8
VMEM_LIMIT = 56 * 1024 * 1024

TM_FFN = 512
FF_CHUNK = 256
N_FF_CHUNKS = D_FF // FF_CHUNK
TM_PROJ = 512
HY_CT = 256
HY_FREQ_TILE = 512
RG_CT = 128
RG_TC = 32
RG_PAD_LO = 2
RG_PAD_HI = 6
HEAD_SLOT = 128
TQ = 512
ATTN_HEADS = 4
Q_SCALE = MLA_SCALE * math.log2(math.e)
ADA_TN = 1152


def _dot(a, b):
    return jnp.dot(a, b, preferred_element_type=F32)


def _dot_nt(a, b):
    return lax.dot_general(a, b, (((1,), (1,)), ((), ())), preferred_element_type=F32)


def _resident(shape):
    nd = len(shape)
    return pl.BlockSpec(shape, lambda *_: (0,) * nd, pipeline_mode=pl.Buffered(1))


def _group_of_tile(i, tm):
    return jnp.where(i < T_LAT // tm, i // (SEQ // tm), BATCH)


def _params(*sem):
    return pltpu.CompilerParams(dimension_semantics=sem, vmem_limit_bytes=VMEM_LIMIT)


def _layernorm(z, g, b):
    mu = jnp.mean(z, axis=-1, keepdims=True)
    zc = z - mu
    var = jnp.mean(zc * zc, axis=-1, keepdims=True)
    return zc * lax.rsqrt(var + LN_EPS) * g + b


def _adaln_kernel(c_ref, w_ref, b_ref, o_ref):
    c = c_ref[...]
    s = (c * jax.nn.sigmoid(c)).astype(BF16)
    o_ref[0] = _dot(s, w_ref[0].astype(BF16)) + b_ref[0]


def _adaln(cond, ada_w, ada_b):
    n_out = N_ADA * D_MODEL
    rows = cond.shape[0]
    return pl.pallas_call(
        _adaln_kernel,
        grid=(DEPTH, n_out // ADA_TN),
        in_specs=[
            pl.BlockSpec((rows, D_MODEL), lambda l, j: (0, 0)),
            pl.BlockSpec((1, D_MODEL, ADA_TN), lambda l, j: (l, 0, j)),
            pl.BlockSpec((1, 1, ADA_TN), lambda l, j: (l, 0, j)),
        ],
        out_specs=pl.BlockSpec((1, rows, ADA_TN), lambda l, j: (l, 0, j)),
        out_shape=jax.ShapeDtypeStruct((DEPTH, rows, n_out), F32),
        compiler_params=_params("arbitrary", "arbitrary"),
        name="adaln",
    )(cond, ada_w, ada_b.reshape(DEPTH, 1, n_out))


def _ffn_body(x, mod_ref, wi_ref, wo_ref, g_ref, b_ref, o_ref, act_ref):
    shift, scale, gate = mod_ref[0, 0], mod_ref[0, 1], mod_ref[0, 2]
    h = (x * (1.0 + scale) + shift).astype(BF16)
    for j in range(N_FF_CHUNKS):
        lo = j * FF_CHUNK
        a = _dot(h, wi_ref[:, lo:lo + FF_CHUNK])
        u = _dot(h, wi_ref[:, D_FF + lo:D_FF + lo + FF_CHUNK])
        act_ref[:, lo:lo + FF_CHUNK] = (a * jax.nn.sigmoid(a) * u).astype(BF16)
    y = _dot(act_ref[...], wo_ref[...])
    o_ref[...] = _layernorm(ALPHA * x + (0.5 * gate) * y, g_ref[...], b_ref[...])


def _ffn_kernel(x_ref, *rest):
    _ffn_body(x_ref[...], *rest)


def _ffn_split_kernel(xl_ref, xc_ref, *rest):
    x = jnp.where(pl.program_id(0) < T_LAT // TM_FFN, xl_ref[...], xc_ref[...])
    _ffn_body(x, *rest)


def _ffn(xs, mod, w_in, w_out, g, b, n_rows):
    tm = TM_FFN
    if isinstance(xs, tuple):
        n_lat = T_LAT // tm
        body = _ffn_split_kernel
        x_specs = [pl.BlockSpec((tm, D_MODEL), lambda i: (jnp.minimum(i, n_lat - 1), 0)),
                   pl.BlockSpec((tm, D_MODEL), lambda i: (jnp.maximum(i - n_lat, 0), 0))]
    else:
        body = _ffn_kernel
        x_specs = [pl.BlockSpec((tm, D_MODEL), lambda i: (i, 0))]
        xs = (xs,)
    return pl.pallas_call(
        body,
        grid=(n_rows // tm,),
        in_specs=[
            *x_specs,
            pl.BlockSpec((1, 3, 1, D_MODEL), lambda i: (_group_of_tile(i, tm), 0, 0, 0)),
            _resident((D_MODEL, 2 * D_FF)),
            _resident((D_FF, D_MODEL)),
            _resident((1, D_MODEL)),
            _resident((1, D_MODEL)),
        ],
        out_specs=pl.BlockSpec((tm, D_MODEL), lambda i: (i, 0)),
        out_shape=jax.ShapeDtypeStruct((n_rows, D_MODEL), F32),
        scratch_shapes=[pltpu.VMEM((tm, D_FF), BF16)],
        compiler_params=_params("arbitrary"),
        name="half_ffn",
    )(*xs, mod, w_in.astype(BF16), w_out.astype(BF16), g.reshape(1, D_MODEL), b.reshape(1, D_MODEL))


def _evproj_kernel(x_ref, mod_ref, w_ref, hy_ref, rx_ref, gt_ref):
    shift, scale = mod_ref[0, 0], mod_ref[0, 1]
    h = (x_ref[...] * (1.0 + scale) + shift).astype(BF16)
    n_hy = 3 * HY_WIDTH
    hy_ref[...] = _dot(h, w_ref[:, :n_hy])
    rx_ref[...] = _dot(h, w_ref[:, n_hy:n_hy + RG_WIDTH])
    gt_ref[...] = _dot(h, w_ref[:, n_hy + RG_WIDTH:])


def _evproj(x, mod, w_in):
    tm = TM_PROJ
    row = lambda i: (i, 0)
    return pl.pallas_call(
        _evproj_kernel,
        grid=(T_ALL // tm,),
        in_specs=[
            pl.BlockSpec((tm, D_MODEL), row),
            pl.BlockSpec((1, 2, 1, D_MODEL), lambda i: (_group_of_tile(i, tm), 0, 0, 0)),
            _resident((D_MODEL, EV_IN)),
        ],
        out_specs=[
            pl.BlockSpec((tm, 3 * HY_WIDTH), row),
            pl.BlockSpec((tm, RG_WIDTH), row),
            pl.BlockSpec((tm, RG_WIDTH), row),
        ],
        out_shape=[
            jax.ShapeDtypeStruct((T_ALL, 3 * HY_WIDTH), F32),
            jax.ShapeDtypeStruct((T_ALL, RG_WIDTH), F32),
            jax.ShapeDtypeStruct((T_ALL, RG_WIDTH), F32),
        ],
        compiler_params=_params("arbitrary"),
        name="even_in_proj",
    )(x, mod, w_in.astype(BF16))


def _rg_kernel(rxl_ref, rxc_ref, cw_ref, cb_ref, w_ref, bias_ref, lam_ref, hl_ref, hc_ref,
               af_ref, bf_ref, ab_ref, bb_ref):
    neg_lam = -lam_ref[...]
    softplus = jnp.maximum(neg_lam, 0.0) + jnp.log1p(jnp.exp(-jnp.abs(neg_lam)))
    cb = cb_ref[...]

    def coeffs(src_ref, t0, d, a_ref, b_ref):
        xc = cb + src_ref[pl.ds(t0, RG_TC)] * cw_ref[0:1, :]
        for k in range(1, 4):
            xc = xc + src_ref[pl.ds(t0 + k, RG_TC)] * cw_ref[k:k + 1, :]
        x2 = xc.reshape(RG_TC * SUBLANES, RG_CT)
        xb = x2.astype(BF16)
        r = jax.nn.sigmoid(_dot(xb, w_ref[0, 2 * d]) + bias_ref[2 * d:2 * d + 1, :])
        i = jax.nn.sigmoid(_dot(xb, w_ref[0, 2 * d + 1]) + bias_ref[2 * d + 1:2 * d + 2, :])
        log_a = (-RG_C) * r * softplus[d:d + 1, :]
        a = jnp.exp(log_a)
        b = jnp.sqrt(-jnp.tanh(log_a) * (a * a + 1.0)) * (i * x2)
        a_ref[...] = a.reshape(RG_TC, SUBLANES, RG_CT)
        b_ref[...] = b.reshape(RG_TC, SUBLANES, RG_CT)

    def sweep(src_ref, out_ref, length, h_fwd, h_bwd):
        n_chunks = length // RG_TC

        def body(j, carry):
            hf, hb = carry
            tf = j * RG_TC
            tb = (n_chunks - 1 - j) * RG_TC
            coeffs(src_ref, tf, 0, af_ref, bf_ref)
            coeffs(src_ref, tb, 1, ab_ref, bb_ref)
            for s in range(RG_TC):
                hf = af_ref[s] * hf + bf_ref[s]
                out_ref[tf + s] += hf
                sb = RG_TC - 1 - s
                hb = ab_ref[sb] * hb + bb_ref[sb]
                out_ref[tb + sb] += hb
            return hf, hb

        return lax.fori_loop(0, n_chunks, body, (h_fwd, h_bwd))

    hl_ref[...] = jnp.zeros_like(hl_ref)
    hc_ref[...] = jnp.zeros_like(hc_ref)
    zero = jnp.zeros((SUBLANES, RG_CT), F32)
    hf, hb = sweep(rxc_ref, hc_ref, CTX_LEN, zero, zero)
    sweep(rxl_ref, hl_ref, SEQ, hf, hb)


def _block_diag_tiles(w):
    per = RG_CT // RG_BLOCK_DIM
    w = w.reshape(RG_WIDTH // RG_CT, per, RG_BLOCK_DIM, RG_BLOCK_DIM)
    eye = jnp.eye(per, dtype=w.dtype)
    return jnp.einsum('cpde,pq->cpdqe', w, eye).reshape(RG_WIDTH // RG_CT, RG_CT, RG_CT)


def _rglru(rx, conv_w, conv_b, a_w, a_b, x_w, x_b, lam):
    def time_major(v, length):
        v = v.reshape(BATCH, length, RG_WIDTH).transpose(1, 0, 2)
        return jnp.pad(v, ((RG_PAD_LO, RG_PAD_HI), (0, 0), (0, 0)))

    rxl = time_major(rx[:T_LAT], SEQ)
    rxc = time_major(rx[T_LAT:], CTX_LEN)
    w = jnp.stack([_block_diag_tiles(a_w[0]), _block_diag_tiles(x_w[0]),
                   _block_diag_tiles(a_w[1]), _block_diag_tiles(x_w[1])], axis=1).astype(BF16)
    bias = jnp.stack([a_b[0], x_b[0], a_b[1], x_b[1]], axis=0)
    n_ct = RG_WIDTH // RG_CT
    chan = lambda c: (0, 0, c)
    lane = lambda c: (0, c)
    coef = pltpu.VMEM((RG_TC, SUBLANES, RG_CT), F32)
    hl, hc = pl.pallas_call(
        _rg_kernel,
        grid=(n_ct,),
        in_specs=[
            pl.BlockSpec((SEQ + RG_PAD_LO + RG_PAD_HI, BATCH, RG_CT), chan),
            pl.BlockSpec((CTX_LEN + RG_PAD_LO + RG_PAD_HI, BATCH, RG_CT), chan),
            pl.BlockSpec((4, RG_CT), lane),
            pl.BlockSpec((1, RG_CT), lane),
            pl.BlockSpec((1, 4, RG_CT, RG_CT), lambda c: (c, 0, 0, 0)),
            pl.BlockSpec((4, RG_CT), lane),
            pl.BlockSpec((2, RG_CT), lane),
        ],
        out_specs=[
            pl.BlockSpec((SEQ, BATCH, RG_CT), chan),
            pl.BlockSpec((CTX_LEN, BATCH, RG_CT), chan),
        ],
        out_shape=[
            jax.ShapeDtypeStruct((SEQ, BATCH, RG_WIDTH), F32),
            jax.ShapeDtypeStruct((CTX_LEN, BATCH, RG_WIDTH), F32),
        ],
        scratch_shapes=[coef, coef, coef, coef],
        compiler_params=_params("arbitrary"),
        name="rglru_scan",
    )(rxl, rxc, conv_w, conv_b.reshape(1, RG_WIDTH), w, bias, lam)
    return jnp.concatenate([hl.transpose(1, 0, 2).reshape(T_LAT, RG_WIDTH),
                            hc.transpose(1, 0, 2).reshape(T_CTX, RG_WIDTH)], axis=0)


def _dft_tables(length):
    step = 1 << (length.bit_length() // 2)
    s = jnp.arange(length, dtype=jnp.int32)

    def thin(kvec):
        ang = ((kvec[:, None] * s[None, :]) % (2 * length)).astype(F32) * (math.pi / length)
        return jnp.cos(ang)[:, None, :], jnp.sin(ang)[:, None, :]

    c_hi, s_hi = thin(jnp.arange(0, length, step, dtype=jnp.int32))
    c_lo, s_lo = thin(jnp.arange(step, dtype=jnp.int32))
    c_lo, s_lo = c_lo.reshape(1, step, length), s_lo.reshape(1, step, length)
    fc = (c_hi * c_lo - s_hi * s_lo).reshape(length, length)
    fs = (s_hi * c_lo + c_hi * s_lo).reshape(length, length)
    return fc.astype(BF16), fs.astype(BF16)


def _filter_features(length):
    pos = jnp.arange(length, dtype=F32)
    t = pos / length
    bands = jnp.linspace(1e-4, HY_BANDS - 1, HY_BANDS, dtype=F32)
    ang = (2.0 * math.pi * pos / length)[:, None] * bands[None, :]
    feats = jnp.concatenate([t[:, None], jnp.cos(ang), -jnp.sin(ang)], axis=-1)
    feats = jnp.pad(feats, ((0, 0), (0, LANES - HY_EMB)))
    deltas = jnp.abs(jnp.linspace(HY_MIN_DECAY, HY_MAX_DECAY, HY_WIDTH, dtype=F32))
    window = jnp.exp(-t[:, None] * deltas[None, :]) + HY_SHIFT
    return feats, window


def _filter_kernel(feat_ref, w1_ref, b1_ref, w2_ref, b2_ref, w3f_ref, w3b_ref, sf_ref, win_ref,
                   fc_ref, fs_ref, kc_ref, ks_ref, kn_ref, *, length):
    n = 2 * length
    h = jnp.sin(sf_ref[0:1, :] * (_dot(feat_ref[...].astype(BF16), w1_ref[...]) + b1_ref[...]))
    h = jnp.sin(sf_ref[1:2, :] * (_dot(h.astype(BF16), w2_ref[...]) + b2_ref[...]))
    hb16 = h.astype(BF16)
    win = win_ref[...]
    row = lax.broadcasted_iota(jnp.int32, (length, HY_CT), 0)
    h_fwd = _dot(hb16, w3f_ref[...]) * win
    h_bwd0 = jnp.where(row == 0, 0.0, _dot(hb16, w3b_ref[...]) * win)
    h_sum = h_fwd + h_bwd0
    h_dif = h_fwd - h_bwd0
    weight = jnp.where(row == 0, 1.0 / n, 2.0 / n)
    kc_ref[...] = _dot(fc_ref[...], h_sum.astype(BF16)) * weight
    ks_ref[...] = _dot(fs_ref[...], h_dif.astype(BF16)) * weight
    sign = jnp.where((row & 1) == 0, 1.0, -1.0)
    kn_ref[...] = jnp.sum(h_sum * sign, axis=0, keepdims=True) * (1.0 / n)


def _hyena_filter_spectrum(length, fc, fs, w1, b1, w2, b2, w3, sin_freq):
    feats, window = _filter_features(length)
    hp = LANES - HY_FILT_HIDDEN
    w1p = jnp.pad(w1, ((0, LANES - HY_EMB), (0, hp))).astype(BF16)
    w2p = jnp.pad(w2, ((0, hp), (0, hp))).astype(BF16)
    w3p = jnp.pad(w3, ((0, hp), (0, 0))).astype(BF16)
    b1p = jnp.pad(b1, (0, hp)).reshape(1, LANES)
    b2p = jnp.pad(b2, (0, hp)).reshape(1, LANES)
    sfp = jnp.pad(sin_freq, ((0, 0), (0, hp)))
    spec = jax.ShapeDtypeStruct((length, HY_WIDTH), F32)
    chan = lambda c: (0, c)
    return pl.pallas_call(
        functools.partial(_filter_kernel, length=length),
        grid=(HY_WIDTH // HY_CT,),
        in_specs=[
            _resident((length, LANES)),
            _resident((LANES, LANES)),
            _resident((1, LANES)),
            _resident((LANES, LANES)),
            _resident((1, LANES)),
            pl.BlockSpec((LANES, HY_CT), chan),
            pl.BlockSpec((LANES, HY_CT), chan),
            _resident((2, LANES)),
            pl.BlockSpec((length, HY_CT), chan),
            _resident((length, length)),
            _resident((length, length)),
        ],
        out_specs=[
            pl.BlockSpec((length, HY_CT), chan),
            pl.BlockSpec((length, HY_CT), chan),
            pl.BlockSpec((1, HY_CT), chan),
        ],
        out_shape=[spec, spec, jax.ShapeDtypeStruct((1, HY_WIDTH), F32)],
        compiler_params=_params("arbitrary"),
        name=f"hyena_filter_{length}",
    )(feats, w1p, b1p, w2p, b2p, w3p[:, :HY_WIDTH], w3p[:, HY_WIDTH:], sfp, window, fc, fs)


def _hyena_sequence(raw, cw_ref, cb_ref, bias_ref, fc_ref, fs_ref, kc_ref, ks_ref, kn_ref, length):
    row = lax.broadcasted_iota(jnp.int32, (length, HY_CT), 0)

    def short_conv(x, part):
        prev = jnp.where(row == 0, 0.0, pltpu.roll(x, 1, 0))
        nxt = jnp.where(row == length - 1, 0.0, pltpu.roll(x, length - 1, 0))
        return (prev * cw_ref[part, 0:1, :] + x * cw_ref[part, 1:2, :] + nxt * cw_ref[part, 2:3, :]
                + cb_ref[part:part + 1, :])

    x0 = short_conv(raw[0], 0)
    x1 = short_conv(raw[1], 1)
    v = short_conv(raw[2], 2)
    vx = v * x1
    vb = vx.astype(BF16)
    ft = min(length, HY_FREQ_TILE)
    y = None
    for f0 in range(0, length, ft):
        spec_c = _dot(fc_ref[f0:f0 + ft, :], vb)
        spec_s = _dot(fs_ref[f0:f0 + ft, :], vb)
        kc = kc_ref[f0:f0 + ft, :]
        ks = ks_ref[f0:f0 + ft, :]
        p = (spec_c * kc - spec_s * ks).astype(BF16)
        q = (spec_c * ks + spec_s * kc).astype(BF16)
        part = _dot(fc_ref[:, f0:f0 + ft], p) + _dot(fs_ref[:, f0:f0 + ft], q)
        y = part if y is None else y + part
    sign = jnp.where((row & 1) == 0, 1.0, -1.0)
    nyquist = jnp.sum(vx * sign, axis=0, keepdims=True) * kn_ref[...]
    return x0 * (y + sign * nyquist + vx * bias_ref[...])


def _hyena_kernel(x0_ref, x1_ref, v_ref, cw_ref, cb_ref, bias_ref,
                  fcl_ref, fsl_ref, kcl_ref, ksl_ref, knl_ref,
                  fcc_ref, fsc_ref, kcc_ref, ksc_ref, knc_ref, o_ref):
    step = pl.program_id(1)
    common = (cw_ref, cb_ref, bias_ref)

    @pl.when(step < BATCH)
    def _():
        raw = (x0_ref[...], x1_ref[...], v_ref[...])
        o_ref[...] = _hyena_sequence(raw, *common, fcl_ref, fsl_ref, kcl_ref, ksl_ref, knl_ref, SEQ)

    @pl.when(step == BATCH)
    def _():
        for r in range(BATCH):
            rows = slice(r * CTX_LEN, (r + 1) * CTX_LEN)
            raw = (x0_ref[rows, :], x1_ref[rows, :], v_ref[rows, :])
            o_ref[rows, :] = _hyena_sequence(raw, *common, fcc_ref, fsc_ref, kcc_ref, ksc_ref, knc_ref, CTX_LEN)


def _hyena(hy, tab_lat, spec_lat, tab_ctx, spec_ctx, conv_w, conv_b, bias):
    assert T_CTX == SEQ, "the context rows must form exactly one latent-sized row block"
    n_ct = HY_WIDTH // HY_CT
    cw = conv_w.reshape(3, 3, HY_WIDTH).transpose(1, 0, 2)
    cb = conv_b.reshape(3, HY_WIDTH)

    def part_spec(part):
        return pl.BlockSpec((SEQ, HY_CT), lambda c, b: (b, part * n_ct + c))

    chan = lambda c, b: (0, c)

    def table_specs(length):
        return [
            _resident((length, length)),
            _resident((length, length)),
            pl.BlockSpec((length, HY_CT), chan, pipeline_mode=pl.Buffered(1)),
            pl.BlockSpec((length, HY_CT), chan, pipeline_mode=pl.Buffered(1)),
            pl.BlockSpec((1, HY_CT), chan),
        ]

    return pl.pallas_call(
        _hyena_kernel,
        grid=(n_ct, BATCH + 1),
        in_specs=[
            part_spec(0), part_spec(1), part_spec(2),
            pl.BlockSpec((3, 3, HY_CT), lambda c, b: (0, 0, c)),
            pl.BlockSpec((3, HY_CT), chan),
            pl.BlockSpec((1, HY_CT), chan),
            *table_specs(SEQ),
            *table_specs(CTX_LEN),
        ],
        out_specs=pl.BlockSpec((SEQ, HY_CT), lambda c, b: (b, c)),
        out_shape=jax.ShapeDtypeStruct((T_ALL, HY_WIDTH), F32),
        compiler_params=_params("arbitrary", "arbitrary"),
        name="hyena_conv",
    )(hy, hy, hy, cw, cb, bias.reshape(1, HY_WIDTH), *tab_lat, *spec_lat, *tab_ctx, *spec_ctx)


def _evout_kernel(x_ref, yh_ref, hs_ref, gt_ref, mod_ref, w_ref, g_ref, b_ref, o_ref):
    x = x_ref[...]
    y_rg = hs_ref[...] * jax.nn.gelu(gt_ref[...], approximate=True)
    y = (_dot(yh_ref[...].astype(BF16), w_ref[:HY_WIDTH, :])
         + _dot(y_rg.astype(BF16), w_ref[HY_WIDTH:, :]))
    o_ref[...] = _layernorm(ALPHA * x + mod_ref[0, 0] * y, g_ref[...], b_ref[...])


def _evout(x, y_hy, h_sum, gate_in, mod, w_out, g, b):
    tm = TM_PROJ
    row = lambda i: (i, 0)
    return pl.pallas_call(
        _evout_kernel,
        grid=(T_ALL // tm,),
        in_specs=[
            pl.BlockSpec((tm, D_MODEL), row),
            pl.BlockSpec((tm, HY_WIDTH), row),
            pl.BlockSpec((tm, RG_WIDTH), row),
            pl.BlockSpec((tm, RG_WIDTH), row),
            pl.BlockSpec((1, 1, 1, D_MODEL), lambda i: (_group_of_tile(i, tm), 0, 0, 0)),
            _resident((HY_WIDTH + RG_WIDTH, D_MODEL)),
            _resident((1, D_MODEL)),
            _resident((1, D_MODEL)),
        ],
        out_specs=pl.BlockSpec((tm, D_MODEL), row),
        out_shape=jax.ShapeDtypeStruct((T_ALL, D_MODEL), F32),
        compiler_params=_params("arbitrary"),
        name="even_out_proj",
    )(x, y_hy, h_sum, gate_in, mod, w_out.astype(BF16), g.reshape(1, D_MODEL), b.reshape(1, D_MODEL))


def _rope_tables():
    rows = jnp.repeat(jnp.arange(SEQ // GRID_W, dtype=F32), GRID_W)
    cols = jnp.tile(jnp.arange(GRID_W, dtype=F32), SEQ // GRID_W)
    inv_freq = ROPE_BASE ** (-jnp.arange(ROPE_AXIS_PAIRS, dtype=F32) / ROPE_AXIS_PAIRS)
    ang_r = rows[:, None] * inv_freq
    ang_c = cols[:, None] * inv_freq
    cos32 = jnp.concatenate([jnp.cos(ang_r)] * 2 + [jnp.cos(ang_c)] * 2, axis=1)
    sin32 = jnp.concatenate([-jnp.sin(ang_r), jnp.sin(ang_r), -jnp.sin(ang_c), jnp.sin(ang_c)], axis=1)
    ones = jnp.ones((SEQ, MLA_NOPE), F32)
    zeros = jnp.zeros((SEQ, MLA_NOPE), F32)
    pad = jnp.zeros((SEQ, HEAD_SLOT - MLA_NOPE - MLA_ROPE), F32)
    first = (jnp.arange(MLA_ROPE) % (2 * ROPE_AXIS_PAIRS)) < ROPE_AXIS_PAIRS
    q_cos = jnp.concatenate([ones, cos32, pad], axis=1) * Q_SCALE
    q_sin_up = jnp.concatenate([zeros, jnp.where(first, sin32, 0.0), pad], axis=1) * Q_SCALE
    q_sin_dn = jnp.concatenate([zeros, jnp.where(first, 0.0, sin32), pad], axis=1) * Q_SCALE
    k_cos = jnp.concatenate([zeros, cos32, pad], axis=1)
    k_sin = jnp.concatenate([zeros, sin32, pad], axis=1)
    ident = jnp.zeros((TM_PROJ, HEAD_SLOT), F32).at[:, MLA_NOPE:MLA_NOPE + MLA_ROPE].set(1.0)
    zero = jnp.zeros((TM_PROJ, HEAD_SLOT), F32)
    return (jnp.concatenate([q_cos, zero], 0), jnp.concatenate([q_sin_up, zero], 0),
            jnp.concatenate([q_sin_dn, zero], 0),
            jnp.concatenate([k_cos, ident], 0), jnp.concatenate([k_sin, zero], 0))


def _rope_partner(n):
    idx = jnp.arange(n)
    return jnp.where((idx % 16) < ROPE_AXIS_PAIRS, idx + ROPE_AXIS_PAIRS, idx - ROPE_AXIS_PAIRS)


def _mla_weights(w_in, w_q_up, w_kv_up):
    w_q = w_in[:, :MLA_Q_LORA]
    w_kv = w_in[:, MLA_Q_LORA:MLA_Q_LORA + MLA_KV_LORA]
    w_kr = w_in[:, MLA_Q_LORA + MLA_KV_LORA:]
    part = _rope_partner(MLA_ROPE)
    lane_pad = ((0, 0), (MLA_NOPE, HEAD_SLOT - MLA_NOPE - MLA_ROPE))
    w_kr2 = jnp.concatenate([jnp.pad(w_kr, lane_pad), jnp.pad(w_kr[:, part], lane_pad)], axis=1)

    qh = w_q_up.reshape(MLA_Q_LORA, MLA_HEADS, MLA_NOPE + MLA_ROPE)
    slot_pad = ((0, 0), (0, 0), (0, HEAD_SLOT - MLA_NOPE - MLA_ROPE))
    wq_slot = jnp.pad(qh, slot_pad).reshape(MLA_Q_LORA, MLA_HEADS * HEAD_SLOT)

    kvh = w_kv_up.reshape(MLA_KV_LORA, MLA_HEADS, MLA_NOPE + MLA_V)
    wk_slot = jnp.pad(kvh[..., :MLA_NOPE], ((0, 0), (0, 0), (0, HEAD_SLOT - MLA_NOPE)))
    wk_slot = wk_slot.reshape(MLA_KV_LORA, MLA_HEADS * HEAD_SLOT)
    wv_slot = jnp.pad(kvh[..., MLA_NOPE:], ((0, 0), (0, 0), (0, HEAD_SLOT - MLA_V)))
    wv_slot = wv_slot.reshape(MLA_KV_LORA, MLA_HEADS * HEAD_SLOT)
    return [w.astype(BF16) for w in (w_q, w_kv, w_kr2, wq_slot, wk_slot, wv_slot)]


def _mlaproj_kernel(x_ref, mod_ref, wq_ref, wkv_ref, wkr_ref, wqs_ref, wks_ref, wv_ref,
                    vone_ref, qg_ref, kvg_ref, qcos_ref, qsup_ref, qsdn_ref, kcos_ref, ksin_ref,
                    q_ref, k_ref, v_ref):
    shift, scale = mod_ref[0, 0], mod_ref[0, 1]
    h = (x_ref[...] * (1.0 + scale) + shift).astype(BF16)

    def rmsnorm(y, g):
        return (y * lax.rsqrt(jnp.mean(y * y, axis=-1, keepdims=True) + RMS_EPS) * g).astype(BF16)

    qn = rmsnorm(_dot(h, wq_ref[...]), qg_ref[...])
    kvn = rmsnorm(_dot(h, wkv_ref[...]), kvg_ref[...])
    kr2 = _dot(h, wkr_ref[...])
    k_rope = kr2[:, :HEAD_SLOT] * kcos_ref[...] + kr2[:, HEAD_SLOT:] * ksin_ref[...]
    k_nope = _dot(kvn, wks_ref[...])
    v_ref[...] = (_dot(kvn, wv_ref[...]) + vone_ref[...]).astype(BF16)
    q_all = _dot(qn, wqs_ref[...])
    q_cos = qcos_ref[...]
    q_sin_up = qsup_ref[...]
    q_sin_dn = qsdn_ref[...]
    for hd in range(MLA_HEADS):
        sl = slice(hd * HEAD_SLOT, (hd + 1) * HEAD_SLOT)
        qh = q_all[:, sl]
        rotated = (qh * q_cos + pltpu.roll(qh, HEAD_SLOT - ROPE_AXIS_PAIRS, 1) * q_sin_up
                   + pltpu.roll(qh, ROPE_AXIS_PAIRS, 1) * q_sin_dn)
        q_ref[:, sl] = rotated.astype(BF16)
        k_ref[:, sl] = (k_nope[:, sl] + k_rope).astype(BF16)


def _mlaproj(x, mod, w_in, q_norm, kv_norm, w_q_up, w_kv_up):
    tm = TM_PROJ
    weights = _mla_weights(w_in, w_q_up, w_kv_up)
    tables = _rope_tables()
    row = lambda i: (i, 0)
    tab = lambda i: (jnp.where(i < T_LAT // tm, i % (SEQ // tm), SEQ // tm), 0)
    wide = MLA_HEADS * HEAD_SLOT
    v_one = (jnp.arange(wide) % HEAD_SLOT == MLA_V).astype(F32).reshape(1, wide)
    return pl.pallas_call(
        _mlaproj_kernel,
        grid=(T_ALL // tm,),
        in_specs=[
            pl.BlockSpec((tm, D_MODEL), row),
            pl.BlockSpec((1, 2, 1, D_MODEL), lambda i: (_group_of_tile(i, tm), 0, 0, 0)),
            *[_resident(w.shape) for w in weights],
            _resident((1, wide)),
            _resident((1, MLA_Q_LORA)),
            _resident((1, MLA_KV_LORA)),
            *[pl.BlockSpec((tm, HEAD_SLOT), tab) for _ in tables],
        ],
        out_specs=[pl.BlockSpec((tm, wide), row)] * 3,
        out_shape=[jax.ShapeDtypeStruct((T_ALL, wide), BF16)] * 3,
        compiler_params=_params("arbitrary"),
        name="mla_proj",
    )(x, mod, *weights, v_one, q_norm.reshape(1, MLA_Q_LORA), kv_norm.reshape(1, MLA_KV_LORA), *tables)


def _attn_kernel(q_ref, kl_ref, kc_ref, vl_ref, vc_ref, o_ref):
    lane = lax.broadcasted_iota(jnp.int32, (TQ, HEAD_SLOT), 1)
    outs = []
    for hd in range(ATTN_HEADS):
        sl = slice(hd * HEAD_SLOT, (hd + 1) * HEAD_SLOT)
        q = q_ref[:, sl]
        s_lat = _dot_nt(q, kl_ref[:, sl])
        s_ctx = _dot_nt(q, kc_ref[:, sl])
        m = jnp.maximum(jnp.max(s_lat, axis=-1, keepdims=True), jnp.max(s_ctx, axis=-1, keepdims=True))
        p_lat = jnp.exp2(s_lat - m).astype(BF16)
        p_ctx = jnp.exp2(s_ctx - m).astype(BF16)
        o = _dot(p_lat, vl_ref[:, sl]) + _dot(p_ctx, vc_ref[:, sl])
        outs.append(o / o[:, MLA_V:MLA_V + 1])
    for pair in range(ATTN_HEADS // 2):
        packed = jnp.where(lane < MLA_V, outs[2 * pair], pltpu.roll(outs[2 * pair + 1], MLA_V, 1))
        o_ref[:, pair * HEAD_SLOT:(pair + 1) * HEAD_SLOT] = packed.astype(BF16)


def _attention(q, k, v):
    n_q = SEQ // TQ
    n_groups = MLA_HEADS // ATTN_HEADS
    ctx_block0 = T_LAT // CTX_LEN
    wide = ATTN_HEADS * HEAD_SLOT
    lat = lambda b, h, i: (b, h)
    ctx = lambda b, h, i: (ctx_block0 + b, h)
    return pl.pallas_call(
        _attn_kernel,
        grid=(BATCH, n_groups, n_q),
        in_specs=[
            pl.BlockSpec((TQ, wide), lambda b, h, i: (b * n_q + i, h)),
            pl.BlockSpec((SEQ, wide), lat),
            pl.BlockSpec((CTX_LEN, wide), ctx),
            pl.BlockSpec((SEQ, wide), lat),
            pl.BlockSpec((CTX_LEN, wide), ctx),
        ],
        out_specs=pl.BlockSpec((TQ, ATTN_HEADS * MLA_V), lambda b, h, i: (b * n_q + i, h)),
        out_shape=jax.ShapeDtypeStruct((T_LAT, MLA_HEADS * MLA_V), BF16),
        compiler_params=_params("arbitrary", "arbitrary", "arbitrary"),
        name="mla_attention",
    )(q, k, k, v, v)


def _mlaout_kernel(x_ref, o_in_ref, mod_ref, w_ref, g_ref, b_ref, o_ref):
    y = _dot(o_in_ref[...], w_ref[...])
    o_ref[...] = _layernorm(ALPHA * x_ref[...] + mod_ref[0, 0] * y, g_ref[...], b_ref[...])


def _mlaout(x, o, mod, w_out, g, b):
    tm = TM_PROJ
    row = lambda i: (i, 0)
    return pl.pallas_call(
        _mlaout_kernel,
        grid=(T_LAT // tm,),
        in_specs=[
            pl.BlockSpec((tm, D_MODEL), row),
            pl.BlockSpec((tm, MLA_HEADS * MLA_V), row),
            pl.BlockSpec((1, 1, 1, D_MODEL), lambda i: (_group_of_tile(i, tm), 0, 0, 0)),
            _resident((MLA_HEADS * MLA_V, D_MODEL)),
            _resident((1, D_MODEL)),
            _resident((1, D_MODEL)),
        ],
        out_specs=pl.BlockSpec((tm, D_MODEL), row),
        out_shape=jax.ShapeDtypeStruct((T_LAT, D_MODEL), F32),
        compiler_params=_params("arbitrary"),
        name="mla_out_proj",
    )(x, o, mod, w_out.astype(BF16), g.reshape(1, D_MODEL), b.reshape(1, D_MODEL))


def kernel(x, c, ctx, c_ctx, ada_w, ada_b, ln_g, ln_b, ffn_w_in, ffn_w_out, ev_w_in, ev_w_out, hy_conv_w, hy_conv_b, hy_filt_w1, hy_filt_b1, hy_filt_w2, hy_filt_b2, hy_filt_w3, hy_sin_freq, hy_bias, rg_conv_w, rg_conv_b, rg_a_w, rg_a_b, rg_x_w, rg_x_b, rg_lambda, mla_w_in, mla_q_norm, mla_kv_norm, mla_w_q_up, mla_w_kv_up, mla_w_out):
    assert x.shape == (BATCH, SEQ, D_MODEL) and ctx.shape == (BATCH, CTX_LEN, D_MODEL)
    xs = (x.reshape(T_LAT, D_MODEL), ctx.reshape(T_CTX, D_MODEL))

    cond = jnp.concatenate([c, c_ctx[None], jnp.zeros((2 * SUBLANES - N_GROUPS, D_MODEL), F32)], axis=0)
    mods = _adaln(cond, ada_w, ada_b)[:, :N_GROUPS].reshape(DEPTH, N_GROUPS, N_ADA, 1, D_MODEL)

    m = mods[0]
    xs = _ffn(xs, m[:, 0:3], ffn_w_in[0, 0], ffn_w_out[0, 0], ln_g[0, 0], ln_b[0, 0], T_ALL)
    hy, rx, gate_in = _evproj(xs, m[:, 3:5], ev_w_in[0])
    h_sum = _rglru(rx, rg_conv_w[0], rg_conv_b[0], rg_a_w[0], rg_a_b[0], rg_x_w[0], rg_x_b[0], rg_lambda[0])
    filt = (hy_filt_w1[0], hy_filt_b1[0], hy_filt_w2[0], hy_filt_b2[0], hy_filt_w3[0], hy_sin_freq[0])
    tab_l = _dft_tables(SEQ)
    tab_c = _dft_tables(CTX_LEN)
    spec_l = _hyena_filter_spectrum(SEQ, *tab_l, *filt)
    spec_c = _hyena_filter_spectrum(CTX_LEN, *tab_c, *filt)
    y_hy = _hyena(hy, tab_l, spec_l, tab_c, spec_c, hy_conv_w[0], hy_conv_b[0], hy_bias[0])
    xs = _evout(xs, y_hy, h_sum, gate_in, m[:, 5:6], ev_w_out[0], ln_g[0, 1], ln_b[0, 1])
    xs = _ffn(xs, m[:, 6:9], ffn_w_in[0, 1], ffn_w_out[0, 1], ln_g[0, 2], ln_b[0, 2], T_ALL)

    m = mods[1]
    xs = _ffn(xs, m[:, 0:3], ffn_w_in[1, 0], ffn_w_out[1, 0], ln_g[1, 0], ln_b[1, 0], T_ALL)
    q, k, v = _mlaproj(xs, m[:, 3:5], mla_w_in[0], mla_q_norm[0], mla_kv_norm[0], mla_w_q_up[0], mla_w_kv_up[0])
    o = _attention(q, k, v)
    xl = _mlaout(xs, o, m[:, 5:6], mla_w_out[0], ln_g[1, 1], ln_b[1, 1])
    xl = _ffn(xl, m[:, 6:9], ffn_w_in[1, 1], ffn_w_out[1, 1], ln_g[1, 2], ln_b[1, 2], T_LAT)
    return xl.reshape(BATCH, SEQ, D_MODEL)
```

```python
import functools
import math

import jax
import jax.numpy as jnp
import numpy as np
from jax import lax
from jax.experimental import pallas as pl
from jax.experimental.pallas import tpu as pltpu

F32 = jnp.float32
BF16 = jnp.bfloat16

D_MODEL = 1024
BATCH = 8
SEQ = 2048
DEPTH = 2
CTX_LEN = 256
GRID_W = 64
N_ADA = 9
D_FF = 2816

HY_WIDTH = 512
RG_WIDTH = 512
EV_IN = 3 * HY_WIDTH + 2 * RG_WIDTH
HY_EMB = 33
HY_BANDS = (HY_EMB - 1) // 2
HY_FILT_HIDDEN = 64
HY_TARGET = 1e-2
HY_MAX_DECAY = math.log(HY_TARGET) / 0.3
HY_MIN_DECAY = math.log(HY_TARGET) / 1.5
HY_SHIFT = 0.05
RG_BLOCKS = 8
RG_BLOCK_DIM = RG_WIDTH // RG_BLOCKS
RG_C = 8.0

MLA_HEADS = 16
MLA_Q_LORA = 768
MLA_KV_LORA = 256
MLA_NOPE = 64
MLA_ROPE = 32
MLA_V = 64
MLA_SCALE = (MLA_NOPE + MLA_ROPE) ** -0.5
ROPE_AXIS_PAIRS = MLA_ROPE // 4
ROPE_BASE = 10000.0

ALPHA = (2.0 * DEPTH) ** 0.25
LN_EPS = 1e-6
RMS_EPS = 1e-6

T_LAT = BATCH * SEQ
T_CTX = BATCH * CTX_LEN
T_ALL = T_LAT + T_CTX
N_GROUPS = BATCH + 1

LANES = 128
SUBLANES = 8
VMEM_LIMIT = 56 * 1024 * 1024

TM_FFN = 512
FF_CHUNK = 256
N_FF_CHUNKS = D_FF // FF_CHUNK
TM_PROJ = 512
HY_CT = 256
HY_FREQ_TILE = 512
RG_CT = 128
RG_TC = 32
RG_PAD_LO = 2
RG_PAD_HI = 6
HEAD_SLOT = 128
TQ = 512
ATTN_HEADS = 8
Q_SCALE = MLA_SCALE * math.log2(math.e)
ADA_TN = 1152


def _dot(a, b):
    return jnp.dot(a, b, preferred_element_type=F32)


def _dot_nt(a, b):
    return lax.dot_general(a, b, (((1,), (1,)), ((), ())), preferred_element_type=F32)


def _resident(shape):
    nd = len(shape)
    return pl.BlockSpec(shape, lambda *_: (0,) * nd, pipeline_mode=pl.Buffered(1))


def _group_of_tile(i, tm):
    return jnp.where(i < T_LAT // tm, i // (SEQ // tm), BATCH)


def _params(*sem):
    return pltpu.CompilerParams(dimension_semantics=sem, vmem_limit_bytes=VMEM_LIMIT)


def _layernorm(z, g, b):
    mu = jnp.mean(z, axis=-1, keepdims=True)
    zc = z - mu
    var = jnp.mean(zc * zc, axis=-1, keepdims=True)
    return zc * lax.rsqrt(var + LN_EPS) * g + b


def _adaln_kernel(c_ref, w_ref, b_ref, o_ref):
    c = c_ref[...]
    s = (c * jax.nn.sigmoid(c)).astype(BF16)
    o_ref[0] = _dot(s, w_ref[0].astype(BF16)) + b_ref[0]


def _adaln(cond, ada_w, ada_b):
    n_out = N_ADA * D_MODEL
    rows = cond.shape[0]
    return pl.pallas_call(
        _adaln_kernel,
        grid=(DEPTH, n_out // ADA_TN),
        in_specs=[
            pl.BlockSpec((rows, D_MODEL), lambda l, j: (0, 0)),
            pl.BlockSpec((1, D_MODEL, ADA_TN), lambda l, j: (l, 0, j)),
            pl.BlockSpec((1, 1, ADA_TN), lambda l, j: (l, 0, j)),
        ],
        out_specs=pl.BlockSpec((1, rows, ADA_TN), lambda l, j: (l, 0, j)),
        out_shape=jax.ShapeDtypeStruct((DEPTH, rows, n_out), F32),
        compiler_params=_params("arbitrary", "arbitrary"),
        name="adaln",
    )(cond, ada_w, ada_b.reshape(DEPTH, 1, n_out))


def _ffn_body(x, mod_ref, wi_ref, wo_ref, g_ref, b_ref, o_ref, act_ref):
    n_slots = mod_ref.shape[1]
    shift, scale, gate = mod_ref[0, n_slots - 3], mod_ref[0, n_slots - 2], mod_ref[0, n_slots - 1]
    h = (x * (1.0 + scale) + shift).astype(BF16)
    for j in range(N_FF_CHUNKS):
        lo = j * FF_CHUNK
        a = _dot(h, wi_ref[:, lo:lo + FF_CHUNK])
        u = _dot(h, wi_ref[:, D_FF + lo:D_FF + lo + FF_CHUNK])
        act_ref[:, lo:lo + FF_CHUNK] = (a * jax.nn.sigmoid(a) * u).astype(BF16)
    y = _dot(act_ref[...], wo_ref[...])
    o_ref[...] = _layernorm(ALPHA * x + (0.5 * gate) * y, g_ref[...], b_ref[...])


def _ffn_kernel(x_ref, *rest):
    _ffn_body(x_ref[...], *rest)


def _ffn_split_kernel(xl_ref, xc_ref, *rest):
    x = jnp.where(pl.program_id(0) < T_LAT // TM_FFN, xl_ref[...], xc_ref[...])
    _ffn_body(x, *rest)


def _ffn_after_even_kernel(x_ref, yh_ref, hs_ref, gt_ref, wmix_ref, g1_ref, b1_ref, mod_ref, *rest):
    y_rg = hs_ref[...] * jax.nn.gelu(gt_ref[...], approximate=True)
    y = (_dot(yh_ref[...].astype(BF16), wmix_ref[:HY_WIDTH, :])
         + _dot(y_rg.astype(BF16), wmix_ref[HY_WIDTH:, :]))
    x = _layernorm(ALPHA * x_ref[...] + mod_ref[0, 0] * y, g1_ref[...], b1_ref[...])
    _ffn_body(x, mod_ref, *rest)


def _ffn_after_mla_kernel(x_ref, att_ref, wmix_ref, g1_ref, b1_ref, mod_ref, *rest):
    y = _dot(att_ref[...], wmix_ref[...])
    x = _layernorm(ALPHA * x_ref[...] + mod_ref[0, 0] * y, g1_ref[...], b1_ref[...])
    _ffn_body(x, mod_ref, *rest)


def _ffn(body, lead_args, lead_specs, mod, ffn_w, which, g, b, n_rows):
    tm = TM_FFN
    layer, half = which
    w_in_all, w_out_all = ffn_w
    return pl.pallas_call(
        body,
        grid=(n_rows // tm,),
        in_specs=[
            *lead_specs,
            pl.BlockSpec((1, mod.shape[1], 1, D_MODEL), lambda i: (_group_of_tile(i, tm), 0, 0, 0)),
            pl.BlockSpec((None, None, D_MODEL, 2 * D_FF), lambda i: (layer, half, 0, 0),
                         pipeline_mode=pl.Buffered(1)),
            pl.BlockSpec((None, None, D_FF, D_MODEL), lambda i: (layer, half, 0, 0),
                         pipeline_mode=pl.Buffered(1)),
            _resident((1, D_MODEL)),
            _resident((1, D_MODEL)),
        ],
        out_specs=pl.BlockSpec((tm, D_MODEL), lambda i: (i, 0)),
        out_shape=jax.ShapeDtypeStruct((n_rows, D_MODEL), F32),
        scratch_shapes=[pltpu.VMEM((tm, D_FF), BF16)],
        compiler_params=_params("arbitrary"),
        name="half_ffn",
    )(*lead_args, mod, w_in_all, w_out_all, g.reshape(1, D_MODEL), b.reshape(1, D_MODEL))


def _row_spec(width, tm=TM_FFN):
    return pl.BlockSpec((tm, width), lambda i: (i, 0))


def _ffn_plain(x, *args):
    return _ffn(_ffn_kernel, (x,), [_row_spec(D_MODEL)], *args)


def _ffn_split(x_lat, x_ctx, *args):
    n_lat = T_LAT // TM_FFN
    specs = [pl.BlockSpec((TM_FFN, D_MODEL), lambda i: (jnp.minimum(i, n_lat - 1), 0)),
             pl.BlockSpec((TM_FFN, D_MODEL), lambda i: (jnp.maximum(i - n_lat, 0), 0))]
    return _ffn(_ffn_split_kernel, (x_lat, x_ctx), specs, *args)


def _ffn_after_even(x, y_hy, h_sum, gate_in, w_mix, g1, b1, *args):
    lead = (x, y_hy, h_sum, gate_in, w_mix.astype(BF16), g1.reshape(1, D_MODEL), b1.reshape(1, D_MODEL))
    specs = [_row_spec(D_MODEL), _row_spec(HY_WIDTH), _row_spec(RG_WIDTH), _row_spec(RG_WIDTH),
             _resident((HY_WIDTH + RG_WIDTH, D_MODEL)), _resident((1, D_MODEL)), _resident((1, D_MODEL))]
    return _ffn(_ffn_after_even_kernel, lead, specs, *args)


def _ffn_after_mla(x, att, w_mix, g1, b1, *args):
    lead = (x, att, w_mix.astype(BF16), g1.reshape(1, D_MODEL), b1.reshape(1, D_MODEL))
    specs = [_row_spec(D_MODEL), _row_spec(MLA_HEADS * MLA_V),
             _resident((MLA_HEADS * MLA_V, D_MODEL)), _resident((1, D_MODEL)), _resident((1, D_MODEL))]
    return _ffn(_ffn_after_mla_kernel, lead, specs, *args)


def _evproj_kernel(x_ref, mod_ref, w_ref, hy_ref, rx_ref, gt_ref):
    shift, scale = mod_ref[0, 0], mod_ref[0, 1]
    h = (x_ref[...] * (1.0 + scale) + shift).astype(BF16)
    n_hy = 3 * HY_WIDTH
    hy_ref[...] = _dot(h, w_ref[:, :n_hy])
    rx_ref[...] = _dot(h, w_ref[:, n_hy:n_hy + RG_WIDTH])
    gt_ref[...] = _dot(h, w_ref[:, n_hy + RG_WIDTH:])


def _evproj(x, mod, w_in):
    tm = TM_PROJ
    row = lambda i: (i, 0)
    return pl.pallas_call(
        _evproj_kernel,
        grid=(T_ALL // tm,),
        in_specs=[
            pl.BlockSpec((tm, D_MODEL), row),
            pl.BlockSpec((1, 2, 1, D_MODEL), lambda i: (_group_of_tile(i, tm), 0, 0, 0)),
            _resident((D_MODEL, EV_IN)),
        ],
        out_specs=[
            pl.BlockSpec((tm, 3 * HY_WIDTH), row),
            pl.BlockSpec((tm, RG_WIDTH), row),
            pl.BlockSpec((tm, RG_WIDTH), row),
        ],
        out_shape=[
            jax.ShapeDtypeStruct((T_ALL, 3 * HY_WIDTH), F32),
            jax.ShapeDtypeStruct((T_ALL, RG_WIDTH), F32),
            jax.ShapeDtypeStruct((T_ALL, RG_WIDTH), F32),
        ],
        compiler_params=_params("arbitrary"),
        name="even_in_proj",
    )(x, mod, w_in.astype(BF16))


def _rg_kernel(rxl_ref, rxc_ref, cw_ref, cb_ref, w_ref, bias_ref, lam_ref, hl_ref, hc_ref,
               af_ref, bf_ref, ab_ref, bb_ref):
    neg_lam = -lam_ref[...]
    softplus = jnp.maximum(neg_lam, 0.0) + jnp.log1p(jnp.exp(-jnp.abs(neg_lam)))
    cb = cb_ref[...]

    def coeffs(src_ref, t0, d, a_ref, b_ref):
        xc = cb + src_ref[pl.ds(t0, RG_TC)] * cw_ref[0:1, :]
        for k in range(1, 4):
            xc = xc + src_ref[pl.ds(t0 + k, RG_TC)] * cw_ref[k:k + 1, :]
        x2 = xc.reshape(RG_TC * SUBLANES, RG_CT)
        xb = x2.astype(BF16)
        r = jax.nn.sigmoid(_dot(xb, w_ref[0, 2 * d]) + bias_ref[2 * d:2 * d + 1, :])
        i = jax.nn.sigmoid(_dot(xb, w_ref[0, 2 * d + 1]) + bias_ref[2 * d + 1:2 * d + 2, :])
        log_a = (-RG_C) * r * softplus[d:d + 1, :]
        a = jnp.exp(log_a)
        b = jnp.sqrt(-jnp.tanh(log_a) * (a * a + 1.0)) * (i * x2)
        a_ref[...] = a.reshape(RG_TC, SUBLANES, RG_CT)
        b_ref[...] = b.reshape(RG_TC, SUBLANES, RG_CT)

    def sweep(src_ref, out_ref, length, h_fwd, h_bwd):
        n_chunks = length // RG_TC

        def body(j, carry):
            hf, hb = carry
            tf = j * RG_TC
            tb = (n_chunks - 1 - j) * RG_TC
            coeffs(src_ref, tf, 0, af_ref, bf_ref)
            coeffs(src_ref, tb, 1, ab_ref, bb_ref)
            for s in range(RG_TC):
                hf = af_ref[s] * hf + bf_ref[s]
                out_ref[tf + s] += hf
                sb = RG_TC - 1 - s
                hb = ab_ref[sb] * hb + bb_ref[sb]
                out_ref[tb + sb] += hb
            return hf, hb

        return lax.fori_loop(0, n_chunks, body, (h_fwd, h_bwd))

    hl_ref[...] = jnp.zeros_like(hl_ref)
    hc_ref[...] = jnp.zeros_like(hc_ref)
    zero = jnp.zeros((SUBLANES, RG_CT), F32)
    hf, hb = sweep(rxc_ref, hc_ref, CTX_LEN, zero, zero)
    sweep(rxl_ref, hl_ref, SEQ, hf, hb)


def _block_diag_tiles(w):
    per = RG_CT // RG_BLOCK_DIM
    w = w.reshape(RG_WIDTH // RG_CT, per, RG_BLOCK_DIM, RG_BLOCK_DIM)
    eye = jnp.eye(per, dtype=w.dtype)
    return jnp.einsum('cpde,pq->cpdqe', w, eye).reshape(RG_WIDTH // RG_CT, RG_CT, RG_CT)


def _rglru(rx, conv_w, conv_b, a_w, a_b, x_w, x_b, lam):
    def time_major(v, length):
        v = v.reshape(BATCH, length, RG_WIDTH).transpose(1, 0, 2)
        return jnp.pad(v, ((RG_PAD_LO, RG_PAD_HI), (0, 0), (0, 0)))

    rxl = time_major(rx[:T_LAT], SEQ)
    rxc = time_major(rx[T_LAT:], CTX_LEN)
    w = jnp.stack([_block_diag_tiles(a_w[0]), _block_diag_tiles(x_w[0]),
                   _block_diag_tiles(a_w[1]), _block_diag_tiles(x_w[1])], axis=1).astype(BF16)
    bias = jnp.stack([a_b[0], x_b[0], a_b[1], x_b[1]], axis=0)
    n_ct = RG_WIDTH // RG_CT
    chan = lambda c: (0, 0, c)
    lane = lambda c: (0, c)
    coef = pltpu.VMEM((RG_TC, SUBLANES, RG_CT), F32)
    hl, hc = pl.pallas_call(
        _rg_kernel,
        grid=(n_ct,),
        in_specs=[
            pl.BlockSpec((SEQ + RG_PAD_LO + RG_PAD_HI, BATCH, RG_CT), chan),
            pl.BlockSpec((CTX_LEN + RG_PAD_LO + RG_PAD_HI, BATCH, RG_CT), chan),
            pl.BlockSpec((4, RG_CT), lane),
            pl.BlockSpec((1, RG_CT), lane),
            pl.BlockSpec((1, 4, RG_CT, RG_CT), lambda c: (c, 0, 0, 0)),
            pl.BlockSpec((4, RG_CT), lane),
            pl.BlockSpec((2, RG_CT), lane),
        ],
        out_specs=[
            pl.BlockSpec((SEQ, BATCH, RG_CT), chan),
            pl.BlockSpec((CTX_LEN, BATCH, RG_CT), chan),
        ],
        out_shape=[
            jax.ShapeDtypeStruct((SEQ, BATCH, RG_WIDTH), F32),
            jax.ShapeDtypeStruct((CTX_LEN, BATCH, RG_WIDTH), F32),
        ],
        scratch_shapes=[coef, coef, coef, coef],
        compiler_params=_params("arbitrary"),
        name="rglru_scan",
    )(rxl, rxc, conv_w, conv_b.reshape(1, RG_WIDTH), w, bias, lam)
    return jnp.concatenate([hl.transpose(1, 0, 2).reshape(T_LAT, RG_WIDTH),
                            hc.transpose(1, 0, 2).reshape(T_CTX, RG_WIDTH)], axis=0)


def _dft_tables(length):
    step = 1 << (length.bit_length() // 2)
    s = np.arange(length, dtype=np.int64)

    def thin(kvec):
        ang = ((kvec[:, None] * s[None, :]) % (2 * length)) * (math.pi / length)
        return (jnp.asarray(np.cos(ang)[:, None, :], dtype=F32),
                jnp.asarray(np.sin(ang)[:, None, :], dtype=F32))

    c_hi, s_hi = thin(np.arange(0, length, step, dtype=np.int64))
    c_lo, s_lo = thin(np.arange(step, dtype=np.int64))
    c_lo, s_lo = c_lo.reshape(1, step, length), s_lo.reshape(1, step, length)
    fc = (c_hi * c_lo - s_hi * s_lo).reshape(length, length)
    fs = (s_hi * c_lo + c_hi * s_lo).reshape(length, length)
    return fc.astype(BF16), fs.astype(BF16)


def _filter_features(length):
    f32 = np.float32
    pos = np.arange(length, dtype=f32)
    t = pos / f32(length)
    bands = np.linspace(1e-4, HY_BANDS - 1, HY_BANDS, dtype=f32)
    ang = (f32(2.0 * math.pi) * pos / f32(length))[:, None] * bands[None, :]
    feats = np.concatenate([t[:, None], np.cos(ang), -np.sin(ang)], axis=-1).astype(f32)
    feats = np.pad(feats, ((0, 0), (0, LANES - HY_EMB)))
    deltas = np.abs(np.linspace(HY_MIN_DECAY, HY_MAX_DECAY, HY_WIDTH, dtype=f32))
    window = (np.exp(-t[:, None] * deltas[None, :]) + f32(HY_SHIFT)).astype(f32)
    return jnp.asarray(feats), jnp.asarray(window)


def _filter_kernel(feat_ref, w1_ref, b1_ref, w2_ref, b2_ref, w3f_ref, w3b_ref, sf_ref, win_ref,
                   fc_ref, fs_ref, kc_ref, ks_ref, kn_ref, *, length):
    n = 2 * length
    h = jnp.sin(sf_ref[0:1, :] * (_dot(feat_ref[...].astype(BF16), w1_ref[...]) + b1_ref[...]))
    h = jnp.sin(sf_ref[1:2, :] * (_dot(h.astype(BF16), w2_ref[...]) + b2_ref[...]))
    hb16 = h.astype(BF16)
    win = win_ref[...]
    row = lax.broadcasted_iota(jnp.int32, (length, HY_CT), 0)
    h_fwd = _dot(hb16, w3f_ref[...]) * win
    h_bwd0 = jnp.where(row == 0, 0.0, _dot(hb16, w3b_ref[...]) * win)
    h_sum = h_fwd + h_bwd0
    h_dif = h_fwd - h_bwd0
    weight = jnp.where(row == 0, 1.0 / n, 2.0 / n)
    kc_ref[...] = _dot(fc_ref[...], h_sum.astype(BF16)) * weight
    ks_ref[...] = _dot(fs_ref[...], h_dif.astype(BF16)) * weight
    sign = jnp.where((row & 1) == 0, 1.0, -1.0)
    kn_ref[...] = jnp.sum(h_sum * sign, axis=0, keepdims=True) * (1.0 / n)


def _hyena_filter_spectrum(length, fc, fs, w1, b1, w2, b2, w3, sin_freq):
    feats, window = _filter_features(length)
    hp = LANES - HY_FILT_HIDDEN
    w1p = jnp.pad(w1, ((0, LANES - HY_EMB), (0, hp))).astype(BF16)
    w2p = jnp.pad(w2, ((0, hp), (0, hp))).astype(BF16)
    w3p = jnp.pad(w3, ((0, hp), (0, 0))).astype(BF16)
    b1p = jnp.pad(b1, (0, hp)).reshape(1, LANES)
    b2p = jnp.pad(b2, (0, hp)).reshape(1, LANES)
    sfp = jnp.pad(sin_freq, ((0, 0), (0, hp)))
    spec = jax.ShapeDtypeStruct((length, HY_WIDTH), F32)
    chan = lambda c: (0, c)
    return pl.pallas_call(
        functools.partial(_filter_kernel, length=length),
        grid=(HY_WIDTH // HY_CT,),
        in_specs=[
            _resident((length, LANES)),
            _resident((LANES, LANES)),
            _resident((1, LANES)),
            _resident((LANES, LANES)),
            _resident((1, LANES)),
            pl.BlockSpec((LANES, HY_CT), chan),
            pl.BlockSpec((LANES, HY_CT), chan),
            _resident((2, LANES)),
            pl.BlockSpec((length, HY_CT), chan),
            _resident((length, length)),
            _resident((length, length)),
        ],
        out_specs=[
            pl.BlockSpec((length, HY_CT), chan),
            pl.BlockSpec((length, HY_CT), chan),
            pl.BlockSpec((1, HY_CT), chan),
        ],
        out_shape=[spec, spec, jax.ShapeDtypeStruct((1, HY_WIDTH), F32)],
        compiler_params=_params("arbitrary"),
        name=f"hyena_filter_{length}",
    )(feats, w1p, b1p, w2p, b2p, w3p[:, :HY_WIDTH], w3p[:, HY_WIDTH:], sfp, window, fc, fs)


def _hyena_sequence(raw, cw_ref, cb_ref, bias_ref, fc_ref, fs_ref, kc_ref, ks_ref, kn_ref, length):
    row = lax.broadcasted_iota(jnp.int32, (length, HY_CT), 0)

    def short_conv(x, part):
        prev = jnp.where(row == 0, 0.0, pltpu.roll(x, 1, 0))
        nxt = jnp.where(row == length - 1, 0.0, pltpu.roll(x, length - 1, 0))
        return (prev * cw_ref[part, 0:1, :] + x * cw_ref[part, 1:2, :] + nxt * cw_ref[part, 2:3, :]
                + cb_ref[part:part + 1, :])

    x0 = short_conv(raw[0], 0)
    x1 = short_conv(raw[1], 1)
    v = short_conv(raw[2], 2)
    vx = v * x1
    vb = vx.astype(BF16)
    ft = min(length, HY_FREQ_TILE)
    y = None
    for f0 in range(0, length, ft):
        spec_c = _dot(fc_ref[f0:f0 + ft, :], vb)
        spec_s = _dot(fs_ref[f0:f0 + ft, :], vb)
        kc = kc_ref[f0:f0 + ft, :]
        ks = ks_ref[f0:f0 + ft, :]
        p = (spec_c * kc - spec_s * ks).astype(BF16)
        q = (spec_c * ks + spec_s * kc).astype(BF16)
        part = _dot(fc_ref[:, f0:f0 + ft], p) + _dot(fs_ref[:, f0:f0 + ft], q)
        y = part if y is None else y + part
    sign = jnp.where((row & 1) == 0, 1.0, -1.0)
    nyquist = jnp.sum(vx * sign, axis=0, keepdims=True) * kn_ref[...]
    return x0 * (y + sign * nyquist + vx * bias_ref[...])


def _hyena_kernel(x0_ref, x1_ref, v_ref, cw_ref, cb_ref, bias_ref,
                  fcl_ref, fsl_ref, kcl_ref, ksl_ref, knl_ref,
                  fcc_ref, fsc_ref, kcc_ref, ksc_ref, knc_ref, o_ref):
    step = pl.program_id(1)
    common = (cw_ref, cb_ref, bias_ref)

    @pl.when(step < BATCH)
    def _():
        raw = (x0_ref[...], x1_ref[...], v_ref[...])
        o_ref[...] = _hyena_sequence(raw, *common, fcl_ref, fsl_ref, kcl_ref, ksl_ref, knl_ref, SEQ)

    @pl.when(step == BATCH)
    def _():
        for r in range(BATCH):
            rows = slice(r * CTX_LEN, (r + 1) * CTX_LEN)
            raw = (x0_ref[rows, :], x1_ref[rows, :], v_ref[rows, :])
            o_ref[rows, :] = _hyena_sequence(raw, *common, fcc_ref, fsc_ref, kcc_ref, ksc_ref, knc_ref, CTX_LEN)


def _hyena(hy, tab_lat, spec_lat, tab_ctx, spec_ctx, conv_w, conv_b, bias):
    assert T_CTX == SEQ, "the context rows must form exactly one latent-sized row block"
    n_ct = HY_WIDTH // HY_CT
    cw = conv_w.reshape(3, 3, HY_WIDTH).transpose(1, 0, 2)
    cb = conv_b.reshape(3, HY_WIDTH)

    def part_spec(part):
        return pl.BlockSpec((SEQ, HY_CT), lambda c, b: (b, part * n_ct + c))

    chan = lambda c, b: (0, c)

    def table_specs(length):
        return [
            _resident((length, length)),
            _resident((length, length)),
            pl.BlockSpec((length, HY_CT), chan, pipeline_mode=pl.Buffered(1)),
            pl.BlockSpec((length, HY_CT), chan, pipeline_mode=pl.Buffered(1)),
            pl.BlockSpec((1, HY_CT), chan),
        ]

    return pl.pallas_call(
        _hyena_kernel,
        grid=(n_ct, BATCH + 1),
        in_specs=[
            part_spec(0), part_spec(1), part_spec(2),
            pl.BlockSpec((3, 3, HY_CT), lambda c, b: (0, 0, c)),
            pl.BlockSpec((3, HY_CT), chan),
            pl.BlockSpec((1, HY_CT), chan),
            *table_specs(SEQ),
            *table_specs(CTX_LEN),
        ],
        out_specs=pl.BlockSpec((SEQ, HY_CT), lambda c, b: (b, c)),
        out_shape=jax.ShapeDtypeStruct((T_ALL, HY_WIDTH), F32),
        compiler_params=_params("arbitrary", "arbitrary"),
        name="hyena_conv",
    )(hy, hy, hy, cw, cb, bias.reshape(1, HY_WIDTH), *tab_lat, *spec_lat, *tab_ctx, *spec_ctx)


def _rope_tables():
    f32 = np.float32
    rows = np.repeat(np.arange(SEQ // GRID_W, dtype=f32), GRID_W)
    cols = np.tile(np.arange(GRID_W, dtype=f32), SEQ // GRID_W)
    inv_freq = (f32(ROPE_BASE) ** (-np.arange(ROPE_AXIS_PAIRS, dtype=f32) / f32(ROPE_AXIS_PAIRS))).astype(f32)
    ang_r = rows[:, None] * inv_freq
    ang_c = cols[:, None] * inv_freq
    cos32 = np.concatenate([np.cos(ang_r)] * 2 + [np.cos(ang_c)] * 2, axis=1)
    sin32 = np.concatenate([-np.sin(ang_r), np.sin(ang_r), -np.sin(ang_c), np.sin(ang_c)], axis=1)
    ones = np.ones((SEQ, MLA_NOPE), f32)
    zeros = np.zeros((SEQ, MLA_NOPE), f32)
    pad = np.zeros((SEQ, HEAD_SLOT - MLA_NOPE - MLA_ROPE), f32)
    first = (np.arange(MLA_ROPE) % (2 * ROPE_AXIS_PAIRS)) < ROPE_AXIS_PAIRS
    q_cos = np.concatenate([ones, cos32, pad], axis=1) * f32(Q_SCALE)
    q_sin_up = np.concatenate([zeros, np.where(first, sin32, 0.0), pad], axis=1) * f32(Q_SCALE)
    q_sin_dn = np.concatenate([zeros, np.where(first, 0.0, sin32), pad], axis=1) * f32(Q_SCALE)
    k_cos = np.concatenate([zeros, cos32, pad], axis=1)
    k_sin = np.concatenate([zeros, sin32, pad], axis=1)
    ident = np.zeros((TM_PROJ, HEAD_SLOT), f32)
    ident[:, MLA_NOPE:MLA_NOPE + MLA_ROPE] = 1.0
    zero = np.zeros((TM_PROJ, HEAD_SLOT), f32)
    tables = ((q_cos, zero), (q_sin_up, zero), (q_sin_dn, zero), (k_cos, ident), (k_sin, zero))
    return tuple(jnp.asarray(np.concatenate(t, 0).astype(f32)) for t in tables)


def _rope_partner(n):
    idx = np.arange(n)
    return np.where((idx % (2 * ROPE_AXIS_PAIRS)) < ROPE_AXIS_PAIRS, idx + ROPE_AXIS_PAIRS, idx - ROPE_AXIS_PAIRS)


def _mla_weights(w_in, w_q_up, w_kv_up):
    w_q = w_in[:, :MLA_Q_LORA]
    w_kv = w_in[:, MLA_Q_LORA:MLA_Q_LORA + MLA_KV_LORA]
    w_kr = w_in[:, MLA_Q_LORA + MLA_KV_LORA:]
    part = _rope_partner(MLA_ROPE)
    lane_pad = ((0, 0), (MLA_NOPE, HEAD_SLOT - MLA_NOPE - MLA_ROPE))
    w_kr2 = jnp.concatenate([jnp.pad(w_kr, lane_pad), jnp.pad(w_kr[:, part], lane_pad)], axis=1)

    qh = w_q_up.reshape(MLA_Q_LORA, MLA_HEADS, MLA_NOPE + MLA_ROPE)
    slot_pad = ((0, 0), (0, 0), (0, HEAD_SLOT - MLA_NOPE - MLA_ROPE))
    wq_slot = jnp.pad(qh, slot_pad).reshape(MLA_Q_LORA, MLA_HEADS * HEAD_SLOT)

    kvh = w_kv_up.reshape(MLA_KV_LORA, MLA_HEADS, MLA_NOPE + MLA_V)
    wk_slot = jnp.pad(kvh[..., :MLA_NOPE], ((0, 0), (0, 0), (0, HEAD_SLOT - MLA_NOPE)))
    wk_slot = wk_slot.reshape(MLA_KV_LORA, MLA_HEADS * HEAD_SLOT)
    wv_slot = jnp.pad(kvh[..., MLA_NOPE:], ((0, 0), (0, 0), (0, HEAD_SLOT - MLA_V)))
    wv_slot = wv_slot.reshape(MLA_KV_LORA, MLA_HEADS * HEAD_SLOT)
    return [w.astype(BF16) for w in (w_q, w_kv, w_kr2, wq_slot, wk_slot, wv_slot)]


def _mlaproj_kernel(x_ref, mod_ref, wq_ref, wkv_ref, wkr_ref, wqs_ref, wks_ref, wv_ref,
                    vone_ref, qg_ref, kvg_ref, qcos_ref, qsup_ref, qsdn_ref, kcos_ref, ksin_ref,
                    q_ref, k_ref, v_ref):
    shift, scale = mod_ref[0, 0], mod_ref[0, 1]
    h = (x_ref[...] * (1.0 + scale) + shift).astype(BF16)

    def rmsnorm(y, g):
        return (y * lax.rsqrt(jnp.mean(y * y, axis=-1, keepdims=True) + RMS_EPS) * g).astype(BF16)

    qn = rmsnorm(_dot(h, wq_ref[...]), qg_ref[...])
    kvn = rmsnorm(_dot(h, wkv_ref[...]), kvg_ref[...])
    kr2 = _dot(h, wkr_ref[...])
    k_rope = kr2[:, :HEAD_SLOT] * kcos_ref[...] + kr2[:, HEAD_SLOT:] * ksin_ref[...]
    k_nope = _dot(kvn, wks_ref[...])
    v_ref[...] = (_dot(kvn, wv_ref[...]) + vone_ref[...]).astype(BF16)
    q_all = _dot(qn, wqs_ref[...])
    q_cos = qcos_ref[...]
    q_sin_up = qsup_ref[...]
    q_sin_dn = qsdn_ref[...]
    for hd in range(MLA_HEADS):
        sl = slice(hd * HEAD_SLOT, (hd + 1) * HEAD_SLOT)
        qh = q_all[:, sl]
        rotated = (qh * q_cos + pltpu.roll(qh, HEAD_SLOT - ROPE_AXIS_PAIRS, 1) * q_sin_up
                   + pltpu.roll(qh, ROPE_AXIS_PAIRS, 1) * q_sin_dn)
        q_ref[:, sl] = rotated.astype(BF16)
        k_ref[:, sl] = (k_nope[:, sl] + k_rope).astype(BF16)


def _mlaproj(x, mod, w_in, q_norm, kv_norm, w_q_up, w_kv_up):
    tm = TM_PROJ
    weights = _mla_weights(w_in, w_q_up, w_kv_up)
    tables = _rope_tables()
    row = lambda i: (i, 0)
    tab = lambda i: (jnp.where(i < T_LAT // tm, i % (SEQ // tm), SEQ // tm), 0)
    wide = MLA_HEADS * HEAD_SLOT
    v_one = jnp.asarray((np.arange(wide) % HEAD_SLOT == MLA_V).astype(np.float32).reshape(1, wide))
    return pl.pallas_call(
        _mlaproj_kernel,
        grid=(T_ALL // tm,),
        in_specs=[
            pl.BlockSpec((tm, D_MODEL), row),
            pl.BlockSpec((1, 2, 1, D_MODEL), lambda i: (_group_of_tile(i, tm), 0, 0, 0)),
            *[_resident(w.shape) for w in weights],
            _resident((1, wide)),
            _resident((1, MLA_Q_LORA)),
            _resident((1, MLA_KV_LORA)),
            *[pl.BlockSpec((tm, HEAD_SLOT), tab) for _ in tables],
        ],
        out_specs=[pl.BlockSpec((tm, wide), row)] * 3,
        out_shape=[jax.ShapeDtypeStruct((T_ALL, wide), BF16)] * 3,
        compiler_params=_params("arbitrary"),
        name="mla_proj",
    )(x, mod, *weights, v_one, q_norm.reshape(1, MLA_Q_LORA), kv_norm.reshape(1, MLA_KV_LORA), *tables)


def _attn_kernel(q_ref, kl_ref, kc_ref, vl_ref, vc_ref, o_ref):
    lane = lax.broadcasted_iota(jnp.int32, (TQ, HEAD_SLOT), 1)
    outs = []
    for hd in range(ATTN_HEADS):
        sl = slice(hd * HEAD_SLOT, (hd + 1) * HEAD_SLOT)
        q = q_ref[:, sl]
        s_lat = _dot_nt(q, kl_ref[:, sl])
        s_ctx = _dot_nt(q, kc_ref[:, sl])
        m = jnp.maximum(jnp.max(s_lat, axis=-1, keepdims=True), jnp.max(s_ctx, axis=-1, keepdims=True))
        p_lat = jnp.exp2(s_lat - m).astype(BF16)
        p_ctx = jnp.exp2(s_ctx - m).astype(BF16)
        o = _dot(p_lat, vl_ref[:, sl]) + _dot(p_ctx, vc_ref[:, sl])
        outs.append(o / o[:, MLA_V:MLA_V + 1])
    for pair in range(ATTN_HEADS // 2):
        packed = jnp.where(lane < MLA_V, outs[2 * pair], pltpu.roll(outs[2 * pair + 1], MLA_V, 1))
        o_ref[:, pair * HEAD_SLOT:(pair + 1) * HEAD_SLOT] = packed.astype(BF16)


def _attention(q, k, v):
    n_q = SEQ // TQ
    n_groups = MLA_HEADS // ATTN_HEADS
    ctx_block0 = T_LAT // CTX_LEN
    wide = ATTN_HEADS * HEAD_SLOT
    lat = lambda b, h, i: (b, h)
    ctx = lambda b, h, i: (ctx_block0 + b, h)
    return pl.pallas_call(
        _attn_kernel,
        grid=(BATCH, n_groups, n_q),
        in_specs=[
            pl.BlockSpec((TQ, wide), lambda b, h, i: (b * n_q + i, h)),
            pl.BlockSpec((SEQ, wide), lat),
            pl.BlockSpec((CTX_LEN, wide), ctx),
            pl.BlockSpec((SEQ, wide), lat),
            pl.BlockSpec((CTX_LEN, wide), ctx),
        ],
        out_specs=pl.BlockSpec((TQ, ATTN_HEADS * MLA_V), lambda b, h, i: (b * n_q + i, h)),
        out_shape=jax.ShapeDtypeStruct((T_LAT, MLA_HEADS * MLA_V), BF16),
        compiler_params=_params("arbitrary", "arbitrary", "arbitrary"),
        name="mla_attention",
    )(q, k, k, v, v)


def kernel(x, c, ctx, c_ctx, ada_w, ada_b, ln_g, ln_b, ffn_w_in, ffn_w_out, ev_w_in, ev_w_out, hy_conv_w, hy_conv_b, hy_filt_w1, hy_filt_b1, hy_filt_w2, hy_filt_b2, hy_filt_w3, hy_sin_freq, hy_bias, rg_conv_w, rg_conv_b, rg_a_w, rg_a_b, rg_x_w, rg_x_b, rg_lambda, mla_w_in, mla_q_norm, mla_kv_norm, mla_w_q_up, mla_w_kv_up, mla_w_out):
    assert x.shape == (BATCH, SEQ, D_MODEL) and ctx.shape == (BATCH, CTX_LEN, D_MODEL)
    cond = jnp.concatenate([c, c_ctx[None], jnp.zeros((2 * SUBLANES - N_GROUPS, D_MODEL), F32)], axis=0)
    mods = _adaln(cond, ada_w, ada_b)[:, :N_GROUPS].reshape(DEPTH, N_GROUPS, N_ADA, 1, D_MODEL)
    ffn_w = (ffn_w_in.astype(BF16), ffn_w_out.astype(BF16))

    m = mods[0]
    xs = _ffn_split(x.reshape(T_LAT, D_MODEL), ctx.reshape(T_CTX, D_MODEL),
                    m[:, 0:3], ffn_w, (0, 0), ln_g[0, 0], ln_b[0, 0], T_ALL)
    hy, rx, gate_in = _evproj(xs, m[:, 3:5], ev_w_in[0])
    h_sum = _rglru(rx, rg_conv_w[0], rg_conv_b[0], rg_a_w[0], rg_a_b[0], rg_x_w[0], rg_x_b[0], rg_lambda[0])
    filt = (hy_filt_w1[0], hy_filt_b1[0], hy_filt_w2[0], hy_filt_b2[0], hy_filt_w3[0], hy_sin_freq[0])
    tab_l = _dft_tables(SEQ)
    tab_c = _dft_tables(CTX_LEN)
    spec_l = _hyena_filter_spectrum(SEQ, *tab_l, *filt)
    spec_c = _hyena_filter_spectrum(CTX_LEN, *tab_c, *filt)
    y_hy = _hyena(hy, tab_l, spec_l, tab_c, spec_c, hy_conv_w[0], hy_conv_b[0], hy_bias[0])
    xs = _ffn_after_even(xs, y_hy, h_sum, gate_in, ev_w_out[0], ln_g[0, 1], ln_b[0, 1],
                         m[:, 5:9], ffn_w, (0, 1), ln_g[0, 2], ln_b[0, 2], T_ALL)

    m = mods[1]
    xs = _ffn_plain(xs, m[:, 0:3], ffn_w, (1, 0), ln_g[1, 0], ln_b[1, 0], T_ALL)
    q, k, v = _mlaproj(xs, m[:, 3:5], mla_w_in[0], mla_q_norm[0], mla_kv_norm[0], mla_w_q_up[0], mla_w_kv_up[0])
    att = _attention(q, k, v)
    xl = _ffn_after_mla(xs, att, mla_w_out[0], ln_g[1, 1], ln_b[1, 1],
                        m[:, 5:9], ffn_w, (1, 1), ln_g[1, 2], ln_b[1, 2], T_LAT)
    return xl.reshape(BATCH, SEQ, D_MODEL)
```

```python
import functools
import math

import jax
import jax.numpy as jnp
import numpy as np
from jax import lax
from jax.experimental import pallas as pl
from jax.experimental.pallas import tpu as pltpu

F32 = jnp.float32
BF16 = jnp.bfloat16

D_MODEL = 1024
BATCH = 8
SEQ = 2048
DEPTH = 2
CTX_LEN = 256
GRID_W = 64
N_ADA = 9
D_FF = 2816

HY_WIDTH = 512
RG_WIDTH = 512
EV_IN = 3 * HY_WIDTH + 2 * RG_WIDTH
HY_EMB = 33
HY_BANDS = (HY_EMB - 1) // 2
HY_FILT_HIDDEN = 64
HY_TARGET = 1e-2
HY_MAX_DECAY = math.log(HY_TARGET) / 0.3
HY_MIN_DECAY = math.log(HY_TARGET) / 1.5
HY_SHIFT = 0.05
RG_BLOCKS = 8
RG_BLOCK_DIM = RG_WIDTH // RG_BLOCKS
RG_C = 8.0

MLA_HEADS = 16
MLA_Q_LORA = 768
MLA_KV_LORA = 256
MLA_NOPE = 64
MLA_ROPE = 32
MLA_V = 64
MLA_SCALE = (MLA_NOPE + MLA_ROPE) ** -0.5
ROPE_AXIS_PAIRS = MLA_ROPE // 4
ROPE_BASE = 10000.0

LOG2_E = math.log2(math.e)
ALPHA = (2.0 * DEPTH) ** 0.25
LN_EPS = 1e-6
RMS_EPS = 1e-6

T_LAT = BATCH * SEQ
T_CTX = BATCH * CTX_LEN
T_ALL = T_LAT + T_CTX
N_GROUPS = BATCH + 1

LANES = 128
SUBLANES = 8
VMEM_LIMIT = 56 * 1024 * 1024

TM_FFN = 512
FF_CHUNK = 256
N_FF_CHUNKS = D_FF // FF_CHUNK
TM_PROJ = 512
HY_CT = 256
HY_FREQ_TILE = 512
RG_CT = 128
RG_TC = 32
RG_HALO = SUBLANES
HEAD_SLOT = 128
TQ = 512
ATTN_HEADS = 8
Q_SCALE = MLA_SCALE * math.log2(math.e)
ADA_TN = 1152


def _dot(a, b):
    return jnp.dot(a, b, preferred_element_type=F32)


def _dot_nt(a, b):
    return lax.dot_general(a, b, (((1,), (1,)), ((), ())), preferred_element_type=F32)


def _resident(shape):
    nd = len(shape)
    return pl.BlockSpec(shape, lambda *_: (0,) * nd, pipeline_mode=pl.Buffered(1))


def _group_of_tile(i, tm):
    return jnp.where(i < T_LAT // tm, i // (SEQ // tm), BATCH)


def _params(*sem):
    return pltpu.CompilerParams(dimension_semantics=sem, vmem_limit_bytes=VMEM_LIMIT)


def _layernorm(z, g, b):
    mu = jnp.mean(z, axis=-1, keepdims=True)
    zc = z - mu
    var = jnp.mean(zc * zc, axis=-1, keepdims=True)
    return zc * lax.rsqrt(var + LN_EPS) * g + b


def _adaln_kernel(c_ref, w_ref, b_ref, o_ref):
    c = c_ref[...]
    s = (c * jax.nn.sigmoid(c)).astype(BF16)
    o_ref[0] = _dot(s, w_ref[0].astype(BF16)) + b_ref[0]


def _adaln(cond, ada_w, ada_b):
    n_out = N_ADA * D_MODEL
    rows = cond.shape[0]
    return pl.pallas_call(
        _adaln_kernel,
        grid=(DEPTH, n_out // ADA_TN),
        in_specs=[
            pl.BlockSpec((rows, D_MODEL), lambda l, j: (0, 0)),
            pl.BlockSpec((1, D_MODEL, ADA_TN), lambda l, j: (l, 0, j)),
            pl.BlockSpec((1, 1, ADA_TN), lambda l, j: (l, 0, j)),
        ],
        out_specs=pl.BlockSpec((1, rows, ADA_TN), lambda l, j: (l, 0, j)),
        out_shape=jax.ShapeDtypeStruct((DEPTH, rows, n_out), F32),
        compiler_params=_params("arbitrary", "arbitrary"),
        name="adaln",
    )(cond, ada_w, ada_b.reshape(DEPTH, 1, n_out))


def _ffn_body(x, mod_ref, wi_ref, wo_ref, g_ref, b_ref, o_ref, act_ref):
    n_slots = mod_ref.shape[1]
    shift, scale, gate = mod_ref[0, n_slots - 3], mod_ref[0, n_slots - 2], mod_ref[0, n_slots - 1]
    h = (x * (1.0 + scale) + shift).astype(BF16)
    for j in range(N_FF_CHUNKS):
        lo = j * FF_CHUNK
        a = _dot(h, wi_ref[:, lo:lo + FF_CHUNK])
        u = _dot(h, wi_ref[:, D_FF + lo:D_FF + lo + FF_CHUNK])
        act_ref[:, lo:lo + FF_CHUNK] = (a * jax.nn.sigmoid(a) * u).astype(BF16)
    y = _dot(act_ref[...], wo_ref[...])
    o_ref[...] = _layernorm(ALPHA * x + (0.5 * gate) * y, g_ref[...], b_ref[...])


def _ffn_kernel(x_ref, *rest):
    _ffn_body(x_ref[...], *rest)


def _ffn_split_kernel(xl_ref, xc_ref, *rest):
    x = jnp.where(pl.program_id(0) < T_LAT // TM_FFN, xl_ref[...], xc_ref[...])
    _ffn_body(x, *rest)


def _ffn_after_even_kernel(x_ref, yh_ref, hs_ref, gt_ref, wmix_ref, g1_ref, b1_ref, mod_ref, *rest):
    y_rg = hs_ref[...] * jax.nn.gelu(gt_ref[...], approximate=True)
    y = (_dot(yh_ref[...].astype(BF16), wmix_ref[:HY_WIDTH, :])
         + _dot(y_rg.astype(BF16), wmix_ref[HY_WIDTH:, :]))
    x = _layernorm(ALPHA * x_ref[...] + mod_ref[0, 0] * y, g1_ref[...], b1_ref[...])
    _ffn_body(x, mod_ref, *rest)


def _ffn_after_mla_kernel(x_ref, att_ref, wmix_ref, g1_ref, b1_ref, mod_ref, *rest):
    y = _dot(att_ref[...], wmix_ref[...])
    x = _layernorm(ALPHA * x_ref[...] + mod_ref[0, 0] * y, g1_ref[...], b1_ref[...])
    _ffn_body(x, mod_ref, *rest)


def _ffn(body, lead_args, lead_specs, mod, ffn_w, which, g, b, n_rows):
    tm = TM_FFN
    layer, half = which
    w_in_all, w_out_all = ffn_w
    return pl.pallas_call(
        body,
        grid=(n_rows // tm,),
        in_specs=[
            *lead_specs,
            pl.BlockSpec((1, mod.shape[1], 1, D_MODEL), lambda i: (_group_of_tile(i, tm), 0, 0, 0)),
            pl.BlockSpec((None, None, D_MODEL, 2 * D_FF), lambda i: (layer, half, 0, 0),
                         pipeline_mode=pl.Buffered(1)),
            pl.BlockSpec((None, None, D_FF, D_MODEL), lambda i: (layer, half, 0, 0),
                         pipeline_mode=pl.Buffered(1)),
            _resident((1, D_MODEL)),
            _resident((1, D_MODEL)),
        ],
        out_specs=pl.BlockSpec((tm, D_MODEL), lambda i: (i, 0)),
        out_shape=jax.ShapeDtypeStruct((n_rows, D_MODEL), F32),
        scratch_shapes=[pltpu.VMEM((tm, D_FF), BF16)],
        compiler_params=_params("arbitrary"),
        name="half_ffn",
    )(*lead_args, mod, w_in_all, w_out_all, g.reshape(1, D_MODEL), b.reshape(1, D_MODEL))


def _row_spec(width, tm=TM_FFN):
    return pl.BlockSpec((tm, width), lambda i: (i, 0))


def _ffn_plain(x, *args):
    return _ffn(_ffn_kernel, (x,), [_row_spec(D_MODEL)], *args)


def _ffn_split(x_lat, x_ctx, *args):
    n_lat = T_LAT // TM_FFN
    specs = [pl.BlockSpec((TM_FFN, D_MODEL), lambda i: (jnp.minimum(i, n_lat - 1), 0)),
             pl.BlockSpec((TM_FFN, D_MODEL), lambda i: (jnp.maximum(i - n_lat, 0), 0))]
    return _ffn(_ffn_split_kernel, (x_lat, x_ctx), specs, *args)


def _ffn_after_even(x, y_hy, h_sum, gate_in, w_mix, g1, b1, *args):
    lead = (x, y_hy, h_sum, gate_in, w_mix.astype(BF16), g1.reshape(1, D_MODEL), b1.reshape(1, D_MODEL))
    specs = [_row_spec(D_MODEL), _row_spec(HY_WIDTH), _row_spec(RG_WIDTH), _row_spec(RG_WIDTH),
             _resident((HY_WIDTH + RG_WIDTH, D_MODEL)), _resident((1, D_MODEL)), _resident((1, D_MODEL))]
    return _ffn(_ffn_after_even_kernel, lead, specs, *args)


def _ffn_after_mla(x, att, w_mix, g1, b1, *args):
    lead = (x, att, w_mix.astype(BF16), g1.reshape(1, D_MODEL), b1.reshape(1, D_MODEL))
    specs = [_row_spec(D_MODEL), _row_spec(MLA_HEADS * MLA_V),
             _resident((MLA_HEADS * MLA_V, D_MODEL)), _resident((1, D_MODEL)), _resident((1, D_MODEL))]
    return _ffn(_ffn_after_mla_kernel, lead, specs, *args)


def _evproj_kernel(x_ref, mod_ref, w_ref, hy_ref, rx_ref, gt_ref):
    shift, scale = mod_ref[0, 0], mod_ref[0, 1]
    h = (x_ref[...] * (1.0 + scale) + shift).astype(BF16)
    n_hy = 3 * HY_WIDTH
    hy_ref[...] = _dot(h, w_ref[:, :n_hy])
    rx_ref[...] = _dot(h, w_ref[:, n_hy:n_hy + RG_WIDTH])
    gt_ref[...] = _dot(h, w_ref[:, n_hy + RG_WIDTH:])


def _evproj(x, mod, w_in):
    tm = TM_PROJ
    row = lambda i: (i, 0)
    return pl.pallas_call(
        _evproj_kernel,
        grid=(T_ALL // tm,),
        in_specs=[
            pl.BlockSpec((tm, D_MODEL), row),
            pl.BlockSpec((1, 2, 1, D_MODEL), lambda i: (_group_of_tile(i, tm), 0, 0, 0)),
            _resident((D_MODEL, EV_IN)),
        ],
        out_specs=[
            pl.BlockSpec((tm, 3 * HY_WIDTH), row),
            pl.BlockSpec((tm, RG_WIDTH), row),
            pl.BlockSpec((tm, RG_WIDTH), row),
        ],
        out_shape=[
            jax.ShapeDtypeStruct((T_ALL, 3 * HY_WIDTH), F32),
            jax.ShapeDtypeStruct((T_ALL, RG_WIDTH), F32),
            jax.ShapeDtypeStruct((T_ALL, RG_WIDTH), F32),
        ],
        compiler_params=_params("arbitrary"),
        name="even_in_proj",
    )(x, mod, w_in.astype(BF16))


def _block_diag_tiles(w):
    per = RG_CT // RG_BLOCK_DIM
    w = w.reshape(RG_WIDTH // RG_CT, per, RG_BLOCK_DIM, RG_BLOCK_DIM)
    eye = jnp.eye(per, dtype=w.dtype)
    return jnp.einsum('cpde,pq->cpdqe', w, eye).reshape(RG_WIDTH // RG_CT, RG_CT, RG_CT)


def _rg_kernel(rx_ref, cw_ref, cb_ref, w_ref, bias_ref, lam_ref, out_ref,
                padl_ref, padc_ref, xt_f_ref, xt_b_ref, coef_a_ref, coef_b_ref, hbuf_f_ref, hbuf_b_ref):
    neg_lam = -lam_ref[...]
    softplus = jnp.maximum(neg_lam, 0.0) + jnp.log1p(jnp.exp(-jnp.abs(neg_lam)))
    neg_log_a_scale = RG_C * softplus
    a_exp2_scale = -LOG2_E * neg_log_a_scale
    cb = cb_ref[...]
    xts = (xt_f_ref, xt_b_ref)
    hbuf = (hbuf_f_ref, hbuf_b_ref)
    slab_rows = RG_TC + 2 * RG_HALO

    def step_rows(s):
        return slice(s * SUBLANES, (s + 1) * SUBLANES)

    halo = jnp.zeros((BATCH, RG_HALO, RG_CT), F32)
    for pad_ref, length in ((padl_ref, SEQ), (padc_ref, CTX_LEN)):
        pad_ref[:, 0:RG_HALO, :] = halo
        pad_ref[:, RG_HALO + length:, :] = halo
    padl_ref[:, RG_HALO:RG_HALO + SEQ, :] = rx_ref[0:BATCH]
    for b in range(BATCH):
        padc_ref[b, RG_HALO:RG_HALO + CTX_LEN, :] = rx_ref[BATCH, b * CTX_LEN:(b + 1) * CTX_LEN, :]

    def add_out(latent, t_start, d):
        t_start = pl.multiple_of(t_start, SUBLANES)
        for b in range(BATCH):
            rows = hbuf[d][pl.ds(b, RG_TC, stride=SUBLANES), :]
            if latent:
                out_ref[b, pl.ds(t_start, RG_TC), :] += rows
            else:
                out_ref[BATCH, pl.ds(b * CTX_LEN + t_start, RG_TC), :] += rows

    def sigmoid(z):
        return 1.0 / (1.0 + jnp.exp2(z * (-LOG2_E)))

    def coeffs(latent, length, t0, d, dst_ref):
        t0 = pl.multiple_of(jnp.asarray(t0, jnp.int32), SUBLANES)
        pad_ref = padl_ref if latent else padc_ref
        xt = xts[d]
        for b in range(BATCH):
            xt[pl.ds(b, slab_rows, stride=SUBLANES), :] = pad_ref[b, pl.ds(t0, slab_rows), :]
        x2 = cb
        for k in range(4):
            start = (RG_HALO - 2 + k) * SUBLANES
            x2 = x2 + xt[start:start + RG_TC * SUBLANES, :] * cw_ref[k:k + 1, :]
        xb = x2.astype(BF16)
        r = sigmoid(_dot(xb, w_ref[0, 2 * d]) + bias_ref[2 * d:2 * d + 1, :])
        i = sigmoid(_dot(xb, w_ref[0, 2 * d + 1]) + bias_ref[2 * d + 1:2 * d + 2, :])
        a = jnp.exp2(r * a_exp2_scale[d:d + 1, :])
        b = jnp.sqrt(jnp.tanh(r * neg_log_a_scale[d:d + 1, :]) * (a * a + 1.0)) * (i * x2)
        dst_ref[d, 0] = a
        dst_ref[d, 1] = b

    def scan(latent, tf, tb, src_ref, hf, hb):
        tf = jnp.asarray(tf, jnp.int32)
        tb = jnp.asarray(tb, jnp.int32)
        for s in range(RG_TC):
            rows = step_rows(s)
            hf = src_ref[0, 0, rows, :] * hf + src_ref[0, 1, rows, :]
            hbuf_f_ref[rows, :] = hf
            rows = step_rows(RG_TC - 1 - s)
            hb = src_ref[1, 0, rows, :] * hb + src_ref[1, 1, rows, :]
            hbuf_b_ref[rows, :] = hb
        add_out(latent, tf, 0)
        add_out(latent, tb, 1)
        return hf, hb

    def sweep(latent, length, h_fwd, h_bwd):
        n_chunks = length // RG_TC
        last = n_chunks - 1

        def both(j, dst_ref):
            coeffs(latent, length, j * RG_TC, 0, dst_ref)
            coeffs(latent, length, (last - j) * RG_TC, 1, dst_ref)

        def body(i, carry):
            j = 2 * i
            both(j + 1, coef_b_ref)
            carry = scan(latent, j * RG_TC, (last - j) * RG_TC, coef_a_ref, *carry)
            both(jnp.minimum(j + 2, last), coef_a_ref)
            return scan(latent, (j + 1) * RG_TC, (last - j - 1) * RG_TC, coef_b_ref, *carry)

        both(0, coef_a_ref)
        return lax.fori_loop(0, n_chunks // 2, body, (h_fwd, h_bwd))

    out_ref[...] = jnp.zeros_like(out_ref)
    zero = jnp.zeros((SUBLANES, RG_CT), F32)
    hf, hb = sweep(False, CTX_LEN, zero, zero)
    sweep(True, SEQ, hf, hb)


def _rglru(rx, conv_w, conv_b, a_w, a_b, x_w, x_b, lam):
    assert T_CTX == SEQ, "the context rows must form exactly one latent-sized row block"
    w = jnp.stack([_block_diag_tiles(a_w[0]), _block_diag_tiles(x_w[0]),
                   _block_diag_tiles(a_w[1]), _block_diag_tiles(x_w[1])], axis=1).astype(BF16)
    bias = jnp.stack([a_b[0], x_b[0], a_b[1], x_b[1]], axis=0)
    n_ct = RG_WIDTH // RG_CT
    n_blocks = BATCH + 1
    chan = lambda c: (0, 0, c)
    lane = lambda c: (0, c)
    coef = pltpu.VMEM((2, 2, RG_TC * BATCH, RG_CT), F32)
    slab = pltpu.VMEM(((RG_TC + 2 * RG_HALO) * BATCH, RG_CT), F32)
    chunk = pltpu.VMEM((RG_TC * BATCH, RG_CT), F32)
    out = pl.pallas_call(
        _rg_kernel,
        grid=(n_ct,),
        in_specs=[
            pl.BlockSpec((n_blocks, SEQ, RG_CT), chan),
            pl.BlockSpec((4, RG_CT), lane),
            pl.BlockSpec((1, RG_CT), lane),
            pl.BlockSpec((1, 4, RG_CT, RG_CT), lambda c: (c, 0, 0, 0)),
            pl.BlockSpec((4, RG_CT), lane),
            pl.BlockSpec((2, RG_CT), lane),
        ],
        out_specs=pl.BlockSpec((n_blocks, SEQ, RG_CT), chan),
        out_shape=jax.ShapeDtypeStruct((n_blocks, SEQ, RG_WIDTH), F32),
        scratch_shapes=[
            pltpu.VMEM((BATCH, SEQ + 2 * RG_HALO, RG_CT), F32),
            pltpu.VMEM((BATCH, CTX_LEN + 2 * RG_HALO, RG_CT), F32),
            slab, slab,
            coef, coef,
            chunk, chunk,
        ],
        compiler_params=_params("arbitrary"),
        name="rglru_scan",
    )(rx.reshape(n_blocks, SEQ, RG_WIDTH), conv_w, conv_b.reshape(1, RG_WIDTH), w, bias, lam)
    return out.reshape(T_ALL, RG_WIDTH)


def _dft_tables(length):
    step = 1 << (length.bit_length() // 2)
    s = np.arange(length, dtype=np.int64)

    def thin(kvec):
        ang = ((kvec[:, None] * s[None, :]) % (2 * length)) * (math.pi / length)
        return (jnp.asarray(np.cos(ang)[:, None, :], dtype=F32),
                jnp.asarray(np.sin(ang)[:, None, :], dtype=F32))

    c_hi, s_hi = thin(np.arange(0, length, step, dtype=np.int64))
    c_lo, s_lo = thin(np.arange(step, dtype=np.int64))
    c_lo, s_lo = c_lo.reshape(1, step, length), s_lo.reshape(1, step, length)
    fc = (c_hi * c_lo - s_hi * s_lo).reshape(length, length)
    fs = (s_hi * c_lo + c_hi * s_lo).reshape(length, length)
    return fc.astype(BF16), fs.astype(BF16)


def _filter_features(length):
    f32 = np.float32
    pos = np.arange(length, dtype=f32)
    t = pos / f32(length)
    bands = np.linspace(1e-4, HY_BANDS - 1, HY_BANDS, dtype=f32)
    ang = (f32(2.0 * math.pi) * pos / f32(length))[:, None] * bands[None, :]
    feats = np.concatenate([t[:, None], np.cos(ang), -np.sin(ang)], axis=-1).astype(f32)
    feats = np.pad(feats, ((0, 0), (0, LANES - HY_EMB)))
    deltas = np.abs(np.linspace(HY_MIN_DECAY, HY_MAX_DECAY, HY_WIDTH, dtype=f32))
    window = (np.exp(-t[:, None] * deltas[None, :]) + f32(HY_SHIFT)).astype(f32)
    return jnp.asarray(feats), jnp.asarray(window)


def _filter_kernel(feat_ref, w1_ref, b1_ref, w2_ref, b2_ref, w3f_ref, w3b_ref, sf_ref, win_ref,
                   fc_ref, fs_ref, kc_ref, ks_ref, kn_ref, *, length):
    n = 2 * length
    h = jnp.sin(sf_ref[0:1, :] * (_dot(feat_ref[...].astype(BF16), w1_ref[...]) + b1_ref[...]))
    h = jnp.sin(sf_ref[1:2, :] * (_dot(h.astype(BF16), w2_ref[...]) + b2_ref[...]))
    hb16 = h.astype(BF16)
    win = win_ref[...]
    row = lax.broadcasted_iota(jnp.int32, (length, HY_CT), 0)
    h_fwd = _dot(hb16, w3f_ref[...]) * win
    h_bwd0 = jnp.where(row == 0, 0.0, _dot(hb16, w3b_ref[...]) * win)
    h_sum = h_fwd + h_bwd0
    h_dif = h_fwd - h_bwd0
    weight = jnp.where(row == 0, 1.0 / n, 2.0 / n)
    kc_ref[...] = _dot(fc_ref[...], h_sum.astype(BF16)) * weight
    ks_ref[...] = _dot(fs_ref[...], h_dif.astype(BF16)) * weight
    sign = jnp.where((row & 1) == 0, 1.0, -1.0)
    kn_ref[...] = jnp.sum(h_sum * sign, axis=0, keepdims=True) * (1.0 / n)


def _hyena_filter_spectrum(length, fc, fs, w1, b1, w2, b2, w3, sin_freq):
    feats, window = _filter_features(length)
    hp = LANES - HY_FILT_HIDDEN
    w1p = jnp.pad(w1, ((0, LANES - HY_EMB), (0, hp))).astype(BF16)
    w2p = jnp.pad(w2, ((0, hp), (0, hp))).astype(BF16)
    w3p = jnp.pad(w3, ((0, hp), (0, 0))).astype(BF16)
    b1p = jnp.pad(b1, (0, hp)).reshape(1, LANES)
    b2p = jnp.pad(b2, (0, hp)).reshape(1, LANES)
    sfp = jnp.pad(sin_freq, ((0, 0), (0, hp)))
    spec = jax.ShapeDtypeStruct((length, HY_WIDTH), F32)
    chan = lambda c: (0, c)
    return pl.pallas_call(
        functools.partial(_filter_kernel, length=length),
        grid=(HY_WIDTH // HY_CT,),
        in_specs=[
            _resident((length, LANES)),
            _resident((LANES, LANES)),
            _resident((1, LANES)),
            _resident((LANES, LANES)),
            _resident((1, LANES)),
            pl.BlockSpec((LANES, HY_CT), chan),
            pl.BlockSpec((LANES, HY_CT), chan),
            _resident((2, LANES)),
            pl.BlockSpec((length, HY_CT), chan),
            _resident((length, length)),
            _resident((length, length)),
        ],
        out_specs=[
            pl.BlockSpec((length, HY_CT), chan),
            pl.BlockSpec((length, HY_CT), chan),
            pl.BlockSpec((1, HY_CT), chan),
        ],
        out_shape=[spec, spec, jax.ShapeDtypeStruct((1, HY_WIDTH), F32)],
        compiler_params=_params("arbitrary"),
        name=f"hyena_filter_{length}",
    )(feats, w1p, b1p, w2p, b2p, w3p[:, :HY_WIDTH], w3p[:, HY_WIDTH:], sfp, window, fc, fs)


def _hyena_sequence(raw, cw_ref, cb_ref, bias_ref, fc_ref, fs_ref, kc_ref, ks_ref, kn_ref, length):
    row = lax.broadcasted_iota(jnp.int32, (length, HY_CT), 0)

    def short_conv(x, part):
        prev = jnp.where(row == 0, 0.0, pltpu.roll(x, 1, 0))
        nxt = jnp.where(row == length - 1, 0.0, pltpu.roll(x, length - 1, 0))
        return (prev * cw_ref[part, 0:1, :] + x * cw_ref[part, 1:2, :] + nxt * cw_ref[part, 2:3, :]
                + cb_ref[part:part + 1, :])

    x0 = short_conv(raw[0], 0)
    x1 = short_conv(raw[1], 1)
    v = short_conv(raw[2], 2)
    vx = v * x1
    vb = vx.astype(BF16)
    ft = min(length, HY_FREQ_TILE)
    y = None
    for f0 in range(0, length, ft):
        spec_c = _dot(fc_ref[f0:f0 + ft, :], vb)
        spec_s = _dot(fs_ref[f0:f0 + ft, :], vb)
        kc = kc_ref[f0:f0 + ft, :]
        ks = ks_ref[f0:f0 + ft, :]
        p = (spec_c * kc - spec_s * ks).astype(BF16)
        q = (spec_c * ks + spec_s * kc).astype(BF16)
        part = _dot(fc_ref[:, f0:f0 + ft], p) + _dot(fs_ref[:, f0:f0 + ft], q)
        y = part if y is None else y + part
    sign = jnp.where((row & 1) == 0, 1.0, -1.0)
    nyquist = jnp.sum(vx * sign, axis=0, keepdims=True) * kn_ref[...]
    return x0 * (y + sign * nyquist + vx * bias_ref[...])


def _hyena_kernel(x0_ref, x1_ref, v_ref, cw_ref, cb_ref, bias_ref,
                  fcl_ref, fsl_ref, kcl_ref, ksl_ref, knl_ref,
                  fcc_ref, fsc_ref, kcc_ref, ksc_ref, knc_ref, o_ref):
    step = pl.program_id(1)
    common = (cw_ref, cb_ref, bias_ref)

    @pl.when(step < BATCH)
    def _():
        raw = (x0_ref[...], x1_ref[...], v_ref[...])
        o_ref[...] = _hyena_sequence(raw, *common, fcl_ref, fsl_ref, kcl_ref, ksl_ref, knl_ref, SEQ)

    @pl.when(step == BATCH)
    def _():
        for r in range(BATCH):
            rows = slice(r * CTX_LEN, (r + 1) * CTX_LEN)
            raw = (x0_ref[rows, :], x1_ref[rows, :], v_ref[rows, :])
            o_ref[rows, :] = _hyena_sequence(raw, *common, fcc_ref, fsc_ref, kcc_ref, ksc_ref, knc_ref, CTX_LEN)


def _hyena(hy, tab_lat, spec_lat, tab_ctx, spec_ctx, conv_w, conv_b, bias):
    assert T_CTX == SEQ, "the context rows must form exactly one latent-sized row block"
    n_ct = HY_WIDTH // HY_CT
    cw = conv_w.reshape(3, 3, HY_WIDTH).transpose(1, 0, 2)
    cb = conv_b.reshape(3, HY_WIDTH)

    def part_spec(part):
        return pl.BlockSpec((SEQ, HY_CT), lambda c, b: (b, part * n_ct + c))

    chan = lambda c, b: (0, c)

    def table_specs(length):
        return [
            _resident((length, length)),
            _resident((length, length)),
            pl.BlockSpec((length, HY_CT), chan, pipeline_mode=pl.Buffered(1)),
            pl.BlockSpec((length, HY_CT), chan, pipeline_mode=pl.Buffered(1)),
            pl.BlockSpec((1, HY_CT), chan),
        ]

    return pl.pallas_call(
        _hyena_kernel,
        grid=(n_ct, BATCH + 1),
        in_specs=[
            part_spec(0), part_spec(1), part_spec(2),
            pl.BlockSpec((3, 3, HY_CT), lambda c, b: (0, 0, c)),
            pl.BlockSpec((3, HY_CT), chan),
            pl.BlockSpec((1, HY_CT), chan),
            *table_specs(SEQ),
            *table_specs(CTX_LEN),
        ],
        out_specs=pl.BlockSpec((SEQ, HY_CT), lambda c, b: (b, c)),
        out_shape=jax.ShapeDtypeStruct((T_ALL, HY_WIDTH), F32),
        compiler_params=_params("arbitrary", "arbitrary"),
        name="hyena_conv",
    )(hy, hy, hy, cw, cb, bias.reshape(1, HY_WIDTH), *tab_lat, *spec_lat, *tab_ctx, *spec_ctx)


def _rope_tables():
    f32 = np.float32
    rows = np.repeat(np.arange(SEQ // GRID_W, dtype=f32), GRID_W)
    cols = np.tile(np.arange(GRID_W, dtype=f32), SEQ // GRID_W)
    inv_freq = (f32(ROPE_BASE) ** (-np.arange(ROPE_AXIS_PAIRS, dtype=f32) / f32(ROPE_AXIS_PAIRS))).astype(f32)
    ang_r = rows[:, None] * inv_freq
    ang_c = cols[:, None] * inv_freq
    cos32 = np.concatenate([np.cos(ang_r)] * 2 + [np.cos(ang_c)] * 2, axis=1)
    sin32 = np.concatenate([-np.sin(ang_r), np.sin(ang_r), -np.sin(ang_c), np.sin(ang_c)], axis=1)
    ones = np.ones((SEQ, MLA_NOPE), f32)
    zeros = np.zeros((SEQ, MLA_NOPE), f32)
    pad = np.zeros((SEQ, HEAD_SLOT - MLA_NOPE - MLA_ROPE), f32)
    first = (np.arange(MLA_ROPE) % (2 * ROPE_AXIS_PAIRS)) < ROPE_AXIS_PAIRS
    q_cos = np.concatenate([ones, cos32, pad], axis=1) * f32(Q_SCALE)
    q_sin_up = np.concatenate([zeros, np.where(first, sin32, 0.0), pad], axis=1) * f32(Q_SCALE)
    q_sin_dn = np.concatenate([zeros, np.where(first, 0.0, sin32), pad], axis=1) * f32(Q_SCALE)
    k_cos = np.concatenate([zeros, cos32, pad], axis=1)
    k_sin = np.concatenate([zeros, sin32, pad], axis=1)
    ident = np.zeros((TM_PROJ, HEAD_SLOT), f32)
    ident[:, MLA_NOPE:MLA_NOPE + MLA_ROPE] = 1.0
    zero = np.zeros((TM_PROJ, HEAD_SLOT), f32)
    tables = ((q_cos, zero), (q_sin_up, zero), (q_sin_dn, zero), (k_cos, ident), (k_sin, zero))
    return tuple(jnp.asarray(np.concatenate(t, 0).astype(f32)) for t in tables)


def _rope_partner(n):
    idx = np.arange(n)
    return np.where((idx % (2 * ROPE_AXIS_PAIRS)) < ROPE_AXIS_PAIRS, idx + ROPE_AXIS_PAIRS, idx - ROPE_AXIS_PAIRS)


def _mla_weights(w_in, w_q_up, w_kv_up):
    w_q = w_in[:, :MLA_Q_LORA]
    w_kv = w_in[:, MLA_Q_LORA:MLA_Q_LORA + MLA_KV_LORA]
    w_kr = w_in[:, MLA_Q_LORA + MLA_KV_LORA:]
    part = _rope_partner(MLA_ROPE)
    lane_pad = ((0, 0), (MLA_NOPE, HEAD_SLOT - MLA_NOPE - MLA_ROPE))
    w_kr2 = jnp.concatenate([jnp.pad(w_kr, lane_pad), jnp.pad(w_kr[:, part], lane_pad)], axis=1)

    qh = w_q_up.reshape(MLA_Q_LORA, MLA_HEADS, MLA_NOPE + MLA_ROPE)
    slot_pad = ((0, 0), (0, 0), (0, HEAD_SLOT - MLA_NOPE - MLA_ROPE))
    wq_slot = jnp.pad(qh, slot_pad).reshape(MLA_Q_LORA, MLA_HEADS * HEAD_SLOT)

    kvh = w_kv_up.reshape(MLA_KV_LORA, MLA_HEADS, MLA_NOPE + MLA_V)
    wk_slot = jnp.pad(kvh[..., :MLA_NOPE], ((0, 0), (0, 0), (0, HEAD_SLOT - MLA_NOPE)))
    wk_slot = wk_slot.reshape(MLA_KV_LORA, MLA_HEADS * HEAD_SLOT)
    wv_slot = jnp.pad(kvh[..., MLA_NOPE:], ((0, 0), (0, 0), (0, HEAD_SLOT - MLA_V)))
    wv_slot = wv_slot.reshape(MLA_KV_LORA, MLA_HEADS * HEAD_SLOT)
    return [w.astype(BF16) for w in (w_q, w_kv, w_kr2, wq_slot, wk_slot, wv_slot)]


def _mlaproj_kernel(x_ref, mod_ref, wq_ref, wkv_ref, wkr_ref, wqs_ref, wks_ref, wv_ref,
                    vone_ref, qg_ref, kvg_ref, qcos_ref, qsup_ref, qsdn_ref, kcos_ref, ksin_ref,
                    q_ref, k_ref, v_ref):
    shift, scale = mod_ref[0, 0], mod_ref[0, 1]
    h = (x_ref[...] * (1.0 + scale) + shift).astype(BF16)

    def rmsnorm(y, g):
        return (y * lax.rsqrt(jnp.mean(y * y, axis=-1, keepdims=True) + RMS_EPS) * g).astype(BF16)

    qn = rmsnorm(_dot(h, wq_ref[...]), qg_ref[...])
    kvn = rmsnorm(_dot(h, wkv_ref[...]), kvg_ref[...])
    kr2 = _dot(h, wkr_ref[...])
    k_rope = kr2[:, :HEAD_SLOT] * kcos_ref[...] + kr2[:, HEAD_SLOT:] * ksin_ref[...]
    k_nope = _dot(kvn, wks_ref[...])
    v_ref[...] = (_dot(kvn, wv_ref[...]) + vone_ref[...]).astype(BF16)
    q_all = _dot(qn, wqs_ref[...])
    q_cos = qcos_ref[...]
    q_sin_up = qsup_ref[...]
    q_sin_dn = qsdn_ref[...]
    for hd in range(MLA_HEADS):
        sl = slice(hd * HEAD_SLOT, (hd + 1) * HEAD_SLOT)
        qh = q_all[:, sl]
        rotated = (qh * q_cos + pltpu.roll(qh, HEAD_SLOT - ROPE_AXIS_PAIRS, 1) * q_sin_up
                   + pltpu.roll(qh, ROPE_AXIS_PAIRS, 1) * q_sin_dn)
        q_ref[:, sl] = rotated.astype(BF16)
        k_ref[:, sl] = (k_nope[:, sl] + k_rope).astype(BF16)


def _mlaproj(x, mod, w_in, q_norm, kv_norm, w_q_up, w_kv_up):
    tm = TM_PROJ
    weights = _mla_weights(w_in, w_q_up, w_kv_up)
    tables = _rope_tables()
    row = lambda i: (i, 0)
    tab = lambda i: (jnp.where(i < T_LAT // tm, i % (SEQ // tm), SEQ // tm), 0)
    wide = MLA_HEADS * HEAD_SLOT
    v_one = jnp.asarray((np.arange(wide) % HEAD_SLOT == MLA_V).astype(np.float32).reshape(1, wide))
    return pl.pallas_call(
        _mlaproj_kernel,
        grid=(T_ALL // tm,),
        in_specs=[
            pl.BlockSpec((tm, D_MODEL), row),
            pl.BlockSpec((1, 2, 1, D_MODEL), lambda i: (_group_of_tile(i, tm), 0, 0, 0)),
            *[_resident(w.shape) for w in weights],
            _resident((1, wide)),
            _resident((1, MLA_Q_LORA)),
            _resident((1, MLA_KV_LORA)),
            *[pl.BlockSpec((tm, HEAD_SLOT), tab) for _ in tables],
        ],
        out_specs=[pl.BlockSpec((tm, wide), row)] * 3,
        out_shape=[jax.ShapeDtypeStruct((T_ALL, wide), BF16)] * 3,
        compiler_params=_params("arbitrary"),
        name="mla_proj",
    )(x, mod, *weights, v_one, q_norm.reshape(1, MLA_Q_LORA), kv_norm.reshape(1, MLA_KV_LORA), *tables)


def _attn_kernel(q_ref, kl_ref, kc_ref, vl_ref, vc_ref, o_ref):
    lane = lax.broadcasted_iota(jnp.int32, (TQ, HEAD_SLOT), 1)
    outs = []
    for hd in range(ATTN_HEADS):
        sl = slice(hd * HEAD_SLOT, (hd + 1) * HEAD_SLOT)
        q = q_ref[:, sl]
        s_lat = _dot_nt(q, kl_ref[:, sl])
        s_ctx = _dot_nt(q, kc_ref[:, sl])
        m = jnp.maximum(jnp.max(s_lat, axis=-1, keepdims=True), jnp.max(s_ctx, axis=-1, keepdims=True))
        p_lat = jnp.exp2(s_lat - m).astype(BF16)
        p_ctx = jnp.exp2(s_ctx - m).astype(BF16)
        o = _dot(p_lat, vl_ref[:, sl]) + _dot(p_ctx, vc_ref[:, sl])
        outs.append(o / o[:, MLA_V:MLA_V + 1])
    for pair in range(ATTN_HEADS // 2):
        packed = jnp.where(lane < MLA_V, outs[2 * pair], pltpu.roll(outs[2 * pair + 1], MLA_V, 1))
        o_ref[:, pair * HEAD_SLOT:(pair + 1) * HEAD_SLOT] = packed.astype(BF16)


def _attention(q, k, v):
    n_q = SEQ // TQ
    n_groups = MLA_HEADS // ATTN_HEADS
    ctx_block0 = T_LAT // CTX_LEN
    wide = ATTN_HEADS * HEAD_SLOT
    lat = lambda b, h, i: (b, h)
    ctx = lambda b, h, i: (ctx_block0 + b, h)
    return pl.pallas_call(
        _attn_kernel,
        grid=(BATCH, n_groups, n_q),
        in_specs=[
            pl.BlockSpec((TQ, wide), lambda b, h, i: (b * n_q + i, h)),
            pl.BlockSpec((SEQ, wide), lat),
            pl.BlockSpec((CTX_LEN, wide), ctx),
            pl.BlockSpec((SEQ, wide), lat),
            pl.BlockSpec((CTX_LEN, wide), ctx),
        ],
        out_specs=pl.BlockSpec((TQ, ATTN_HEADS * MLA_V), lambda b, h, i: (b * n_q + i, h)),
        out_shape=jax.ShapeDtypeStruct((T_LAT, MLA_HEADS * MLA_V), BF16),
        compiler_params=_params("arbitrary", "arbitrary", "arbitrary"),
        name="mla_attention",
    )(q, k, k, v, v)


def kernel(x, c, ctx, c_ctx, ada_w, ada_b, ln_g, ln_b, ffn_w_in, ffn_w_out, ev_w_in, ev_w_out, hy_conv_w, hy_conv_b, hy_filt_w1, hy_filt_b1, hy_filt_w2, hy_filt_b2, hy_filt_w3, hy_sin_freq, hy_bias, rg_conv_w, rg_conv_b, rg_a_w, rg_a_b, rg_x_w, rg_x_b, rg_lambda, mla_w_in, mla_q_norm, mla_kv_norm, mla_w_q_up, mla_w_kv_up, mla_w_out):
    assert x.shape == (BATCH, SEQ, D_MODEL) and ctx.shape == (BATCH, CTX_LEN, D_MODEL)
    cond = jnp.concatenate([c, c_ctx[None], jnp.zeros((2 * SUBLANES - N_GROUPS, D_MODEL), F32)], axis=0)
    mods = _adaln(cond, ada_w, ada_b)[:, :N_GROUPS].reshape(DEPTH, N_GROUPS, N_ADA, 1, D_MODEL)
    ffn_w = (ffn_w_in.astype(BF16), ffn_w_out.astype(BF16))

    m = mods[0]
    xs = _ffn_split(x.reshape(T_LAT, D_MODEL), ctx.reshape(T_CTX, D_MODEL),
                    m[:, 0:3], ffn_w, (0, 0), ln_g[0, 0], ln_b[0, 0], T_ALL)
    hy, rx, gate_in = _evproj(xs, m[:, 3:5], ev_w_in[0])
    h_sum = _rglru(rx, rg_conv_w[0], rg_conv_b[0], rg_a_w[0], rg_a_b[0], rg_x_w[0], rg_x_b[0], rg_lambda[0])
    filt = (hy_filt_w1[0], hy_filt_b1[0], hy_filt_w2[0], hy_filt_b2[0], hy_filt_w3[0], hy_sin_freq[0])
    tab_l = _dft_tables(SEQ)
    tab_c = _dft_tables(CTX_LEN)
    spec_l = _hyena_filter_spectrum(SEQ, *tab_l, *filt)
    spec_c = _hyena_filter_spectrum(CTX_LEN, *tab_c, *filt)
    y_hy = _hyena(hy, tab_l, spec_l, tab_c, spec_c, hy_conv_w[0], hy_conv_b[0], hy_bias[0])
    xs = _ffn_after_even(xs, y_hy, h_sum, gate_in, ev_w_out[0], ln_g[0, 1], ln_b[0, 1],
                         m[:, 5:9], ffn_w, (0, 1), ln_g[0, 2], ln_b[0, 2], T_ALL)

    m = mods[1]
    xs = _ffn_plain(xs, m[:, 0:3], ffn_w, (1, 0), ln_g[1, 0], ln_b[1, 0], T_ALL)
    q, k, v = _mlaproj(xs, m[:, 3:5], mla_w_in[0], mla_q_norm[0], mla_kv_norm[0], mla_w_q_up[0], mla_w_kv_up[0])
    att = _attention(q, k, v)
    xl = _ffn_after_mla(xs, att, mla_w_out[0], ln_g[1, 1], ln_b[1, 1],
                        m[:, 5:9], ffn_w, (1, 1), ln_g[1, 2], ln_b[1, 2], T_LAT)
    return xl.reshape(BATCH, SEQ, D_MODEL)
```

```python
import functools
import math

import jax
import jax.numpy as jnp
import numpy as np
from jax import lax
from jax.experimental import pallas as pl
from jax.experimental.pallas import tpu as pltpu

F32 = jnp.float32
BF16 = jnp.bfloat16

D_MODEL = 1024
BATCH = 8
SEQ = 2048
DEPTH = 2
CTX_LEN = 256
GRID_W = 64
N_ADA = 9
D_FF = 2816

HY_WIDTH = 512
RG_WIDTH = 512
EV_IN = 3 * HY_WIDTH + 2 * RG_WIDTH
HY_EMB = 33
HY_BANDS = (HY_EMB - 1) // 2
HY_FILT_HIDDEN = 64
HY_TARGET = 1e-2
HY_MAX_DECAY = math.log(HY_TARGET) / 0.3
HY_MIN_DECAY = math.log(HY_TARGET) / 1.5
HY_SHIFT = 0.05
RG_BLOCKS = 8
RG_BLOCK_DIM = RG_WIDTH // RG_BLOCKS
RG_C = 8.0

MLA_HEADS = 16
MLA_Q_LORA = 768
MLA_KV_LORA = 256
MLA_NOPE = 64
MLA_ROPE = 32
MLA_V = 64
MLA_SCALE = (MLA_NOPE + MLA_ROPE) ** -0.5
ROPE_AXIS_PAIRS = MLA_ROPE // 4
ROPE_BASE = 10000.0

LOG2_E = math.log2(math.e)
ALPHA = (2.0 * DEPTH) ** 0.25
LN_EPS = 1e-6
RMS_EPS = 1e-6

T_LAT = BATCH * SEQ
T_CTX = BATCH * CTX_LEN
T_ALL = T_LAT + T_CTX
N_GROUPS = BATCH + 1

LANES = 128
SUBLANES = 8
VMEM_LIMIT = 56 * 1024 * 1024

TM_FFN = 512
FF_CHUNK = 256
N_FF_CHUNKS = D_FF // FF_CHUNK
TM_PROJ = 512
HY_CT = 256
HY_FREQ_TILE = 512
RG_CT = 128
RG_TC = 32
RG_HALO = SUBLANES
HEAD_SLOT = 128
TQ = 512
ATTN_HEADS = 16
Q_SCALE = MLA_SCALE * math.log2(math.e)
ADA_TN = 1152


def _dot(a, b):
    return jnp.dot(a, b, preferred_element_type=F32)


def _dot_nt(a, b):
    return lax.dot_general(a, b, (((1,), (1,)), ((), ())), preferred_element_type=F32)


def _resident(shape):
    nd = len(shape)
    return pl.BlockSpec(shape, lambda *_: (0,) * nd, pipeline_mode=pl.Buffered(1))


def _group_of_tile(i, tm):
    return jnp.where(i < T_LAT // tm, i // (SEQ // tm), BATCH)


def _params(*sem):
    return pltpu.CompilerParams(dimension_semantics=sem, vmem_limit_bytes=VMEM_LIMIT)


def _layernorm(z, g, b):
    mu = jnp.mean(z, axis=-1, keepdims=True)
    zc = z - mu
    var = jnp.mean(zc * zc, axis=-1, keepdims=True)
    return zc * lax.rsqrt(var + LN_EPS) * g + b


def _adaln_kernel(c_ref, w_ref, b_ref, o_ref):
    c = c_ref[...]
    s = (c * jax.nn.sigmoid(c)).astype(BF16)
    o_ref[0] = _dot(s, w_ref[0].astype(BF16)) + b_ref[0]


def _adaln(cond, ada_w, ada_b):
    n_out = N_ADA * D_MODEL
    rows = cond.shape[0]
    return pl.pallas_call(
        _adaln_kernel,
        grid=(DEPTH, n_out // ADA_TN),
        in_specs=[
            pl.BlockSpec((rows, D_MODEL), lambda l, j: (0, 0)),
            pl.BlockSpec((1, D_MODEL, ADA_TN), lambda l, j: (l, 0, j)),
            pl.BlockSpec((1, 1, ADA_TN), lambda l, j: (l, 0, j)),
        ],
        out_specs=pl.BlockSpec((1, rows, ADA_TN), lambda l, j: (l, 0, j)),
        out_shape=jax.ShapeDtypeStruct((DEPTH, rows, n_out), F32),
        compiler_params=_params("arbitrary", "arbitrary"),
        name="adaln",
    )(cond, ada_w, ada_b.reshape(DEPTH, 1, n_out))


def _ffn_body(x, mod_ref, wi_ref, wo_ref, g_ref, b_ref, o_ref, act_ref):
    n_slots = mod_ref.shape[1]
    shift, scale, gate = mod_ref[0, n_slots - 3], mod_ref[0, n_slots - 2], mod_ref[0, n_slots - 1]
    h = (x * (1.0 + scale) + shift).astype(BF16)
    for j in range(N_FF_CHUNKS):
        lo = j * FF_CHUNK
        a = _dot(h, wi_ref[:, lo:lo + FF_CHUNK])
        u = _dot(h, wi_ref[:, D_FF + lo:D_FF + lo + FF_CHUNK])
        act_ref[:, lo:lo + FF_CHUNK] = (a * jax.nn.sigmoid(a) * u).astype(BF16)
    y = _dot(act_ref[...], wo_ref[...])
    o_ref[...] = _layernorm(ALPHA * x + (0.5 * gate) * y, g_ref[...], b_ref[...])


def _ffn_kernel(x_ref, *rest):
    _ffn_body(x_ref[...], *rest)


def _ffn_split_kernel(xl_ref, xc_ref, *rest):
    x = jnp.where(pl.program_id(0) < T_LAT // TM_FFN, xl_ref[...], xc_ref[...])
    _ffn_body(x, *rest)


def _ffn_after_even_kernel(x_ref, yh_ref, hs_ref, gt_ref, wmix_ref, g1_ref, b1_ref, mod_ref, *rest):
    y_rg = hs_ref[...] * jax.nn.gelu(gt_ref[...], approximate=True)
    y = (_dot(yh_ref[...].astype(BF16), wmix_ref[:HY_WIDTH, :])
         + _dot(y_rg.astype(BF16), wmix_ref[HY_WIDTH:, :]))
    x = _layernorm(ALPHA * x_ref[...] + mod_ref[0, 0] * y, g1_ref[...], b1_ref[...])
    _ffn_body(x, mod_ref, *rest)


def _ffn_after_mla_kernel(x_ref, att_ref, wmix_ref, g1_ref, b1_ref, mod_ref, *rest):
    y = _dot(att_ref[...], wmix_ref[...])
    x = _layernorm(ALPHA * x_ref[...] + mod_ref[0, 0] * y, g1_ref[...], b1_ref[...])
    _ffn_body(x, mod_ref, *rest)


def _ffn(body, lead_args, lead_specs, mod, ffn_w, which, g, b, n_rows):
    tm = TM_FFN
    layer, half = which
    w_in_all, w_out_all = ffn_w
    return pl.pallas_call(
        body,
        grid=(n_rows // tm,),
        in_specs=[
            *lead_specs,
            pl.BlockSpec((1, mod.shape[1], 1, D_MODEL), lambda i: (_group_of_tile(i, tm), 0, 0, 0)),
            pl.BlockSpec((None, None, D_MODEL, 2 * D_FF), lambda i: (layer, half, 0, 0),
                         pipeline_mode=pl.Buffered(1)),
            pl.BlockSpec((None, None, D_FF, D_MODEL), lambda i: (layer, half, 0, 0),
                         pipeline_mode=pl.Buffered(1)),
            _resident((1, D_MODEL)),
            _resident((1, D_MODEL)),
        ],
        out_specs=pl.BlockSpec((tm, D_MODEL), lambda i: (i, 0)),
        out_shape=jax.ShapeDtypeStruct((n_rows, D_MODEL), F32),
        scratch_shapes=[pltpu.VMEM((tm, D_FF), BF16)],
        compiler_params=_params("arbitrary"),
        name="half_ffn",
    )(*lead_args, mod, w_in_all, w_out_all, g.reshape(1, D_MODEL), b.reshape(1, D_MODEL))


def _row_spec(width, tm=TM_FFN):
    return pl.BlockSpec((tm, width), lambda i: (i, 0))


def _ffn_plain(x, *args):
    return _ffn(_ffn_kernel, (x,), [_row_spec(D_MODEL)], *args)


def _ffn_split(x_lat, x_ctx, *args):
    n_lat = T_LAT // TM_FFN
    specs = [pl.BlockSpec((TM_FFN, D_MODEL), lambda i: (jnp.minimum(i, n_lat - 1), 0)),
             pl.BlockSpec((TM_FFN, D_MODEL), lambda i: (jnp.maximum(i - n_lat, 0), 0))]
    return _ffn(_ffn_split_kernel, (x_lat, x_ctx), specs, *args)


def _ffn_after_even(x, y_hy, h_sum, gate_in, w_mix, g1, b1, *args):
    lead = (x, y_hy, h_sum, gate_in, w_mix.astype(BF16), g1.reshape(1, D_MODEL), b1.reshape(1, D_MODEL))
    specs = [_row_spec(D_MODEL), _row_spec(HY_WIDTH), _row_spec(RG_WIDTH), _row_spec(RG_WIDTH),
             _resident((HY_WIDTH + RG_WIDTH, D_MODEL)), _resident((1, D_MODEL)), _resident((1, D_MODEL))]
    return _ffn(_ffn_after_even_kernel, lead, specs, *args)


def _ffn_after_mla(x, att, w_mix, g1, b1, *args):
    lead = (x, att, w_mix.astype(BF16), g1.reshape(1, D_MODEL), b1.reshape(1, D_MODEL))
    specs = [_row_spec(D_MODEL), _row_spec(MLA_HEADS * MLA_V),
             _resident((MLA_HEADS * MLA_V, D_MODEL)), _resident((1, D_MODEL)), _resident((1, D_MODEL))]
    return _ffn(_ffn_after_mla_kernel, lead, specs, *args)


def _evproj_kernel(x_ref, mod_ref, w_ref, hy_ref, rx_ref, gt_ref):
    shift, scale = mod_ref[0, 0], mod_ref[0, 1]
    h = (x_ref[...] * (1.0 + scale) + shift).astype(BF16)
    n_hy = 3 * HY_WIDTH
    hy_ref[...] = _dot(h, w_ref[:, :n_hy])
    rx_ref[...] = _dot(h, w_ref[:, n_hy:n_hy + RG_WIDTH])
    gt_ref[...] = _dot(h, w_ref[:, n_hy + RG_WIDTH:])


def _evproj(x, mod, w_in):
    tm = TM_PROJ
    row = lambda i: (i, 0)
    return pl.pallas_call(
        _evproj_kernel,
        grid=(T_ALL // tm,),
        in_specs=[
            pl.BlockSpec((tm, D_MODEL), row),
            pl.BlockSpec((1, 2, 1, D_MODEL), lambda i: (_group_of_tile(i, tm), 0, 0, 0)),
            _resident((D_MODEL, EV_IN)),
        ],
        out_specs=[
            pl.BlockSpec((tm, 3 * HY_WIDTH), row),
            pl.BlockSpec((tm, RG_WIDTH), row),
            pl.BlockSpec((tm, RG_WIDTH), row),
        ],
        out_shape=[
            jax.ShapeDtypeStruct((T_ALL, 3 * HY_WIDTH), F32),
            jax.ShapeDtypeStruct((T_ALL, RG_WIDTH), F32),
            jax.ShapeDtypeStruct((T_ALL, RG_WIDTH), F32),
        ],
        compiler_params=_params("arbitrary"),
        name="even_in_proj",
    )(x, mod, w_in.astype(BF16))


def _block_diag_tiles(w):
    per = RG_CT // RG_BLOCK_DIM
    w = w.reshape(RG_WIDTH // RG_CT, per, RG_BLOCK_DIM, RG_BLOCK_DIM)
    eye = jnp.eye(per, dtype=w.dtype)
    return jnp.einsum('cpde,pq->cpdqe', w, eye).reshape(RG_WIDTH // RG_CT, RG_CT, RG_CT)


def _rg_kernel(rx_ref, cw_ref, cb_ref, w_ref, bias_ref, lam_ref, out_ref,
                padl_ref, padc_ref, xt_f_ref, xt_b_ref, coef_a_ref, coef_b_ref, hbuf_f_ref, hbuf_b_ref):
    neg_lam = -lam_ref[...]
    softplus = jnp.maximum(neg_lam, 0.0) + jnp.log1p(jnp.exp(-jnp.abs(neg_lam)))
    neg_log_a_scale = RG_C * softplus
    a_exp2_scale = -LOG2_E * neg_log_a_scale
    cb = cb_ref[...]
    xts = (xt_f_ref, xt_b_ref)
    hbuf = (hbuf_f_ref, hbuf_b_ref)
    slab_rows = RG_TC + 2 * RG_HALO

    def step_rows(s):
        return slice(s * SUBLANES, (s + 1) * SUBLANES)

    halo = jnp.zeros((BATCH, RG_HALO, RG_CT), F32)
    for pad_ref, length in ((padl_ref, SEQ), (padc_ref, CTX_LEN)):
        pad_ref[:, 0:RG_HALO, :] = halo
        pad_ref[:, RG_HALO + length:, :] = halo
    padl_ref[:, RG_HALO:RG_HALO + SEQ, :] = rx_ref[0:BATCH]
    for b in range(BATCH):
        padc_ref[b, RG_HALO:RG_HALO + CTX_LEN, :] = rx_ref[BATCH, b * CTX_LEN:(b + 1) * CTX_LEN, :]

    def add_out(latent, t_start, d):
        t_start = pl.multiple_of(t_start, SUBLANES)
        for b in range(BATCH):
            rows = hbuf[d][pl.ds(b, RG_TC, stride=SUBLANES), :]
            if latent:
                out_ref[b, pl.ds(t_start, RG_TC), :] += rows
            else:
                out_ref[BATCH, pl.ds(b * CTX_LEN + t_start, RG_TC), :] += rows

    def sigmoid(z):
        return 1.0 / (1.0 + jnp.exp2(z * (-LOG2_E)))

    def coeffs(latent, length, t0, d, dst_ref):
        t0 = pl.multiple_of(jnp.asarray(t0, jnp.int32), SUBLANES)
        pad_ref = padl_ref if latent else padc_ref
        xt = xts[d]
        for b in range(BATCH):
            xt[pl.ds(b, slab_rows, stride=SUBLANES), :] = pad_ref[b, pl.ds(t0, slab_rows), :]
        x2 = cb
        for k in range(4):
            start = (RG_HALO - 2 + k) * SUBLANES
            x2 = x2 + xt[start:start + RG_TC * SUBLANES, :] * cw_ref[k:k + 1, :]
        xb = x2.astype(BF16)
        r = sigmoid(_dot(xb, w_ref[0, 2 * d]) + bias_ref[2 * d:2 * d + 1, :])
        i = sigmoid(_dot(xb, w_ref[0, 2 * d + 1]) + bias_ref[2 * d + 1:2 * d + 2, :])
        a = jnp.exp2(r * a_exp2_scale[d:d + 1, :])
        b = jnp.sqrt(jnp.tanh(r * neg_log_a_scale[d:d + 1, :]) * (a * a + 1.0)) * (i * x2)
        dst_ref[d, 0] = a
        dst_ref[d, 1] = b

    def scan(latent, tf, tb, src_ref, hf, hb):
        tf = jnp.asarray(tf, jnp.int32)
        tb = jnp.asarray(tb, jnp.int32)
        for s in range(RG_TC):
            rows = step_rows(s)
            hf = src_ref[0, 0, rows, :] * hf + src_ref[0, 1, rows, :]
            hbuf_f_ref[rows, :] = hf
            rows = step_rows(RG_TC - 1 - s)
            hb = src_ref[1, 0, rows, :] * hb + src_ref[1, 1, rows, :]
            hbuf_b_ref[rows, :] = hb
        add_out(latent, tf, 0)
        add_out(latent, tb, 1)
        return hf, hb

    def sweep(latent, length, h_fwd, h_bwd):
        n_chunks = length // RG_TC
        last = n_chunks - 1

        def both(j, dst_ref):
            coeffs(latent, length, j * RG_TC, 0, dst_ref)
            coeffs(latent, length, (last - j) * RG_TC, 1, dst_ref)

        def body(i, carry):
            j = 2 * i
            both(j + 1, coef_b_ref)
            carry = scan(latent, j * RG_TC, (last - j) * RG_TC, coef_a_ref, *carry)
            both(jnp.minimum(j + 2, last), coef_a_ref)
            return scan(latent, (j + 1) * RG_TC, (last - j - 1) * RG_TC, coef_b_ref, *carry)

        both(0, coef_a_ref)
        return lax.fori_loop(0, n_chunks // 2, body, (h_fwd, h_bwd))

    out_ref[...] = jnp.zeros_like(out_ref)
    zero = jnp.zeros((SUBLANES, RG_CT), F32)
    hf, hb = sweep(False, CTX_LEN, zero, zero)
    sweep(True, SEQ, hf, hb)


def _rglru(rx, conv_w, conv_b, a_w, a_b, x_w, x_b, lam):
    assert T_CTX == SEQ, "the context rows must form exactly one latent-sized row block"
    w = jnp.stack([_block_diag_tiles(a_w[0]), _block_diag_tiles(x_w[0]),
                   _block_diag_tiles(a_w[1]), _block_diag_tiles(x_w[1])], axis=1).astype(BF16)
    bias = jnp.stack([a_b[0], x_b[0], a_b[1], x_b[1]], axis=0)
    n_ct = RG_WIDTH // RG_CT
    n_blocks = BATCH + 1
    chan = lambda c: (0, 0, c)
    lane = lambda c: (0, c)
    coef = pltpu.VMEM((2, 2, RG_TC * BATCH, RG_CT), F32)
    slab = pltpu.VMEM(((RG_TC + 2 * RG_HALO) * BATCH, RG_CT), F32)
    chunk = pltpu.VMEM((RG_TC * BATCH, RG_CT), F32)
    out = pl.pallas_call(
        _rg_kernel,
        grid=(n_ct,),
        in_specs=[
            pl.BlockSpec((n_blocks, SEQ, RG_CT), chan),
            pl.BlockSpec((4, RG_CT), lane),
            pl.BlockSpec((1, RG_CT), lane),
            pl.BlockSpec((1, 4, RG_CT, RG_CT), lambda c: (c, 0, 0, 0)),
            pl.BlockSpec((4, RG_CT), lane),
            pl.BlockSpec((2, RG_CT), lane),
        ],
        out_specs=pl.BlockSpec((n_blocks, SEQ, RG_CT), chan),
        out_shape=jax.ShapeDtypeStruct((n_blocks, SEQ, RG_WIDTH), F32),
        scratch_shapes=[
            pltpu.VMEM((BATCH, SEQ + 2 * RG_HALO, RG_CT), F32),
            pltpu.VMEM((BATCH, CTX_LEN + 2 * RG_HALO, RG_CT), F32),
            slab, slab,
            coef, coef,
            chunk, chunk,
        ],
        compiler_params=_params("arbitrary"),
        name="rglru_scan",
    )(rx.reshape(n_blocks, SEQ, RG_WIDTH), conv_w, conv_b.reshape(1, RG_WIDTH), w, bias, lam)
    return out.reshape(T_ALL, RG_WIDTH)


def _dft_tables(length):
    step = 1 << (length.bit_length() // 2)
    s = np.arange(length, dtype=np.int64)

    def thin(kvec):
        ang = ((kvec[:, None] * s[None, :]) % (2 * length)) * (math.pi / length)
        return (jnp.asarray(np.cos(ang)[:, None, :], dtype=F32),
                jnp.asarray(np.sin(ang)[:, None, :], dtype=F32))

    c_hi, s_hi = thin(np.arange(0, length, step, dtype=np.int64))
    c_lo, s_lo = thin(np.arange(step, dtype=np.int64))
    c_lo, s_lo = c_lo.reshape(1, step, length), s_lo.reshape(1, step, length)
    fc = (c_hi * c_lo - s_hi * s_lo).reshape(length, length)
    fs = (s_hi * c_lo + c_hi * s_lo).reshape(length, length)
    return fc.astype(BF16), fs.astype(BF16)


def _filter_features(length):
    f32 = np.float32
    pos = np.arange(length, dtype=f32)
    t = pos / f32(length)
    bands = np.linspace(1e-4, HY_BANDS - 1, HY_BANDS, dtype=f32)
    ang = (f32(2.0 * math.pi) * pos / f32(length))[:, None] * bands[None, :]
    feats = np.concatenate([t[:, None], np.cos(ang), -np.sin(ang)], axis=-1).astype(f32)
    feats = np.pad(feats, ((0, 0), (0, LANES - HY_EMB)))
    deltas = np.abs(np.linspace(HY_MIN_DECAY, HY_MAX_DECAY, HY_WIDTH, dtype=f32))
    window = (np.exp(-t[:, None] * deltas[None, :]) + f32(HY_SHIFT)).astype(f32)
    return jnp.asarray(feats), jnp.asarray(window)


def _filter_kernel(feat_ref, w1_ref, b1_ref, w2_ref, b2_ref, w3f_ref, w3b_ref, sf_ref, win_ref,
                   fc_ref, fs_ref, kc_ref, ks_ref, kn_ref, *, length):
    n = 2 * length
    h = jnp.sin(sf_ref[0:1, :] * (_dot(feat_ref[...].astype(BF16), w1_ref[...]) + b1_ref[...]))
    h = jnp.sin(sf_ref[1:2, :] * (_dot(h.astype(BF16), w2_ref[...]) + b2_ref[...]))
    hb16 = h.astype(BF16)
    win = win_ref[...]
    row = lax.broadcasted_iota(jnp.int32, (length, HY_CT), 0)
    h_fwd = _dot(hb16, w3f_ref[...]) * win
    h_bwd0 = jnp.where(row == 0, 0.0, _dot(hb16, w3b_ref[...]) * win)
    h_sum = h_fwd + h_bwd0
    h_dif = h_fwd - h_bwd0
    weight = jnp.where(row == 0, 1.0 / n, 2.0 / n)
    kc_ref[...] = _dot(fc_ref[...], h_sum.astype(BF16)) * weight
    ks_ref[...] = _dot(fs_ref[...], h_dif.astype(BF16)) * weight
    sign = jnp.where((row & 1) == 0, 1.0, -1.0)
    kn_ref[...] = jnp.sum(h_sum * sign, axis=0, keepdims=True) * (1.0 / n)


def _hyena_filter_spectrum(length, fc, fs, w1, b1, w2, b2, w3, sin_freq):
    feats, window = _filter_features(length)
    hp = LANES - HY_FILT_HIDDEN
    w1p = jnp.pad(w1, ((0, LANES - HY_EMB), (0, hp))).astype(BF16)
    w2p = jnp.pad(w2, ((0, hp), (0, hp))).astype(BF16)
    w3p = jnp.pad(w3, ((0, hp), (0, 0))).astype(BF16)
    b1p = jnp.pad(b1, (0, hp)).reshape(1, LANES)
    b2p = jnp.pad(b2, (0, hp)).reshape(1, LANES)
    sfp = jnp.pad(sin_freq, ((0, 0), (0, hp)))
    spec = jax.ShapeDtypeStruct((length, HY_WIDTH), F32)
    chan = lambda c: (0, c)
    return pl.pallas_call(
        functools.partial(_filter_kernel, length=length),
        grid=(HY_WIDTH // HY_CT,),
        in_specs=[
            _resident((length, LANES)),
            _resident((LANES, LANES)),
            _resident((1, LANES)),
            _resident((LANES, LANES)),
            _resident((1, LANES)),
            pl.BlockSpec((LANES, HY_CT), chan),
            pl.BlockSpec((LANES, HY_CT), chan),
            _resident((2, LANES)),
            pl.BlockSpec((length, HY_CT), chan),
            _resident((length, length)),
            _resident((length, length)),
        ],
        out_specs=[
            pl.BlockSpec((length, HY_CT), chan),
            pl.BlockSpec((length, HY_CT), chan),
            pl.BlockSpec((1, HY_CT), chan),
        ],
        out_shape=[spec, spec, jax.ShapeDtypeStruct((1, HY_WIDTH), F32)],
        compiler_params=_params("arbitrary"),
        name=f"hyena_filter_{length}",
    )(feats, w1p, b1p, w2p, b2p, w3p[:, :HY_WIDTH], w3p[:, HY_WIDTH:], sfp, window, fc, fs)


def _hyena_sequence(raw, cw_ref, cb_ref, bias_ref, fc_ref, fs_ref, kc_ref, ks_ref, kn_ref, length):
    row = lax.broadcasted_iota(jnp.int32, (length, HY_CT), 0)

    def short_conv(x, part):
        prev = jnp.where(row == 0, 0.0, pltpu.roll(x, 1, 0))
        nxt = jnp.where(row == length - 1, 0.0, pltpu.roll(x, length - 1, 0))
        return (prev * cw_ref[part, 0:1, :] + x * cw_ref[part, 1:2, :] + nxt * cw_ref[part, 2:3, :]
                + cb_ref[part:part + 1, :])

    x0 = short_conv(raw[0], 0)
    x1 = short_conv(raw[1], 1)
    v = short_conv(raw[2], 2)
    vx = v * x1
    vb = vx.astype(BF16)
    ft = min(length, HY_FREQ_TILE)
    y = None
    for f0 in range(0, length, ft):
        spec_c = _dot(fc_ref[f0:f0 + ft, :], vb)
        spec_s = _dot(fs_ref[f0:f0 + ft, :], vb)
        kc = kc_ref[f0:f0 + ft, :]
        ks = ks_ref[f0:f0 + ft, :]
        p = (spec_c * kc - spec_s * ks).astype(BF16)
        q = (spec_c * ks + spec_s * kc).astype(BF16)
        part = _dot(fc_ref[:, f0:f0 + ft], p) + _dot(fs_ref[:, f0:f0 + ft], q)
        y = part if y is None else y + part
    sign = jnp.where((row & 1) == 0, 1.0, -1.0)
    nyquist = jnp.sum(vx * sign, axis=0, keepdims=True) * kn_ref[...]
    return x0 * (y + sign * nyquist + vx * bias_ref[...])


def _hyena_kernel(x0_ref, x1_ref, v_ref, cw_ref, cb_ref, bias_ref,
                  fcl_ref, fsl_ref, kcl_ref, ksl_ref, knl_ref,
                  fcc_ref, fsc_ref, kcc_ref, ksc_ref, knc_ref, o_ref):
    step = pl.program_id(1)
    common = (cw_ref, cb_ref, bias_ref)

    @pl.when(step < BATCH)
    def _():
        raw = (x0_ref[...], x1_ref[...], v_ref[...])
        o_ref[...] = _hyena_sequence(raw, *common, fcl_ref, fsl_ref, kcl_ref, ksl_ref, knl_ref, SEQ)

    @pl.when(step == BATCH)
    def _():
        for r in range(BATCH):
            rows = slice(r * CTX_LEN, (r + 1) * CTX_LEN)
            raw = (x0_ref[rows, :], x1_ref[rows, :], v_ref[rows, :])
            o_ref[rows, :] = _hyena_sequence(raw, *common, fcc_ref, fsc_ref, kcc_ref, ksc_ref, knc_ref, CTX_LEN)


def _hyena(hy, tab_lat, spec_lat, tab_ctx, spec_ctx, conv_w, conv_b, bias):
    assert T_CTX == SEQ, "the context rows must form exactly one latent-sized row block"
    n_ct = HY_WIDTH // HY_CT
    cw = conv_w.reshape(3, 3, HY_WIDTH).transpose(1, 0, 2)
    cb = conv_b.reshape(3, HY_WIDTH)

    def part_spec(part):
        return pl.BlockSpec((SEQ, HY_CT), lambda c, b: (b, part * n_ct + c))

    chan = lambda c, b: (0, c)

    def table_specs(length):
        return [
            _resident((length, length)),
            _resident((length, length)),
            pl.BlockSpec((length, HY_CT), chan, pipeline_mode=pl.Buffered(1)),
            pl.BlockSpec((length, HY_CT), chan, pipeline_mode=pl.Buffered(1)),
            pl.BlockSpec((1, HY_CT), chan),
        ]

    return pl.pallas_call(
        _hyena_kernel,
        grid=(n_ct, BATCH + 1),
        in_specs=[
            part_spec(0), part_spec(1), part_spec(2),
            pl.BlockSpec((3, 3, HY_CT), lambda c, b: (0, 0, c)),
            pl.BlockSpec((3, HY_CT), chan),
            pl.BlockSpec((1, HY_CT), chan),
            *table_specs(SEQ),
            *table_specs(CTX_LEN),
        ],
        out_specs=pl.BlockSpec((SEQ, HY_CT), lambda c, b: (b, c)),
        out_shape=jax.ShapeDtypeStruct((T_ALL, HY_WIDTH), F32),
        compiler_params=_params("arbitrary", "arbitrary"),
        name="hyena_conv",
    )(hy, hy, hy, cw, cb, bias.reshape(1, HY_WIDTH), *tab_lat, *spec_lat, *tab_ctx, *spec_ctx)


def _rope_tables():
    f32 = np.float32
    rows = np.repeat(np.arange(SEQ // GRID_W, dtype=f32), GRID_W)
    cols = np.tile(np.arange(GRID_W, dtype=f32), SEQ // GRID_W)
    inv_freq = (f32(ROPE_BASE) ** (-np.arange(ROPE_AXIS_PAIRS, dtype=f32) / f32(ROPE_AXIS_PAIRS))).astype(f32)
    ang_r = rows[:, None] * inv_freq
    ang_c = cols[:, None] * inv_freq
    cos32 = np.concatenate([np.cos(ang_r)] * 2 + [np.cos(ang_c)] * 2, axis=1)
    sin32 = np.concatenate([-np.sin(ang_r), np.sin(ang_r), -np.sin(ang_c), np.sin(ang_c)], axis=1)
    ones = np.ones((SEQ, MLA_NOPE), f32)
    zeros = np.zeros((SEQ, MLA_NOPE), f32)
    pad = np.zeros((SEQ, HEAD_SLOT - MLA_NOPE - MLA_ROPE), f32)
    first = (np.arange(MLA_ROPE) % (2 * ROPE_AXIS_PAIRS)) < ROPE_AXIS_PAIRS
    q_cos = np.concatenate([ones, cos32, pad], axis=1) * f32(Q_SCALE)
    q_sin_up = np.concatenate([zeros, np.where(first, sin32, 0.0), pad], axis=1) * f32(Q_SCALE)
    q_sin_dn = np.concatenate([zeros, np.where(first, 0.0, sin32), pad], axis=1) * f32(Q_SCALE)
    k_cos = np.concatenate([zeros, cos32, pad], axis=1)
    k_sin = np.concatenate([zeros, sin32, pad], axis=1)
    ident = np.zeros((TM_PROJ, HEAD_SLOT), f32)
    ident[:, MLA_NOPE:MLA_NOPE + MLA_ROPE] = 1.0
    zero = np.zeros((TM_PROJ, HEAD_SLOT), f32)
    tables = ((q_cos, zero), (q_sin_up, zero), (q_sin_dn, zero), (k_cos, ident), (k_sin, zero))
    return tuple(jnp.asarray(np.concatenate(t, 0).astype(f32)) for t in tables)


def _rope_partner(n):
    idx = np.arange(n)
    return np.where((idx % (2 * ROPE_AXIS_PAIRS)) < ROPE_AXIS_PAIRS, idx + ROPE_AXIS_PAIRS, idx - ROPE_AXIS_PAIRS)


def _mla_weights(w_in, w_q_up, w_kv_up):
    w_q = w_in[:, :MLA_Q_LORA]
    w_kv = w_in[:, MLA_Q_LORA:MLA_Q_LORA + MLA_KV_LORA]
    w_kr = w_in[:, MLA_Q_LORA + MLA_KV_LORA:]
    part = _rope_partner(MLA_ROPE)
    lane_pad = ((0, 0), (MLA_NOPE, HEAD_SLOT - MLA_NOPE - MLA_ROPE))
    w_kr2 = jnp.concatenate([jnp.pad(w_kr, lane_pad), jnp.pad(w_kr[:, part], lane_pad)], axis=1)

    qh = w_q_up.reshape(MLA_Q_LORA, MLA_HEADS, MLA_NOPE + MLA_ROPE)
    slot_pad = ((0, 0), (0, 0), (0, HEAD_SLOT - MLA_NOPE - MLA_ROPE))
    wq_slot = jnp.pad(qh, slot_pad).reshape(MLA_Q_LORA, MLA_HEADS * HEAD_SLOT)

    kvh = w_kv_up.reshape(MLA_KV_LORA, MLA_HEADS, MLA_NOPE + MLA_V)
    wk_slot = jnp.pad(kvh[..., :MLA_NOPE], ((0, 0), (0, 0), (0, HEAD_SLOT - MLA_NOPE)))
    wk_slot = wk_slot.reshape(MLA_KV_LORA, MLA_HEADS * HEAD_SLOT)
    wv_slot = jnp.pad(kvh[..., MLA_NOPE:], ((0, 0), (0, 0), (0, HEAD_SLOT - MLA_V)))
    wv_slot = wv_slot.reshape(MLA_KV_LORA, MLA_HEADS * HEAD_SLOT)
    return [w.astype(BF16) for w in (w_q, w_kv, w_kr2, wq_slot, wk_slot, wv_slot)]


def _mlaproj_kernel(x_ref, mod_ref, wq_ref, wkv_ref, wkr_ref, wqs_ref, wks_ref, wv_ref,
                    vone_ref, qg_ref, kvg_ref, qcos_ref, qsup_ref, qsdn_ref, kcos_ref, ksin_ref,
                    q_ref, k_ref, v_ref):
    shift, scale = mod_ref[0, 0], mod_ref[0, 1]
    h = (x_ref[...] * (1.0 + scale) + shift).astype(BF16)

    def rmsnorm(y, g):
        return (y * lax.rsqrt(jnp.mean(y * y, axis=-1, keepdims=True) + RMS_EPS) * g).astype(BF16)

    qn = rmsnorm(_dot(h, wq_ref[...]), qg_ref[...])
    kvn = rmsnorm(_dot(h, wkv_ref[...]), kvg_ref[...])
    kr2 = _dot(h, wkr_ref[...])
    k_rope = kr2[:, :HEAD_SLOT] * kcos_ref[...] + kr2[:, HEAD_SLOT:] * ksin_ref[...]
    k_nope = _dot(kvn, wks_ref[...])
    v_ref[...] = (_dot(kvn, wv_ref[...]) + vone_ref[...]).astype(BF16)
    q_all = _dot(qn, wqs_ref[...])
    q_cos = qcos_ref[...]
    q_sin_up = qsup_ref[...]
    q_sin_dn = qsdn_ref[...]
    for hd in range(MLA_HEADS):
        sl = slice(hd * HEAD_SLOT, (hd + 1) * HEAD_SLOT)
        qh = q_all[:, sl]
        rotated = (qh * q_cos + pltpu.roll(qh, HEAD_SLOT - ROPE_AXIS_PAIRS, 1) * q_sin_up
                   + pltpu.roll(qh, ROPE_AXIS_PAIRS, 1) * q_sin_dn)
        q_ref[:, sl] = rotated.astype(BF16)
        k_ref[:, sl] = (k_nope[:, sl] + k_rope).astype(BF16)


def _mlaproj(x, mod, w_in, q_norm, kv_norm, w_q_up, w_kv_up):
    tm = TM_PROJ
    weights = _mla_weights(w_in, w_q_up, w_kv_up)
    tables = _rope_tables()
    row = lambda i: (i, 0)
    tab = lambda i: (jnp.where(i < T_LAT // tm, i % (SEQ // tm), SEQ // tm), 0)
    wide = MLA_HEADS * HEAD_SLOT
    v_one = jnp.asarray((np.arange(wide) % HEAD_SLOT == MLA_V).astype(np.float32).reshape(1, wide))
    return pl.pallas_call(
        _mlaproj_kernel,
        grid=(T_ALL // tm,),
        in_specs=[
            pl.BlockSpec((tm, D_MODEL), row),
            pl.BlockSpec((1, 2, 1, D_MODEL), lambda i: (_group_of_tile(i, tm), 0, 0, 0)),
            *[_resident(w.shape) for w in weights],
            _resident((1, wide)),
            _resident((1, MLA_Q_LORA)),
            _resident((1, MLA_KV_LORA)),
            *[pl.BlockSpec((tm, HEAD_SLOT), tab) for _ in tables],
        ],
        out_specs=[pl.BlockSpec((tm, wide), row)] * 3,
        out_shape=[jax.ShapeDtypeStruct((T_ALL, wide), BF16)] * 3,
        compiler_params=_params("arbitrary"),
        name="mla_proj",
    )(x, mod, *weights, v_one, q_norm.reshape(1, MLA_Q_LORA), kv_norm.reshape(1, MLA_KV_LORA), *tables)


def _attn_kernel(q_ref, kl_ref, kc_ref, vl_ref, vc_ref, o_ref):
    lane = lax.broadcasted_iota(jnp.int32, (TQ, HEAD_SLOT), 1)
    outs = []
    for hd in range(ATTN_HEADS):
        sl = slice(hd * HEAD_SLOT, (hd + 1) * HEAD_SLOT)
        q = q_ref[:, sl]
        s_lat = _dot_nt(q, kl_ref[:, sl])
        s_ctx = _dot_nt(q, kc_ref[:, sl])
        m = jnp.maximum(jnp.max(s_lat, axis=-1, keepdims=True), jnp.max(s_ctx, axis=-1, keepdims=True))
        p_lat = jnp.exp2(s_lat - m).astype(BF16)
        p_ctx = jnp.exp2(s_ctx - m).astype(BF16)
        o = _dot(p_lat, vl_ref[:, sl]) + _dot(p_ctx, vc_ref[:, sl])
        outs.append(o / o[:, MLA_V:MLA_V + 1])
    for pair in range(ATTN_HEADS // 2):
        packed = jnp.where(lane < MLA_V, outs[2 * pair], pltpu.roll(outs[2 * pair + 1], MLA_V, 1))
        o_ref[:, pair * HEAD_SLOT:(pair + 1) * HEAD_SLOT] = packed.astype(BF16)


def _attention(q, k, v):
    n_q = SEQ // TQ
    n_groups = MLA_HEADS // ATTN_HEADS
    ctx_block0 = T_LAT // CTX_LEN
    wide = ATTN_HEADS * HEAD_SLOT
    lat = lambda b, h, i: (b, h)
    ctx = lambda b, h, i: (ctx_block0 + b, h)
    return pl.pallas_call(
        _attn_kernel,
        grid=(BATCH, n_groups, n_q),
        in_specs=[
            pl.BlockSpec((TQ, wide), lambda b, h, i: (b * n_q + i, h)),
            pl.BlockSpec((SEQ, wide), lat),
            pl.BlockSpec((CTX_LEN, wide), ctx),
            pl.BlockSpec((SEQ, wide), lat),
            pl.BlockSpec((CTX_LEN, wide), ctx),
        ],
        out_specs=pl.BlockSpec((TQ, ATTN_HEADS * MLA_V), lambda b, h, i: (b * n_q + i, h)),
        out_shape=jax.ShapeDtypeStruct((T_LAT, MLA_HEADS * MLA_V), BF16),
        compiler_params=_params("arbitrary", "arbitrary", "arbitrary"),
        name="mla_attention",
    )(q, k, k, v, v)


def kernel(x, c, ctx, c_ctx, ada_w, ada_b, ln_g, ln_b, ffn_w_in, ffn_w_out, ev_w_in, ev_w_out, hy_conv_w, hy_conv_b, hy_filt_w1, hy_filt_b1, hy_filt_w2, hy_filt_b2, hy_filt_w3, hy_sin_freq, hy_bias, rg_conv_w, rg_conv_b, rg_a_w, rg_a_b, rg_x_w, rg_x_b, rg_lambda, mla_w_in, mla_q_norm, mla_kv_norm, mla_w_q_up, mla_w_kv_up, mla_w_out):
    assert x.shape == (BATCH, SEQ, D_MODEL) and ctx.shape == (BATCH, CTX_LEN, D_MODEL)
    cond = jnp.concatenate([c, c_ctx[None], jnp.zeros((2 * SUBLANES - N_GROUPS, D_MODEL), F32)], axis=0)
    mods = _adaln(cond, ada_w, ada_b)[:, :N_GROUPS].reshape(DEPTH, N_GROUPS, N_ADA, 1, D_MODEL)
    ffn_w = (ffn_w_in.astype(BF16), ffn_w_out.astype(BF16))

    m = mods[0]
    xs = _ffn_split(x.reshape(T_LAT, D_MODEL), ctx.reshape(T_CTX, D_MODEL),
                    m[:, 0:3], ffn_w, (0, 0), ln_g[0, 0], ln_b[0, 0], T_ALL)
    hy, rx, gate_in = _evproj(xs, m[:, 3:5], ev_w_in[0])
    h_sum = _rglru(rx, rg_conv_w[0], rg_conv_b[0], rg_a_w[0], rg_a_b[0], rg_x_w[0], rg_x_b[0], rg_lambda[0])
    filt = (hy_filt_w1[0], hy_filt_b1[0], hy_filt_w2[0], hy_filt_b2[0], hy_filt_w3[0], hy_sin_freq[0])
    tab_l = _dft_tables(SEQ)
    tab_c = _dft_tables(CTX_LEN)
    spec_l = _hyena_filter_spectrum(SEQ, *tab_l, *filt)
    spec_c = _hyena_filter_spectrum(CTX_LEN, *tab_c, *filt)
    y_hy = _hyena(hy, tab_l, spec_l, tab_c, spec_c, hy_conv_w[0], hy_conv_b[0], hy_bias[0])
    xs = _ffn_after_even(xs, y_hy, h_sum, gate_in, ev_w_out[0], ln_g[0, 1], ln_b[0, 1],
                         m[:, 5:9], ffn_w, (0, 1), ln_g[0, 2], ln_b[0, 2], T_ALL)

    m = mods[1]
    xs = _ffn_plain(xs, m[:, 0:3], ffn_w, (1, 0), ln_g[1, 0], ln_b[1, 0], T_ALL)
    q, k, v = _mlaproj(xs, m[:, 3:5], mla_w_in[0], mla_q_norm[0], mla_kv_norm[0], mla_w_q_up[0], mla_w_kv_up[0])
    att = _attention(q, k, v)
    xl = _ffn_after_mla(xs, att, mla_w_out[0], ln_g[1, 1], ln_b[1, 1],
                        m[:, 5:9], ffn_w, (1, 1), ln_g[1, 2], ln_b[1, 2], T_LAT)
    return xl.reshape(BATCH, SEQ, D_MODEL)
```

```python
import functools
import math

import jax
import jax.numpy as jnp
import numpy as np
from jax import lax
from jax.experimental import pallas as pl
from jax.experimental.pallas import tpu as pltpu

F32 = jnp.float32
BF16 = jnp.bfloat16

D_MODEL = 1024
BATCH = 8
SEQ = 2048
DEPTH = 2
CTX_LEN = 256
GRID_W = 64
N_ADA = 9
D_FF = 2816

HY_WIDTH = 512
RG_WIDTH = 512
EV_IN = 3 * HY_WIDTH + 2 * RG_WIDTH
HY_EMB = 33
HY_BANDS = (HY_EMB - 1) // 2
HY_FILT_HIDDEN = 64
HY_TARGET = 1e-2
HY_MAX_DECAY = math.log(HY_TARGET) / 0.3
HY_MIN_DECAY = math.log(HY_TARGET) / 1.5
HY_SHIFT = 0.05
RG_BLOCKS = 8
RG_BLOCK_DIM = RG_WIDTH // RG_BLOCKS
RG_C = 8.0

MLA_HEADS = 16
MLA_Q_LORA = 768
MLA_KV_LORA = 256
MLA_NOPE = 64
MLA_ROPE = 32
MLA_V = 64
MLA_SCALE = (MLA_NOPE + MLA_ROPE) ** -0.5
ROPE_AXIS_PAIRS = MLA_ROPE // 4
ROPE_BASE = 10000.0

LOG2_E = math.log2(math.e)
ALPHA = (2.0 * DEPTH) ** 0.25
LN_EPS = 1e-6
RMS_EPS = 1e-6

T_LAT = BATCH * SEQ
T_CTX = BATCH * CTX_LEN
T_ALL = T_LAT + T_CTX
N_GROUPS = BATCH + 1

LANES = 128
SUBLANES = 8
VMEM_LIMIT = 56 * 1024 * 1024

TM_FFN = 512
FF_CHUNK = 256
N_FF_CHUNKS = D_FF // FF_CHUNK
TM_PROJ = 512
HY_CT = 256
HY_FREQ_TILE = 256
RG_CT = 128
RG_TC = 32
RG_HALO = SUBLANES
HEAD_SLOT = 128
TQ = 512
ATTN_HEADS = 16
Q_SCALE = MLA_SCALE * math.log2(math.e)
ADA_TN = 1152


def _dot(a, b):
    return jnp.dot(a, b, preferred_element_type=F32)


def _dot_nt(a, b):
    return lax.dot_general(a, b, (((1,), (1,)), ((), ())), preferred_element_type=F32)


def _resident(shape):
    nd = len(shape)
    return pl.BlockSpec(shape, lambda *_: (0,) * nd, pipeline_mode=pl.Buffered(1))


def _group_of_tile(i, tm):
    return jnp.where(i < T_LAT // tm, i // (SEQ // tm), BATCH)


def _params(*sem):
    return pltpu.CompilerParams(dimension_semantics=sem, vmem_limit_bytes=VMEM_LIMIT)


def _layernorm(z, g, b):
    mu = jnp.mean(z, axis=-1, keepdims=True)
    zc = z - mu
    var = jnp.mean(zc * zc, axis=-1, keepdims=True)
    return zc * lax.rsqrt(var + LN_EPS) * g + b


def _adaln_kernel(c_ref, w_ref, b_ref, o_ref):
    c = c_ref[...]
    s = (c * jax.nn.sigmoid(c)).astype(BF16)
    o_ref[0] = _dot(s, w_ref[0].astype(BF16)) + b_ref[0]


def _adaln(cond, ada_w, ada_b):
    n_out = N_ADA * D_MODEL
    rows = cond.shape[0]
    return pl.pallas_call(
        _adaln_kernel,
        grid=(DEPTH, n_out // ADA_TN),
        in_specs=[
            pl.BlockSpec((rows, D_MODEL), lambda l, j: (0, 0)),
            pl.BlockSpec((1, D_MODEL, ADA_TN), lambda l, j: (l, 0, j)),
            pl.BlockSpec((1, 1, ADA_TN), lambda l, j: (l, 0, j)),
        ],
        out_specs=pl.BlockSpec((1, rows, ADA_TN), lambda l, j: (l, 0, j)),
        out_shape=jax.ShapeDtypeStruct((DEPTH, rows, n_out), F32),
        compiler_params=_params("arbitrary", "arbitrary"),
        name="adaln",
    )(cond, ada_w, ada_b.reshape(DEPTH, 1, n_out))


def _ffn_body(x, mod_ref, wi_ref, wo_ref, g_ref, b_ref, o_ref, act_ref):
    n_slots = mod_ref.shape[1]
    shift, scale, gate = mod_ref[0, n_slots - 3], mod_ref[0, n_slots - 2], mod_ref[0, n_slots - 1]
    h = (x * (1.0 + scale) + shift).astype(BF16)
    for j in range(N_FF_CHUNKS):
        lo = j * FF_CHUNK
        a = _dot(h, wi_ref[:, lo:lo + FF_CHUNK])
        u = _dot(h, wi_ref[:, D_FF + lo:D_FF + lo + FF_CHUNK])
        act_ref[:, lo:lo + FF_CHUNK] = (a * jax.nn.sigmoid(a) * u).astype(BF16)
    y = _dot(act_ref[...], wo_ref[...])
    o_ref[...] = _layernorm(ALPHA * x + (0.5 * gate) * y, g_ref[...], b_ref[...])


def _ffn_kernel(x_ref, *rest):
    _ffn_body(x_ref[...], *rest)


def _ffn_split_kernel(xl_ref, xc_ref, *rest):
    x = jnp.where(pl.program_id(0) < T_LAT // TM_FFN, xl_ref[...], xc_ref[...])
    _ffn_body(x, *rest)


def _ffn_after_even_kernel(x_ref, yh_ref, hs_ref, gt_ref, wmix_ref, g1_ref, b1_ref, mod_ref, *rest):
    y_rg = hs_ref[...] * jax.nn.gelu(gt_ref[...], approximate=True)
    y = (_dot(yh_ref[...].astype(BF16), wmix_ref[:HY_WIDTH, :])
         + _dot(y_rg.astype(BF16), wmix_ref[HY_WIDTH:, :]))
    x = _layernorm(ALPHA * x_ref[...] + mod_ref[0, 0] * y, g1_ref[...], b1_ref[...])
    _ffn_body(x, mod_ref, *rest)


def _ffn_after_mla_kernel(x_ref, att_ref, wmix_ref, g1_ref, b1_ref, mod_ref, *rest):
    y = _dot(att_ref[...], wmix_ref[...])
    x = _layernorm(ALPHA * x_ref[...] + mod_ref[0, 0] * y, g1_ref[...], b1_ref[...])
    _ffn_body(x, mod_ref, *rest)


def _ffn(body, lead_args, lead_specs, mod, ffn_w, which, g, b, n_rows):
    tm = TM_FFN
    layer, half = which
    w_in_all, w_out_all = ffn_w
    return pl.pallas_call(
        body,
        grid=(n_rows // tm,),
        in_specs=[
            *lead_specs,
            pl.BlockSpec((1, mod.shape[1], 1, D_MODEL), lambda i: (_group_of_tile(i, tm), 0, 0, 0)),
            pl.BlockSpec((None, None, D_MODEL, 2 * D_FF), lambda i: (layer, half, 0, 0),
                         pipeline_mode=pl.Buffered(1)),
            pl.BlockSpec((None, None, D_FF, D_MODEL), lambda i: (layer, half, 0, 0),
                         pipeline_mode=pl.Buffered(1)),
            _resident((1, D_MODEL)),
            _resident((1, D_MODEL)),
        ],
        out_specs=pl.BlockSpec((tm, D_MODEL), lambda i: (i, 0)),
        out_shape=jax.ShapeDtypeStruct((n_rows, D_MODEL), F32),
        scratch_shapes=[pltpu.VMEM((tm, D_FF), BF16)],
        compiler_params=_params("arbitrary"),
        name="half_ffn",
    )(*lead_args, mod, w_in_all, w_out_all, g.reshape(1, D_MODEL), b.reshape(1, D_MODEL))


def _row_spec(width, tm=TM_FFN):
    return pl.BlockSpec((tm, width), lambda i: (i, 0))


def _ffn_plain(x, *args):
    return _ffn(_ffn_kernel, (x,), [_row_spec(D_MODEL)], *args)


def _ffn_split(x_lat, x_ctx, *args):
    n_lat = T_LAT // TM_FFN
    specs = [pl.BlockSpec((TM_FFN, D_MODEL), lambda i: (jnp.minimum(i, n_lat - 1), 0)),
             pl.BlockSpec((TM_FFN, D_MODEL), lambda i: (jnp.maximum(i - n_lat, 0), 0))]
    return _ffn(_ffn_split_kernel, (x_lat, x_ctx), specs, *args)


def _ffn_after_even(x, y_hy, h_sum, gate_in, w_mix, g1, b1, *args):
    lead = (x, y_hy, h_sum, gate_in, w_mix.astype(BF16), g1.reshape(1, D_MODEL), b1.reshape(1, D_MODEL))
    specs = [_row_spec(D_MODEL), _row_spec(HY_WIDTH), _row_spec(RG_WIDTH), _row_spec(RG_WIDTH),
             _resident((HY_WIDTH + RG_WIDTH, D_MODEL)), _resident((1, D_MODEL)), _resident((1, D_MODEL))]
    return _ffn(_ffn_after_even_kernel, lead, specs, *args)


def _ffn_after_mla(x, att, w_mix, g1, b1, *args):
    lead = (x, att, w_mix.astype(BF16), g1.reshape(1, D_MODEL), b1.reshape(1, D_MODEL))
    specs = [_row_spec(D_MODEL), _row_spec(MLA_HEADS * MLA_V),
             _resident((MLA_HEADS * MLA_V, D_MODEL)), _resident((1, D_MODEL)), _resident((1, D_MODEL))]
    return _ffn(_ffn_after_mla_kernel, lead, specs, *args)


def _evproj_kernel(x_ref, mod_ref, w_ref, hy_ref, rx_ref, gt_ref):
    shift, scale = mod_ref[0, 0], mod_ref[0, 1]
    h = (x_ref[...] * (1.0 + scale) + shift).astype(BF16)
    n_hy = 3 * HY_WIDTH
    hy_ref[...] = _dot(h, w_ref[:, :n_hy])
    rx_ref[...] = _dot(h, w_ref[:, n_hy:n_hy + RG_WIDTH])
    gt_ref[...] = _dot(h, w_ref[:, n_hy + RG_WIDTH:])


def _evproj(x, mod, w_in):
    tm = TM_PROJ
    row = lambda i: (i, 0)
    return pl.pallas_call(
        _evproj_kernel,
        grid=(T_ALL // tm,),
        in_specs=[
            pl.BlockSpec((tm, D_MODEL), row),
            pl.BlockSpec((1, 2, 1, D_MODEL), lambda i: (_group_of_tile(i, tm), 0, 0, 0)),
            _resident((D_MODEL, EV_IN)),
        ],
        out_specs=[
            pl.BlockSpec((tm, 3 * HY_WIDTH), row),
            pl.BlockSpec((tm, RG_WIDTH), row),
            pl.BlockSpec((tm, RG_WIDTH), row),
        ],
        out_shape=[
            jax.ShapeDtypeStruct((T_ALL, 3 * HY_WIDTH), F32),
            jax.ShapeDtypeStruct((T_ALL, RG_WIDTH), F32),
            jax.ShapeDtypeStruct((T_ALL, RG_WIDTH), F32),
        ],
        compiler_params=_params("arbitrary"),
        name="even_in_proj",
    )(x, mod, w_in.astype(BF16))


def _block_diag_tiles(w):
    per = RG_CT // RG_BLOCK_DIM
    w = w.reshape(RG_WIDTH // RG_CT, per, RG_BLOCK_DIM, RG_BLOCK_DIM)
    eye = jnp.eye(per, dtype=w.dtype)
    return jnp.einsum('cpde,pq->cpdqe', w, eye).reshape(RG_WIDTH // RG_CT, RG_CT, RG_CT)


def _rg_kernel(rx_ref, cw_ref, cb_ref, w_ref, bias_ref, lam_ref, out_ref,
                padl_ref, padc_ref, xt_f_ref, xt_b_ref, coef_a_ref, coef_b_ref, hbuf_f_ref, hbuf_b_ref):
    neg_lam = -lam_ref[...]
    softplus = jnp.maximum(neg_lam, 0.0) + jnp.log1p(jnp.exp(-jnp.abs(neg_lam)))
    neg_log_a_scale = RG_C * softplus
    a_exp2_scale = -LOG2_E * neg_log_a_scale
    cb = cb_ref[...]
    xts = (xt_f_ref, xt_b_ref)
    hbuf = (hbuf_f_ref, hbuf_b_ref)
    slab_rows = RG_TC + 2 * RG_HALO

    def step_rows(s):
        return slice(s * SUBLANES, (s + 1) * SUBLANES)

    halo = jnp.zeros((BATCH, RG_HALO, RG_CT), F32)
    for pad_ref, length in ((padl_ref, SEQ), (padc_ref, CTX_LEN)):
        pad_ref[:, 0:RG_HALO, :] = halo
        pad_ref[:, RG_HALO + length:, :] = halo
    padl_ref[:, RG_HALO:RG_HALO + SEQ, :] = rx_ref[0:BATCH]
    for b in range(BATCH):
        padc_ref[b, RG_HALO:RG_HALO + CTX_LEN, :] = rx_ref[BATCH, b * CTX_LEN:(b + 1) * CTX_LEN, :]

    def add_out(latent, t_start, d):
        t_start = pl.multiple_of(t_start, SUBLANES)
        for b in range(BATCH):
            rows = hbuf[d][pl.ds(b, RG_TC, stride=SUBLANES), :]
            if latent:
                out_ref[b, pl.ds(t_start, RG_TC), :] += rows
            else:
                out_ref[BATCH, pl.ds(b * CTX_LEN + t_start, RG_TC), :] += rows

    def sigmoid(z):
        return 1.0 / (1.0 + jnp.exp2(z * (-LOG2_E)))

    def coeffs(latent, length, t0, d, dst_ref):
        t0 = pl.multiple_of(jnp.asarray(t0, jnp.int32), SUBLANES)
        pad_ref = padl_ref if latent else padc_ref
        xt = xts[d]
        for b in range(BATCH):
            xt[pl.ds(b, slab_rows, stride=SUBLANES), :] = pad_ref[b, pl.ds(t0, slab_rows), :]
        x2 = cb
        for k in range(4):
            start = (RG_HALO - 2 + k) * SUBLANES
            x2 = x2 + xt[start:start + RG_TC * SUBLANES, :] * cw_ref[k:k + 1, :]
        xb = x2.astype(BF16)
        r = sigmoid(_dot(xb, w_ref[0, 2 * d]) + bias_ref[2 * d:2 * d + 1, :])
        i = sigmoid(_dot(xb, w_ref[0, 2 * d + 1]) + bias_ref[2 * d + 1:2 * d + 2, :])
        a = jnp.exp2(r * a_exp2_scale[d:d + 1, :])
        b = jnp.sqrt(jnp.tanh(r * neg_log_a_scale[d:d + 1, :]) * (a * a + 1.0)) * (i * x2)
        dst_ref[d, 0] = a
        dst_ref[d, 1] = b

    def scan(latent, tf, tb, src_ref, hf, hb):
        tf = jnp.asarray(tf, jnp.int32)
        tb = jnp.asarray(tb, jnp.int32)
        for s in range(RG_TC):
            rows = step_rows(s)
            hf = src_ref[0, 0, rows, :] * hf + src_ref[0, 1, rows, :]
            hbuf_f_ref[rows, :] = hf
            rows = step_rows(RG_TC - 1 - s)
            hb = src_ref[1, 0, rows, :] * hb + src_ref[1, 1, rows, :]
            hbuf_b_ref[rows, :] = hb
        add_out(latent, tf, 0)
        add_out(latent, tb, 1)
        return hf, hb

    def sweep(latent, length, h_fwd, h_bwd):
        n_chunks = length // RG_TC
        last = n_chunks - 1

        def both(j, dst_ref):
            coeffs(latent, length, j * RG_TC, 0, dst_ref)
            coeffs(latent, length, (last - j) * RG_TC, 1, dst_ref)

        def body(i, carry):
            j = 2 * i
            both(j + 1, coef_b_ref)
            carry = scan(latent, j * RG_TC, (last - j) * RG_TC, coef_a_ref, *carry)
            both(jnp.minimum(j + 2, last), coef_a_ref)
            return scan(latent, (j + 1) * RG_TC, (last - j - 1) * RG_TC, coef_b_ref, *carry)

        both(0, coef_a_ref)
        return lax.fori_loop(0, n_chunks // 2, body, (h_fwd, h_bwd))

    out_ref[...] = jnp.zeros_like(out_ref)
    zero = jnp.zeros((SUBLANES, RG_CT), F32)
    hf, hb = sweep(False, CTX_LEN, zero, zero)
    sweep(True, SEQ, hf, hb)


def _rglru(rx, conv_w, conv_b, a_w, a_b, x_w, x_b, lam):
    assert T_CTX == SEQ, "the context rows must form exactly one latent-sized row block"
    w = jnp.stack([_block_diag_tiles(a_w[0]), _block_diag_tiles(x_w[0]),
                   _block_diag_tiles(a_w[1]), _block_diag_tiles(x_w[1])], axis=1).astype(BF16)
    bias = jnp.stack([a_b[0], x_b[0], a_b[1], x_b[1]], axis=0)
    n_ct = RG_WIDTH // RG_CT
    n_blocks = BATCH + 1
    chan = lambda c: (0, 0, c)
    lane = lambda c: (0, c)
    coef = pltpu.VMEM((2, 2, RG_TC * BATCH, RG_CT), F32)
    slab = pltpu.VMEM(((RG_TC + 2 * RG_HALO) * BATCH, RG_CT), F32)
    chunk = pltpu.VMEM((RG_TC * BATCH, RG_CT), F32)
    out = pl.pallas_call(
        _rg_kernel,
        grid=(n_ct,),
        in_specs=[
            pl.BlockSpec((n_blocks, SEQ, RG_CT), chan),
            pl.BlockSpec((4, RG_CT), lane),
            pl.BlockSpec((1, RG_CT), lane),
            pl.BlockSpec((1, 4, RG_CT, RG_CT), lambda c: (c, 0, 0, 0)),
            pl.BlockSpec((4, RG_CT), lane),
            pl.BlockSpec((2, RG_CT), lane),
        ],
        out_specs=pl.BlockSpec((n_blocks, SEQ, RG_CT), chan),
        out_shape=jax.ShapeDtypeStruct((n_blocks, SEQ, RG_WIDTH), F32),
        scratch_shapes=[
            pltpu.VMEM((BATCH, SEQ + 2 * RG_HALO, RG_CT), F32),
            pltpu.VMEM((BATCH, CTX_LEN + 2 * RG_HALO, RG_CT), F32),
            slab, slab,
            coef, coef,
            chunk, chunk,
        ],
        compiler_params=_params("arbitrary"),
        name="rglru_scan",
    )(rx.reshape(n_blocks, SEQ, RG_WIDTH), conv_w, conv_b.reshape(1, RG_WIDTH), w, bias, lam)
    return out.reshape(T_ALL, RG_WIDTH)


def _dft_tables(length):
    step = 1 << (length.bit_length() // 2)
    s = np.arange(length, dtype=np.int64)

    def thin(kvec):
        ang = ((kvec[:, None] * s[None, :]) % (2 * length)) * (math.pi / length)
        return (jnp.asarray(np.cos(ang)[:, None, :], dtype=F32),
                jnp.asarray(np.sin(ang)[:, None, :], dtype=F32))

    c_hi, s_hi = thin(np.arange(0, length, step, dtype=np.int64))
    c_lo, s_lo = thin(np.arange(step, dtype=np.int64))
    c_lo, s_lo = c_lo.reshape(1, step, length), s_lo.reshape(1, step, length)
    fc = (c_hi * c_lo - s_hi * s_lo).reshape(length, length)
    fs = (s_hi * c_lo + c_hi * s_lo).reshape(length, length)
    return fc.astype(BF16), fs.astype(BF16)


def _filter_features(length):
    f32 = np.float32
    pos = np.arange(length, dtype=f32)
    t = pos / f32(length)
    bands = np.linspace(1e-4, HY_BANDS - 1, HY_BANDS, dtype=f32)
    ang = (f32(2.0 * math.pi) * pos / f32(length))[:, None] * bands[None, :]
    feats = np.concatenate([t[:, None], np.cos(ang), -np.sin(ang)], axis=-1).astype(f32)
    feats = np.pad(feats, ((0, 0), (0, LANES - HY_EMB)))
    deltas = np.abs(np.linspace(HY_MIN_DECAY, HY_MAX_DECAY, HY_WIDTH, dtype=f32))
    window = (np.exp(-t[:, None] * deltas[None, :]) + f32(HY_SHIFT)).astype(f32)
    return jnp.asarray(feats), jnp.asarray(window)


def _filter_kernel(feat_ref, w1_ref, b1_ref, w2_ref, b2_ref, w3f_ref, w3b_ref, sf_ref, win_ref,
                   fc_ref, fs_ref, kc_ref, ks_ref, kn_ref, *, length):
    n = 2 * length
    h = jnp.sin(sf_ref[0:1, :] * (_dot(feat_ref[...].astype(BF16), w1_ref[...]) + b1_ref[...]))
    h = jnp.sin(sf_ref[1:2, :] * (_dot(h.astype(BF16), w2_ref[...]) + b2_ref[...]))
    hb16 = h.astype(BF16)
    win = win_ref[...]
    row = lax.broadcasted_iota(jnp.int32, (length, HY_CT), 0)
    h_fwd = _dot(hb16, w3f_ref[...]) * win
    h_bwd0 = jnp.where(row == 0, 0.0, _dot(hb16, w3b_ref[...]) * win)
    h_sum = h_fwd + h_bwd0
    h_dif = h_fwd - h_bwd0
    weight = jnp.where(row == 0, 1.0 / n, 2.0 / n)
    kc_ref[...] = _dot(fc_ref[...], h_sum.astype(BF16)) * weight
    ks_ref[...] = _dot(fs_ref[...], h_dif.astype(BF16)) * weight
    sign = jnp.where((row & 1) == 0, 1.0, -1.0)
    kn_ref[...] = jnp.sum(h_sum * sign, axis=0, keepdims=True) * (1.0 / n)


def _hyena_filter_spectrum(length, fc, fs, w1, b1, w2, b2, w3, sin_freq):
    feats, window = _filter_features(length)
    hp = LANES - HY_FILT_HIDDEN
    w1p = jnp.pad(w1, ((0, LANES - HY_EMB), (0, hp))).astype(BF16)
    w2p = jnp.pad(w2, ((0, hp), (0, hp))).astype(BF16)
    w3p = jnp.pad(w3, ((0, hp), (0, 0))).astype(BF16)
    b1p = jnp.pad(b1, (0, hp)).reshape(1, LANES)
    b2p = jnp.pad(b2, (0, hp)).reshape(1, LANES)
    sfp = jnp.pad(sin_freq, ((0, 0), (0, hp)))
    spec = jax.ShapeDtypeStruct((length, HY_WIDTH), F32)
    chan = lambda c: (0, c)
    return pl.pallas_call(
        functools.partial(_filter_kernel, length=length),
        grid=(HY_WIDTH // HY_CT,),
        in_specs=[
            _resident((length, LANES)),
            _resident((LANES, LANES)),
            _resident((1, LANES)),
            _resident((LANES, LANES)),
            _resident((1, LANES)),
            pl.BlockSpec((LANES, HY_CT), chan),
            pl.BlockSpec((LANES, HY_CT), chan),
            _resident((2, LANES)),
            pl.BlockSpec((length, HY_CT), chan),
            _resident((length, length)),
            _resident((length, length)),
        ],
        out_specs=[
            pl.BlockSpec((length, HY_CT), chan),
            pl.BlockSpec((length, HY_CT), chan),
            pl.BlockSpec((1, HY_CT), chan),
        ],
        out_shape=[spec, spec, jax.ShapeDtypeStruct((1, HY_WIDTH), F32)],
        compiler_params=_params("arbitrary"),
        name=f"hyena_filter_{length}",
    )(feats, w1p, b1p, w2p, b2p, w3p[:, :HY_WIDTH], w3p[:, HY_WIDTH:], sfp, window, fc, fs)


def _hyena_sequence(raw, cw_ref, cb_ref, bias_ref, fc_ref, fs_ref, kc_ref, ks_ref, kn_ref, length):
    row = lax.broadcasted_iota(jnp.int32, (length, HY_CT), 0)

    def short_conv(x, part):
        prev = jnp.where(row == 0, 0.0, pltpu.roll(x, 1, 0))
        nxt = jnp.where(row == length - 1, 0.0, pltpu.roll(x, length - 1, 0))
        return (prev * cw_ref[part, 0:1, :] + x * cw_ref[part, 1:2, :] + nxt * cw_ref[part, 2:3, :]
                + cb_ref[part:part + 1, :])

    x0 = short_conv(raw[0], 0)
    x1 = short_conv(raw[1], 1)
    v = short_conv(raw[2], 2)
    vx = v * x1
    vb = vx.astype(BF16)
    ft = min(length, HY_FREQ_TILE)
    y = None
    for f0 in range(0, length, ft):
        spec_c = _dot(fc_ref[f0:f0 + ft, :], vb)
        spec_s = _dot(fs_ref[f0:f0 + ft, :], vb)
        kc = kc_ref[f0:f0 + ft, :]
        ks = ks_ref[f0:f0 + ft, :]
        p = (spec_c * kc - spec_s * ks).astype(BF16)
        q = (spec_c * ks + spec_s * kc).astype(BF16)
        part = _dot(fc_ref[:, f0:f0 + ft], p) + _dot(fs_ref[:, f0:f0 + ft], q)
        y = part if y is None else y + part
    sign = jnp.where((row & 1) == 0, 1.0, -1.0)
    nyquist = jnp.sum(vx * sign, axis=0, keepdims=True) * kn_ref[...]
    return x0 * (y + sign * nyquist + vx * bias_ref[...])


def _hyena_kernel(x0_ref, x1_ref, v_ref, cw_ref, cb_ref, bias_ref,
                  fcl_ref, fsl_ref, kcl_ref, ksl_ref, knl_ref,
                  fcc_ref, fsc_ref, kcc_ref, ksc_ref, knc_ref, o_ref):
    step = pl.program_id(1)
    common = (cw_ref, cb_ref, bias_ref)

    @pl.when(step < BATCH)
    def _():
        raw = (x0_ref[...], x1_ref[...], v_ref[...])
        o_ref[...] = _hyena_sequence(raw, *common, fcl_ref, fsl_ref, kcl_ref, ksl_ref, knl_ref, SEQ)

    @pl.when(step == BATCH)
    def _():
        for r in range(BATCH):
            rows = slice(r * CTX_LEN, (r + 1) * CTX_LEN)
            raw = (x0_ref[rows, :], x1_ref[rows, :], v_ref[rows, :])
            o_ref[rows, :] = _hyena_sequence(raw, *common, fcc_ref, fsc_ref, kcc_ref, ksc_ref, knc_ref, CTX_LEN)


def _hyena(hy, tab_lat, spec_lat, tab_ctx, spec_ctx, conv_w, conv_b, bias):
    assert T_CTX == SEQ, "the context rows must form exactly one latent-sized row block"
    n_ct = HY_WIDTH // HY_CT
    cw = conv_w.reshape(3, 3, HY_WIDTH).transpose(1, 0, 2)
    cb = conv_b.reshape(3, HY_WIDTH)

    def part_spec(part):
        return pl.BlockSpec((SEQ, HY_CT), lambda c, b: (b, part * n_ct + c))

    chan = lambda c, b: (0, c)

    def table_specs(length):
        return [
            _resident((length, length)),
            _resident((length, length)),
            pl.BlockSpec((length, HY_CT), chan, pipeline_mode=pl.Buffered(1)),
            pl.BlockSpec((length, HY_CT), chan, pipeline_mode=pl.Buffered(1)),
            pl.BlockSpec((1, HY_CT), chan),
        ]

    return pl.pallas_call(
        _hyena_kernel,
        grid=(n_ct, BATCH + 1),
        in_specs=[
            part_spec(0), part_spec(1), part_spec(2),
            pl.BlockSpec((3, 3, HY_CT), lambda c, b: (0, 0, c)),
            pl.BlockSpec((3, HY_CT), chan),
            pl.BlockSpec((1, HY_CT), chan),
            *table_specs(SEQ),
            *table_specs(CTX_LEN),
        ],
        out_specs=pl.BlockSpec((SEQ, HY_CT), lambda c, b: (b, c)),
        out_shape=jax.ShapeDtypeStruct((T_ALL, HY_WIDTH), F32),
        compiler_params=_params("arbitrary", "arbitrary"),
        name="hyena_conv",
    )(hy, hy, hy, cw, cb, bias.reshape(1, HY_WIDTH), *tab_lat, *spec_lat, *tab_ctx, *spec_ctx)


def _rope_partner(n):
    idx = np.arange(n)
    return np.where((idx % (2 * ROPE_AXIS_PAIRS)) < ROPE_AXIS_PAIRS, idx + ROPE_AXIS_PAIRS, idx - ROPE_AXIS_PAIRS)


def _slot_lanes():
    d = np.arange(MLA_ROPE)
    axis, second, pair = d // (2 * ROPE_AXIS_PAIRS), (d // ROPE_AXIS_PAIRS) % 2, d % ROPE_AXIS_PAIRS
    rope_lane = axis * ROPE_AXIS_PAIRS + pair + (HEAD_SLOT // 2) * second
    free = np.setdiff1d(np.arange(HEAD_SLOT), rope_lane)
    return rope_lane, free[:MLA_NOPE]


def _to_slot(nope, rope):
    rope_lane, nope_lane = _slot_lanes()
    source = [(None, 0)] * HEAD_SLOT
    for arr, lanes in ((nope, nope_lane), (rope, rope_lane)):
        if arr is not None:
            for idx, lane in enumerate(lanes):
                source[lane] = (arr, idx)
    like = nope if nope is not None else rope
    pieces, lane = [], 0
    while lane < HEAD_SLOT:
        arr, start = source[lane]
        run = 1
        while (lane + run < HEAD_SLOT and source[lane + run][0] is arr
               and (arr is None or source[lane + run][1] == start + run)):
            run += 1
        pieces.append(jnp.zeros(like.shape[:-1] + (run,), like.dtype) if arr is None else arr[..., start:start + run])
        lane += run
    return jnp.concatenate(pieces, axis=-1)


def _rope_tables():
    f32 = np.float32
    rope_lane, nope_lane = _slot_lanes()
    rows = np.repeat(np.arange(SEQ // GRID_W, dtype=f32), GRID_W)
    cols = np.tile(np.arange(GRID_W, dtype=f32), SEQ // GRID_W)
    inv_freq = (f32(ROPE_BASE) ** (-np.arange(ROPE_AXIS_PAIRS, dtype=f32) / f32(ROPE_AXIS_PAIRS))).astype(f32)
    ang_r = rows[:, None] * inv_freq
    ang_c = cols[:, None] * inv_freq
    cos32 = np.concatenate([np.cos(ang_r)] * 2 + [np.cos(ang_c)] * 2, axis=1)
    sin32 = np.concatenate([-np.sin(ang_r), np.sin(ang_r), -np.sin(ang_c), np.sin(ang_c)], axis=1)

    def table(n_rows, rope_vals, nope_val):
        t = np.zeros((n_rows, HEAD_SLOT), f32)
        t[:, rope_lane] = rope_vals
        t[:, nope_lane] = nope_val
        return t

    q_cos = table(SEQ, cos32, 1.0) * f32(Q_SCALE)
    q_sin = table(SEQ, sin32, 0.0) * f32(Q_SCALE)
    k_cos = table(SEQ, cos32, 0.0)
    k_sin = table(SEQ, sin32, 0.0)
    ident = table(TM_PROJ, 1.0, 0.0)
    zero = np.zeros((TM_PROJ, HEAD_SLOT), f32)
    tables = ((q_cos, zero), (q_sin, zero), (k_cos, ident), (k_sin, zero))
    return tuple(jnp.asarray(np.concatenate(t, 0).astype(f32)) for t in tables)


def _mla_weights(w_in, w_q_up, w_kv_up):
    w_q = w_in[:, :MLA_Q_LORA]
    w_kv = w_in[:, MLA_Q_LORA:MLA_Q_LORA + MLA_KV_LORA]
    w_kr = w_in[:, MLA_Q_LORA + MLA_KV_LORA:]
    w_kr2 = jnp.concatenate([_to_slot(None, w_kr), _to_slot(None, w_kr[:, _rope_partner(MLA_ROPE)])], axis=1)

    qh = w_q_up.reshape(MLA_Q_LORA, MLA_HEADS, MLA_NOPE + MLA_ROPE)
    wq_slot = _to_slot(qh[..., :MLA_NOPE], qh[..., MLA_NOPE:]).reshape(MLA_Q_LORA, MLA_HEADS * HEAD_SLOT)

    kvh = w_kv_up.reshape(MLA_KV_LORA, MLA_HEADS, MLA_NOPE + MLA_V)
    wk_slot = _to_slot(kvh[..., :MLA_NOPE], None).reshape(MLA_KV_LORA, MLA_HEADS * HEAD_SLOT)
    wv_slot = jnp.pad(kvh[..., MLA_NOPE:], ((0, 0), (0, 0), (0, HEAD_SLOT - MLA_V)))
    wv_slot = wv_slot.reshape(MLA_KV_LORA, MLA_HEADS * HEAD_SLOT)
    return [w.astype(BF16) for w in (w_q, w_kv, w_kr2, wq_slot, wk_slot, wv_slot)]


def _mlaproj_kernel(x_ref, mod_ref, wq_ref, wkv_ref, wkr_ref, wqs_ref, wks_ref, wv_ref,
                    vone_ref, qg_ref, kvg_ref, qcos_ref, qsin_ref, kcos_ref, ksin_ref,
                    q_ref, k_ref, v_ref):
    shift, scale = mod_ref[0, 0], mod_ref[0, 1]
    h = (x_ref[...] * (1.0 + scale) + shift).astype(BF16)

    def rmsnorm(y, g):
        return (y * lax.rsqrt(jnp.mean(y * y, axis=-1, keepdims=True) + RMS_EPS) * g).astype(BF16)

    qn = rmsnorm(_dot(h, wq_ref[...]), qg_ref[...])
    kvn = rmsnorm(_dot(h, wkv_ref[...]), kvg_ref[...])
    kr2 = _dot(h, wkr_ref[...])
    k_rope = kr2[:, :HEAD_SLOT] * kcos_ref[...] + kr2[:, HEAD_SLOT:] * ksin_ref[...]
    k_nope = _dot(kvn, wks_ref[...])
    v_ref[...] = (_dot(kvn, wv_ref[...]) + vone_ref[...]).astype(BF16)
    q_all = _dot(qn, wqs_ref[...])
    q_cos = qcos_ref[...]
    q_sin = qsin_ref[...]
    for hd in range(MLA_HEADS):
        sl = slice(hd * HEAD_SLOT, (hd + 1) * HEAD_SLOT)
        qh = q_all[:, sl]
        rotated = qh * q_cos + pltpu.roll(qh, HEAD_SLOT // 2, 1) * q_sin
        q_ref[:, sl] = rotated.astype(BF16)
        k_ref[:, sl] = (k_nope[:, sl] + k_rope).astype(BF16)


def _mlaproj(x, mod, w_in, q_norm, kv_norm, w_q_up, w_kv_up):
    tm = TM_PROJ
    weights = _mla_weights(w_in, w_q_up, w_kv_up)
    tables = _rope_tables()
    row = lambda i: (i, 0)
    tab = lambda i: (jnp.where(i < T_LAT // tm, i % (SEQ // tm), SEQ // tm), 0)
    wide = MLA_HEADS * HEAD_SLOT
    v_one = jnp.asarray((np.arange(wide) % HEAD_SLOT == MLA_V).astype(np.float32).reshape(1, wide))
    return pl.pallas_call(
        _mlaproj_kernel,
        grid=(T_ALL // tm,),
        in_specs=[
            pl.BlockSpec((tm, D_MODEL), row),
            pl.BlockSpec((1, 2, 1, D_MODEL), lambda i: (_group_of_tile(i, tm), 0, 0, 0)),
            *[_resident(w.shape) for w in weights],
            _resident((1, wide)),
            _resident((1, MLA_Q_LORA)),
            _resident((1, MLA_KV_LORA)),
            *[pl.BlockSpec((tm, HEAD_SLOT), tab) for _ in tables],
        ],
        out_specs=[pl.BlockSpec((tm, wide), row)] * 3,
        out_shape=[jax.ShapeDtypeStruct((T_ALL, wide), BF16)] * 3,
        compiler_params=_params("arbitrary"),
        name="mla_proj",
    )(x, mod, *weights, v_one, q_norm.reshape(1, MLA_Q_LORA), kv_norm.reshape(1, MLA_KV_LORA), *tables)


def _attn_kernel(q_ref, kl_ref, kc_ref, vl_ref, vc_ref, o_ref):
    lane = lax.broadcasted_iota(jnp.int32, (TQ, HEAD_SLOT), 1)
    outs = []
    for hd in range(ATTN_HEADS):
        sl = slice(hd * HEAD_SLOT, (hd + 1) * HEAD_SLOT)
        q = q_ref[:, sl]
        s_lat = _dot_nt(q, kl_ref[:, sl])
        s_ctx = _dot_nt(q, kc_ref[:, sl])
        m = jnp.maximum(jnp.max(s_lat, axis=-1, keepdims=True), jnp.max(s_ctx, axis=-1, keepdims=True))
        p_lat = jnp.exp2(s_lat - m).astype(BF16)
        p_ctx = jnp.exp2(s_ctx - m).astype(BF16)
        o = _dot(p_lat, vl_ref[:, sl]) + _dot(p_ctx, vc_ref[:, sl])
        outs.append(o / o[:, MLA_V:MLA_V + 1])
    for pair in range(ATTN_HEADS // 2):
        packed = jnp.where(lane < MLA_V, outs[2 * pair], pltpu.roll(outs[2 * pair + 1], MLA_V, 1))
        o_ref[:, pair * HEAD_SLOT:(pair + 1) * HEAD_SLOT] = packed.astype(BF16)


def _attention(q, k, v):
    n_q = SEQ // TQ
    n_groups = MLA_HEADS // ATTN_HEADS
    ctx_block0 = T_LAT // CTX_LEN
    wide = ATTN_HEADS * HEAD_SLOT
    lat = lambda b, h, i: (b, h)
    ctx = lambda b, h, i: (ctx_block0 + b, h)
    return pl.pallas_call(
        _attn_kernel,
        grid=(BATCH, n_groups, n_q),
        in_specs=[
            pl.BlockSpec((TQ, wide), lambda b, h, i: (b * n_q + i, h)),
            pl.BlockSpec((SEQ, wide), lat),
            pl.BlockSpec((CTX_LEN, wide), ctx),
            pl.BlockSpec((SEQ, wide), lat),
            pl.BlockSpec((CTX_LEN, wide), ctx),
        ],
        out_specs=pl.BlockSpec((TQ, ATTN_HEADS * MLA_V), lambda b, h, i: (b * n_q + i, h)),
        out_shape=jax.ShapeDtypeStruct((T_LAT, MLA_HEADS * MLA_V), BF16),
        compiler_params=_params("arbitrary", "arbitrary", "arbitrary"),
        name="mla_attention",
    )(q, k, k, v, v)


def kernel(x, c, ctx, c_ctx, ada_w, ada_b, ln_g, ln_b, ffn_w_in, ffn_w_out, ev_w_in, ev_w_out, hy_conv_w, hy_conv_b, hy_filt_w1, hy_filt_b1, hy_filt_w2, hy_filt_b2, hy_filt_w3, hy_sin_freq, hy_bias, rg_conv_w, rg_conv_b, rg_a_w, rg_a_b, rg_x_w, rg_x_b, rg_lambda, mla_w_in, mla_q_norm, mla_kv_norm, mla_w_q_up, mla_w_kv_up, mla_w_out):
    assert x.shape == (BATCH, SEQ, D_MODEL) and ctx.shape == (BATCH, CTX_LEN, D_MODEL)
    cond = jnp.concatenate([c, c_ctx[None], jnp.zeros((2 * SUBLANES - N_GROUPS, D_MODEL), F32)], axis=0)
    mods = _adaln(cond, ada_w, ada_b)[:, :N_GROUPS].reshape(DEPTH, N_GROUPS, N_ADA, 1, D_MODEL)
    ffn_w = (ffn_w_in.astype(BF16), ffn_w_out.astype(BF16))

    m = mods[0]
    xs = _ffn_split(x.reshape(T_LAT, D_MODEL), ctx.reshape(T_CTX, D_MODEL),
                    m[:, 0:3], ffn_w, (0, 0), ln_g[0, 0], ln_b[0, 0], T_ALL)
    hy, rx, gate_in = _evproj(xs, m[:, 3:5], ev_w_in[0])
    h_sum = _rglru(rx, rg_conv_w[0], rg_conv_b[0], rg_a_w[0], rg_a_b[0], rg_x_w[0], rg_x_b[0], rg_lambda[0])
    filt = (hy_filt_w1[0], hy_filt_b1[0], hy_filt_w2[0], hy_filt_b2[0], hy_filt_w3[0], hy_sin_freq[0])
    tab_l = _dft_tables(SEQ)
    tab_c = _dft_tables(CTX_LEN)
    spec_l = _hyena_filter_spectrum(SEQ, *tab_l, *filt)
    spec_c = _hyena_filter_spectrum(CTX_LEN, *tab_c, *filt)
    y_hy = _hyena(hy, tab_l, spec_l, tab_c, spec_c, hy_conv_w[0], hy_conv_b[0], hy_bias[0])
    xs = _ffn_after_even(xs, y_hy, h_sum, gate_in, ev_w_out[0], ln_g[0, 1], ln_b[0, 1],
                         m[:, 5:9], ffn_w, (0, 1), ln_g[0, 2], ln_b[0, 2], T_ALL)

    m = mods[1]
    xs = _ffn_plain(xs, m[:, 0:3], ffn_w, (1, 0), ln_g[1, 0], ln_b[1, 0], T_ALL)
    q, k, v = _mlaproj(xs, m[:, 3:5], mla_w_in[0], mla_q_norm[0], mla_kv_norm[0], mla_w_q_up[0], mla_w_kv_up[0])
    att = _attention(q, k, v)
    xl = _ffn_after_mla(xs, att, mla_w_out[0], ln_g[1, 1], ln_b[1, 1],
                        m[:, 5:9], ffn_w, (1, 1), ln_g[1, 2], ln_b[1, 2], T_LAT)
    return xl.reshape(BATCH, SEQ, D_MODEL)
```

```python
import functools
import math

import jax
import jax.numpy as jnp
import numpy as np
from jax import lax
from jax.experimental import pallas as pl
from jax.experimental.pallas import tpu as pltpu

F32 = jnp.float32
BF16 = jnp.bfloat16

D_MODEL = 1024
BATCH = 8
SEQ = 2048
DEPTH = 2
CTX_LEN = 256
GRID_W = 64
N_ADA = 9
D_FF = 2816

HY_WIDTH = 512
RG_WIDTH = 512
EV_IN = 3 * HY_WIDTH + 2 * RG_WIDTH
HY_EMB = 33
HY_BANDS = (HY_EMB - 1) // 2
HY_FILT_HIDDEN = 64
HY_TARGET = 1e-2
HY_MAX_DECAY = math.log(HY_TARGET) / 0.3
HY_MIN_DECAY = math.log(HY_TARGET) / 1.5
HY_SHIFT = 0.05
RG_BLOCKS = 8
RG_BLOCK_DIM = RG_WIDTH // RG_BLOCKS
RG_C = 8.0

MLA_HEADS = 16
MLA_Q_LORA = 768
MLA_KV_LORA = 256
MLA_NOPE = 64
MLA_ROPE = 32
MLA_V = 64
MLA_SCALE = (MLA_NOPE + MLA_ROPE) ** -0.5
ROPE_AXIS_PAIRS = MLA_ROPE // 4
ROPE_BASE = 10000.0

LOG2_E = math.log2(math.e)
ALPHA = (2.0 * DEPTH) ** 0.25
LN_EPS = 1e-6
RMS_EPS = 1e-6

T_LAT = BATCH * SEQ
T_CTX = BATCH * CTX_LEN
T_ALL = T_LAT + T_CTX
N_GROUPS = BATCH + 1

LANES = 128
SUBLANES = 8
VMEM_LIMIT = 56 * 1024 * 1024

TM_FFN = 512
FF_CHUNK = 256
N_FF_CHUNKS = D_FF // FF_CHUNK
FFN_WEIGHT_CHUNKS = 8
TM_PROJ = 512
HY_CT = 256
HY_FREQ_TILE = 256
RG_CT = 128
RG_TC = 32
RG_HALO = SUBLANES
HEAD_SLOT = 128
TQ = 512
ATTN_HEADS = 16
Q_SCALE = MLA_SCALE * math.log2(math.e)
ADA_TN = 1152


def _dot(a, b):
    return jnp.dot(a, b, preferred_element_type=F32)


def _dot_nt(a, b):
    return lax.dot_general(a, b, (((1,), (1,)), ((), ())), preferred_element_type=F32)


def _resident(shape):
    nd = len(shape)
    return pl.BlockSpec(shape, lambda *_: (0,) * nd, pipeline_mode=pl.Buffered(1))


def _group_of_tile(i, tm):
    return jnp.where(i < T_LAT // tm, i // (SEQ // tm), BATCH)


def _params(*sem):
    return pltpu.CompilerParams(dimension_semantics=sem, vmem_limit_bytes=VMEM_LIMIT)


def _layernorm(z, g, b):
    mu = jnp.mean(z, axis=-1, keepdims=True)
    zc = z - mu
    var = jnp.mean(zc * zc, axis=-1, keepdims=True)
    return zc * lax.rsqrt(var + LN_EPS) * g + b


def _adaln_kernel(c_ref, w_ref, b_ref, o_ref):
    c = c_ref[...]
    s = (c * jax.nn.sigmoid(c)).astype(BF16)
    o_ref[0] = _dot(s, w_ref[0].astype(BF16)) + b_ref[0]


def _adaln(cond, ada_w, ada_b):
    n_out = N_ADA * D_MODEL
    rows = cond.shape[0]
    return pl.pallas_call(
        _adaln_kernel,
        grid=(DEPTH, n_out // ADA_TN),
        in_specs=[
            pl.BlockSpec((rows, D_MODEL), lambda l, j: (0, 0)),
            pl.BlockSpec((1, D_MODEL, ADA_TN), lambda l, j: (l, 0, j)),
            pl.BlockSpec((1, 1, ADA_TN), lambda l, j: (l, 0, j)),
        ],
        out_specs=pl.BlockSpec((1, rows, ADA_TN), lambda l, j: (l, 0, j)),
        out_shape=jax.ShapeDtypeStruct((DEPTH, rows, n_out), F32),
        compiler_params=_params("arbitrary", "arbitrary"),
        name="adaln",
    )(cond, ada_w, ada_b.reshape(DEPTH, 1, n_out))


def _ffn_body(x, mod_ref, wi_ref, wo_ref, g_ref, b_ref, o_ref, act_ref):
    n_slots = mod_ref.shape[1]
    shift, scale, gate = mod_ref[0, n_slots - 3], mod_ref[0, n_slots - 2], mod_ref[0, n_slots - 1]
    h = (x * (1.0 + scale) + shift).astype(BF16)
    for j in range(N_FF_CHUNKS):
        lo = j * FF_CHUNK
        a = _dot(h, wi_ref[:, lo:lo + FF_CHUNK])
        u = _dot(h, wi_ref[:, D_FF + lo:D_FF + lo + FF_CHUNK])
        act_ref[:, lo:lo + FF_CHUNK] = (a * jax.nn.sigmoid(a) * u).astype(BF16)
    y = _dot(act_ref[...], wo_ref[...])
    o_ref[...] = _layernorm(ALPHA * x + (0.5 * gate) * y, g_ref[...], b_ref[...])


def _load_weight_bf16(w_hbm, which, dst_ref, stage_ref, sem):
    layer, half = which
    rows = stage_ref.shape[1]
    n_chunks = dst_ref.shape[0] // rows

    def copy(c):
        return pltpu.make_async_copy(w_hbm.at[layer, half, pl.ds(c * rows, rows)], stage_ref.at[c % 2], sem.at[c % 2])

    copy(0).start()
    for c in range(n_chunks):
        if c + 1 < n_chunks:
            copy(c + 1).start()
        copy(c).wait()
        dst_ref[c * rows:(c + 1) * rows, :] = stage_ref[c % 2].astype(BF16)


def _ffn_steps(x_fn, which, mod_ref, wi_hbm, wo_hbm, g_ref, b_ref, o_ref,
               act_ref, wi_ref, wo_ref, stage_in_ref, stage_out_ref, sem_in, sem_out):
    @pl.when(pl.program_id(0) == 0)
    def _():
        _load_weight_bf16(wi_hbm, which, wi_ref, stage_in_ref, sem_in)
        _load_weight_bf16(wo_hbm, which, wo_ref, stage_out_ref, sem_out)

    _ffn_body(x_fn(), mod_ref, wi_ref, wo_ref, g_ref, b_ref, o_ref, act_ref)


def _ffn_kernel(x_ref, *rest, which):
    _ffn_steps(lambda: x_ref[...], which, *rest)


def _ffn_split_kernel(xl_ref, xc_ref, *rest, which):
    _ffn_steps(lambda: jnp.where(pl.program_id(0) < T_LAT // TM_FFN, xl_ref[...], xc_ref[...]), which, *rest)


def _ffn_after_even_kernel(x_ref, yh_ref, hs_ref, gt_ref, wmix_ref, g1_ref, b1_ref, mod_ref, *rest, which):
    def x_fn():
        y_rg = hs_ref[...] * jax.nn.gelu(gt_ref[...], approximate=True)
        y = (_dot(yh_ref[...].astype(BF16), wmix_ref[:HY_WIDTH, :])
             + _dot(y_rg.astype(BF16), wmix_ref[HY_WIDTH:, :]))
        return _layernorm(ALPHA * x_ref[...] + mod_ref[0, 0] * y, g1_ref[...], b1_ref[...])

    _ffn_steps(x_fn, which, mod_ref, *rest)


def _ffn_after_mla_kernel(x_ref, att_ref, wmix_ref, g1_ref, b1_ref, mod_ref, *rest, which):
    def x_fn():
        y = _dot(att_ref[...], wmix_ref[...])
        return _layernorm(ALPHA * x_ref[...] + mod_ref[0, 0] * y, g1_ref[...], b1_ref[...])

    _ffn_steps(x_fn, which, mod_ref, *rest)


def _ffn(body, lead_args, lead_specs, mod, ffn_w, which, g, b, n_rows):
    tm = TM_FFN
    w_in_all, w_out_all = ffn_w
    return pl.pallas_call(
        functools.partial(body, which=which),
        grid=(n_rows // tm,),
        in_specs=[
            *lead_specs,
            pl.BlockSpec((1, mod.shape[1], 1, D_MODEL), lambda i: (_group_of_tile(i, tm), 0, 0, 0)),
            pl.BlockSpec(memory_space=pl.ANY),
            pl.BlockSpec(memory_space=pl.ANY),
            _resident((1, D_MODEL)),
            _resident((1, D_MODEL)),
        ],
        out_specs=pl.BlockSpec((tm, D_MODEL), lambda i: (i, 0)),
        out_shape=jax.ShapeDtypeStruct((n_rows, D_MODEL), F32),
        scratch_shapes=[
            pltpu.VMEM((tm, D_FF), BF16),
            pltpu.VMEM((D_MODEL, 2 * D_FF), BF16),
            pltpu.VMEM((D_FF, D_MODEL), BF16),
            pltpu.VMEM((2, D_MODEL // FFN_WEIGHT_CHUNKS, 2 * D_FF), F32),
            pltpu.VMEM((2, D_FF // FFN_WEIGHT_CHUNKS, D_MODEL), F32),
            pltpu.SemaphoreType.DMA((2,)),
            pltpu.SemaphoreType.DMA((2,)),
        ],
        compiler_params=_params("arbitrary"),
        name="half_ffn",
    )(*lead_args, mod, w_in_all, w_out_all, g.reshape(1, D_MODEL), b.reshape(1, D_MODEL))


def _row_spec(width, tm=TM_FFN):
    return pl.BlockSpec((tm, width), lambda i: (i, 0))


def _ffn_plain(x, *args):
    return _ffn(_ffn_kernel, (x,), [_row_spec(D_MODEL)], *args)


def _ffn_split(x_lat, x_ctx, *args):
    n_lat = T_LAT // TM_FFN
    specs = [pl.BlockSpec((TM_FFN, D_MODEL), lambda i: (jnp.minimum(i, n_lat - 1), 0)),
             pl.BlockSpec((TM_FFN, D_MODEL), lambda i: (jnp.maximum(i - n_lat, 0), 0))]
    return _ffn(_ffn_split_kernel, (x_lat, x_ctx), specs, *args)


def _ffn_after_even(x, y_hy, h_sum, gate_in, w_mix, g1, b1, *args):
    lead = (x, y_hy, h_sum, gate_in, w_mix.astype(BF16), g1.reshape(1, D_MODEL), b1.reshape(1, D_MODEL))
    specs = [_row_spec(D_MODEL), _row_spec(HY_WIDTH), _row_spec(RG_WIDTH), _row_spec(RG_WIDTH),
             _resident((HY_WIDTH + RG_WIDTH, D_MODEL)), _resident((1, D_MODEL)), _resident((1, D_MODEL))]
    return _ffn(_ffn_after_even_kernel, lead, specs, *args)


def _ffn_after_mla(x, att, w_mix, g1, b1, *args):
    lead = (x, att, w_mix.astype(BF16), g1.reshape(1, D_MODEL), b1.reshape(1, D_MODEL))
    specs = [_row_spec(D_MODEL), _row_spec(MLA_HEADS * MLA_V),
             _resident((MLA_HEADS * MLA_V, D_MODEL)), _resident((1, D_MODEL)), _resident((1, D_MODEL))]
    return _ffn(_ffn_after_mla_kernel, lead, specs, *args)


def _evproj_kernel(x_ref, mod_ref, w_ref, hy_ref, rx_ref, gt_ref):
    shift, scale = mod_ref[0, 0], mod_ref[0, 1]
    h = (x_ref[...] * (1.0 + scale) + shift).astype(BF16)
    n_hy = 3 * HY_WIDTH
    hy_ref[...] = _dot(h, w_ref[:, :n_hy])
    rx_ref[...] = _dot(h, w_ref[:, n_hy:n_hy + RG_WIDTH])
    gt_ref[...] = _dot(h, w_ref[:, n_hy + RG_WIDTH:])


def _evproj(x, mod, w_in):
    tm = TM_PROJ
    row = lambda i: (i, 0)
    return pl.pallas_call(
        _evproj_kernel,
        grid=(T_ALL // tm,),
        in_specs=[
            pl.BlockSpec((tm, D_MODEL), row),
            pl.BlockSpec((1, 2, 1, D_MODEL), lambda i: (_group_of_tile(i, tm), 0, 0, 0)),
            _resident((D_MODEL, EV_IN)),
        ],
        out_specs=[
            pl.BlockSpec((tm, 3 * HY_WIDTH), row),
            pl.BlockSpec((tm, RG_WIDTH), row),
            pl.BlockSpec((tm, RG_WIDTH), row),
        ],
        out_shape=[
            jax.ShapeDtypeStruct((T_ALL, 3 * HY_WIDTH), F32),
            jax.ShapeDtypeStruct((T_ALL, RG_WIDTH), F32),
            jax.ShapeDtypeStruct((T_ALL, RG_WIDTH), F32),
        ],
        compiler_params=_params("arbitrary"),
        name="even_in_proj",
    )(x, mod, w_in.astype(BF16))


def _block_diag_tiles(w):
    per = RG_CT // RG_BLOCK_DIM
    w = w.reshape(RG_WIDTH // RG_CT, per, RG_BLOCK_DIM, RG_BLOCK_DIM)
    eye = jnp.eye(per, dtype=w.dtype)
    return jnp.einsum('cpde,pq->cpdqe', w, eye).reshape(RG_WIDTH // RG_CT, RG_CT, RG_CT)


def _rg_kernel(rx_ref, cw_ref, cb_ref, w_ref, bias_ref, lam_ref, out_ref,
                padl_ref, padc_ref, xt_f_ref, xt_b_ref, coef_a_ref, coef_b_ref, hbuf_f_ref, hbuf_b_ref):
    neg_lam = -lam_ref[...]
    softplus = jnp.maximum(neg_lam, 0.0) + jnp.log1p(jnp.exp(-jnp.abs(neg_lam)))
    neg_log_a_scale = RG_C * softplus
    a_exp2_scale = -LOG2_E * neg_log_a_scale
    cb = cb_ref[...]
    xts = (xt_f_ref, xt_b_ref)
    hbuf = (hbuf_f_ref, hbuf_b_ref)
    slab_rows = RG_TC + 2 * RG_HALO

    def step_rows(s):
        return slice(s * SUBLANES, (s + 1) * SUBLANES)

    halo = jnp.zeros((BATCH, RG_HALO, RG_CT), F32)
    for pad_ref, length in ((padl_ref, SEQ), (padc_ref, CTX_LEN)):
        pad_ref[:, 0:RG_HALO, :] = halo
        pad_ref[:, RG_HALO + length:, :] = halo
    padl_ref[:, RG_HALO:RG_HALO + SEQ, :] = rx_ref[0:BATCH]
    for b in range(BATCH):
        padc_ref[b, RG_HALO:RG_HALO + CTX_LEN, :] = rx_ref[BATCH, b * CTX_LEN:(b + 1) * CTX_LEN, :]

    def add_out(latent, t_start, d):
        t_start = pl.multiple_of(t_start, SUBLANES)
        for b in range(BATCH):
            rows = hbuf[d][pl.ds(b, RG_TC, stride=SUBLANES), :]
            if latent:
                out_ref[b, pl.ds(t_start, RG_TC), :] += rows
            else:
                out_ref[BATCH, pl.ds(b * CTX_LEN + t_start, RG_TC), :] += rows

    def sigmoid(z):
        return 1.0 / (1.0 + jnp.exp2(z * (-LOG2_E)))

    def coeffs(latent, length, t0, d, dst_ref):
        t0 = pl.multiple_of(jnp.asarray(t0, jnp.int32), SUBLANES)
        pad_ref = padl_ref if latent else padc_ref
        xt = xts[d]
        for b in range(BATCH):
            xt[pl.ds(b, slab_rows, stride=SUBLANES), :] = pad_ref[b, pl.ds(t0, slab_rows), :]
        x2 = cb
        for k in range(4):
            start = (RG_HALO - 2 + k) * SUBLANES
            x2 = x2 + xt[start:start + RG_TC * SUBLANES, :] * cw_ref[k:k + 1, :]
        xb = x2.astype(BF16)
        r = sigmoid(_dot(xb, w_ref[0, 2 * d]) + bias_ref[2 * d:2 * d + 1, :])
        i = sigmoid(_dot(xb, w_ref[0, 2 * d + 1]) + bias_ref[2 * d + 1:2 * d + 2, :])
        a = jnp.exp2(r * a_exp2_scale[d:d + 1, :])
        b = jnp.sqrt(jnp.tanh(r * neg_log_a_scale[d:d + 1, :]) * (a * a + 1.0)) * (i * x2)
        dst_ref[d, 0] = a
        dst_ref[d, 1] = b

    def scan(latent, tf, tb, src_ref, hf, hb):
        tf = jnp.asarray(tf, jnp.int32)
        tb = jnp.asarray(tb, jnp.int32)
        for s in range(RG_TC):
            rows = step_rows(s)
            hf = src_ref[0, 0, rows, :] * hf + src_ref[0, 1, rows, :]
            hbuf_f_ref[rows, :] = hf
            rows = step_rows(RG_TC - 1 - s)
            hb = src_ref[1, 0, rows, :] * hb + src_ref[1, 1, rows, :]
            hbuf_b_ref[rows, :] = hb
        add_out(latent, tf, 0)
        add_out(latent, tb, 1)
        return hf, hb

    def sweep(latent, length, h_fwd, h_bwd):
        n_chunks = length // RG_TC
        last = n_chunks - 1

        def both(j, dst_ref):
            coeffs(latent, length, j * RG_TC, 0, dst_ref)
            coeffs(latent, length, (last - j) * RG_TC, 1, dst_ref)

        def body(i, carry):
            j = 2 * i
            both(j + 1, coef_b_ref)
            carry = scan(latent, j * RG_TC, (last - j) * RG_TC, coef_a_ref, *carry)
            both(jnp.minimum(j + 2, last), coef_a_ref)
            return scan(latent, (j + 1) * RG_TC, (last - j - 1) * RG_TC, coef_b_ref, *carry)

        both(0, coef_a_ref)
        return lax.fori_loop(0, n_chunks // 2, body, (h_fwd, h_bwd))

    out_ref[...] = jnp.zeros_like(out_ref)
    zero = jnp.zeros((SUBLANES, RG_CT), F32)
    hf, hb = sweep(False, CTX_LEN, zero, zero)
    sweep(True, SEQ, hf, hb)


def _rglru(rx, conv_w, conv_b, a_w, a_b, x_w, x_b, lam):
    assert T_CTX == SEQ, "the context rows must form exactly one latent-sized row block"
    w = jnp.stack([_block_diag_tiles(a_w[0]), _block_diag_tiles(x_w[0]),
                   _block_diag_tiles(a_w[1]), _block_diag_tiles(x_w[1])], axis=1).astype(BF16)
    bias = jnp.stack([a_b[0], x_b[0], a_b[1], x_b[1]], axis=0)
    n_ct = RG_WIDTH // RG_CT
    n_blocks = BATCH + 1
    chan = lambda c: (0, 0, c)
    lane = lambda c: (0, c)
    coef = pltpu.VMEM((2, 2, RG_TC * BATCH, RG_CT), F32)
    slab = pltpu.VMEM(((RG_TC + 2 * RG_HALO) * BATCH, RG_CT), F32)
    chunk = pltpu.VMEM((RG_TC * BATCH, RG_CT), F32)
    out = pl.pallas_call(
        _rg_kernel,
        grid=(n_ct,),
        in_specs=[
            pl.BlockSpec((n_blocks, SEQ, RG_CT), chan),
            pl.BlockSpec((4, RG_CT), lane),
            pl.BlockSpec((1, RG_CT), lane),
            pl.BlockSpec((1, 4, RG_CT, RG_CT), lambda c: (c, 0, 0, 0)),
            pl.BlockSpec((4, RG_CT), lane),
            pl.BlockSpec((2, RG_CT), lane),
        ],
        out_specs=pl.BlockSpec((n_blocks, SEQ, RG_CT), chan),
        out_shape=jax.ShapeDtypeStruct((n_blocks, SEQ, RG_WIDTH), F32),
        scratch_shapes=[
            pltpu.VMEM((BATCH, SEQ + 2 * RG_HALO, RG_CT), F32),
            pltpu.VMEM((BATCH, CTX_LEN + 2 * RG_HALO, RG_CT), F32),
            slab, slab,
            coef, coef,
            chunk, chunk,
        ],
        compiler_params=_params("arbitrary"),
        name="rglru_scan",
    )(rx.reshape(n_blocks, SEQ, RG_WIDTH), conv_w, conv_b.reshape(1, RG_WIDTH), w, bias, lam)
    return out.reshape(T_ALL, RG_WIDTH)


def _dft_tables(length):
    step = 1 << (length.bit_length() // 2)
    s = np.arange(length, dtype=np.int64)

    def thin(kvec):
        ang = ((kvec[:, None] * s[None, :]) % (2 * length)) * (math.pi / length)
        return (jnp.asarray(np.cos(ang)[:, None, :], dtype=F32),
                jnp.asarray(np.sin(ang)[:, None, :], dtype=F32))

    c_hi, s_hi = thin(np.arange(0, length, step, dtype=np.int64))
    c_lo, s_lo = thin(np.arange(step, dtype=np.int64))
    c_lo, s_lo = c_lo.reshape(1, step, length), s_lo.reshape(1, step, length)
    fc = (c_hi * c_lo - s_hi * s_lo).reshape(length, length)
    fs = (s_hi * c_lo + c_hi * s_lo).reshape(length, length)
    return fc.astype(BF16), fs.astype(BF16)


def _filter_features(length):
    f32 = np.float32
    pos = np.arange(length, dtype=f32)
    t = pos / f32(length)
    bands = np.linspace(1e-4, HY_BANDS - 1, HY_BANDS, dtype=f32)
    ang = (f32(2.0 * math.pi) * pos / f32(length))[:, None] * bands[None, :]
    feats = np.concatenate([t[:, None], np.cos(ang), -np.sin(ang)], axis=-1).astype(f32)
    feats = np.pad(feats, ((0, 0), (0, LANES - HY_EMB)))
    deltas = np.abs(np.linspace(HY_MIN_DECAY, HY_MAX_DECAY, HY_WIDTH, dtype=f32))
    window = (np.exp(-t[:, None] * deltas[None, :]) + f32(HY_SHIFT)).astype(f32)
    return jnp.asarray(feats), jnp.asarray(window)


def _filter_kernel(feat_ref, w1_ref, b1_ref, w2_ref, b2_ref, w3f_ref, w3b_ref, sf_ref, win_ref,
                   fc_ref, fs_ref, kc_ref, ks_ref, kn_ref, *, length):
    n = 2 * length
    h = jnp.sin(sf_ref[0:1, :] * (_dot(feat_ref[...].astype(BF16), w1_ref[...]) + b1_ref[...]))
    h = jnp.sin(sf_ref[1:2, :] * (_dot(h.astype(BF16), w2_ref[...]) + b2_ref[...]))
    hb16 = h.astype(BF16)
    win = win_ref[...]
    row = lax.broadcasted_iota(jnp.int32, (length, HY_CT), 0)
    h_fwd = _dot(hb16, w3f_ref[...]) * win
    h_bwd0 = jnp.where(row == 0, 0.0, _dot(hb16, w3b_ref[...]) * win)
    h_sum = h_fwd + h_bwd0
    h_dif = h_fwd - h_bwd0
    weight = jnp.where(row == 0, 1.0 / n, 2.0 / n)
    kc_ref[...] = _dot(fc_ref[...], h_sum.astype(BF16)) * weight
    ks_ref[...] = _dot(fs_ref[...], h_dif.astype(BF16)) * weight
    sign = jnp.where((row & 1) == 0, 1.0, -1.0)
    kn_ref[...] = jnp.sum(h_sum * sign, axis=0, keepdims=True) * (1.0 / n)


def _hyena_filter_spectrum(length, fc, fs, w1, b1, w2, b2, w3, sin_freq):
    feats, window = _filter_features(length)
    hp = LANES - HY_FILT_HIDDEN
    w1p = jnp.pad(w1, ((0, LANES - HY_EMB), (0, hp))).astype(BF16)
    w2p = jnp.pad(w2, ((0, hp), (0, hp))).astype(BF16)
    w3p = jnp.pad(w3, ((0, hp), (0, 0))).astype(BF16)
    b1p = jnp.pad(b1, (0, hp)).reshape(1, LANES)
    b2p = jnp.pad(b2, (0, hp)).reshape(1, LANES)
    sfp = jnp.pad(sin_freq, ((0, 0), (0, hp)))
    spec = jax.ShapeDtypeStruct((length, HY_WIDTH), F32)
    chan = lambda c: (0, c)
    return pl.pallas_call(
        functools.partial(_filter_kernel, length=length),
        grid=(HY_WIDTH // HY_CT,),
        in_specs=[
            _resident((length, LANES)),
            _resident((LANES, LANES)),
            _resident((1, LANES)),
            _resident((LANES, LANES)),
            _resident((1, LANES)),
            pl.BlockSpec((LANES, HY_CT), chan),
            pl.BlockSpec((LANES, HY_CT), chan),
            _resident((2, LANES)),
            pl.BlockSpec((length, HY_CT), chan),
            _resident((length, length)),
            _resident((length, length)),
        ],
        out_specs=[
            pl.BlockSpec((length, HY_CT), chan),
            pl.BlockSpec((length, HY_CT), chan),
            pl.BlockSpec((1, HY_CT), chan),
        ],
        out_shape=[spec, spec, jax.ShapeDtypeStruct((1, HY_WIDTH), F32)],
        compiler_params=_params("arbitrary"),
        name=f"hyena_filter_{length}",
    )(feats, w1p, b1p, w2p, b2p, w3p[:, :HY_WIDTH], w3p[:, HY_WIDTH:], sfp, window, fc, fs)


def _hyena_sequence(raw, cw_ref, cb_ref, bias_ref, fc_ref, fs_ref, kc_ref, ks_ref, kn_ref, length):
    row = lax.broadcasted_iota(jnp.int32, (length, HY_CT), 0)

    def short_conv(x, part):
        prev = jnp.where(row == 0, 0.0, pltpu.roll(x, 1, 0))
        nxt = jnp.where(row == length - 1, 0.0, pltpu.roll(x, length - 1, 0))
        return (prev * cw_ref[part, 0:1, :] + x * cw_ref[part, 1:2, :] + nxt * cw_ref[part, 2:3, :]
                + cb_ref[part:part + 1, :])

    x0 = short_conv(raw[0], 0)
    x1 = short_conv(raw[1], 1)
    v = short_conv(raw[2], 2)
    vx = v * x1
    vb = vx.astype(BF16)
    ft = min(length, HY_FREQ_TILE)
    y = None
    for f0 in range(0, length, ft):
        spec_c = _dot(fc_ref[f0:f0 + ft, :], vb)
        spec_s = _dot(fs_ref[f0:f0 + ft, :], vb)
        kc = kc_ref[f0:f0 + ft, :]
        ks = ks_ref[f0:f0 + ft, :]
        p = (spec_c * kc - spec_s * ks).astype(BF16)
        q = (spec_c * ks + spec_s * kc).astype(BF16)
        part = _dot(fc_ref[:, f0:f0 + ft], p) + _dot(fs_ref[:, f0:f0 + ft], q)
        y = part if y is None else y + part
    sign = jnp.where((row & 1) == 0, 1.0, -1.0)
    nyquist = jnp.sum(vx * sign, axis=0, keepdims=True) * kn_ref[...]
    return x0 * (y + sign * nyquist + vx * bias_ref[...])


def _hyena_kernel(x0_ref, x1_ref, v_ref, cw_ref, cb_ref, bias_ref,
                  fcl_ref, fsl_ref, kcl_ref, ksl_ref, knl_ref,
                  fcc_ref, fsc_ref, kcc_ref, ksc_ref, knc_ref, o_ref):
    step = pl.program_id(1)
    common = (cw_ref, cb_ref, bias_ref)

    @pl.when(step < BATCH)
    def _():
        raw = (x0_ref[...], x1_ref[...], v_ref[...])
        o_ref[...] = _hyena_sequence(raw, *common, fcl_ref, fsl_ref, kcl_ref, ksl_ref, knl_ref, SEQ)

    @pl.when(step == BATCH)
    def _():
        for r in range(BATCH):
            rows = slice(r * CTX_LEN, (r + 1) * CTX_LEN)
            raw = (x0_ref[rows, :], x1_ref[rows, :], v_ref[rows, :])
            o_ref[rows, :] = _hyena_sequence(raw, *common, fcc_ref, fsc_ref, kcc_ref, ksc_ref, knc_ref, CTX_LEN)


def _hyena(hy, tab_lat, spec_lat, tab_ctx, spec_ctx, conv_w, conv_b, bias):
    assert T_CTX == SEQ, "the context rows must form exactly one latent-sized row block"
    n_ct = HY_WIDTH // HY_CT
    cw = conv_w.reshape(3, 3, HY_WIDTH).transpose(1, 0, 2)
    cb = conv_b.reshape(3, HY_WIDTH)

    def part_spec(part):
        return pl.BlockSpec((SEQ, HY_CT), lambda c, b: (b, part * n_ct + c))

    chan = lambda c, b: (0, c)

    def table_specs(length):
        return [
            _resident((length, length)),
            _resident((length, length)),
            pl.BlockSpec((length, HY_CT), chan, pipeline_mode=pl.Buffered(1)),
            pl.BlockSpec((length, HY_CT), chan, pipeline_mode=pl.Buffered(1)),
            pl.BlockSpec((1, HY_CT), chan),
        ]

    return pl.pallas_call(
        _hyena_kernel,
        grid=(n_ct, BATCH + 1),
        in_specs=[
            part_spec(0), part_spec(1), part_spec(2),
            pl.BlockSpec((3, 3, HY_CT), lambda c, b: (0, 0, c)),
            pl.BlockSpec((3, HY_CT), chan),
            pl.BlockSpec((1, HY_CT), chan),
            *table_specs(SEQ),
            *table_specs(CTX_LEN),
        ],
        out_specs=pl.BlockSpec((SEQ, HY_CT), lambda c, b: (b, c)),
        out_shape=jax.ShapeDtypeStruct((T_ALL, HY_WIDTH), F32),
        compiler_params=_params("arbitrary", "arbitrary"),
        name="hyena_conv",
    )(hy, hy, hy, cw, cb, bias.reshape(1, HY_WIDTH), *tab_lat, *spec_lat, *tab_ctx, *spec_ctx)


def _rope_partner(n):
    idx = np.arange(n)
    return np.where((idx % (2 * ROPE_AXIS_PAIRS)) < ROPE_AXIS_PAIRS, idx + ROPE_AXIS_PAIRS, idx - ROPE_AXIS_PAIRS)


def _slot_lanes():
    d = np.arange(MLA_ROPE)
    axis, second, pair = d // (2 * ROPE_AXIS_PAIRS), (d // ROPE_AXIS_PAIRS) % 2, d % ROPE_AXIS_PAIRS
    rope_lane = axis * ROPE_AXIS_PAIRS + pair + (HEAD_SLOT // 2) * second
    free = np.setdiff1d(np.arange(HEAD_SLOT), rope_lane)
    return rope_lane, free[:MLA_NOPE]


def _to_slot(nope, rope):
    rope_lane, nope_lane = _slot_lanes()
    source = [(None, 0)] * HEAD_SLOT
    for arr, lanes in ((nope, nope_lane), (rope, rope_lane)):
        if arr is not None:
            for idx, lane in enumerate(lanes):
                source[lane] = (arr, idx)
    like = nope if nope is not None else rope
    pieces, lane = [], 0
    while lane < HEAD_SLOT:
        arr, start = source[lane]
        run = 1
        while (lane + run < HEAD_SLOT and source[lane + run][0] is arr
               and (arr is None or source[lane + run][1] == start + run)):
            run += 1
        pieces.append(jnp.zeros(like.shape[:-1] + (run,), like.dtype) if arr is None else arr[..., start:start + run])
        lane += run
    return jnp.concatenate(pieces, axis=-1)


def _rope_tables():
    f32 = np.float32
    rope_lane, nope_lane = _slot_lanes()
    rows = np.repeat(np.arange(SEQ // GRID_W, dtype=f32), GRID_W)
    cols = np.tile(np.arange(GRID_W, dtype=f32), SEQ // GRID_W)
    inv_freq = (f32(ROPE_BASE) ** (-np.arange(ROPE_AXIS_PAIRS, dtype=f32) / f32(ROPE_AXIS_PAIRS))).astype(f32)
    ang_r = rows[:, None] * inv_freq
    ang_c = cols[:, None] * inv_freq
    cos32 = np.concatenate([np.cos(ang_r)] * 2 + [np.cos(ang_c)] * 2, axis=1)
    sin32 = np.concatenate([-np.sin(ang_r), np.sin(ang_r), -np.sin(ang_c), np.sin(ang_c)], axis=1)

    def table(n_rows, rope_vals, nope_val):
        t = np.zeros((n_rows, HEAD_SLOT), f32)
        t[:, rope_lane] = rope_vals
        t[:, nope_lane] = nope_val
        return t

    q_cos = table(SEQ, cos32, 1.0) * f32(Q_SCALE)
    q_sin = table(SEQ, sin32, 0.0) * f32(Q_SCALE)
    k_cos = table(SEQ, cos32, 0.0)
    k_sin = table(SEQ, sin32, 0.0)
    ident = table(TM_PROJ, 1.0, 0.0)
    zero = np.zeros((TM_PROJ, HEAD_SLOT), f32)
    tables = ((q_cos, zero), (q_sin, zero), (k_cos, ident), (k_sin, zero))
    return tuple(jnp.asarray(np.concatenate(t, 0).astype(f32)) for t in tables)


def _mla_weights(w_in, w_q_up, w_kv_up):
    w_q = w_in[:, :MLA_Q_LORA]
    w_kv = w_in[:, MLA_Q_LORA:MLA_Q_LORA + MLA_KV_LORA]
    w_kr = w_in[:, MLA_Q_LORA + MLA_KV_LORA:]
    w_kr2 = jnp.concatenate([_to_slot(None, w_kr), _to_slot(None, w_kr[:, _rope_partner(MLA_ROPE)])], axis=1)

    qh = w_q_up.reshape(MLA_Q_LORA, MLA_HEADS, MLA_NOPE + MLA_ROPE)
    wq_slot = _to_slot(qh[..., :MLA_NOPE], qh[..., MLA_NOPE:]).reshape(MLA_Q_LORA, MLA_HEADS * HEAD_SLOT)

    kvh = w_kv_up.reshape(MLA_KV_LORA, MLA_HEADS, MLA_NOPE + MLA_V)
    wk_slot = _to_slot(kvh[..., :MLA_NOPE], None).reshape(MLA_KV_LORA, MLA_HEADS * HEAD_SLOT)
    wv_slot = jnp.pad(kvh[..., MLA_NOPE:], ((0, 0), (0, 0), (0, HEAD_SLOT - MLA_V)))
    wv_slot = wv_slot.reshape(MLA_KV_LORA, MLA_HEADS * HEAD_SLOT)
    return [w.astype(BF16) for w in (w_q, w_kv, w_kr2, wq_slot, wk_slot, wv_slot)]


def _mlaproj_kernel(x_ref, mod_ref, wq_ref, wkv_ref, wkr_ref, wqs_ref, wks_ref, wv_ref,
                    vone_ref, qg_ref, kvg_ref, qcos_ref, qsin_ref, kcos_ref, ksin_ref,
                    q_ref, k_ref, v_ref):
    shift, scale = mod_ref[0, 0], mod_ref[0, 1]
    h = (x_ref[...] * (1.0 + scale) + shift).astype(BF16)

    def rmsnorm(y, g):
        return (y * lax.rsqrt(jnp.mean(y * y, axis=-1, keepdims=True) + RMS_EPS) * g).astype(BF16)

    qn = rmsnorm(_dot(h, wq_ref[...]), qg_ref[...])
    kvn = rmsnorm(_dot(h, wkv_ref[...]), kvg_ref[...])
    kr2 = _dot(h, wkr_ref[...])
    k_rope = kr2[:, :HEAD_SLOT] * kcos_ref[...] + kr2[:, HEAD_SLOT:] * ksin_ref[...]
    k_nope = _dot(kvn, wks_ref[...])
    v_ref[...] = (_dot(kvn, wv_ref[...]) + vone_ref[...]).astype(BF16)
    q_all = _dot(qn, wqs_ref[...])
    q_cos = qcos_ref[...]
    q_sin = qsin_ref[...]
    for hd in range(MLA_HEADS):
        sl = slice(hd * HEAD_SLOT, (hd + 1) * HEAD_SLOT)
        qh = q_all[:, sl]
        rotated = qh * q_cos + pltpu.roll(qh, HEAD_SLOT // 2, 1) * q_sin
        q_ref[:, sl] = rotated.astype(BF16)
        k_ref[:, sl] = (k_nope[:, sl] + k_rope).astype(BF16)


def _mlaproj(x, mod, w_in, q_norm, kv_norm, w_q_up, w_kv_up):
    tm = TM_PROJ
    weights = _mla_weights(w_in, w_q_up, w_kv_up)
    tables = _rope_tables()
    row = lambda i: (i, 0)
    tab = lambda i: (jnp.where(i < T_LAT // tm, i % (SEQ // tm), SEQ // tm), 0)
    wide = MLA_HEADS * HEAD_SLOT
    v_one = jnp.asarray((np.arange(wide) % HEAD_SLOT == MLA_V).astype(np.float32).reshape(1, wide))
    return pl.pallas_call(
        _mlaproj_kernel,
        grid=(T_ALL // tm,),
        in_specs=[
            pl.BlockSpec((tm, D_MODEL), row),
            pl.BlockSpec((1, 2, 1, D_MODEL), lambda i: (_group_of_tile(i, tm), 0, 0, 0)),
            *[_resident(w.shape) for w in weights],
            _resident((1, wide)),
            _resident((1, MLA_Q_LORA)),
            _resident((1, MLA_KV_LORA)),
            *[pl.BlockSpec((tm, HEAD_SLOT), tab) for _ in tables],
        ],
        out_specs=[pl.BlockSpec((tm, wide), row)] * 3,
        out_shape=[jax.ShapeDtypeStruct((T_ALL, wide), BF16)] * 3,
        compiler_params=_params("arbitrary"),
        name="mla_proj",
    )(x, mod, *weights, v_one, q_norm.reshape(1, MLA_Q_LORA), kv_norm.reshape(1, MLA_KV_LORA), *tables)


def _attn_kernel(q_ref, kl_ref, kc_ref, vl_ref, vc_ref, o_ref):
    lane = lax.broadcasted_iota(jnp.int32, (TQ, HEAD_SLOT), 1)
    outs = []
    for hd in range(ATTN_HEADS):
        sl = slice(hd * HEAD_SLOT, (hd + 1) * HEAD_SLOT)
        q = q_ref[:, sl]
        s_lat = _dot_nt(q, kl_ref[:, sl])
        s_ctx = _dot_nt(q, kc_ref[:, sl])
        m = jnp.maximum(jnp.max(s_lat, axis=-1, keepdims=True), jnp.max(s_ctx, axis=-1, keepdims=True))
        p_lat = jnp.exp2(s_lat - m).astype(BF16)
        p_ctx = jnp.exp2(s_ctx - m).astype(BF16)
        o = _dot(p_lat, vl_ref[:, sl]) + _dot(p_ctx, vc_ref[:, sl])
        outs.append(o / o[:, MLA_V:MLA_V + 1])
    for pair in range(ATTN_HEADS // 2):
        packed = jnp.where(lane < MLA_V, outs[2 * pair], pltpu.roll(outs[2 * pair + 1], MLA_V, 1))
        o_ref[:, pair * HEAD_SLOT:(pair + 1) * HEAD_SLOT] = packed.astype(BF16)


def _attention(q, k, v):
    n_q = SEQ // TQ
    n_groups = MLA_HEADS // ATTN_HEADS
    ctx_block0 = T_LAT // CTX_LEN
    wide = ATTN_HEADS * HEAD_SLOT
    lat = lambda b, h, i: (b, h)
    ctx = lambda b, h, i: (ctx_block0 + b, h)
    return pl.pallas_call(
        _attn_kernel,
        grid=(BATCH, n_groups, n_q),
        in_specs=[
            pl.BlockSpec((TQ, wide), lambda b, h, i: (b * n_q + i, h)),
            pl.BlockSpec((SEQ, wide), lat),
            pl.BlockSpec((CTX_LEN, wide), ctx),
            pl.BlockSpec((SEQ, wide), lat),
            pl.BlockSpec((CTX_LEN, wide), ctx),
        ],
        out_specs=pl.BlockSpec((TQ, ATTN_HEADS * MLA_V), lambda b, h, i: (b * n_q + i, h)),
        out_shape=jax.ShapeDtypeStruct((T_LAT, MLA_HEADS * MLA_V), BF16),
        compiler_params=_params("arbitrary", "arbitrary", "arbitrary"),
        name="mla_attention",
    )(q, k, k, v, v)


def kernel(x, c, ctx, c_ctx, ada_w, ada_b, ln_g, ln_b, ffn_w_in, ffn_w_out, ev_w_in, ev_w_out, hy_conv_w, hy_conv_b, hy_filt_w1, hy_filt_b1, hy_filt_w2, hy_filt_b2, hy_filt_w3, hy_sin_freq, hy_bias, rg_conv_w, rg_conv_b, rg_a_w, rg_a_b, rg_x_w, rg_x_b, rg_lambda, mla_w_in, mla_q_norm, mla_kv_norm, mla_w_q_up, mla_w_kv_up, mla_w_out):
    assert x.shape == (BATCH, SEQ, D_MODEL) and ctx.shape == (BATCH, CTX_LEN, D_MODEL)
    cond = jnp.concatenate([c, c_ctx[None], jnp.zeros((2 * SUBLANES - N_GROUPS, D_MODEL), F32)], axis=0)
    mods = _adaln(cond, ada_w, ada_b)[:, :N_GROUPS].reshape(DEPTH, N_GROUPS, N_ADA, 1, D_MODEL)
    ffn_w = (ffn_w_in, ffn_w_out)

    m = mods[0]
    xs = _ffn_split(x.reshape(T_LAT, D_MODEL), ctx.reshape(T_CTX, D_MODEL),
                    m[:, 0:3], ffn_w, (0, 0), ln_g[0, 0], ln_b[0, 0], T_ALL)
    hy, rx, gate_in = _evproj(xs, m[:, 3:5], ev_w_in[0])
    h_sum = _rglru(rx, rg_conv_w[0], rg_conv_b[0], rg_a_w[0], rg_a_b[0], rg_x_w[0], rg_x_b[0], rg_lambda[0])
    filt = (hy_filt_w1[0], hy_filt_b1[0], hy_filt_w2[0], hy_filt_b2[0], hy_filt_w3[0], hy_sin_freq[0])
    tab_l = _dft_tables(SEQ)
    tab_c = _dft_tables(CTX_LEN)
    spec_l = _hyena_filter_spectrum(SEQ, *tab_l, *filt)
    spec_c = _hyena_filter_spectrum(CTX_LEN, *tab_c, *filt)
    y_hy = _hyena(hy, tab_l, spec_l, tab_c, spec_c, hy_conv_w[0], hy_conv_b[0], hy_bias[0])
    xs = _ffn_after_even(xs, y_hy, h_sum, gate_in, ev_w_out[0], ln_g[0, 1], ln_b[0, 1],
                         m[:, 5:9], ffn_w, (0, 1), ln_g[0, 2], ln_b[0, 2], T_ALL)

    m = mods[1]
    xs = _ffn_plain(xs, m[:, 0:3], ffn_w, (1, 0), ln_g[1, 0], ln_b[1, 0], T_ALL)
    q, k, v = _mlaproj(xs, m[:, 3:5], mla_w_in[0], mla_q_norm[0], mla_kv_norm[0], mla_w_q_up[0], mla_w_kv_up[0])
    att = _attention(q, k, v)
    xl = _ffn_after_mla(xs, att, mla_w_out[0], ln_g[1, 1], ln_b[1, 1],
                        m[:, 5:9], ffn_w, (1, 1), ln_g[1, 2], ln_b[1, 2], T_LAT)
    return xl.reshape(BATCH, SEQ, D_MODEL)
```

```python
import functools
import math

import jax
import jax.numpy as jnp
import numpy as np
from jax import lax
from jax.experimental import pallas as pl
from jax.experimental.pallas import tpu as pltpu

F32 = jnp.float32
BF16 = jnp.bfloat16

D_MODEL = 1024
BATCH = 8
SEQ = 2048
DEPTH = 2
CTX_LEN = 256
GRID_W = 64
N_ADA = 9
D_FF = 2816

HY_WIDTH = 512
RG_WIDTH = 512
EV_IN = 3 * HY_WIDTH + 2 * RG_WIDTH
HY_EMB = 33
HY_BANDS = (HY_EMB - 1) // 2
HY_FILT_HIDDEN = 64
HY_TARGET = 1e-2
HY_MAX_DECAY = math.log(HY_TARGET) / 0.3
HY_MIN_DECAY = math.log(HY_TARGET) / 1.5
HY_SHIFT = 0.05
RG_BLOCKS = 8
RG_BLOCK_DIM = RG_WIDTH // RG_BLOCKS
RG_C = 8.0
RG_CONV = 4
RG_PAD_L = 2

MLA_HEADS = 16
MLA_Q_LORA = 768
MLA_KV_LORA = 256
MLA_NOPE = 64
MLA_ROPE = 32
MLA_V = 64
MLA_SCALE = (MLA_NOPE + MLA_ROPE) ** -0.5
ROPE_AXIS_PAIRS = MLA_ROPE // 4
ROPE_BASE = 10000.0

LOG2_E = math.log2(math.e)
ALPHA = (2.0 * DEPTH) ** 0.25
LN_EPS = 1e-6
RMS_EPS = 1e-6

T_LAT = BATCH * SEQ
T_CTX = BATCH * CTX_LEN
T_ALL = T_LAT + T_CTX
N_GROUPS = BATCH + 1

LANES = 128
SUBLANES = 8
VMEM_LIMIT = 56 * 1024 * 1024

TM_FFN = 512
FF_CHUNK = 256
N_FF_CHUNKS = D_FF // FF_CHUNK
FFN_WEIGHT_CHUNKS = 8
TM_PROJ = 512
TM_EVPROJ = 1024
HY_CT = 256
HY_FREQ_TILE = 256
RG_CT = 128
RG_TC = 32
RG_HALO = SUBLANES
HEAD_SLOT = 128
TQ = 512
ATTN_HEADS = 16
Q_SCALE = MLA_SCALE * math.log2(math.e)
ADA_TN = 1152


def _dot(a, b):
    return jnp.dot(a, b, preferred_element_type=F32)


def _dot_nt(a, b):
    return lax.dot_general(a, b, (((1,), (1,)), ((), ())), preferred_element_type=F32)


def _resident(shape):
    nd = len(shape)
    return pl.BlockSpec(shape, lambda *_: (0,) * nd, pipeline_mode=pl.Buffered(1))


def _group_of_tile(i, tm):
    return jnp.where(i < T_LAT // tm, i // (SEQ // tm), BATCH)


def _params(*sem):
    return pltpu.CompilerParams(dimension_semantics=sem, vmem_limit_bytes=VMEM_LIMIT)


def _layernorm(z, g, b):
    mu = jnp.mean(z, axis=-1, keepdims=True)
    zc = z - mu
    var = jnp.mean(zc * zc, axis=-1, keepdims=True)
    return zc * lax.rsqrt(var + LN_EPS) * g + b


def _adaln_kernel(c_ref, w_ref, b_ref, o_ref):
    c = c_ref[...]
    s = (c * jax.nn.sigmoid(c)).astype(BF16)
    o_ref[0] = _dot(s, w_ref[0].astype(BF16)) + b_ref[0]


def _adaln(cond, ada_w, ada_b):
    n_out = N_ADA * D_MODEL
    rows = cond.shape[0]
    return pl.pallas_call(
        _adaln_kernel,
        grid=(DEPTH, n_out // ADA_TN),
        in_specs=[
            pl.BlockSpec((rows, D_MODEL), lambda l, j: (0, 0)),
            pl.BlockSpec((1, D_MODEL, ADA_TN), lambda l, j: (l, 0, j)),
            pl.BlockSpec((1, 1, ADA_TN), lambda l, j: (l, 0, j)),
        ],
        out_specs=pl.BlockSpec((1, rows, ADA_TN), lambda l, j: (l, 0, j)),
        out_shape=jax.ShapeDtypeStruct((DEPTH, rows, n_out), F32),
        compiler_params=_params("arbitrary", "arbitrary"),
        name="adaln",
    )(cond, ada_w, ada_b.reshape(DEPTH, 1, n_out))


def _ffn_body(x, mod_ref, wi_ref, wo_ref, g_ref, b_ref, o_ref, act_ref):
    n_slots = mod_ref.shape[1]
    shift, scale, gate = mod_ref[0, n_slots - 3], mod_ref[0, n_slots - 2], mod_ref[0, n_slots - 1]
    h = (x * (1.0 + scale) + shift).astype(BF16)
    for j in range(N_FF_CHUNKS):
        lo = j * FF_CHUNK
        a = _dot(h, wi_ref[:, lo:lo + FF_CHUNK])
        u = _dot(h, wi_ref[:, D_FF + lo:D_FF + lo + FF_CHUNK])
        act_ref[:, lo:lo + FF_CHUNK] = (a * jax.nn.sigmoid(a) * u).astype(BF16)
    y = _dot(act_ref[...], wo_ref[...])
    o_ref[...] = _layernorm(ALPHA * x + (0.5 * gate) * y, g_ref[...], b_ref[...])


def _load_weight_bf16(w_hbm, which, dst_ref, stage_ref, sem):
    layer, half = which
    rows = stage_ref.shape[1]
    n_chunks = dst_ref.shape[0] // rows

    def copy(c):
        return pltpu.make_async_copy(w_hbm.at[layer, half, pl.ds(c * rows, rows)], stage_ref.at[c % 2], sem.at[c % 2])

    copy(0).start()
    for c in range(n_chunks):
        if c + 1 < n_chunks:
            copy(c + 1).start()
        copy(c).wait()
        dst_ref[c * rows:(c + 1) * rows, :] = stage_ref[c % 2].astype(BF16)


def _ffn_steps(x_fn, which, mod_ref, wi_hbm, wo_hbm, g_ref, b_ref, o_ref,
               act_ref, wi_ref, wo_ref, stage_in_ref, stage_out_ref, sem_in, sem_out):
    @pl.when(pl.program_id(0) == 0)
    def _():
        _load_weight_bf16(wi_hbm, which, wi_ref, stage_in_ref, sem_in)
        _load_weight_bf16(wo_hbm, which, wo_ref, stage_out_ref, sem_out)

    _ffn_body(x_fn(), mod_ref, wi_ref, wo_ref, g_ref, b_ref, o_ref, act_ref)


def _ffn_kernel(x_ref, *rest, which):
    _ffn_steps(lambda: x_ref[...], which, *rest)


def _ffn_split_kernel(xl_ref, xc_ref, *rest, which):
    _ffn_steps(lambda: jnp.where(pl.program_id(0) < T_LAT // TM_FFN, xl_ref[...], xc_ref[...]), which, *rest)


def _ffn_after_even_kernel(x_ref, yh_ref, hs_ref, gt_ref, wmix_ref, g1_ref, b1_ref, mod_ref, *rest, which):
    def x_fn():
        y_rg = hs_ref[...] * jax.nn.gelu(gt_ref[...], approximate=True)
        y = (_dot(yh_ref[...].astype(BF16), wmix_ref[:HY_WIDTH, :])
             + _dot(y_rg.astype(BF16), wmix_ref[HY_WIDTH:, :]))
        return _layernorm(ALPHA * x_ref[...] + mod_ref[0, 0] * y, g1_ref[...], b1_ref[...])

    _ffn_steps(x_fn, which, mod_ref, *rest)


def _ffn_after_mla_kernel(x_ref, att_ref, wmix_ref, g1_ref, b1_ref, mod_ref, *rest, which):
    def x_fn():
        y = _dot(att_ref[...], wmix_ref[...])
        return _layernorm(ALPHA * x_ref[...] + mod_ref[0, 0] * y, g1_ref[...], b1_ref[...])

    _ffn_steps(x_fn, which, mod_ref, *rest)


def _ffn(body, lead_args, lead_specs, mod, ffn_w, which, g, b, n_rows):
    tm = TM_FFN
    w_in_all, w_out_all = ffn_w
    return pl.pallas_call(
        functools.partial(body, which=which),
        grid=(n_rows // tm,),
        in_specs=[
            *lead_specs,
            pl.BlockSpec((1, mod.shape[1], 1, D_MODEL), lambda i: (_group_of_tile(i, tm), 0, 0, 0)),
            pl.BlockSpec(memory_space=pl.ANY),
            pl.BlockSpec(memory_space=pl.ANY),
            _resident((1, D_MODEL)),
            _resident((1, D_MODEL)),
        ],
        out_specs=pl.BlockSpec((tm, D_MODEL), lambda i: (i, 0)),
        out_shape=jax.ShapeDtypeStruct((n_rows, D_MODEL), F32),
        scratch_shapes=[
            pltpu.VMEM((tm, D_FF), BF16),
            pltpu.VMEM((D_MODEL, 2 * D_FF), BF16),
            pltpu.VMEM((D_FF, D_MODEL), BF16),
            pltpu.VMEM((2, D_MODEL // FFN_WEIGHT_CHUNKS, 2 * D_FF), F32),
            pltpu.VMEM((2, D_FF // FFN_WEIGHT_CHUNKS, D_MODEL), F32),
            pltpu.SemaphoreType.DMA((2,)),
            pltpu.SemaphoreType.DMA((2,)),
        ],
        compiler_params=_params("arbitrary"),
        name="half_ffn",
    )(*lead_args, mod, w_in_all, w_out_all, g.reshape(1, D_MODEL), b.reshape(1, D_MODEL))


def _row_spec(width, tm=TM_FFN):
    return pl.BlockSpec((tm, width), lambda i: (i, 0))


def _ffn_plain(x, *args):
    return _ffn(_ffn_kernel, (x,), [_row_spec(D_MODEL)], *args)


def _ffn_split(x_lat, x_ctx, *args):
    n_lat = T_LAT // TM_FFN
    specs = [pl.BlockSpec((TM_FFN, D_MODEL), lambda i: (jnp.minimum(i, n_lat - 1), 0)),
             pl.BlockSpec((TM_FFN, D_MODEL), lambda i: (jnp.maximum(i - n_lat, 0), 0))]
    return _ffn(_ffn_split_kernel, (x_lat, x_ctx), specs, *args)


def _ffn_after_even(x, y_hy, h_sum, gate_in, w_mix, g1, b1, *args):
    lead = (x, y_hy, h_sum, gate_in, w_mix.astype(BF16), g1.reshape(1, D_MODEL), b1.reshape(1, D_MODEL))
    specs = [_row_spec(D_MODEL), _row_spec(HY_WIDTH), _row_spec(RG_WIDTH), _row_spec(RG_WIDTH),
             _resident((HY_WIDTH + RG_WIDTH, D_MODEL)), _resident((1, D_MODEL)), _resident((1, D_MODEL))]
    return _ffn(_ffn_after_even_kernel, lead, specs, *args)


def _ffn_after_mla(x, att, w_mix, g1, b1, *args):
    lead = (x, att, w_mix.astype(BF16), g1.reshape(1, D_MODEL), b1.reshape(1, D_MODEL))
    specs = [_row_spec(D_MODEL), _row_spec(MLA_HEADS * MLA_V),
             _resident((MLA_HEADS * MLA_V, D_MODEL)), _resident((1, D_MODEL)), _resident((1, D_MODEL))]
    return _ffn(_ffn_after_mla_kernel, lead, specs, *args)


def _evproj_kernel(x_ref, mod_ref, w_ref, hy_ref, rx_ref, gt_ref):
    shift, scale = mod_ref[0, 0], mod_ref[0, 1]
    h = (x_ref[...] * (1.0 + scale) + shift).astype(BF16)
    n_hy = 3 * HY_WIDTH
    hy_ref[...] = _dot(h, w_ref[:, :n_hy])
    rx_ref[...] = _dot(h, w_ref[:, n_hy:n_hy + RG_WIDTH])
    gt_ref[...] = _dot(h, w_ref[:, n_hy + RG_WIDTH:])


def _evproj(x, mod, w_in):
    tm = TM_EVPROJ
    row = lambda i: (i, 0)
    return pl.pallas_call(
        _evproj_kernel,
        grid=(T_ALL // tm,),
        in_specs=[
            pl.BlockSpec((tm, D_MODEL), row),
            pl.BlockSpec((1, 2, 1, D_MODEL), lambda i: (_group_of_tile(i, tm), 0, 0, 0)),
            _resident((D_MODEL, EV_IN)),
        ],
        out_specs=[
            pl.BlockSpec((tm, 3 * HY_WIDTH), row),
            pl.BlockSpec((tm, RG_WIDTH), row),
            pl.BlockSpec((tm, RG_WIDTH), row),
        ],
        out_shape=[
            jax.ShapeDtypeStruct((T_ALL, 3 * HY_WIDTH), F32),
            jax.ShapeDtypeStruct((T_ALL, RG_WIDTH), F32),
            jax.ShapeDtypeStruct((T_ALL, RG_WIDTH), F32),
        ],
        compiler_params=_params("arbitrary"),
        name="even_in_proj",
    )(x, mod, w_in.astype(BF16))


def _block_diag_tiles(w):
    per = RG_CT // RG_BLOCK_DIM
    w = w.reshape(RG_WIDTH // RG_CT, per, RG_BLOCK_DIM, RG_BLOCK_DIM)
    eye = jnp.eye(per, dtype=w.dtype)
    return jnp.einsum('cpde,pq->cpdqe', w, eye).reshape(RG_WIDTH // RG_CT, RG_CT, RG_CT)


def _rg_kernel(rx_ref, cw_ref, cb_ref, w_ref, bias_ref, lam_ref, out_ref,
                padl_ref, padc_ref, xt_f_ref, xt_b_ref, coef_a_ref, coef_b_ref, hbuf_f_ref, hbuf_b_ref):
    neg_lam = -lam_ref[...]
    softplus = jnp.maximum(neg_lam, 0.0) + jnp.log1p(jnp.exp(-jnp.abs(neg_lam)))
    neg_log_a_scale = RG_C * softplus
    a_exp2_scale = -LOG2_E * neg_log_a_scale
    cb = cb_ref[...]
    xts = (xt_f_ref, xt_b_ref)
    hbuf = (hbuf_f_ref, hbuf_b_ref)
    slab_rows = RG_TC + 2 * RG_HALO

    def step_rows(s):
        return slice(s * SUBLANES, (s + 1) * SUBLANES)

    halo = jnp.zeros((BATCH, RG_HALO, RG_CT), F32)
    for pad_ref, length in ((padl_ref, SEQ), (padc_ref, CTX_LEN)):
        pad_ref[:, 0:RG_HALO, :] = halo
        pad_ref[:, RG_HALO + length:, :] = halo
    padl_ref[:, RG_HALO:RG_HALO + SEQ, :] = rx_ref[0:BATCH]
    for b in range(BATCH):
        padc_ref[b, RG_HALO:RG_HALO + CTX_LEN, :] = rx_ref[BATCH, b * CTX_LEN:(b + 1) * CTX_LEN, :]

    def add_out(latent, t_start, d):
        t_start = pl.multiple_of(t_start, SUBLANES)
        for b in range(BATCH):
            rows = hbuf[d][pl.ds(b, RG_TC, stride=SUBLANES), :]
            if latent:
                out_ref[b, pl.ds(t_start, RG_TC), :] += rows
            else:
                out_ref[BATCH, pl.ds(b * CTX_LEN + t_start, RG_TC), :] += rows

    def sigmoid(z):
        return 1.0 / (1.0 + jnp.exp2(z * (-LOG2_E)))

    def coeffs(latent, length, t0, d, dst_ref):
        t0 = pl.multiple_of(jnp.asarray(t0, jnp.int32), SUBLANES)
        pad_ref = padl_ref if latent else padc_ref
        xt = xts[d]
        for b in range(BATCH):
            xt[pl.ds(b, slab_rows, stride=SUBLANES), :] = pad_ref[b, pl.ds(t0, slab_rows), :]
        x2 = cb
        for k in range(RG_CONV):
            start = (RG_HALO - RG_PAD_L + k) * SUBLANES
            x2 = x2 + xt[start:start + RG_TC * SUBLANES, :] * cw_ref[k:k + 1, :]
        xb = x2.astype(BF16)
        r = sigmoid(_dot(xb, w_ref[0, 2 * d]) + bias_ref[2 * d:2 * d + 1, :])
        i = sigmoid(_dot(xb, w_ref[0, 2 * d + 1]) + bias_ref[2 * d + 1:2 * d + 2, :])
        a = jnp.exp2(r * a_exp2_scale[d:d + 1, :])
        b = jnp.sqrt(jnp.tanh(r * neg_log_a_scale[d:d + 1, :]) * (a * a + 1.0)) * (i * x2)
        dst_ref[d, 0] = a
        dst_ref[d, 1] = b

    def scan(latent, tf, tb, src_ref, hf, hb):
        tf = jnp.asarray(tf, jnp.int32)
        tb = jnp.asarray(tb, jnp.int32)
        for s in range(RG_TC):
            rows = step_rows(s)
            hf = src_ref[0, 0, rows, :] * hf + src_ref[0, 1, rows, :]
            hbuf_f_ref[rows, :] = hf
            rows = step_rows(RG_TC - 1 - s)
            hb = src_ref[1, 0, rows, :] * hb + src_ref[1, 1, rows, :]
            hbuf_b_ref[rows, :] = hb
        add_out(latent, tf, 0)
        add_out(latent, tb, 1)
        return hf, hb

    def sweep(latent, length, h_fwd, h_bwd):
        n_chunks = length // RG_TC
        last = n_chunks - 1

        def both(j, dst_ref):
            coeffs(latent, length, j * RG_TC, 0, dst_ref)
            coeffs(latent, length, (last - j) * RG_TC, 1, dst_ref)

        def body(i, carry):
            j = 2 * i
            both(j + 1, coef_b_ref)
            carry = scan(latent, j * RG_TC, (last - j) * RG_TC, coef_a_ref, *carry)
            both(jnp.minimum(j + 2, last), coef_a_ref)
            return scan(latent, (j + 1) * RG_TC, (last - j - 1) * RG_TC, coef_b_ref, *carry)

        both(0, coef_a_ref)
        return lax.fori_loop(0, n_chunks // 2, body, (h_fwd, h_bwd))

    out_ref[...] = jnp.zeros_like(out_ref)
    zero = jnp.zeros((SUBLANES, RG_CT), F32)
    hf, hb = sweep(False, CTX_LEN, zero, zero)
    sweep(True, SEQ, hf, hb)


def _rglru(rx, conv_w, conv_b, a_w, a_b, x_w, x_b, lam):
    assert T_CTX == SEQ, "the context rows must form exactly one latent-sized row block"
    w = jnp.stack([_block_diag_tiles(a_w[0]), _block_diag_tiles(x_w[0]),
                   _block_diag_tiles(a_w[1]), _block_diag_tiles(x_w[1])], axis=1).astype(BF16)
    bias = jnp.stack([a_b[0], x_b[0], a_b[1], x_b[1]], axis=0)
    n_ct = RG_WIDTH // RG_CT
    n_blocks = BATCH + 1
    chan = lambda c: (0, 0, c)
    lane = lambda c: (0, c)
    coef = pltpu.VMEM((2, 2, RG_TC * BATCH, RG_CT), F32)
    slab = pltpu.VMEM(((RG_TC + 2 * RG_HALO) * BATCH, RG_CT), F32)
    chunk = pltpu.VMEM((RG_TC * BATCH, RG_CT), F32)
    out = pl.pallas_call(
        _rg_kernel,
        grid=(n_ct,),
        in_specs=[
            pl.BlockSpec((n_blocks, SEQ, RG_CT), chan),
            pl.BlockSpec((4, RG_CT), lane),
            pl.BlockSpec((1, RG_CT), lane),
            pl.BlockSpec((1, 4, RG_CT, RG_CT), lambda c: (c, 0, 0, 0)),
            pl.BlockSpec((4, RG_CT), lane),
            pl.BlockSpec((2, RG_CT), lane),
        ],
        out_specs=pl.BlockSpec((n_blocks, SEQ, RG_CT), chan),
        out_shape=jax.ShapeDtypeStruct((n_blocks, SEQ, RG_WIDTH), F32),
        scratch_shapes=[
            pltpu.VMEM((BATCH, SEQ + 2 * RG_HALO, RG_CT), F32),
            pltpu.VMEM((BATCH, CTX_LEN + 2 * RG_HALO, RG_CT), F32),
            slab, slab,
            coef, coef,
            chunk, chunk,
        ],
        compiler_params=_params("arbitrary"),
        name="rglru_scan",
    )(rx.reshape(n_blocks, SEQ, RG_WIDTH), conv_w, conv_b.reshape(1, RG_WIDTH), w, bias, lam)
    return out.reshape(T_ALL, RG_WIDTH)


def _dft_tables(length):
    step = 1 << (length.bit_length() // 2)
    s = np.arange(length, dtype=np.int64)

    def thin(kvec):
        ang = ((kvec[:, None] * s[None, :]) % (2 * length)) * (math.pi / length)
        return (jnp.asarray(np.cos(ang)[:, None, :], dtype=F32),
                jnp.asarray(np.sin(ang)[:, None, :], dtype=F32))

    c_hi, s_hi = thin(np.arange(0, length, step, dtype=np.int64))
    c_lo, s_lo = thin(np.arange(step, dtype=np.int64))
    c_lo, s_lo = c_lo.reshape(1, step, length), s_lo.reshape(1, step, length)
    fc = (c_hi * c_lo - s_hi * s_lo).reshape(length, length)
    fs = (s_hi * c_lo + c_hi * s_lo).reshape(length, length)
    return fc.astype(BF16), fs.astype(BF16)


def _filter_features(length):
    f32 = np.float32
    pos = np.arange(length, dtype=f32)
    t = pos / f32(length)
    bands = np.linspace(1e-4, HY_BANDS - 1, HY_BANDS, dtype=f32)
    ang = (f32(2.0 * math.pi) * pos / f32(length))[:, None] * bands[None, :]
    feats = np.concatenate([t[:, None], np.cos(ang), -np.sin(ang)], axis=-1).astype(f32)
    feats = np.pad(feats, ((0, 0), (0, LANES - HY_EMB)))
    deltas = np.abs(np.linspace(HY_MIN_DECAY, HY_MAX_DECAY, HY_WIDTH, dtype=f32))
    window = (np.exp(-t[:, None] * deltas[None, :]) + f32(HY_SHIFT)).astype(f32)
    return jnp.asarray(feats), jnp.asarray(window)


def _filter_kernel(feat_ref, w1_ref, b1_ref, w2_ref, b2_ref, w3f_ref, w3b_ref, sf_ref, win_ref,
                   fc_ref, fs_ref, kc_ref, ks_ref, kn_ref, *, length):
    n = 2 * length
    h = jnp.sin(sf_ref[0:1, :] * (_dot(feat_ref[...].astype(BF16), w1_ref[...]) + b1_ref[...]))
    h = jnp.sin(sf_ref[1:2, :] * (_dot(h.astype(BF16), w2_ref[...]) + b2_ref[...]))
    hb16 = h.astype(BF16)
    win = win_ref[...]
    row = lax.broadcasted_iota(jnp.int32, (length, HY_CT), 0)
    h_fwd = _dot(hb16, w3f_ref[...]) * win
    h_bwd0 = jnp.where(row == 0, 0.0, _dot(hb16, w3b_ref[...]) * win)
    h_sum = h_fwd + h_bwd0
    h_dif = h_fwd - h_bwd0
    weight = jnp.where(row == 0, 1.0 / n, 2.0 / n)
    kc_ref[...] = _dot(fc_ref[...], h_sum.astype(BF16)) * weight
    ks_ref[...] = _dot(fs_ref[...], h_dif.astype(BF16)) * weight
    sign = jnp.where((row & 1) == 0, 1.0, -1.0)
    kn_ref[...] = jnp.sum(h_sum * sign, axis=0, keepdims=True) * (1.0 / n)


def _hyena_filter_spectrum(length, fc, fs, w1, b1, w2, b2, w3, sin_freq):
    feats, window = _filter_features(length)
    hp = LANES - HY_FILT_HIDDEN
    w1p = jnp.pad(w1, ((0, LANES - HY_EMB), (0, hp))).astype(BF16)
    w2p = jnp.pad(w2, ((0, hp), (0, hp))).astype(BF16)
    w3p = jnp.pad(w3, ((0, hp), (0, 0))).astype(BF16)
    b1p = jnp.pad(b1, (0, hp)).reshape(1, LANES)
    b2p = jnp.pad(b2, (0, hp)).reshape(1, LANES)
    sfp = jnp.pad(sin_freq, ((0, 0), (0, hp)))
    spec = jax.ShapeDtypeStruct((length, HY_WIDTH), F32)
    chan = lambda c: (0, c)
    return pl.pallas_call(
        functools.partial(_filter_kernel, length=length),
        grid=(HY_WIDTH // HY_CT,),
        in_specs=[
            _resident((length, LANES)),
            _resident((LANES, LANES)),
            _resident((1, LANES)),
            _resident((LANES, LANES)),
            _resident((1, LANES)),
            pl.BlockSpec((LANES, HY_CT), chan),
            pl.BlockSpec((LANES, HY_CT), chan),
            _resident((2, LANES)),
            pl.BlockSpec((length, HY_CT), chan),
            _resident((length, length)),
            _resident((length, length)),
        ],
        out_specs=[
            pl.BlockSpec((length, HY_CT), chan),
            pl.BlockSpec((length, HY_CT), chan),
            pl.BlockSpec((1, HY_CT), chan),
        ],
        out_shape=[spec, spec, jax.ShapeDtypeStruct((1, HY_WIDTH), F32)],
        compiler_params=_params("arbitrary"),
        name=f"hyena_filter_{length}",
    )(feats, w1p, b1p, w2p, b2p, w3p[:, :HY_WIDTH], w3p[:, HY_WIDTH:], sfp, window, fc, fs)


def _hyena_sequence(raw, cw_ref, cb_ref, bias_ref, fc_ref, fs_ref, kc_ref, ks_ref, kn_ref, length):
    row = lax.broadcasted_iota(jnp.int32, (length, HY_CT), 0)

    def short_conv(x, part):
        prev = jnp.where(row == 0, 0.0, pltpu.roll(x, 1, 0))
        nxt = jnp.where(row == length - 1, 0.0, pltpu.roll(x, length - 1, 0))
        return (prev * cw_ref[part, 0:1, :] + x * cw_ref[part, 1:2, :] + nxt * cw_ref[part, 2:3, :]
                + cb_ref[part:part + 1, :])

    x0 = short_conv(raw[0], 0)
    x1 = short_conv(raw[1], 1)
    v = short_conv(raw[2], 2)
    vx = v * x1
    vb = vx.astype(BF16)
    ft = min(length, HY_FREQ_TILE)
    y = None
    for f0 in range(0, length, ft):
        spec_c = _dot(fc_ref[f0:f0 + ft, :], vb)
        spec_s = _dot(fs_ref[f0:f0 + ft, :], vb)
        kc = kc_ref[f0:f0 + ft, :]
        ks = ks_ref[f0:f0 + ft, :]
        p = (spec_c * kc - spec_s * ks).astype(BF16)
        q = (spec_c * ks + spec_s * kc).astype(BF16)
        part = _dot(fc_ref[:, f0:f0 + ft], p) + _dot(fs_ref[:, f0:f0 + ft], q)
        y = part if y is None else y + part
    sign = jnp.where((row & 1) == 0, 1.0, -1.0)
    nyquist = jnp.sum(vx * sign, axis=0, keepdims=True) * kn_ref[...]
    return x0 * (y + sign * nyquist + vx * bias_ref[...])


def _hyena_kernel(x0_ref, x1_ref, v_ref, cw_ref, cb_ref, bias_ref,
                  fcl_ref, fsl_ref, kcl_ref, ksl_ref, knl_ref,
                  fcc_ref, fsc_ref, kcc_ref, ksc_ref, knc_ref, o_ref):
    step = pl.program_id(1)
    common = (cw_ref, cb_ref, bias_ref)

    @pl.when(step < BATCH)
    def _():
        raw = (x0_ref[...], x1_ref[...], v_ref[...])
        o_ref[...] = _hyena_sequence(raw, *common, fcl_ref, fsl_ref, kcl_ref, ksl_ref, knl_ref, SEQ)

    @pl.when(step == BATCH)
    def _():
        for r in range(BATCH):
            rows = slice(r * CTX_LEN, (r + 1) * CTX_LEN)
            raw = (x0_ref[rows, :], x1_ref[rows, :], v_ref[rows, :])
            o_ref[rows, :] = _hyena_sequence(raw, *common, fcc_ref, fsc_ref, kcc_ref, ksc_ref, knc_ref, CTX_LEN)


def _hyena(hy, tab_lat, spec_lat, tab_ctx, spec_ctx, conv_w, conv_b, bias):
    assert T_CTX == SEQ, "the context rows must form exactly one latent-sized row block"
    n_ct = HY_WIDTH // HY_CT
    cw = conv_w.reshape(3, 3, HY_WIDTH).transpose(1, 0, 2)
    cb = conv_b.reshape(3, HY_WIDTH)

    def part_spec(part):
        return pl.BlockSpec((SEQ, HY_CT), lambda c, b: (b, part * n_ct + c))

    chan = lambda c, b: (0, c)

    def table_specs(length):
        return [
            _resident((length, length)),
            _resident((length, length)),
            pl.BlockSpec((length, HY_CT), chan, pipeline_mode=pl.Buffered(1)),
            pl.BlockSpec((length, HY_CT), chan, pipeline_mode=pl.Buffered(1)),
            pl.BlockSpec((1, HY_CT), chan),
        ]

    return pl.pallas_call(
        _hyena_kernel,
        grid=(n_ct, BATCH + 1),
        in_specs=[
            part_spec(0), part_spec(1), part_spec(2),
            pl.BlockSpec((3, 3, HY_CT), lambda c, b: (0, 0, c)),
            pl.BlockSpec((3, HY_CT), chan),
            pl.BlockSpec((1, HY_CT), chan),
            *table_specs(SEQ),
            *table_specs(CTX_LEN),
        ],
        out_specs=pl.BlockSpec((SEQ, HY_CT), lambda c, b: (b, c)),
        out_shape=jax.ShapeDtypeStruct((T_ALL, HY_WIDTH), F32),
        compiler_params=_params("arbitrary", "arbitrary"),
        name="hyena_conv",
    )(hy, hy, hy, cw, cb, bias.reshape(1, HY_WIDTH), *tab_lat, *spec_lat, *tab_ctx, *spec_ctx)


def _rope_partner(n):
    idx = np.arange(n)
    return np.where((idx % (2 * ROPE_AXIS_PAIRS)) < ROPE_AXIS_PAIRS, idx + ROPE_AXIS_PAIRS, idx - ROPE_AXIS_PAIRS)


def _slot_lanes():
    d = np.arange(MLA_ROPE)
    axis, second, pair = d // (2 * ROPE_AXIS_PAIRS), (d // ROPE_AXIS_PAIRS) % 2, d % ROPE_AXIS_PAIRS
    rope_lane = axis * ROPE_AXIS_PAIRS + pair + (HEAD_SLOT // 2) * second
    free = np.setdiff1d(np.arange(HEAD_SLOT), rope_lane)
    return rope_lane, free[:MLA_NOPE]


def _to_slot(nope, rope):
    rope_lane, nope_lane = _slot_lanes()
    source = [(None, 0)] * HEAD_SLOT
    for arr, lanes in ((nope, nope_lane), (rope, rope_lane)):
        if arr is not None:
            for idx, lane in enumerate(lanes):
                source[lane] = (arr, idx)
    like = nope if nope is not None else rope
    pieces, lane = [], 0
    while lane < HEAD_SLOT:
        arr, start = source[lane]
        run = 1
        while (lane + run < HEAD_SLOT and source[lane + run][0] is arr
               and (arr is None or source[lane + run][1] == start + run)):
            run += 1
        pieces.append(jnp.zeros(like.shape[:-1] + (run,), like.dtype) if arr is None else arr[..., start:start + run])
        lane += run
    return jnp.concatenate(pieces, axis=-1)


def _rope_tables():
    f32 = np.float32
    rope_lane, nope_lane = _slot_lanes()
    rows = np.repeat(np.arange(SEQ // GRID_W, dtype=f32), GRID_W)
    cols = np.tile(np.arange(GRID_W, dtype=f32), SEQ // GRID_W)
    inv_freq = (f32(ROPE_BASE) ** (-np.arange(ROPE_AXIS_PAIRS, dtype=f32) / f32(ROPE_AXIS_PAIRS))).astype(f32)
    ang_r = rows[:, None] * inv_freq
    ang_c = cols[:, None] * inv_freq
    cos32 = np.concatenate([np.cos(ang_r)] * 2 + [np.cos(ang_c)] * 2, axis=1)
    sin32 = np.concatenate([-np.sin(ang_r), np.sin(ang_r), -np.sin(ang_c), np.sin(ang_c)], axis=1)

    def table(n_rows, rope_vals, nope_val):
        t = np.zeros((n_rows, HEAD_SLOT), f32)
        t[:, rope_lane] = rope_vals
        t[:, nope_lane] = nope_val
        return t

    q_cos = table(SEQ, cos32, 1.0) * f32(Q_SCALE)
    q_sin = table(SEQ, sin32, 0.0) * f32(Q_SCALE)
    k_cos = table(SEQ, cos32, 0.0)
    k_sin = table(SEQ, sin32, 0.0)
    ident = table(TM_PROJ, 1.0, 0.0)
    zero = np.zeros((TM_PROJ, HEAD_SLOT), f32)
    tables = ((q_cos, zero), (q_sin, zero), (k_cos, ident), (k_sin, zero))
    return tuple(jnp.asarray(np.concatenate(t, 0).astype(f32)) for t in tables)


def _mla_weights(w_in, w_q_up, w_kv_up):
    w_q = w_in[:, :MLA_Q_LORA]
    w_kv = w_in[:, MLA_Q_LORA:MLA_Q_LORA + MLA_KV_LORA]
    w_kr = w_in[:, MLA_Q_LORA + MLA_KV_LORA:]
    w_kr2 = jnp.concatenate([_to_slot(None, w_kr), _to_slot(None, w_kr[:, _rope_partner(MLA_ROPE)])], axis=1)

    qh = w_q_up.reshape(MLA_Q_LORA, MLA_HEADS, MLA_NOPE + MLA_ROPE)
    wq_slot = _to_slot(qh[..., :MLA_NOPE], qh[..., MLA_NOPE:]).reshape(MLA_Q_LORA, MLA_HEADS * HEAD_SLOT)

    kvh = w_kv_up.reshape(MLA_KV_LORA, MLA_HEADS, MLA_NOPE + MLA_V)
    wk_slot = _to_slot(kvh[..., :MLA_NOPE], None).reshape(MLA_KV_LORA, MLA_HEADS * HEAD_SLOT)
    wv_slot = jnp.pad(kvh[..., MLA_NOPE:], ((0, 0), (0, 0), (0, HEAD_SLOT - MLA_V)))
    wv_slot = wv_slot.reshape(MLA_KV_LORA, MLA_HEADS * HEAD_SLOT)
    return [w.astype(BF16) for w in (w_q, w_kv, w_kr2, wq_slot, wk_slot, wv_slot)]


def _mlaproj_kernel(x_ref, mod_ref, wq_ref, wkv_ref, wkr_ref, wqs_ref, wks_ref, wv_ref,
                    vone_ref, qg_ref, kvg_ref, qcos_ref, qsin_ref, kcos_ref, ksin_ref,
                    q_ref, k_ref, v_ref):
    shift, scale = mod_ref[0, 0], mod_ref[0, 1]
    h = (x_ref[...] * (1.0 + scale) + shift).astype(BF16)

    def rmsnorm(y, g):
        return (y * lax.rsqrt(jnp.mean(y * y, axis=-1, keepdims=True) + RMS_EPS) * g).astype(BF16)

    qn = rmsnorm(_dot(h, wq_ref[...]), qg_ref[...])
    kvn = rmsnorm(_dot(h, wkv_ref[...]), kvg_ref[...])
    kr2 = _dot(h, wkr_ref[...])
    k_rope = kr2[:, :HEAD_SLOT] * kcos_ref[...] + kr2[:, HEAD_SLOT:] * ksin_ref[...]
    k_nope = _dot(kvn, wks_ref[...])
    v_ref[...] = (_dot(kvn, wv_ref[...]) + vone_ref[...]).astype(BF16)
    q_all = _dot(qn, wqs_ref[...])
    q_cos = qcos_ref[...]
    q_sin = qsin_ref[...]
    for hd in range(MLA_HEADS):
        sl = slice(hd * HEAD_SLOT, (hd + 1) * HEAD_SLOT)
        qh = q_all[:, sl]
        rotated = qh * q_cos + pltpu.roll(qh, HEAD_SLOT // 2, 1) * q_sin
        q_ref[:, sl] = rotated.astype(BF16)
        k_ref[:, sl] = (k_nope[:, sl] + k_rope).astype(BF16)


def _mlaproj(x, mod, w_in, q_norm, kv_norm, w_q_up, w_kv_up):
    tm = TM_PROJ
    weights = _mla_weights(w_in, w_q_up, w_kv_up)
    tables = _rope_tables()
    row = lambda i: (i, 0)
    tab = lambda i: (jnp.where(i < T_LAT // tm, i % (SEQ // tm), SEQ // tm), 0)
    wide = MLA_HEADS * HEAD_SLOT
    v_one = jnp.asarray((np.arange(wide) % HEAD_SLOT == MLA_V).astype(np.float32).reshape(1, wide))
    return pl.pallas_call(
        _mlaproj_kernel,
        grid=(T_ALL // tm,),
        in_specs=[
            pl.BlockSpec((tm, D_MODEL), row),
            pl.BlockSpec((1, 2, 1, D_MODEL), lambda i: (_group_of_tile(i, tm), 0, 0, 0)),
            *[_resident(w.shape) for w in weights],
            _resident((1, wide)),
            _resident((1, MLA_Q_LORA)),
            _resident((1, MLA_KV_LORA)),
            *[pl.BlockSpec((tm, HEAD_SLOT), tab) for _ in tables],
        ],
        out_specs=[pl.BlockSpec((tm, wide), row)] * 3,
        out_shape=[jax.ShapeDtypeStruct((T_ALL, wide), BF16)] * 3,
        compiler_params=_params("arbitrary"),
        name="mla_proj",
    )(x, mod, *weights, v_one, q_norm.reshape(1, MLA_Q_LORA), kv_norm.reshape(1, MLA_KV_LORA), *tables)


def _attn_kernel(q_ref, kl_ref, kc_ref, vl_ref, vc_ref, o_ref):
    lane = lax.broadcasted_iota(jnp.int32, (TQ, HEAD_SLOT), 1)
    outs = []
    for hd in range(ATTN_HEADS):
        sl = slice(hd * HEAD_SLOT, (hd + 1) * HEAD_SLOT)
        q = q_ref[:, sl]
        s_lat = _dot_nt(q, kl_ref[:, sl])
        s_ctx = _dot_nt(q, kc_ref[:, sl])
        m = jnp.maximum(jnp.max(s_lat, axis=-1, keepdims=True), jnp.max(s_ctx, axis=-1, keepdims=True))
        p_lat = jnp.exp2(s_lat - m).astype(BF16)
        p_ctx = jnp.exp2(s_ctx - m).astype(BF16)
        o = _dot(p_lat, vl_ref[:, sl]) + _dot(p_ctx, vc_ref[:, sl])
        outs.append(o / o[:, MLA_V:MLA_V + 1])
    for pair in range(ATTN_HEADS // 2):
        packed = jnp.where(lane < MLA_V, outs[2 * pair], pltpu.roll(outs[2 * pair + 1], MLA_V, 1))
        o_ref[:, pair * HEAD_SLOT:(pair + 1) * HEAD_SLOT] = packed.astype(BF16)


def _attention(q, k, v):
    n_q = SEQ // TQ
    n_groups = MLA_HEADS // ATTN_HEADS
    ctx_block0 = T_LAT // CTX_LEN
    wide = ATTN_HEADS * HEAD_SLOT
    lat = lambda b, h, i: (b, h)
    ctx = lambda b, h, i: (ctx_block0 + b, h)
    return pl.pallas_call(
        _attn_kernel,
        grid=(BATCH, n_groups, n_q),
        in_specs=[
            pl.BlockSpec((TQ, wide), lambda b, h, i: (b * n_q + i, h)),
            pl.BlockSpec((SEQ, wide), lat),
            pl.BlockSpec((CTX_LEN, wide), ctx),
            pl.BlockSpec((SEQ, wide), lat),
            pl.BlockSpec((CTX_LEN, wide), ctx),
        ],
        out_specs=pl.BlockSpec((TQ, ATTN_HEADS * MLA_V), lambda b, h, i: (b * n_q + i, h)),
        out_shape=jax.ShapeDtypeStruct((T_LAT, MLA_HEADS * MLA_V), BF16),
        compiler_params=_params("arbitrary", "arbitrary", "arbitrary"),
        name="mla_attention",
    )(q, k, k, v, v)


def kernel(x, c, ctx, c_ctx, ada_w, ada_b, ln_g, ln_b, ffn_w_in, ffn_w_out, ev_w_in, ev_w_out, hy_conv_w, hy_conv_b, hy_filt_w1, hy_filt_b1, hy_filt_w2, hy_filt_b2, hy_filt_w3, hy_sin_freq, hy_bias, rg_conv_w, rg_conv_b, rg_a_w, rg_a_b, rg_x_w, rg_x_b, rg_lambda, mla_w_in, mla_q_norm, mla_kv_norm, mla_w_q_up, mla_w_kv_up, mla_w_out):
    assert x.shape == (BATCH, SEQ, D_MODEL) and ctx.shape == (BATCH, CTX_LEN, D_MODEL)
    cond = jnp.concatenate([c, c_ctx[None], jnp.zeros((2 * SUBLANES - N_GROUPS, D_MODEL), F32)], axis=0)
    mods = _adaln(cond, ada_w, ada_b)[:, :N_GROUPS].reshape(DEPTH, N_GROUPS, N_ADA, 1, D_MODEL)
    ffn_w = (ffn_w_in, ffn_w_out)

    m = mods[0]
    xs = _ffn_split(x.reshape(T_LAT, D_MODEL), ctx.reshape(T_CTX, D_MODEL),
                    m[:, 0:3], ffn_w, (0, 0), ln_g[0, 0], ln_b[0, 0], T_ALL)
    hy, rx, gate_in = _evproj(xs, m[:, 3:5], ev_w_in[0])
    h_sum = _rglru(rx, rg_conv_w[0], rg_conv_b[0], rg_a_w[0], rg_a_b[0], rg_x_w[0], rg_x_b[0], rg_lambda[0])
    filt = (hy_filt_w1[0], hy_filt_b1[0], hy_filt_w2[0], hy_filt_b2[0], hy_filt_w3[0], hy_sin_freq[0])
    tab_l = _dft_tables(SEQ)
    tab_c = _dft_tables(CTX_LEN)
    spec_l = _hyena_filter_spectrum(SEQ, *tab_l, *filt)
    spec_c = _hyena_filter_spectrum(CTX_LEN, *tab_c, *filt)
    y_hy = _hyena(hy, tab_l, spec_l, tab_c, spec_c, hy_conv_w[0], hy_conv_b[0], hy_bias[0])
    xs = _ffn_after_even(xs, y_hy, h_sum, gate_in, ev_w_out[0], ln_g[0, 1], ln_b[0, 1],
                         m[:, 5:9], ffn_w, (0, 1), ln_g[0, 2], ln_b[0, 2], T_ALL)

    m = mods[1]
    xs = _ffn_plain(xs, m[:, 0:3], ffn_w, (1, 0), ln_g[1, 0], ln_b[1, 0], T_ALL)
    q, k, v = _mlaproj(xs, m[:, 3:5], mla_w_in[0], mla_q_norm[0], mla_kv_norm[0], mla_w_q_up[0], mla_w_kv_up[0])
    att = _attention(q, k, v)
    xl = _ffn_after_mla(xs, att, mla_w_out[0], ln_g[1, 1], ln_b[1, 1],
                        m[:, 5:9], ffn_w, (1, 1), ln_g[1, 2], ln_b[1, 2], T_LAT)
    return xl.reshape(BATCH, SEQ, D_MODEL)
```

```python
import functools
import math

import jax
import jax.numpy as jnp
import numpy as np
from jax import lax
from jax.experimental import pallas as pl
from jax.experimental.pallas import tpu as pltpu

F32 = jnp.float32
BF16 = jnp.bfloat16

D_MODEL = 1024
BATCH = 8
SEQ = 2048
DEPTH = 2
CTX_LEN = 256
GRID_W = 64
N_ADA = 9
D_FF = 2816

HY_WIDTH = 512
RG_WIDTH = 512
EV_IN = 3 * HY_WIDTH + 2 * RG_WIDTH
HY_EMB = 33
HY_BANDS = (HY_EMB - 1) // 2
HY_FILT_HIDDEN = 64
HY_TARGET = 1e-2
HY_MAX_DECAY = math.log(HY_TARGET) / 0.3
HY_MIN_DECAY = math.log(HY_TARGET) / 1.5
HY_SHIFT = 0.05
RG_BLOCKS = 8
RG_BLOCK_DIM = RG_WIDTH // RG_BLOCKS
RG_C = 8.0
RG_CONV = 4
RG_PAD_L = 2

MLA_HEADS = 16
MLA_Q_LORA = 768
MLA_KV_LORA = 256
MLA_NOPE = 64
MLA_ROPE = 32
MLA_V = 64
MLA_SCALE = (MLA_NOPE + MLA_ROPE) ** -0.5
ROPE_AXIS_PAIRS = MLA_ROPE // 4
ROPE_BASE = 10000.0

LOG2_E = math.log2(math.e)
ALPHA = (2.0 * DEPTH) ** 0.25
LN_EPS = 1e-6
RMS_EPS = 1e-6

T_LAT = BATCH * SEQ
T_CTX = BATCH * CTX_LEN
T_ALL = T_LAT + T_CTX
N_GROUPS = BATCH + 1

LANES = 128
SUBLANES = 8
VMEM_LIMIT = 56 * 1024 * 1024

TM_FFN = 512
FF_CHUNK = 256
N_FF_CHUNKS = D_FF // FF_CHUNK
FFN_WEIGHT_CHUNKS = 8
TM_PROJ = 512
TM_EVPROJ = 1024
HY_CT = 256
HY_FREQ_TILE = 256
RG_CT = 128
RG_TC = 32
RG_HALO = SUBLANES
HEAD_SLOT = 128
TQ = 512
ATTN_HEADS = 16
Q_SCALE = MLA_SCALE * math.log2(math.e)
ADA_TN = 1152


def _dot(a, b):
    return jnp.dot(a, b, preferred_element_type=F32)


def _dot_nt(a, b):
    return lax.dot_general(a, b, (((1,), (1,)), ((), ())), preferred_element_type=F32)


def _resident(shape):
    nd = len(shape)
    return pl.BlockSpec(shape, lambda *_: (0,) * nd, pipeline_mode=pl.Buffered(1))


def _group_of_tile(i, tm):
    return jnp.where(i < T_LAT // tm, i // (SEQ // tm), BATCH)


def _params(*sem):
    return pltpu.CompilerParams(dimension_semantics=sem, vmem_limit_bytes=VMEM_LIMIT)


def _layernorm(z, g, b):
    mu = jnp.mean(z, axis=-1, keepdims=True)
    zc = z - mu
    var = jnp.mean(zc * zc, axis=-1, keepdims=True)
    return zc * lax.rsqrt(var + LN_EPS) * g + b


def _adaln_kernel(c_ref, w_ref, b_ref, o_ref):
    c = c_ref[...]
    s = (c * jax.nn.sigmoid(c)).astype(BF16)
    o_ref[0] = _dot(s, w_ref[0].astype(BF16)) + b_ref[0]


def _adaln(cond, ada_w, ada_b):
    n_out = N_ADA * D_MODEL
    rows = cond.shape[0]
    return pl.pallas_call(
        _adaln_kernel,
        grid=(DEPTH, n_out // ADA_TN),
        in_specs=[
            pl.BlockSpec((rows, D_MODEL), lambda l, j: (0, 0)),
            pl.BlockSpec((1, D_MODEL, ADA_TN), lambda l, j: (l, 0, j)),
            pl.BlockSpec((1, 1, ADA_TN), lambda l, j: (l, 0, j)),
        ],
        out_specs=pl.BlockSpec((1, rows, ADA_TN), lambda l, j: (l, 0, j)),
        out_shape=jax.ShapeDtypeStruct((DEPTH, rows, n_out), F32),
        compiler_params=_params("arbitrary", "arbitrary"),
        name="adaln",
    )(cond, ada_w, ada_b.reshape(DEPTH, 1, n_out))


def _ffn_body(x, mod_ref, wi_ref, wo_ref, g_ref, b_ref, o_ref, act_ref):
    n_slots = mod_ref.shape[1]
    shift, scale, gate = mod_ref[0, n_slots - 3], mod_ref[0, n_slots - 2], mod_ref[0, n_slots - 1]
    h = (x * (1.0 + scale) + shift).astype(BF16)
    for j in range(N_FF_CHUNKS):
        lo = j * FF_CHUNK
        a = _dot(h, wi_ref[:, lo:lo + FF_CHUNK])
        u = _dot(h, wi_ref[:, D_FF + lo:D_FF + lo + FF_CHUNK])
        act_ref[:, lo:lo + FF_CHUNK] = (a * jax.nn.sigmoid(a) * u).astype(BF16)
    y = _dot(act_ref[...], wo_ref[...])
    o_ref[...] = _layernorm(ALPHA * x + (0.5 * gate) * y, g_ref[...], b_ref[...])


def _load_weight_bf16(w_hbm, which, dst_ref, stage_ref, sem):
    layer, half = which
    rows = stage_ref.shape[1]
    n_chunks = dst_ref.shape[0] // rows

    def copy(c):
        return pltpu.make_async_copy(w_hbm.at[layer, half, pl.ds(c * rows, rows)], stage_ref.at[c % 2], sem.at[c % 2])

    copy(0).start()
    for c in range(n_chunks):
        if c + 1 < n_chunks:
            copy(c + 1).start()
        copy(c).wait()
        dst_ref[c * rows:(c + 1) * rows, :] = stage_ref[c % 2].astype(BF16)


def _ffn_steps(x_fn, which, mod_ref, wi_hbm, wo_hbm, g_ref, b_ref, o_ref,
               act_ref, wi_ref, wo_ref, stage_in_ref, stage_out_ref, sem_in, sem_out):
    @pl.when(pl.program_id(0) == 0)
    def _():
        _load_weight_bf16(wi_hbm, which, wi_ref, stage_in_ref, sem_in)
        _load_weight_bf16(wo_hbm, which, wo_ref, stage_out_ref, sem_out)

    _ffn_body(x_fn(), mod_ref, wi_ref, wo_ref, g_ref, b_ref, o_ref, act_ref)


def _ffn_kernel(x_ref, *rest, which):
    _ffn_steps(lambda: x_ref[...], which, *rest)


def _ffn_split_kernel(xl_ref, xc_ref, *rest, which):
    _ffn_steps(lambda: jnp.where(pl.program_id(0) < T_LAT // TM_FFN, xl_ref[...], xc_ref[...]), which, *rest)


def _ffn_after_even_kernel(x_ref, yh_ref, hs_ref, gt_ref, wmix_ref, g1_ref, b1_ref, mod_ref, *rest, which):
    def x_fn():
        y_rg = hs_ref[...] * jax.nn.gelu(gt_ref[...], approximate=True)
        y = (_dot(yh_ref[...].astype(BF16), wmix_ref[:HY_WIDTH, :])
             + _dot(y_rg.astype(BF16), wmix_ref[HY_WIDTH:, :]))
        return _layernorm(ALPHA * x_ref[...] + mod_ref[0, 0] * y, g1_ref[...], b1_ref[...])

    _ffn_steps(x_fn, which, mod_ref, *rest)


def _ffn_after_mla_kernel(x_ref, att_ref, wmix_ref, g1_ref, b1_ref, mod_ref, *rest, which):
    def x_fn():
        y = _dot(att_ref[...], wmix_ref[...])
        return _layernorm(ALPHA * x_ref[...] + mod_ref[0, 0] * y, g1_ref[...], b1_ref[...])

    _ffn_steps(x_fn, which, mod_ref, *rest)


def _ffn(body, lead_args, lead_specs, mod, ffn_w, which, g, b, n_rows):
    tm = TM_FFN
    w_in_all, w_out_all = ffn_w
    return pl.pallas_call(
        functools.partial(body, which=which),
        grid=(n_rows // tm,),
        in_specs=[
            *lead_specs,
            pl.BlockSpec((1, mod.shape[1], 1, D_MODEL), lambda i: (_group_of_tile(i, tm), 0, 0, 0)),
            pl.BlockSpec(memory_space=pl.ANY),
            pl.BlockSpec(memory_space=pl.ANY),
            _resident((1, D_MODEL)),
            _resident((1, D_MODEL)),
        ],
        out_specs=pl.BlockSpec((tm, D_MODEL), lambda i: (i, 0)),
        out_shape=jax.ShapeDtypeStruct((n_rows, D_MODEL), F32),
        scratch_shapes=[
            pltpu.VMEM((tm, D_FF), BF16),
            pltpu.VMEM((D_MODEL, 2 * D_FF), BF16),
            pltpu.VMEM((D_FF, D_MODEL), BF16),
            pltpu.VMEM((2, D_MODEL // FFN_WEIGHT_CHUNKS, 2 * D_FF), F32),
            pltpu.VMEM((2, D_FF // FFN_WEIGHT_CHUNKS, D_MODEL), F32),
            pltpu.SemaphoreType.DMA((2,)),
            pltpu.SemaphoreType.DMA((2,)),
        ],
        compiler_params=_params("arbitrary"),
        name="half_ffn",
    )(*lead_args, mod, w_in_all, w_out_all, g.reshape(1, D_MODEL), b.reshape(1, D_MODEL))


def _row_spec(width, tm=TM_FFN):
    return pl.BlockSpec((tm, width), lambda i: (i, 0))


def _ffn_plain(x, *args):
    return _ffn(_ffn_kernel, (x,), [_row_spec(D_MODEL)], *args)


def _ffn_split(x_lat, x_ctx, *args):
    n_lat = T_LAT // TM_FFN
    specs = [pl.BlockSpec((TM_FFN, D_MODEL), lambda i: (jnp.minimum(i, n_lat - 1), 0)),
             pl.BlockSpec((TM_FFN, D_MODEL), lambda i: (jnp.maximum(i - n_lat, 0), 0))]
    return _ffn(_ffn_split_kernel, (x_lat, x_ctx), specs, *args)


def _ffn_after_even(x, y_hy, h_sum, gate_in, w_mix, g1, b1, *args):
    lead = (x, y_hy, h_sum, gate_in, w_mix.astype(BF16), g1.reshape(1, D_MODEL), b1.reshape(1, D_MODEL))
    specs = [_row_spec(D_MODEL), _row_spec(HY_WIDTH), _row_spec(RG_WIDTH), _row_spec(RG_WIDTH),
             _resident((HY_WIDTH + RG_WIDTH, D_MODEL)), _resident((1, D_MODEL)), _resident((1, D_MODEL))]
    return _ffn(_ffn_after_even_kernel, lead, specs, *args)


def _ffn_after_mla(x, att, w_mix, g1, b1, *args):
    lead = (x, att, w_mix.astype(BF16), g1.reshape(1, D_MODEL), b1.reshape(1, D_MODEL))
    specs = [_row_spec(D_MODEL), _row_spec(MLA_HEADS * MLA_V),
             _resident((MLA_HEADS * MLA_V, D_MODEL)), _resident((1, D_MODEL)), _resident((1, D_MODEL))]
    return _ffn(_ffn_after_mla_kernel, lead, specs, *args)


def _evproj_kernel(x_ref, mod_ref, w_ref, hy_ref, rx_ref, gt_ref):
    shift, scale = mod_ref[0, 0], mod_ref[0, 1]
    h = (x_ref[...] * (1.0 + scale) + shift).astype(BF16)
    n_hy = 3 * HY_WIDTH
    hy_ref[...] = _dot(h, w_ref[:, :n_hy])
    rx_ref[...] = _dot(h, w_ref[:, n_hy:n_hy + RG_WIDTH])
    gt_ref[...] = _dot(h, w_ref[:, n_hy + RG_WIDTH:])


def _evproj(x, mod, w_in):
    tm = TM_EVPROJ
    row = lambda i: (i, 0)
    return pl.pallas_call(
        _evproj_kernel,
        grid=(T_ALL // tm,),
        in_specs=[
            pl.BlockSpec((tm, D_MODEL), row),
            pl.BlockSpec((1, 2, 1, D_MODEL), lambda i: (_group_of_tile(i, tm), 0, 0, 0)),
            _resident((D_MODEL, EV_IN)),
        ],
        out_specs=[
            pl.BlockSpec((tm, 3 * HY_WIDTH), row),
            pl.BlockSpec((tm, RG_WIDTH), row),
            pl.BlockSpec((tm, RG_WIDTH), row),
        ],
        out_shape=[
            jax.ShapeDtypeStruct((T_ALL, 3 * HY_WIDTH), F32),
            jax.ShapeDtypeStruct((T_ALL, RG_WIDTH), F32),
            jax.ShapeDtypeStruct((T_ALL, RG_WIDTH), F32),
        ],
        compiler_params=_params("arbitrary"),
        name="even_in_proj",
    )(x, mod, w_in.astype(BF16))


def _block_diag_tiles(w):
    per = RG_CT // RG_BLOCK_DIM
    w = w.reshape(RG_WIDTH // RG_CT, per, RG_BLOCK_DIM, RG_BLOCK_DIM)
    eye = jnp.eye(per, dtype=w.dtype)
    return jnp.einsum('cpde,pq->cpdqe', w, eye).reshape(RG_WIDTH // RG_CT, RG_CT, RG_CT)


def _rg_kernel(rx_ref, cw_ref, cb_ref, w_ref, bias_ref, lam_ref, out_ref,
                padl_ref, padc_ref, xt_f_ref, xt_b_ref, coef_a_ref, coef_b_ref, hbuf_f_ref, hbuf_b_ref):
    neg_lam = -lam_ref[...]
    softplus = jnp.maximum(neg_lam, 0.0) + jnp.log1p(jnp.exp(-jnp.abs(neg_lam)))
    neg_log_a_scale = RG_C * softplus
    a_exp2_scale = -LOG2_E * neg_log_a_scale
    cb = cb_ref[...]
    xts = (xt_f_ref, xt_b_ref)
    hbuf = (hbuf_f_ref, hbuf_b_ref)
    slab_rows = RG_TC + 2 * RG_HALO

    def step_rows(s):
        return slice(s * SUBLANES, (s + 1) * SUBLANES)

    halo = jnp.zeros((BATCH, RG_HALO, RG_CT), F32)
    for pad_ref, length in ((padl_ref, SEQ), (padc_ref, CTX_LEN)):
        pad_ref[:, 0:RG_HALO, :] = halo
        pad_ref[:, RG_HALO + length:, :] = halo
    padl_ref[:, RG_HALO:RG_HALO + SEQ, :] = rx_ref[0:BATCH]
    for b in range(BATCH):
        padc_ref[b, RG_HALO:RG_HALO + CTX_LEN, :] = rx_ref[BATCH, b * CTX_LEN:(b + 1) * CTX_LEN, :]

    def add_out(latent, t_start, d):
        t_start = pl.multiple_of(t_start, SUBLANES)
        for b in range(BATCH):
            rows = hbuf[d][pl.ds(b, RG_TC, stride=SUBLANES), :]
            if latent:
                out_ref[b, pl.ds(t_start, RG_TC), :] += rows
            else:
                out_ref[BATCH, pl.ds(b * CTX_LEN + t_start, RG_TC), :] += rows

    def sigmoid(z):
        return 1.0 / (1.0 + jnp.exp2(z * (-LOG2_E)))

    def coeffs(latent, length, t0, d, dst_ref):
        t0 = pl.multiple_of(jnp.asarray(t0, jnp.int32), SUBLANES)
        pad_ref = padl_ref if latent else padc_ref
        xt = xts[d]
        for b in range(BATCH):
            xt[pl.ds(b, slab_rows, stride=SUBLANES), :] = pad_ref[b, pl.ds(t0, slab_rows), :]
        x2 = cb
        for k in range(RG_CONV):
            start = (RG_HALO - RG_PAD_L + k) * SUBLANES
            x2 = x2 + xt[start:start + RG_TC * SUBLANES, :] * cw_ref[k:k + 1, :]
        xb = x2.astype(BF16)
        r = sigmoid(_dot(xb, w_ref[0, 2 * d]) + bias_ref[2 * d:2 * d + 1, :])
        i = sigmoid(_dot(xb, w_ref[0, 2 * d + 1]) + bias_ref[2 * d + 1:2 * d + 2, :])
        a = jnp.exp2(r * a_exp2_scale[d:d + 1, :])
        b = jnp.sqrt(jnp.tanh(r * neg_log_a_scale[d:d + 1, :]) * (a * a + 1.0)) * (i * x2)
        dst_ref[d, 0] = a
        dst_ref[d, 1] = b

    def scan(latent, tf, tb, src_ref, hf, hb):
        tf = jnp.asarray(tf, jnp.int32)
        tb = jnp.asarray(tb, jnp.int32)
        for s in range(RG_TC):
            rows = step_rows(s)
            hf = src_ref[0, 0, rows, :] * hf + src_ref[0, 1, rows, :]
            hbuf_f_ref[rows, :] = hf
            rows = step_rows(RG_TC - 1 - s)
            hb = src_ref[1, 0, rows, :] * hb + src_ref[1, 1, rows, :]
            hbuf_b_ref[rows, :] = hb
        add_out(latent, tf, 0)
        add_out(latent, tb, 1)
        return hf, hb

    def sweep(latent, length, h_fwd, h_bwd):
        n_chunks = length // RG_TC
        last = n_chunks - 1

        def both(j, dst_ref):
            coeffs(latent, length, j * RG_TC, 0, dst_ref)
            coeffs(latent, length, (last - j) * RG_TC, 1, dst_ref)

        def body(i, carry):
            j = 2 * i
            both(j + 1, coef_b_ref)
            carry = scan(latent, j * RG_TC, (last - j) * RG_TC, coef_a_ref, *carry)
            both(jnp.minimum(j + 2, last), coef_a_ref)
            return scan(latent, (j + 1) * RG_TC, (last - j - 1) * RG_TC, coef_b_ref, *carry)

        both(0, coef_a_ref)
        return lax.fori_loop(0, n_chunks // 2, body, (h_fwd, h_bwd))

    out_ref[...] = jnp.zeros_like(out_ref)
    zero = jnp.zeros((SUBLANES, RG_CT), F32)
    hf, hb = sweep(False, CTX_LEN, zero, zero)
    sweep(True, SEQ, hf, hb)


def _rglru(rx, conv_w, conv_b, a_w, a_b, x_w, x_b, lam):
    assert T_CTX == SEQ, "the context rows must form exactly one latent-sized row block"
    w = jnp.stack([_block_diag_tiles(a_w[0]), _block_diag_tiles(x_w[0]),
                   _block_diag_tiles(a_w[1]), _block_diag_tiles(x_w[1])], axis=1).astype(BF16)
    bias = jnp.stack([a_b[0], x_b[0], a_b[1], x_b[1]], axis=0)
    n_ct = RG_WIDTH // RG_CT
    n_blocks = BATCH + 1
    chan = lambda c: (0, 0, c)
    lane = lambda c: (0, c)
    coef = pltpu.VMEM((2, 2, RG_TC * BATCH, RG_CT), F32)
    slab = pltpu.VMEM(((RG_TC + 2 * RG_HALO) * BATCH, RG_CT), F32)
    chunk = pltpu.VMEM((RG_TC * BATCH, RG_CT), F32)
    out = pl.pallas_call(
        _rg_kernel,
        grid=(n_ct,),
        in_specs=[
            pl.BlockSpec((n_blocks, SEQ, RG_CT), chan),
            pl.BlockSpec((4, RG_CT), lane),
            pl.BlockSpec((1, RG_CT), lane),
            pl.BlockSpec((1, 4, RG_CT, RG_CT), lambda c: (c, 0, 0, 0)),
            pl.BlockSpec((4, RG_CT), lane),
            pl.BlockSpec((2, RG_CT), lane),
        ],
        out_specs=pl.BlockSpec((n_blocks, SEQ, RG_CT), chan),
        out_shape=jax.ShapeDtypeStruct((n_blocks, SEQ, RG_WIDTH), F32),
        scratch_shapes=[
            pltpu.VMEM((BATCH, SEQ + 2 * RG_HALO, RG_CT), F32),
            pltpu.VMEM((BATCH, CTX_LEN + 2 * RG_HALO, RG_CT), F32),
            slab, slab,
            coef, coef,
            chunk, chunk,
        ],
        compiler_params=_params("arbitrary"),
        name="rglru_scan",
    )(rx.reshape(n_blocks, SEQ, RG_WIDTH), conv_w, conv_b.reshape(1, RG_WIDTH), w, bias, lam)
    return out.reshape(T_ALL, RG_WIDTH)


def _dft_tables(length):
    step = 1 << (length.bit_length() // 2)
    s = np.arange(length, dtype=np.int64)

    def thin(kvec):
        ang = ((kvec[:, None] * s[None, :]) % (2 * length)) * (math.pi / length)
        return (jnp.asarray(np.cos(ang)[:, None, :], dtype=F32),
                jnp.asarray(np.sin(ang)[:, None, :], dtype=F32))

    c_hi, s_hi = thin(np.arange(0, length, step, dtype=np.int64))
    c_lo, s_lo = thin(np.arange(step, dtype=np.int64))
    c_lo, s_lo = c_lo.reshape(1, step, length), s_lo.reshape(1, step, length)
    fc = (c_hi * c_lo - s_hi * s_lo).reshape(length, length)
    fs = (s_hi * c_lo + c_hi * s_lo).reshape(length, length)
    return fc.astype(BF16), fs.astype(BF16)


def _filter_features(length):
    f32 = np.float32
    pos = np.arange(length, dtype=f32)
    t = pos / f32(length)
    bands = np.linspace(1e-4, HY_BANDS - 1, HY_BANDS, dtype=f32)
    ang = (f32(2.0 * math.pi) * pos / f32(length))[:, None] * bands[None, :]
    feats = np.concatenate([t[:, None], np.cos(ang), -np.sin(ang)], axis=-1).astype(f32)
    feats = np.pad(feats, ((0, 0), (0, LANES - HY_EMB)))
    deltas = np.abs(np.linspace(HY_MIN_DECAY, HY_MAX_DECAY, HY_WIDTH, dtype=f32))
    window = (np.exp(-t[:, None] * deltas[None, :]) + f32(HY_SHIFT)).astype(f32)
    return jnp.asarray(feats), jnp.asarray(window)


def _filter_hidden(feat_ref, w1_ref, b1_ref, w2_ref, b2_ref, sf_ref, hid_ref):
    h = jnp.sin(sf_ref[0:1, :] * (_dot(feat_ref[...].astype(BF16), w1_ref[...]) + b1_ref[...]))
    h = jnp.sin(sf_ref[1:2, :] * (_dot(h.astype(BF16), w2_ref[...]) + b2_ref[...]))
    hid_ref[...] = h.astype(BF16)


def _filter_spectrum(hid_ref, win_ref, w3f_ref, w3b_ref, fc_ref, fs_ref, kc_ref, ks_ref, kn_ref, length):
    n = 2 * length
    hb16 = hid_ref[...]
    win = win_ref[...]
    row = lax.broadcasted_iota(jnp.int32, (length, HY_CT), 0)
    h_fwd = _dot(hb16, w3f_ref[...]) * win
    h_bwd0 = jnp.where(row == 0, 0.0, _dot(hb16, w3b_ref[...]) * win)
    h_sum = h_fwd + h_bwd0
    h_dif = h_fwd - h_bwd0
    weight = jnp.where(row == 0, 1.0 / n, 2.0 / n)
    kc_ref[...] = _dot(fc_ref[...], h_sum.astype(BF16)) * weight
    ks_ref[...] = _dot(fs_ref[...], h_dif.astype(BF16)) * weight
    sign = jnp.where((row & 1) == 0, 1.0, -1.0)
    kn_ref[...] = jnp.sum(h_sum * sign, axis=0, keepdims=True) * (1.0 / n)


def _filter_params(w1, b1, w2, b2, w3, sin_freq):
    hp = LANES - HY_FILT_HIDDEN
    w3p = jnp.pad(w3, ((0, hp), (0, 0))).astype(BF16)
    return (jnp.pad(w1, ((0, LANES - HY_EMB), (0, hp))).astype(BF16), jnp.pad(b1, (0, hp)).reshape(1, LANES),
            jnp.pad(w2, ((0, hp), (0, hp))).astype(BF16), jnp.pad(b2, (0, hp)).reshape(1, LANES),
            w3p[:, :HY_WIDTH], w3p[:, HY_WIDTH:], jnp.pad(sin_freq, ((0, 0), (0, hp))))


def _hyena_sequence(raw, cw_ref, cb_ref, bias_ref, fc_ref, fs_ref, kc_ref, ks_ref, kn_ref, length):
    row = lax.broadcasted_iota(jnp.int32, (length, HY_CT), 0)

    def short_conv(x, part):
        prev = jnp.where(row == 0, 0.0, pltpu.roll(x, 1, 0))
        nxt = jnp.where(row == length - 1, 0.0, pltpu.roll(x, length - 1, 0))
        return (prev * cw_ref[part, 0:1, :] + x * cw_ref[part, 1:2, :] + nxt * cw_ref[part, 2:3, :]
                + cb_ref[part:part + 1, :])

    x0 = short_conv(raw[0], 0)
    x1 = short_conv(raw[1], 1)
    v = short_conv(raw[2], 2)
    vx = v * x1
    vb = vx.astype(BF16)
    ft = min(length, HY_FREQ_TILE)
    y = None
    for f0 in range(0, length, ft):
        spec_c = _dot(fc_ref[f0:f0 + ft, :], vb)
        spec_s = _dot(fs_ref[f0:f0 + ft, :], vb)
        kc = kc_ref[f0:f0 + ft, :]
        ks = ks_ref[f0:f0 + ft, :]
        p = (spec_c * kc - spec_s * ks).astype(BF16)
        q = (spec_c * ks + spec_s * kc).astype(BF16)
        part = _dot(fc_ref[:, f0:f0 + ft], p) + _dot(fs_ref[:, f0:f0 + ft], q)
        y = part if y is None else y + part
    sign = jnp.where((row & 1) == 0, 1.0, -1.0)
    nyquist = jnp.sum(vx * sign, axis=0, keepdims=True) * kn_ref[...]
    return x0 * (y + sign * nyquist + vx * bias_ref[...])


def _hyena_kernel(x0_ref, x1_ref, v_ref, cw_ref, cb_ref, bias_ref,
                  featl_ref, winl_ref, featc_ref, winc_ref,
                  w1_ref, b1_ref, w2_ref, b2_ref, w3f_ref, w3b_ref, sf_ref,
                  fcl_ref, fsl_ref, fcc_ref, fsc_ref, o_ref,
                  hidl_ref, hidc_ref, kcl_ref, ksl_ref, knl_ref, kcc_ref, ksc_ref, knc_ref):
    step = pl.program_id(1)
    common = (cw_ref, cb_ref, bias_ref)

    @pl.when(jnp.logical_and(pl.program_id(0) == 0, step == 0))
    def _():
        sine_layers = (w1_ref, b1_ref, w2_ref, b2_ref, sf_ref)
        _filter_hidden(featl_ref, *sine_layers, hidl_ref)
        _filter_hidden(featc_ref, *sine_layers, hidc_ref)

    @pl.when(step == 0)
    def _():
        _filter_spectrum(hidl_ref, winl_ref, w3f_ref, w3b_ref, fcl_ref, fsl_ref, kcl_ref, ksl_ref, knl_ref, SEQ)
        _filter_spectrum(hidc_ref, winc_ref, w3f_ref, w3b_ref, fcc_ref, fsc_ref, kcc_ref, ksc_ref, knc_ref, CTX_LEN)

    @pl.when(step < BATCH)
    def _():
        raw = (x0_ref[...], x1_ref[...], v_ref[...])
        o_ref[...] = _hyena_sequence(raw, *common, fcl_ref, fsl_ref, kcl_ref, ksl_ref, knl_ref, SEQ)

    @pl.when(step == BATCH)
    def _():
        for r in range(BATCH):
            rows = slice(r * CTX_LEN, (r + 1) * CTX_LEN)
            raw = (x0_ref[rows, :], x1_ref[rows, :], v_ref[rows, :])
            o_ref[rows, :] = _hyena_sequence(raw, *common, fcc_ref, fsc_ref, kcc_ref, ksc_ref, knc_ref, CTX_LEN)


def _hyena(hy, filter_params, conv_w, conv_b, bias):
    assert T_CTX == SEQ, "the context rows must form exactly one latent-sized row block"
    n_ct = HY_WIDTH // HY_CT
    cw = conv_w.reshape(3, 3, HY_WIDTH).transpose(1, 0, 2)
    cb = conv_b.reshape(3, HY_WIDTH)
    feat_l, win_l = _filter_features(SEQ)
    feat_c, win_c = _filter_features(CTX_LEN)
    tab_l = _dft_tables(SEQ)
    tab_c = _dft_tables(CTX_LEN)

    def part_spec(part):
        return pl.BlockSpec((SEQ, HY_CT), lambda c, b: (b, part * n_ct + c))

    chan = lambda c, b: (0, c)
    once_per_tile = lambda rows: pl.BlockSpec((rows, HY_CT), chan, pipeline_mode=pl.Buffered(1))
    spectrum = lambda rows: pltpu.VMEM((rows, HY_CT), F32)
    return pl.pallas_call(
        _hyena_kernel,
        grid=(n_ct, BATCH + 1),
        in_specs=[
            part_spec(0), part_spec(1), part_spec(2),
            pl.BlockSpec((3, 3, HY_CT), lambda c, b: (0, 0, c)),
            pl.BlockSpec((3, HY_CT), chan),
            pl.BlockSpec((1, HY_CT), chan),
            _resident((SEQ, LANES)), once_per_tile(SEQ),
            _resident((CTX_LEN, LANES)), once_per_tile(CTX_LEN),
            _resident((LANES, LANES)), _resident((1, LANES)), _resident((LANES, LANES)), _resident((1, LANES)),
            once_per_tile(LANES), once_per_tile(LANES), _resident((2, LANES)),
            _resident((SEQ, SEQ)), _resident((SEQ, SEQ)),
            _resident((CTX_LEN, CTX_LEN)), _resident((CTX_LEN, CTX_LEN)),
        ],
        out_specs=pl.BlockSpec((SEQ, HY_CT), lambda c, b: (b, c)),
        out_shape=jax.ShapeDtypeStruct((T_ALL, HY_WIDTH), F32),
        scratch_shapes=[pltpu.VMEM((SEQ, LANES), BF16), pltpu.VMEM((CTX_LEN, LANES), BF16),
                        spectrum(SEQ), spectrum(SEQ), spectrum(1), spectrum(CTX_LEN), spectrum(CTX_LEN), spectrum(1)],
        compiler_params=_params("arbitrary", "arbitrary"),
        name="hyena_conv",
    )(hy, hy, hy, cw, cb, bias.reshape(1, HY_WIDTH), feat_l, win_l, feat_c, win_c, *filter_params, *tab_l, *tab_c)


def _rope_partner(n):
    idx = np.arange(n)
    return np.where((idx % (2 * ROPE_AXIS_PAIRS)) < ROPE_AXIS_PAIRS, idx + ROPE_AXIS_PAIRS, idx - ROPE_AXIS_PAIRS)


def _slot_lanes():
    d = np.arange(MLA_ROPE)
    axis, second, pair = d // (2 * ROPE_AXIS_PAIRS), (d // ROPE_AXIS_PAIRS) % 2, d % ROPE_AXIS_PAIRS
    rope_lane = axis * ROPE_AXIS_PAIRS + pair + (HEAD_SLOT // 2) * second
    free = np.setdiff1d(np.arange(HEAD_SLOT), rope_lane)
    return rope_lane, free[:MLA_NOPE]


def _to_slot(nope, rope):
    rope_lane, nope_lane = _slot_lanes()
    source = [(None, 0)] * HEAD_SLOT
    for arr, lanes in ((nope, nope_lane), (rope, rope_lane)):
        if arr is not None:
            for idx, lane in enumerate(lanes):
                source[lane] = (arr, idx)
    like = nope if nope is not None else rope
    pieces, lane = [], 0
    while lane < HEAD_SLOT:
        arr, start = source[lane]
        run = 1
        while (lane + run < HEAD_SLOT and source[lane + run][0] is arr
               and (arr is None or source[lane + run][1] == start + run)):
            run += 1
        pieces.append(jnp.zeros(like.shape[:-1] + (run,), like.dtype) if arr is None else arr[..., start:start + run])
        lane += run
    return jnp.concatenate(pieces, axis=-1)


def _rope_tables():
    f32 = np.float32
    rope_lane, nope_lane = _slot_lanes()
    rows = np.repeat(np.arange(SEQ // GRID_W, dtype=f32), GRID_W)
    cols = np.tile(np.arange(GRID_W, dtype=f32), SEQ // GRID_W)
    inv_freq = (f32(ROPE_BASE) ** (-np.arange(ROPE_AXIS_PAIRS, dtype=f32) / f32(ROPE_AXIS_PAIRS))).astype(f32)
    ang_r = rows[:, None] * inv_freq
    ang_c = cols[:, None] * inv_freq
    cos32 = np.concatenate([np.cos(ang_r)] * 2 + [np.cos(ang_c)] * 2, axis=1)
    sin32 = np.concatenate([-np.sin(ang_r), np.sin(ang_r), -np.sin(ang_c), np.sin(ang_c)], axis=1)

    def table(n_rows, rope_vals, nope_val):
        t = np.zeros((n_rows, HEAD_SLOT), f32)
        t[:, rope_lane] = rope_vals
        t[:, nope_lane] = nope_val
        return t

    q_cos = table(SEQ, cos32, 1.0) * f32(Q_SCALE)
    q_sin = table(SEQ, sin32, 0.0) * f32(Q_SCALE)
    k_cos = table(SEQ, cos32, 0.0)
    k_sin = table(SEQ, sin32, 0.0)
    ident = table(TM_PROJ, 1.0, 0.0)
    zero = np.zeros((TM_PROJ, HEAD_SLOT), f32)
    tables = ((q_cos, zero), (q_sin, zero), (k_cos, ident), (k_sin, zero))
    return tuple(jnp.asarray(np.concatenate(t, 0).astype(f32)) for t in tables)


def _mla_weights(w_in, w_q_up, w_kv_up):
    w_q = w_in[:, :MLA_Q_LORA]
    w_kv = w_in[:, MLA_Q_LORA:MLA_Q_LORA + MLA_KV_LORA]
    w_kr = w_in[:, MLA_Q_LORA + MLA_KV_LORA:]
    w_kr2 = jnp.concatenate([_to_slot(None, w_kr), _to_slot(None, w_kr[:, _rope_partner(MLA_ROPE)])], axis=1)

    qh = w_q_up.reshape(MLA_Q_LORA, MLA_HEADS, MLA_NOPE + MLA_ROPE)
    wq_slot = _to_slot(qh[..., :MLA_NOPE], qh[..., MLA_NOPE:]).reshape(MLA_Q_LORA, MLA_HEADS * HEAD_SLOT)

    kvh = w_kv_up.reshape(MLA_KV_LORA, MLA_HEADS, MLA_NOPE + MLA_V)
    wk_slot = _to_slot(kvh[..., :MLA_NOPE], None).reshape(MLA_KV_LORA, MLA_HEADS * HEAD_SLOT)
    wv_slot = jnp.pad(kvh[..., MLA_NOPE:], ((0, 0), (0, 0), (0, HEAD_SLOT - MLA_V)))
    wv_slot = wv_slot.reshape(MLA_KV_LORA, MLA_HEADS * HEAD_SLOT)
    return [w.astype(BF16) for w in (w_q, w_kv, w_kr2, wq_slot, wk_slot, wv_slot)]


def _mlaproj_kernel(x_ref, mod_ref, wq_ref, wkv_ref, wkr_ref, wqs_ref, wks_ref, wv_ref,
                    vone_ref, qg_ref, kvg_ref, qcos_ref, qsin_ref, kcos_ref, ksin_ref,
                    q_ref, k_ref, v_ref):
    shift, scale = mod_ref[0, 0], mod_ref[0, 1]
    h = (x_ref[...] * (1.0 + scale) + shift).astype(BF16)

    def rmsnorm(y, g):
        return (y * lax.rsqrt(jnp.mean(y * y, axis=-1, keepdims=True) + RMS_EPS) * g).astype(BF16)

    qn = rmsnorm(_dot(h, wq_ref[...]), qg_ref[...])
    kvn = rmsnorm(_dot(h, wkv_ref[...]), kvg_ref[...])
    kr2 = _dot(h, wkr_ref[...])
    k_rope = kr2[:, :HEAD_SLOT] * kcos_ref[...] + kr2[:, HEAD_SLOT:] * ksin_ref[...]
    k_nope = _dot(kvn, wks_ref[...])
    v_ref[...] = (_dot(kvn, wv_ref[...]) + vone_ref[...]).astype(BF16)
    q_all = _dot(qn, wqs_ref[...])
    q_cos = qcos_ref[...]
    q_sin = qsin_ref[...]
    for hd in range(MLA_HEADS):
        sl = slice(hd * HEAD_SLOT, (hd + 1) * HEAD_SLOT)
        qh = q_all[:, sl]
        rotated = qh * q_cos + pltpu.roll(qh, HEAD_SLOT // 2, 1) * q_sin
        q_ref[:, sl] = rotated.astype(BF16)
        k_ref[:, sl] = (k_nope[:, sl] + k_rope).astype(BF16)


def _mlaproj(x, mod, w_in, q_norm, kv_norm, w_q_up, w_kv_up):
    tm = TM_PROJ
    weights = _mla_weights(w_in, w_q_up, w_kv_up)
    tables = _rope_tables()
    row = lambda i: (i, 0)
    tab = lambda i: (jnp.where(i < T_LAT // tm, i % (SEQ // tm), SEQ // tm), 0)
    wide = MLA_HEADS * HEAD_SLOT
    v_one = jnp.asarray((np.arange(wide) % HEAD_SLOT == MLA_V).astype(np.float32).reshape(1, wide))
    return pl.pallas_call(
        _mlaproj_kernel,
        grid=(T_ALL // tm,),
        in_specs=[
            pl.BlockSpec((tm, D_MODEL), row),
            pl.BlockSpec((1, 2, 1, D_MODEL), lambda i: (_group_of_tile(i, tm), 0, 0, 0)),
            *[_resident(w.shape) for w in weights],
            _resident((1, wide)),
            _resident((1, MLA_Q_LORA)),
            _resident((1, MLA_KV_LORA)),
            *[pl.BlockSpec((tm, HEAD_SLOT), tab) for _ in tables],
        ],
        out_specs=[pl.BlockSpec((tm, wide), row)] * 3,
        out_shape=[jax.ShapeDtypeStruct((T_ALL, wide), BF16)] * 3,
        compiler_params=_params("arbitrary"),
        name="mla_proj",
    )(x, mod, *weights, v_one, q_norm.reshape(1, MLA_Q_LORA), kv_norm.reshape(1, MLA_KV_LORA), *tables)


def _attn_kernel(q_ref, kl_ref, kc_ref, vl_ref, vc_ref, o_ref):
    lane = lax.broadcasted_iota(jnp.int32, (TQ, HEAD_SLOT), 1)
    outs = []
    for hd in range(ATTN_HEADS):
        sl = slice(hd * HEAD_SLOT, (hd + 1) * HEAD_SLOT)
        q = q_ref[:, sl]
        s_lat = _dot_nt(q, kl_ref[:, sl])
        s_ctx = _dot_nt(q, kc_ref[:, sl])
        m = jnp.maximum(jnp.max(s_lat, axis=-1, keepdims=True), jnp.max(s_ctx, axis=-1, keepdims=True))
        p_lat = jnp.exp2(s_lat - m).astype(BF16)
        p_ctx = jnp.exp2(s_ctx - m).astype(BF16)
        o = _dot(p_lat, vl_ref[:, sl]) + _dot(p_ctx, vc_ref[:, sl])
        outs.append(o / o[:, MLA_V:MLA_V + 1])
    for pair in range(ATTN_HEADS // 2):
        packed = jnp.where(lane < MLA_V, outs[2 * pair], pltpu.roll(outs[2 * pair + 1], MLA_V, 1))
        o_ref[:, pair * HEAD_SLOT:(pair + 1) * HEAD_SLOT] = packed.astype(BF16)


def _attention(q, k, v):
    n_q = SEQ // TQ
    n_groups = MLA_HEADS // ATTN_HEADS
    ctx_block0 = T_LAT // CTX_LEN
    wide = ATTN_HEADS * HEAD_SLOT
    lat = lambda b, h, i: (b, h)
    ctx = lambda b, h, i: (ctx_block0 + b, h)
    return pl.pallas_call(
        _attn_kernel,
        grid=(BATCH, n_groups, n_q),
        in_specs=[
            pl.BlockSpec((TQ, wide), lambda b, h, i: (b * n_q + i, h)),
            pl.BlockSpec((SEQ, wide), lat),
            pl.BlockSpec((CTX_LEN, wide), ctx),
            pl.BlockSpec((SEQ, wide), lat),
            pl.BlockSpec((CTX_LEN, wide), ctx),
        ],
        out_specs=pl.BlockSpec((TQ, ATTN_HEADS * MLA_V), lambda b, h, i: (b * n_q + i, h)),
        out_shape=jax.ShapeDtypeStruct((T_LAT, MLA_HEADS * MLA_V), BF16),
        compiler_params=_params("arbitrary", "arbitrary", "arbitrary"),
        name="mla_attention",
    )(q, k, k, v, v)


def kernel(x, c, ctx, c_ctx, ada_w, ada_b, ln_g, ln_b, ffn_w_in, ffn_w_out, ev_w_in, ev_w_out, hy_conv_w, hy_conv_b, hy_filt_w1, hy_filt_b1, hy_filt_w2, hy_filt_b2, hy_filt_w3, hy_sin_freq, hy_bias, rg_conv_w, rg_conv_b, rg_a_w, rg_a_b, rg_x_w, rg_x_b, rg_lambda, mla_w_in, mla_q_norm, mla_kv_norm, mla_w_q_up, mla_w_kv_up, mla_w_out):
    assert x.shape == (BATCH, SEQ, D_MODEL) and ctx.shape == (BATCH, CTX_LEN, D_MODEL)
    cond = jnp.concatenate([c, c_ctx[None], jnp.zeros((2 * SUBLANES - N_GROUPS, D_MODEL), F32)], axis=0)
    mods = _adaln(cond, ada_w, ada_b)[:, :N_GROUPS].reshape(DEPTH, N_GROUPS, N_ADA, 1, D_MODEL)
    ffn_w = (ffn_w_in, ffn_w_out)

    m = mods[0]
    xs = _ffn_split(x.reshape(T_LAT, D_MODEL), ctx.reshape(T_CTX, D_MODEL),
                    m[:, 0:3], ffn_w, (0, 0), ln_g[0, 0], ln_b[0, 0], T_ALL)
    hy, rx, gate_in = _evproj(xs, m[:, 3:5], ev_w_in[0])
    h_sum = _rglru(rx, rg_conv_w[0], rg_conv_b[0], rg_a_w[0], rg_a_b[0], rg_x_w[0], rg_x_b[0], rg_lambda[0])
    filt = _filter_params(hy_filt_w1[0], hy_filt_b1[0], hy_filt_w2[0], hy_filt_b2[0], hy_filt_w3[0], hy_sin_freq[0])
    y_hy = _hyena(hy, filt, hy_conv_w[0], hy_conv_b[0], hy_bias[0])
    xs = _ffn_after_even(xs, y_hy, h_sum, gate_in, ev_w_out[0], ln_g[0, 1], ln_b[0, 1],
                         m[:, 5:9], ffn_w, (0, 1), ln_g[0, 2], ln_b[0, 2], T_ALL)

    m = mods[1]
    xs = _ffn_plain(xs, m[:, 0:3], ffn_w, (1, 0), ln_g[1, 0], ln_b[1, 0], T_ALL)
    q, k, v = _mlaproj(xs, m[:, 3:5], mla_w_in[0], mla_q_norm[0], mla_kv_norm[0], mla_w_q_up[0], mla_w_kv_up[0])
    att = _attention(q, k, v)
    xl = _ffn_after_mla(xs, att, mla_w_out[0], ln_g[1, 1], ln_b[1, 1],
                        m[:, 5:9], ffn_w, (1, 1), ln_g[1, 2], ln_b[1, 2], T_LAT)
    return xl.reshape(BATCH, SEQ, D_MODEL)
```

```python
import functools
import math

import jax
import jax.numpy as jnp
import numpy as np
from jax import lax
from jax.experimental import pallas as pl
from jax.experimental.pallas import tpu as pltpu

F32 = jnp.float32
BF16 = jnp.bfloat16

D_MODEL = 1024
BATCH = 8
SEQ = 2048
DEPTH = 2
CTX_LEN = 256
GRID_W = 64
N_ADA = 9
D_FF = 2816

HY_WIDTH = 512
RG_WIDTH = 512
EV_IN = 3 * HY_WIDTH + 2 * RG_WIDTH
HY_EMB = 33
HY_BANDS = (HY_EMB - 1) // 2
HY_FILT_HIDDEN = 64
HY_TARGET = 1e-2
HY_MAX_DECAY = math.log(HY_TARGET) / 0.3
HY_MIN_DECAY = math.log(HY_TARGET) / 1.5
HY_SHIFT = 0.05
RG_BLOCKS = 8
RG_BLOCK_DIM = RG_WIDTH // RG_BLOCKS
RG_C = 8.0
RG_CONV = 4
RG_PAD_L = 2

MLA_HEADS = 16
MLA_Q_LORA = 768
MLA_KV_LORA = 256
MLA_NOPE = 64
MLA_ROPE = 32
MLA_V = 64
MLA_SCALE = (MLA_NOPE + MLA_ROPE) ** -0.5
ROPE_AXIS_PAIRS = MLA_ROPE // 4
ROPE_BASE = 10000.0

LOG2_E = math.log2(math.e)
ALPHA = (2.0 * DEPTH) ** 0.25
LN_EPS = 1e-6
RMS_EPS = 1e-6

T_LAT = BATCH * SEQ
T_CTX = BATCH * CTX_LEN
T_ALL = T_LAT + T_CTX
N_GROUPS = BATCH + 1

LANES = 128
SUBLANES = 8
VMEM_LIMIT = 56 * 1024 * 1024

TM_FFN = 512
FF_CHUNK = 256
N_FF_CHUNKS = D_FF // FF_CHUNK
FFN_WEIGHT_CHUNKS = 8
TM_PROJ = 512
TM_EVPROJ = 1024
HY_CT = 256
HY_FREQ_TILE = 256
RG_CT = 128
RG_TC = 32
RG_HALO = SUBLANES
HEAD_SLOT = 128
TQ = 512
ATTN_HEADS = 16
Q_SCALE = MLA_SCALE * math.log2(math.e)
ADA_TN = 1152


def _dot(a, b):
    return jnp.dot(a, b, preferred_element_type=F32)


def _dot_nt(a, b):
    return lax.dot_general(a, b, (((1,), (1,)), ((), ())), preferred_element_type=F32)


def _resident(shape):
    nd = len(shape)
    return pl.BlockSpec(shape, lambda *_: (0,) * nd, pipeline_mode=pl.Buffered(1))


def _group_of_tile(i, tm):
    return jnp.where(i < T_LAT // tm, i // (SEQ // tm), BATCH)


def _params(*sem):
    return pltpu.CompilerParams(dimension_semantics=sem, vmem_limit_bytes=VMEM_LIMIT)


def _layernorm(z, g, b):
    mu = jnp.mean(z, axis=-1, keepdims=True)
    zc = z - mu
    var = jnp.mean(zc * zc, axis=-1, keepdims=True)
    return zc * lax.rsqrt(var + LN_EPS) * g + b


def _adaln_kernel(c_ref, w_ref, b_ref, o_ref):
    c = c_ref[...]
    s = (c * jax.nn.sigmoid(c)).astype(BF16)
    o_ref[0] = _dot(s, w_ref[0].astype(BF16)) + b_ref[0]


def _adaln(cond, ada_w, ada_b):
    n_out = N_ADA * D_MODEL
    rows = cond.shape[0]
    return pl.pallas_call(
        _adaln_kernel,
        grid=(DEPTH, n_out // ADA_TN),
        in_specs=[
            pl.BlockSpec((rows, D_MODEL), lambda l, j: (0, 0)),
            pl.BlockSpec((1, D_MODEL, ADA_TN), lambda l, j: (l, 0, j)),
            pl.BlockSpec((1, 1, ADA_TN), lambda l, j: (l, 0, j)),
        ],
        out_specs=pl.BlockSpec((1, rows, ADA_TN), lambda l, j: (l, 0, j)),
        out_shape=jax.ShapeDtypeStruct((DEPTH, rows, n_out), F32),
        compiler_params=_params("arbitrary", "arbitrary"),
        name="adaln",
    )(cond, ada_w, ada_b.reshape(DEPTH, 1, n_out))


def _ffn_body(x, mod_ref, wi_ref, wo_ref, g_ref, b_ref, o_ref, act_ref):
    n_slots = mod_ref.shape[1]
    shift, scale, gate = mod_ref[0, n_slots - 3], mod_ref[0, n_slots - 2], mod_ref[0, n_slots - 1]
    h = (x * (1.0 + scale) + shift).astype(BF16)
    for j in range(N_FF_CHUNKS):
        lo = j * FF_CHUNK
        a = _dot(h, wi_ref[:, lo:lo + FF_CHUNK])
        u = _dot(h, wi_ref[:, D_FF + lo:D_FF + lo + FF_CHUNK])
        act_ref[:, lo:lo + FF_CHUNK] = (a * jax.nn.sigmoid(a) * u).astype(BF16)
    y = _dot(act_ref[...], wo_ref[...])
    o_ref[...] = _layernorm(ALPHA * x + (0.5 * gate) * y, g_ref[...], b_ref[...])


def _load_weight_bf16(w_hbm, which, dst_ref, stage_ref, sem):
    layer, half = which
    rows = stage_ref.shape[1]
    n_chunks = dst_ref.shape[0] // rows

    def copy(c):
        return pltpu.make_async_copy(w_hbm.at[layer, half, pl.ds(c * rows, rows)], stage_ref.at[c % 2], sem.at[c % 2])

    copy(0).start()
    for c in range(n_chunks):
        if c + 1 < n_chunks:
            copy(c + 1).start()
        copy(c).wait()
        dst_ref[c * rows:(c + 1) * rows, :] = stage_ref[c % 2].astype(BF16)


def _ffn_steps(x_fn, which, mod_ref, wi_hbm, wo_hbm, g_ref, b_ref, o_ref,
               act_ref, wi_ref, wo_ref, stage_in_ref, stage_out_ref, sem_in, sem_out):
    @pl.when(pl.program_id(0) == 0)
    def _():
        _load_weight_bf16(wi_hbm, which, wi_ref, stage_in_ref, sem_in)
        _load_weight_bf16(wo_hbm, which, wo_ref, stage_out_ref, sem_out)

    _ffn_body(x_fn(), mod_ref, wi_ref, wo_ref, g_ref, b_ref, o_ref, act_ref)


def _ffn_kernel(x_ref, *rest, which):
    _ffn_steps(lambda: x_ref[...], which, *rest)


def _ffn_split_kernel(xl_ref, xc_ref, *rest, which):
    _ffn_steps(lambda: jnp.where(pl.program_id(0) < T_LAT // TM_FFN, xl_ref[...], xc_ref[...]), which, *rest)


def _ffn_after_even_kernel(x_ref, yh_ref, hs_ref, gt_ref, wmix_ref, g1_ref, b1_ref, mod_ref, *rest, which):
    def x_fn():
        y_rg = hs_ref[...] * jax.nn.gelu(gt_ref[...], approximate=True)
        y = (_dot(yh_ref[...].astype(BF16), wmix_ref[:HY_WIDTH, :])
             + _dot(y_rg.astype(BF16), wmix_ref[HY_WIDTH:, :]))
        return _layernorm(ALPHA * x_ref[...] + mod_ref[0, 0] * y, g1_ref[...], b1_ref[...])

    _ffn_steps(x_fn, which, mod_ref, *rest)


def _ffn_after_mla_kernel(x_ref, att_ref, wmix_ref, g1_ref, b1_ref, mod_ref, *rest, which):
    def x_fn():
        y = _dot(att_ref[...], wmix_ref[...])
        return _layernorm(ALPHA * x_ref[...] + mod_ref[0, 0] * y, g1_ref[...], b1_ref[...])

    _ffn_steps(x_fn, which, mod_ref, *rest)


def _ffn(body, lead_args, lead_specs, mod, ffn_w, which, g, b, n_rows):
    tm = TM_FFN
    w_in_all, w_out_all = ffn_w
    return pl.pallas_call(
        functools.partial(body, which=which),
        grid=(n_rows // tm,),
        in_specs=[
            *lead_specs,
            pl.BlockSpec((1, mod.shape[1], 1, D_MODEL), lambda i: (_group_of_tile(i, tm), 0, 0, 0)),
            pl.BlockSpec(memory_space=pl.ANY),
            pl.BlockSpec(memory_space=pl.ANY),
            _resident((1, D_MODEL)),
            _resident((1, D_MODEL)),
        ],
        out_specs=pl.BlockSpec((tm, D_MODEL), lambda i: (i, 0)),
        out_shape=jax.ShapeDtypeStruct((n_rows, D_MODEL), F32),
        scratch_shapes=[
            pltpu.VMEM((tm, D_FF), BF16),
            pltpu.VMEM((D_MODEL, 2 * D_FF), BF16),
            pltpu.VMEM((D_FF, D_MODEL), BF16),
            pltpu.VMEM((2, D_MODEL // FFN_WEIGHT_CHUNKS, 2 * D_FF), F32),
            pltpu.VMEM((2, D_FF // FFN_WEIGHT_CHUNKS, D_MODEL), F32),
            pltpu.SemaphoreType.DMA((2,)),
            pltpu.SemaphoreType.DMA((2,)),
        ],
        compiler_params=_params("arbitrary"),
        name="half_ffn",
    )(*lead_args, mod, w_in_all, w_out_all, g.reshape(1, D_MODEL), b.reshape(1, D_MODEL))


def _row_spec(width, tm=TM_FFN):
    return pl.BlockSpec((tm, width), lambda i: (i, 0))


def _ffn_plain(x, *args):
    return _ffn(_ffn_kernel, (x,), [_row_spec(D_MODEL)], *args)


def _ffn_split(x_lat, x_ctx, *args):
    n_lat = T_LAT // TM_FFN
    specs = [pl.BlockSpec((TM_FFN, D_MODEL), lambda i: (jnp.minimum(i, n_lat - 1), 0)),
             pl.BlockSpec((TM_FFN, D_MODEL), lambda i: (jnp.maximum(i - n_lat, 0), 0))]
    return _ffn(_ffn_split_kernel, (x_lat, x_ctx), specs, *args)


def _ffn_after_even(x, y_hy, h_sum, gate_in, w_mix, g1, b1, *args):
    lead = (x, y_hy, h_sum, gate_in, w_mix.astype(BF16), g1.reshape(1, D_MODEL), b1.reshape(1, D_MODEL))
    specs = [_row_spec(D_MODEL), _row_spec(HY_WIDTH), _row_spec(RG_WIDTH), _row_spec(RG_WIDTH),
             _resident((HY_WIDTH + RG_WIDTH, D_MODEL)), _resident((1, D_MODEL)), _resident((1, D_MODEL))]
    return _ffn(_ffn_after_even_kernel, lead, specs, *args)


def _ffn_after_mla(x, att, w_mix, g1, b1, *args):
    lead = (x, att, w_mix.astype(BF16), g1.reshape(1, D_MODEL), b1.reshape(1, D_MODEL))
    specs = [_row_spec(D_MODEL), _row_spec(MLA_HEADS * MLA_V),
             _resident((MLA_HEADS * MLA_V, D_MODEL)), _resident((1, D_MODEL)), _resident((1, D_MODEL))]
    return _ffn(_ffn_after_mla_kernel, lead, specs, *args)


def _evproj_kernel(x_ref, mod_ref, w_ref, hy_ref, rx_ref, gt_ref):
    shift, scale = mod_ref[0, 0], mod_ref[0, 1]
    h = (x_ref[...] * (1.0 + scale) + shift).astype(BF16)
    n_hy = 3 * HY_WIDTH
    hy_ref[...] = _dot(h, w_ref[:, :n_hy])
    rx_ref[...] = _dot(h, w_ref[:, n_hy:n_hy + RG_WIDTH])
    gt_ref[...] = _dot(h, w_ref[:, n_hy + RG_WIDTH:])


def _evproj(x, mod, w_in):
    tm = TM_EVPROJ
    row = lambda i: (i, 0)
    return pl.pallas_call(
        _evproj_kernel,
        grid=(T_ALL // tm,),
        in_specs=[
            pl.BlockSpec((tm, D_MODEL), row),
            pl.BlockSpec((1, 2, 1, D_MODEL), lambda i: (_group_of_tile(i, tm), 0, 0, 0)),
            _resident((D_MODEL, EV_IN)),
        ],
        out_specs=[
            pl.BlockSpec((tm, 3 * HY_WIDTH), row),
            pl.BlockSpec((tm, RG_WIDTH), row),
            pl.BlockSpec((tm, RG_WIDTH), row),
        ],
        out_shape=[
            jax.ShapeDtypeStruct((T_ALL, 3 * HY_WIDTH), F32),
            jax.ShapeDtypeStruct((T_ALL, RG_WIDTH), F32),
            jax.ShapeDtypeStruct((T_ALL, RG_WIDTH), F32),
        ],
        compiler_params=_params("arbitrary"),
        name="even_in_proj",
    )(x, mod, w_in.astype(BF16))


def _block_diag_tiles(w):
    per = RG_CT // RG_BLOCK_DIM
    w = w.reshape(RG_WIDTH // RG_CT, per, RG_BLOCK_DIM, RG_BLOCK_DIM)
    eye = jnp.eye(per, dtype=w.dtype)
    return jnp.einsum('cpde,pq->cpdqe', w, eye).reshape(RG_WIDTH // RG_CT, RG_CT, RG_CT)


def _rg_kernel(rx_ref, cw_ref, cb_ref, w_ref, bias_ref, lam_ref, out_ref,
                padl_ref, padc_ref, xt_f_ref, xt_b_ref, coef_a_ref, coef_b_ref, hbuf_f_ref, hbuf_b_ref):
    neg_lam = -lam_ref[...]
    softplus = jnp.maximum(neg_lam, 0.0) + jnp.log1p(jnp.exp(-jnp.abs(neg_lam)))
    neg_log_a_scale = RG_C * softplus
    a_exp2_scale = -LOG2_E * neg_log_a_scale
    cb = cb_ref[...]
    xts = (xt_f_ref, xt_b_ref)
    hbuf = (hbuf_f_ref, hbuf_b_ref)
    slab_rows = RG_TC + 2 * RG_HALO

    def step_rows(s):
        return slice(s * SUBLANES, (s + 1) * SUBLANES)

    halo = jnp.zeros((BATCH, RG_HALO, RG_CT), F32)
    for pad_ref, length in ((padl_ref, SEQ), (padc_ref, CTX_LEN)):
        pad_ref[:, 0:RG_HALO, :] = halo
        pad_ref[:, RG_HALO + length:, :] = halo
    padl_ref[:, RG_HALO:RG_HALO + SEQ, :] = rx_ref[0:BATCH]
    for b in range(BATCH):
        padc_ref[b, RG_HALO:RG_HALO + CTX_LEN, :] = rx_ref[BATCH, b * CTX_LEN:(b + 1) * CTX_LEN, :]

    def add_out(latent, t_start, d):
        t_start = pl.multiple_of(t_start, SUBLANES)
        for b in range(BATCH):
            rows = hbuf[d][pl.ds(b, RG_TC, stride=SUBLANES), :]
            if latent:
                out_ref[b, pl.ds(t_start, RG_TC), :] += rows
            else:
                out_ref[BATCH, pl.ds(b * CTX_LEN + t_start, RG_TC), :] += rows

    def sigmoid(z):
        return 1.0 / (1.0 + jnp.exp2(z * (-LOG2_E)))

    def coeffs(latent, length, t0, d, dst_ref):
        t0 = pl.multiple_of(jnp.asarray(t0, jnp.int32), SUBLANES)
        pad_ref = padl_ref if latent else padc_ref
        xt = xts[d]
        for b in range(BATCH):
            xt[pl.ds(b, slab_rows, stride=SUBLANES), :] = pad_ref[b, pl.ds(t0, slab_rows), :]
        x2 = cb
        for k in range(RG_CONV):
            start = (RG_HALO - RG_PAD_L + k) * SUBLANES
            x2 = x2 + xt[start:start + RG_TC * SUBLANES, :] * cw_ref[k:k + 1, :]
        xb = x2.astype(BF16)
        r = sigmoid(_dot(xb, w_ref[0, 2 * d]) + bias_ref[2 * d:2 * d + 1, :])
        i = sigmoid(_dot(xb, w_ref[0, 2 * d + 1]) + bias_ref[2 * d + 1:2 * d + 2, :])
        a = jnp.exp2(r * a_exp2_scale[d:d + 1, :])
        b = jnp.sqrt(jnp.tanh(r * neg_log_a_scale[d:d + 1, :]) * (a * a + 1.0)) * (i * x2)
        dst_ref[d, 0] = a
        dst_ref[d, 1] = b

    def scan(latent, tf, tb, src_ref, hf, hb):
        tf = jnp.asarray(tf, jnp.int32)
        tb = jnp.asarray(tb, jnp.int32)
        for s in range(RG_TC):
            rows = step_rows(s)
            hf = src_ref[0, 0, rows, :] * hf + src_ref[0, 1, rows, :]
            hbuf_f_ref[rows, :] = hf
            rows = step_rows(RG_TC - 1 - s)
            hb = src_ref[1, 0, rows, :] * hb + src_ref[1, 1, rows, :]
            hbuf_b_ref[rows, :] = hb
        add_out(latent, tf, 0)
        add_out(latent, tb, 1)
        return hf, hb

    def sweep(latent, length, h_fwd, h_bwd):
        n_chunks = length // RG_TC
        last = n_chunks - 1

        def both(j, dst_ref):
            coeffs(latent, length, j * RG_TC, 0, dst_ref)
            coeffs(latent, length, (last - j) * RG_TC, 1, dst_ref)

        def body(i, carry):
            j = 2 * i
            both(j + 1, coef_b_ref)
            carry = scan(latent, j * RG_TC, (last - j) * RG_TC, coef_a_ref, *carry)
            both(jnp.minimum(j + 2, last), coef_a_ref)
            return scan(latent, (j + 1) * RG_TC, (last - j - 1) * RG_TC, coef_b_ref, *carry)

        both(0, coef_a_ref)
        return lax.fori_loop(0, n_chunks // 2, body, (h_fwd, h_bwd))

    out_ref[...] = jnp.zeros_like(out_ref)
    zero = jnp.zeros((SUBLANES, RG_CT), F32)
    hf, hb = sweep(False, CTX_LEN, zero, zero)
    sweep(True, SEQ, hf, hb)


def _rglru(rx, conv_w, conv_b, a_w, a_b, x_w, x_b, lam):
    assert T_CTX == SEQ, "the context rows must form exactly one latent-sized row block"
    w = jnp.stack([_block_diag_tiles(a_w[0]), _block_diag_tiles(x_w[0]),
                   _block_diag_tiles(a_w[1]), _block_diag_tiles(x_w[1])], axis=1).astype(BF16)
    bias = jnp.stack([a_b[0], x_b[0], a_b[1], x_b[1]], axis=0)
    n_ct = RG_WIDTH // RG_CT
    n_blocks = BATCH + 1
    chan = lambda c: (0, 0, c)
    lane = lambda c: (0, c)
    coef = pltpu.VMEM((2, 2, RG_TC * BATCH, RG_CT), F32)
    slab = pltpu.VMEM(((RG_TC + 2 * RG_HALO) * BATCH, RG_CT), F32)
    chunk = pltpu.VMEM((RG_TC * BATCH, RG_CT), F32)
    out = pl.pallas_call(
        _rg_kernel,
        grid=(n_ct,),
        in_specs=[
            pl.BlockSpec((n_blocks, SEQ, RG_CT), chan),
            pl.BlockSpec((4, RG_CT), lane),
            pl.BlockSpec((1, RG_CT), lane),
            pl.BlockSpec((1, 4, RG_CT, RG_CT), lambda c: (c, 0, 0, 0)),
            pl.BlockSpec((4, RG_CT), lane),
            pl.BlockSpec((2, RG_CT), lane),
        ],
        out_specs=pl.BlockSpec((n_blocks, SEQ, RG_CT), chan),
        out_shape=jax.ShapeDtypeStruct((n_blocks, SEQ, RG_WIDTH), F32),
        scratch_shapes=[
            pltpu.VMEM((BATCH, SEQ + 2 * RG_HALO, RG_CT), F32),
            pltpu.VMEM((BATCH, CTX_LEN + 2 * RG_HALO, RG_CT), F32),
            slab, slab,
            coef, coef,
            chunk, chunk,
        ],
        compiler_params=_params("arbitrary"),
        name="rglru_scan",
    )(rx.reshape(n_blocks, SEQ, RG_WIDTH), conv_w, conv_b.reshape(1, RG_WIDTH), w, bias, lam)
    return out.reshape(T_ALL, RG_WIDTH)


def _dft_tables(length):
    step = 1 << (length.bit_length() // 2)
    s = np.arange(length, dtype=np.int64)

    def thin(kvec):
        ang = ((kvec[:, None] * s[None, :]) % (2 * length)) * (math.pi / length)
        return (jnp.asarray(np.cos(ang)[:, None, :], dtype=F32),
                jnp.asarray(np.sin(ang)[:, None, :], dtype=F32))

    c_hi, s_hi = thin(np.arange(0, length, step, dtype=np.int64))
    c_lo, s_lo = thin(np.arange(step, dtype=np.int64))
    c_lo, s_lo = c_lo.reshape(1, step, length), s_lo.reshape(1, step, length)
    fc = (c_hi * c_lo - s_hi * s_lo).reshape(length, length)
    fs = (s_hi * c_lo + c_hi * s_lo).reshape(length, length)
    return fc.astype(BF16), fs.astype(BF16)


def _filter_features(length):
    f32 = np.float32
    pos = np.arange(length, dtype=f32)
    t = pos / f32(length)
    bands = np.linspace(1e-4, HY_BANDS - 1, HY_BANDS, dtype=f32)
    ang = (f32(2.0 * math.pi) * pos / f32(length))[:, None] * bands[None, :]
    feats = np.concatenate([t[:, None], np.cos(ang), -np.sin(ang)], axis=-1).astype(f32)
    feats = np.pad(feats, ((0, 0), (0, LANES - HY_EMB)))
    deltas = np.abs(np.linspace(HY_MIN_DECAY, HY_MAX_DECAY, HY_WIDTH, dtype=f32))
    window = (np.exp(-t[:, None] * deltas[None, :]) + f32(HY_SHIFT)).astype(f32)
    return jnp.asarray(feats), jnp.asarray(window)


def _filter_hidden(feat_ref, w1_ref, b1_ref, w2_ref, b2_ref, sf_ref, hid_ref):
    h = jnp.sin(sf_ref[0:1, :] * (_dot(feat_ref[...].astype(BF16), w1_ref[...]) + b1_ref[...]))
    h = jnp.sin(sf_ref[1:2, :] * (_dot(h.astype(BF16), w2_ref[...]) + b2_ref[...]))
    hid_ref[...] = h.astype(BF16)


def _filter_spectrum(hid_ref, win_ref, w3f_ref, w3b_ref, fc_ref, fs_ref, kc_ref, ks_ref, kn_ref, length):
    n = 2 * length
    hb16 = hid_ref[...]
    win = win_ref[...]
    row = lax.broadcasted_iota(jnp.int32, (length, HY_CT), 0)
    h_fwd = _dot(hb16, w3f_ref[...]) * win
    h_bwd0 = jnp.where(row == 0, 0.0, _dot(hb16, w3b_ref[...]) * win)
    h_sum = h_fwd + h_bwd0
    h_dif = h_fwd - h_bwd0
    weight = jnp.where(row == 0, 1.0 / n, 2.0 / n)
    kc_ref[...] = _dot(fc_ref[...], h_sum.astype(BF16)) * weight
    ks_ref[...] = _dot(fs_ref[...], h_dif.astype(BF16)) * weight
    sign = jnp.where((row & 1) == 0, 1.0, -1.0)
    kn_ref[...] = jnp.sum(h_sum * sign, axis=0, keepdims=True) * (1.0 / n)


def _filter_params(w1, b1, w2, b2, w3, sin_freq):
    hp = LANES - HY_FILT_HIDDEN
    w3p = jnp.pad(w3, ((0, hp), (0, 0))).astype(BF16)
    return (jnp.pad(w1, ((0, LANES - HY_EMB), (0, hp))).astype(BF16), jnp.pad(b1, (0, hp)).reshape(1, LANES),
            jnp.pad(w2, ((0, hp), (0, hp))).astype(BF16), jnp.pad(b2, (0, hp)).reshape(1, LANES),
            w3p[:, :HY_WIDTH], w3p[:, HY_WIDTH:], jnp.pad(sin_freq, ((0, 0), (0, hp))))


def _hyena_sequence(raw, cw_ref, cb_ref, bias_ref, fc_ref, fs_ref, kc_ref, ks_ref, kn_ref, length):
    row = lax.broadcasted_iota(jnp.int32, (length, HY_CT), 0)

    def short_conv(x, part):
        prev = jnp.where(row == 0, 0.0, pltpu.roll(x, 1, 0))
        nxt = jnp.where(row == length - 1, 0.0, pltpu.roll(x, length - 1, 0))
        return (prev * cw_ref[part, 0:1, :] + x * cw_ref[part, 1:2, :] + nxt * cw_ref[part, 2:3, :]
                + cb_ref[part:part + 1, :])

    x0 = short_conv(raw[0], 0)
    x1 = short_conv(raw[1], 1)
    v = short_conv(raw[2], 2)
    vx = v * x1
    vb = vx.astype(BF16)
    ft = min(length, HY_FREQ_TILE)
    y = None
    for f0 in range(0, length, ft):
        spec_c = _dot(fc_ref[f0:f0 + ft, :], vb)
        spec_s = _dot(fs_ref[f0:f0 + ft, :], vb)
        kc = kc_ref[f0:f0 + ft, :]
        ks = ks_ref[f0:f0 + ft, :]
        p = (spec_c * kc - spec_s * ks).astype(BF16)
        q = (spec_c * ks + spec_s * kc).astype(BF16)
        part = _dot(fc_ref[:, f0:f0 + ft], p) + _dot(fs_ref[:, f0:f0 + ft], q)
        y = part if y is None else y + part
    sign = jnp.where((row & 1) == 0, 1.0, -1.0)
    nyquist = jnp.sum(vx * sign, axis=0, keepdims=True) * kn_ref[...]
    return x0 * (y + sign * nyquist + vx * bias_ref[...])


def _hyena_kernel(x0_ref, x1_ref, v_ref, cw_ref, cb_ref, bias_ref,
                  featl_ref, winl_ref, featc_ref, winc_ref,
                  w1_ref, b1_ref, w2_ref, b2_ref, w3f_ref, w3b_ref, sf_ref,
                  fcl_ref, fsl_ref, fcc_ref, fsc_ref, o_ref,
                  hidl_ref, hidc_ref, kcl_ref, ksl_ref, knl_ref, kcc_ref, ksc_ref, knc_ref):
    step = pl.program_id(1)
    common = (cw_ref, cb_ref, bias_ref)

    @pl.when(jnp.logical_and(pl.program_id(0) == 0, step == 0))
    def _():
        sine_layers = (w1_ref, b1_ref, w2_ref, b2_ref, sf_ref)
        _filter_hidden(featl_ref, *sine_layers, hidl_ref)
        _filter_hidden(featc_ref, *sine_layers, hidc_ref)

    @pl.when(step == 0)
    def _():
        _filter_spectrum(hidl_ref, winl_ref, w3f_ref, w3b_ref, fcl_ref, fsl_ref, kcl_ref, ksl_ref, knl_ref, SEQ)
        _filter_spectrum(hidc_ref, winc_ref, w3f_ref, w3b_ref, fcc_ref, fsc_ref, kcc_ref, ksc_ref, knc_ref, CTX_LEN)

    @pl.when(step < BATCH)
    def _():
        raw = (x0_ref[...], x1_ref[...], v_ref[...])
        o_ref[...] = _hyena_sequence(raw, *common, fcl_ref, fsl_ref, kcl_ref, ksl_ref, knl_ref, SEQ)

    @pl.when(step == BATCH)
    def _():
        for r in range(BATCH):
            rows = slice(r * CTX_LEN, (r + 1) * CTX_LEN)
            raw = (x0_ref[rows, :], x1_ref[rows, :], v_ref[rows, :])
            o_ref[rows, :] = _hyena_sequence(raw, *common, fcc_ref, fsc_ref, kcc_ref, ksc_ref, knc_ref, CTX_LEN)


def _hyena(hy, filter_params, conv_w, conv_b, bias):
    assert T_CTX == SEQ, "the context rows must form exactly one latent-sized row block"
    n_ct = HY_WIDTH // HY_CT
    cw = conv_w.reshape(3, 3, HY_WIDTH).transpose(1, 0, 2)
    cb = conv_b.reshape(3, HY_WIDTH)
    feat_l, win_l = _filter_features(SEQ)
    feat_c, win_c = _filter_features(CTX_LEN)
    tab_l = _dft_tables(SEQ)
    tab_c = _dft_tables(CTX_LEN)

    def part_spec(part):
        return pl.BlockSpec((SEQ, HY_CT), lambda c, b: (b, part * n_ct + c))

    chan = lambda c, b: (0, c)
    once_per_tile = lambda rows: pl.BlockSpec((rows, HY_CT), chan, pipeline_mode=pl.Buffered(1))
    spectrum = lambda rows: pltpu.VMEM((rows, HY_CT), F32)
    return pl.pallas_call(
        _hyena_kernel,
        grid=(n_ct, BATCH + 1),
        in_specs=[
            part_spec(0), part_spec(1), part_spec(2),
            pl.BlockSpec((3, 3, HY_CT), lambda c, b: (0, 0, c)),
            pl.BlockSpec((3, HY_CT), chan),
            pl.BlockSpec((1, HY_CT), chan),
            _resident((SEQ, LANES)), once_per_tile(SEQ),
            _resident((CTX_LEN, LANES)), once_per_tile(CTX_LEN),
            _resident((LANES, LANES)), _resident((1, LANES)), _resident((LANES, LANES)), _resident((1, LANES)),
            once_per_tile(LANES), once_per_tile(LANES), _resident((2, LANES)),
            _resident((SEQ, SEQ)), _resident((SEQ, SEQ)),
            _resident((CTX_LEN, CTX_LEN)), _resident((CTX_LEN, CTX_LEN)),
        ],
        out_specs=pl.BlockSpec((SEQ, HY_CT), lambda c, b: (b, c)),
        out_shape=jax.ShapeDtypeStruct((T_ALL, HY_WIDTH), F32),
        scratch_shapes=[pltpu.VMEM((SEQ, LANES), BF16), pltpu.VMEM((CTX_LEN, LANES), BF16),
                        spectrum(SEQ), spectrum(SEQ), spectrum(1), spectrum(CTX_LEN), spectrum(CTX_LEN), spectrum(1)],
        compiler_params=_params("arbitrary", "arbitrary"),
        name="hyena_conv",
    )(hy, hy, hy, cw, cb, bias.reshape(1, HY_WIDTH), feat_l, win_l, feat_c, win_c, *filter_params, *tab_l, *tab_c)


def _rope_partner(n):
    idx = np.arange(n)
    return np.where((idx % (2 * ROPE_AXIS_PAIRS)) < ROPE_AXIS_PAIRS, idx + ROPE_AXIS_PAIRS, idx - ROPE_AXIS_PAIRS)


def _slot_lanes():
    d = np.arange(MLA_ROPE)
    axis, second, pair = d // (2 * ROPE_AXIS_PAIRS), (d // ROPE_AXIS_PAIRS) % 2, d % ROPE_AXIS_PAIRS
    rope_lane = axis * ROPE_AXIS_PAIRS + pair + (HEAD_SLOT // 2) * second
    free = np.setdiff1d(np.arange(HEAD_SLOT), rope_lane)
    return rope_lane, free[:MLA_NOPE]


def _to_slot(nope, rope):
    rope_lane, nope_lane = _slot_lanes()
    source = [(None, 0)] * HEAD_SLOT
    for arr, lanes in ((nope, nope_lane), (rope, rope_lane)):
        if arr is not None:
            for idx, lane in enumerate(lanes):
                source[lane] = (arr, idx)
    like = nope if nope is not None else rope
    pieces, lane = [], 0
    while lane < HEAD_SLOT:
        arr, start = source[lane]
        run = 1
        while (lane + run < HEAD_SLOT and source[lane + run][0] is arr
               and (arr is None or source[lane + run][1] == start + run)):
            run += 1
        pieces.append(jnp.zeros(like.shape[:-1] + (run,), like.dtype) if arr is None else arr[..., start:start + run])
        lane += run
    return jnp.concatenate(pieces, axis=-1)


def _rope_tables():
    f32 = np.float32
    rope_lane, nope_lane = _slot_lanes()
    rows = np.repeat(np.arange(SEQ // GRID_W, dtype=f32), GRID_W)
    cols = np.tile(np.arange(GRID_W, dtype=f32), SEQ // GRID_W)
    inv_freq = (f32(ROPE_BASE) ** (-np.arange(ROPE_AXIS_PAIRS, dtype=f32) / f32(ROPE_AXIS_PAIRS))).astype(f32)
    ang_r = rows[:, None] * inv_freq
    ang_c = cols[:, None] * inv_freq
    cos32 = np.concatenate([np.cos(ang_r)] * 2 + [np.cos(ang_c)] * 2, axis=1)
    sin32 = np.concatenate([-np.sin(ang_r), np.sin(ang_r), -np.sin(ang_c), np.sin(ang_c)], axis=1)

    def table(n_rows, rope_vals, nope_val):
        t = np.zeros((n_rows, HEAD_SLOT), f32)
        t[:, rope_lane] = rope_vals
        t[:, nope_lane] = nope_val
        return t

    q_cos = table(SEQ, cos32, 1.0) * f32(Q_SCALE)
    q_sin = table(SEQ, sin32, 0.0) * f32(Q_SCALE)
    k_cos = table(SEQ, cos32, 0.0)
    k_sin = table(SEQ, sin32, 0.0)
    ident = table(TM_PROJ, 1.0, 0.0)
    zero = np.zeros((TM_PROJ, HEAD_SLOT), f32)
    tables = ((q_cos, zero), (q_sin, zero), (k_cos, ident), (k_sin, zero))
    return tuple(jnp.asarray(np.concatenate(t, 0).astype(f32)) for t in tables)


def _value_ones():
    assert HEAD_SLOT == 2 * MLA_V
    lane = np.arange(MLA_HEADS * HEAD_SLOT)
    head_is_even = (lane // HEAD_SLOT) % 2 == 0
    upper_half = lane % HEAD_SLOT >= MLA_V
    return jnp.asarray((head_is_even == upper_half).astype(np.float32).reshape(1, -1))


def _mla_weights(w_in, w_q_up, w_kv_up):
    w_q = w_in[:, :MLA_Q_LORA]
    w_kv = w_in[:, MLA_Q_LORA:MLA_Q_LORA + MLA_KV_LORA]
    w_kr = w_in[:, MLA_Q_LORA + MLA_KV_LORA:]
    w_kr2 = jnp.concatenate([_to_slot(None, w_kr), _to_slot(None, w_kr[:, _rope_partner(MLA_ROPE)])], axis=1)

    qh = w_q_up.reshape(MLA_Q_LORA, MLA_HEADS, MLA_NOPE + MLA_ROPE)
    wq_slot = _to_slot(qh[..., :MLA_NOPE], qh[..., MLA_NOPE:]).reshape(MLA_Q_LORA, MLA_HEADS * HEAD_SLOT)

    kvh = w_kv_up.reshape(MLA_KV_LORA, MLA_HEADS, MLA_NOPE + MLA_V)
    wk_slot = _to_slot(kvh[..., :MLA_NOPE], None).reshape(MLA_KV_LORA, MLA_HEADS * HEAD_SLOT)
    even = (jnp.arange(MLA_HEADS) % 2 == 0).astype(F32)[None, :, None]
    wv = kvh[..., MLA_NOPE:]
    wv_slot = jnp.concatenate([wv * even, wv * (1.0 - even)], axis=-1).reshape(MLA_KV_LORA, MLA_HEADS * HEAD_SLOT)
    return [w.astype(BF16) for w in (w_q, w_kv, w_kr2, wq_slot, wk_slot, wv_slot)]


def _mlaproj_kernel(x_ref, mod_ref, wq_ref, wkv_ref, wkr_ref, wqs_ref, wks_ref, wv_ref,
                    vone_ref, qg_ref, kvg_ref, qcos_ref, qsin_ref, kcos_ref, ksin_ref,
                    q_ref, k_ref, v_ref):
    shift, scale = mod_ref[0, 0], mod_ref[0, 1]
    h = (x_ref[...] * (1.0 + scale) + shift).astype(BF16)

    def rmsnorm(y, g):
        return (y * lax.rsqrt(jnp.mean(y * y, axis=-1, keepdims=True) + RMS_EPS) * g).astype(BF16)

    qn = rmsnorm(_dot(h, wq_ref[...]), qg_ref[...])
    kvn = rmsnorm(_dot(h, wkv_ref[...]), kvg_ref[...])
    kr2 = _dot(h, wkr_ref[...])
    k_rope = kr2[:, :HEAD_SLOT] * kcos_ref[...] + kr2[:, HEAD_SLOT:] * ksin_ref[...]
    k_nope = _dot(kvn, wks_ref[...])
    v_ref[...] = (_dot(kvn, wv_ref[...]) + vone_ref[...]).astype(BF16)
    q_all = _dot(qn, wqs_ref[...])
    q_cos = qcos_ref[...]
    q_sin = qsin_ref[...]
    for hd in range(MLA_HEADS):
        sl = slice(hd * HEAD_SLOT, (hd + 1) * HEAD_SLOT)
        qh = q_all[:, sl]
        rotated = qh * q_cos + pltpu.roll(qh, HEAD_SLOT // 2, 1) * q_sin
        q_ref[:, sl] = rotated.astype(BF16)
        k_ref[:, sl] = (k_nope[:, sl] + k_rope).astype(BF16)


def _mlaproj(x, mod, w_in, q_norm, kv_norm, w_q_up, w_kv_up):
    tm = TM_PROJ
    weights = _mla_weights(w_in, w_q_up, w_kv_up)
    tables = _rope_tables()
    row = lambda i: (i, 0)
    tab = lambda i: (jnp.where(i < T_LAT // tm, i % (SEQ // tm), SEQ // tm), 0)
    wide = MLA_HEADS * HEAD_SLOT
    v_one = _value_ones()
    return pl.pallas_call(
        _mlaproj_kernel,
        grid=(T_ALL // tm,),
        in_specs=[
            pl.BlockSpec((tm, D_MODEL), row),
            pl.BlockSpec((1, 2, 1, D_MODEL), lambda i: (_group_of_tile(i, tm), 0, 0, 0)),
            *[_resident(w.shape) for w in weights],
            _resident((1, wide)),
            _resident((1, MLA_Q_LORA)),
            _resident((1, MLA_KV_LORA)),
            *[pl.BlockSpec((tm, HEAD_SLOT), tab) for _ in tables],
        ],
        out_specs=[pl.BlockSpec((tm, wide), row)] * 3,
        out_shape=[jax.ShapeDtypeStruct((T_ALL, wide), BF16)] * 3,
        compiler_params=_params("arbitrary"),
        name="mla_proj",
    )(x, mod, *weights, v_one, q_norm.reshape(1, MLA_Q_LORA), kv_norm.reshape(1, MLA_KV_LORA), *tables)


def _attn_kernel(q_ref, kl_ref, kc_ref, vl_ref, vc_ref, o_ref):
    low = lax.broadcasted_iota(jnp.int32, (TQ, HEAD_SLOT), 1) < MLA_V
    pair_out = []
    for hd in range(ATTN_HEADS):
        sl = slice(hd * HEAD_SLOT, (hd + 1) * HEAD_SLOT)
        q = q_ref[:, sl]
        s_lat = _dot_nt(q, kl_ref[:, sl])
        s_ctx = _dot_nt(q, kc_ref[:, sl])
        m = jnp.maximum(jnp.max(s_lat, axis=-1, keepdims=True), jnp.max(s_ctx, axis=-1, keepdims=True))
        p_lat = jnp.exp2(s_lat - m).astype(BF16)
        p_ctx = jnp.exp2(s_ctx - m).astype(BF16)
        pair_out.append(_dot(p_lat, vl_ref[:, sl]) + _dot(p_ctx, vc_ref[:, sl]))
        if hd % 2 == 1:
            even, odd = pair_out
            pair_out = []
            numer = jnp.where(low, even, odd)
            denom = pltpu.roll(jnp.where(low, odd, even), MLA_V, 1)
            pair = hd // 2
            o_ref[:, pair * HEAD_SLOT:(pair + 1) * HEAD_SLOT] = (numer / denom).astype(BF16)


def _attention(q, k, v):
    n_q = SEQ // TQ
    n_groups = MLA_HEADS // ATTN_HEADS
    ctx_block0 = T_LAT // CTX_LEN
    wide = ATTN_HEADS * HEAD_SLOT
    lat = lambda b, h, i: (b, h)
    ctx = lambda b, h, i: (ctx_block0 + b, h)
    return pl.pallas_call(
        _attn_kernel,
        grid=(BATCH, n_groups, n_q),
        in_specs=[
            pl.BlockSpec((TQ, wide), lambda b, h, i: (b * n_q + i, h)),
            pl.BlockSpec((SEQ, wide), lat),
            pl.BlockSpec((CTX_LEN, wide), ctx),
            pl.BlockSpec((SEQ, wide), lat),
            pl.BlockSpec((CTX_LEN, wide), ctx),
        ],
        out_specs=pl.BlockSpec((TQ, ATTN_HEADS * MLA_V), lambda b, h, i: (b * n_q + i, h)),
        out_shape=jax.ShapeDtypeStruct((T_LAT, MLA_HEADS * MLA_V), BF16),
        compiler_params=_params("arbitrary", "arbitrary", "arbitrary"),
        name="mla_attention",
    )(q, k, k, v, v)


def kernel(x, c, ctx, c_ctx, ada_w, ada_b, ln_g, ln_b, ffn_w_in, ffn_w_out, ev_w_in, ev_w_out, hy_conv_w, hy_conv_b, hy_filt_w1, hy_filt_b1, hy_filt_w2, hy_filt_b2, hy_filt_w3, hy_sin_freq, hy_bias, rg_conv_w, rg_conv_b, rg_a_w, rg_a_b, rg_x_w, rg_x_b, rg_lambda, mla_w_in, mla_q_norm, mla_kv_norm, mla_w_q_up, mla_w_kv_up, mla_w_out):
    assert x.shape == (BATCH, SEQ, D_MODEL) and ctx.shape == (BATCH, CTX_LEN, D_MODEL)
    cond = jnp.concatenate([c, c_ctx[None], jnp.zeros((2 * SUBLANES - N_GROUPS, D_MODEL), F32)], axis=0)
    mods = _adaln(cond, ada_w, ada_b)[:, :N_GROUPS].reshape(DEPTH, N_GROUPS, N_ADA, 1, D_MODEL)
    ffn_w = (ffn_w_in, ffn_w_out)

    m = mods[0]
    xs = _ffn_split(x.reshape(T_LAT, D_MODEL), ctx.reshape(T_CTX, D_MODEL),
                    m[:, 0:3], ffn_w, (0, 0), ln_g[0, 0], ln_b[0, 0], T_ALL)
    hy, rx, gate_in = _evproj(xs, m[:, 3:5], ev_w_in[0])
    h_sum = _rglru(rx, rg_conv_w[0], rg_conv_b[0], rg_a_w[0], rg_a_b[0], rg_x_w[0], rg_x_b[0], rg_lambda[0])
    filt = _filter_params(hy_filt_w1[0], hy_filt_b1[0], hy_filt_w2[0], hy_filt_b2[0], hy_filt_w3[0], hy_sin_freq[0])
    y_hy = _hyena(hy, filt, hy_conv_w[0], hy_conv_b[0], hy_bias[0])
    xs = _ffn_after_even(xs, y_hy, h_sum, gate_in, ev_w_out[0], ln_g[0, 1], ln_b[0, 1],
                         m[:, 5:9], ffn_w, (0, 1), ln_g[0, 2], ln_b[0, 2], T_ALL)

    m = mods[1]
    xs = _ffn_plain(xs, m[:, 0:3], ffn_w, (1, 0), ln_g[1, 0], ln_b[1, 0], T_ALL)
    q, k, v = _mlaproj(xs, m[:, 3:5], mla_w_in[0], mla_q_norm[0], mla_kv_norm[0], mla_w_q_up[0], mla_w_kv_up[0])
    att = _attention(q, k, v)
    xl = _ffn_after_mla(xs, att, mla_w_out[0], ln_g[1, 1], ln_b[1, 1],
                        m[:, 5:9], ffn_w, (1, 1), ln_g[1, 2], ln_b[1, 2], T_LAT)
    return xl.reshape(BATCH, SEQ, D_MODEL)
```

```python
import functools
import math

import jax
import jax.numpy as jnp
import numpy as np
from jax import lax
from jax.experimental import pallas as pl
from jax.experimental.pallas import tpu as pltpu

F32 = jnp.float32
BF16 = jnp.bfloat16

D_MODEL = 1024
BATCH = 8
SEQ = 2048
DEPTH = 2
CTX_LEN = 256
GRID_W = 64
N_ADA = 9
D_FF = 2816

HY_WIDTH = 512
RG_WIDTH = 512
EV_IN = 3 * HY_WIDTH + 2 * RG_WIDTH
HY_EMB = 33
HY_BANDS = (HY_EMB - 1) // 2
HY_FILT_HIDDEN = 64
HY_TARGET = 1e-2
HY_MAX_DECAY = math.log(HY_TARGET) / 0.3
HY_MIN_DECAY = math.log(HY_TARGET) / 1.5
HY_SHIFT = 0.05
RG_BLOCKS = 8
RG_BLOCK_DIM = RG_WIDTH // RG_BLOCKS
RG_C = 8.0
RG_CONV = 4
RG_PAD_L = 2

MLA_HEADS = 16
MLA_Q_LORA = 768
MLA_KV_LORA = 256
MLA_NOPE = 64
MLA_ROPE = 32
MLA_V = 64
MLA_SCALE = (MLA_NOPE + MLA_ROPE) ** -0.5
ROPE_AXIS_PAIRS = MLA_ROPE // 4
ROPE_BASE = 10000.0

LOG2_E = math.log2(math.e)
ALPHA = (2.0 * DEPTH) ** 0.25
LN_EPS = 1e-6
RMS_EPS = 1e-6

T_LAT = BATCH * SEQ
T_CTX = BATCH * CTX_LEN
T_ALL = T_LAT + T_CTX
N_GROUPS = BATCH + 1

LANES = 128
SUBLANES = 8
VMEM_LIMIT = 56 * 1024 * 1024

TM_FFN = 512
FF_CHUNK = 256
N_FF_CHUNKS = D_FF // FF_CHUNK
FFN_WEIGHT_CHUNKS = 8
TM_PROJ = 512
TM_EVPROJ = 1024
HY_CT = 256
HY_FREQ_TILE = 256
RG_CT = 128
RG_TC = 32
RG_HALO = SUBLANES
HEAD_SLOT = 128
TQ = 512
ATTN_HEADS = 16
Q_SCALE = MLA_SCALE * math.log2(math.e)
ADA_TN = 2304


def _dot(a, b):
    return jnp.dot(a, b, preferred_element_type=F32)


def _dot_nt(a, b):
    return lax.dot_general(a, b, (((1,), (1,)), ((), ())), preferred_element_type=F32)


def _resident(shape):
    nd = len(shape)
    return pl.BlockSpec(shape, lambda *_: (0,) * nd, pipeline_mode=pl.Buffered(1))


def _group_of_tile(i, tm):
    return jnp.where(i < T_LAT // tm, i // (SEQ // tm), BATCH)


def _params(*sem):
    return pltpu.CompilerParams(dimension_semantics=sem, vmem_limit_bytes=VMEM_LIMIT)


def _layernorm(z, g, b):
    mu = jnp.mean(z, axis=-1, keepdims=True)
    zc = z - mu
    var = jnp.mean(zc * zc, axis=-1, keepdims=True)
    return zc * lax.rsqrt(var + LN_EPS) * g + b


def _adaln_kernel(c_ref, w_ref, b_ref, o_ref):
    c = c_ref[...]
    s = (c * jax.nn.sigmoid(c)).astype(BF16)
    o_ref[0] = _dot(s, w_ref[0].astype(BF16)) + b_ref[0]


def _adaln(cond, ada_w, ada_b):
    n_out = N_ADA * D_MODEL
    rows = cond.shape[0]
    return pl.pallas_call(
        _adaln_kernel,
        grid=(DEPTH, n_out // ADA_TN),
        in_specs=[
            pl.BlockSpec((rows, D_MODEL), lambda l, j: (0, 0)),
            pl.BlockSpec((1, D_MODEL, ADA_TN), lambda l, j: (l, 0, j)),
            pl.BlockSpec((1, 1, ADA_TN), lambda l, j: (l, 0, j)),
        ],
        out_specs=pl.BlockSpec((1, rows, ADA_TN), lambda l, j: (l, 0, j)),
        out_shape=jax.ShapeDtypeStruct((DEPTH, rows, n_out), F32),
        compiler_params=_params("arbitrary", "arbitrary"),
        name="adaln",
    )(cond, ada_w, ada_b.reshape(DEPTH, 1, n_out))


def _ffn_body(x, mod_ref, wi_ref, wo_ref, g_ref, b_ref, o_ref, act_ref):
    n_slots = mod_ref.shape[1]
    shift, scale, gate = mod_ref[0, n_slots - 3], mod_ref[0, n_slots - 2], mod_ref[0, n_slots - 1]
    h = (x * (1.0 + scale) + shift).astype(BF16)
    for j in range(N_FF_CHUNKS):
        lo = j * FF_CHUNK
        a = _dot(h, wi_ref[:, lo:lo + FF_CHUNK])
        u = _dot(h, wi_ref[:, D_FF + lo:D_FF + lo + FF_CHUNK])
        act_ref[:, lo:lo + FF_CHUNK] = (a * jax.nn.sigmoid(a) * u).astype(BF16)
    y = _dot(act_ref[...], wo_ref[...])
    o_ref[...] = _layernorm(ALPHA * x + (0.5 * gate) * y, g_ref[...], b_ref[...])


def _load_weight_bf16(w_hbm, which, dst_ref, stage_ref, sem):
    layer, half = which
    rows = stage_ref.shape[1]
    n_chunks = dst_ref.shape[0] // rows

    def copy(c):
        return pltpu.make_async_copy(w_hbm.at[layer, half, pl.ds(c * rows, rows)], stage_ref.at[c % 2], sem.at[c % 2])

    copy(0).start()
    for c in range(n_chunks):
        if c + 1 < n_chunks:
            copy(c + 1).start()
        copy(c).wait()
        dst_ref[c * rows:(c + 1) * rows, :] = stage_ref[c % 2].astype(BF16)


def _ffn_steps(x_fn, which, mod_ref, wi_hbm, wo_hbm, g_ref, b_ref, o_ref,
               act_ref, wi_ref, wo_ref, stage_in_ref, stage_out_ref, sem_in, sem_out):
    @pl.when(pl.program_id(0) == 0)
    def _():
        _load_weight_bf16(wi_hbm, which, wi_ref, stage_in_ref, sem_in)
        _load_weight_bf16(wo_hbm, which, wo_ref, stage_out_ref, sem_out)

    _ffn_body(x_fn(), mod_ref, wi_ref, wo_ref, g_ref, b_ref, o_ref, act_ref)


def _ffn_kernel(x_ref, *rest, which):
    _ffn_steps(lambda: x_ref[...], which, *rest)


def _ffn_split_kernel(xl_ref, xc_ref, *rest, which):
    _ffn_steps(lambda: jnp.where(pl.program_id(0) < T_LAT // TM_FFN, xl_ref[...], xc_ref[...]), which, *rest)


def _ffn_after_even_kernel(x_ref, yh_ref, hs_ref, gt_ref, wmix_ref, g1_ref, b1_ref, mod_ref, *rest, which):
    def x_fn():
        y_rg = hs_ref[...] * jax.nn.gelu(gt_ref[...], approximate=True)
        y = (_dot(yh_ref[...].astype(BF16), wmix_ref[:HY_WIDTH, :])
             + _dot(y_rg.astype(BF16), wmix_ref[HY_WIDTH:, :]))
        return _layernorm(ALPHA * x_ref[...] + mod_ref[0, 0] * y, g1_ref[...], b1_ref[...])

    _ffn_steps(x_fn, which, mod_ref, *rest)


def _ffn_after_mla_kernel(x_ref, att_ref, wmix_ref, g1_ref, b1_ref, mod_ref, *rest, which):
    def x_fn():
        y = _dot(att_ref[...], wmix_ref[...])
        return _layernorm(ALPHA * x_ref[...] + mod_ref[0, 0] * y, g1_ref[...], b1_ref[...])

    _ffn_steps(x_fn, which, mod_ref, *rest)


def _ffn(body, lead_args, lead_specs, mod, ffn_w, which, g, b, n_rows):
    tm = TM_FFN
    w_in_all, w_out_all = ffn_w
    return pl.pallas_call(
        functools.partial(body, which=which),
        grid=(n_rows // tm,),
        in_specs=[
            *lead_specs,
            pl.BlockSpec((1, mod.shape[1], 1, D_MODEL), lambda i: (_group_of_tile(i, tm), 0, 0, 0)),
            pl.BlockSpec(memory_space=pl.ANY),
            pl.BlockSpec(memory_space=pl.ANY),
            _resident((1, D_MODEL)),
            _resident((1, D_MODEL)),
        ],
        out_specs=pl.BlockSpec((tm, D_MODEL), lambda i: (i, 0)),
        out_shape=jax.ShapeDtypeStruct((n_rows, D_MODEL), F32),
        scratch_shapes=[
            pltpu.VMEM((tm, D_FF), BF16),
            pltpu.VMEM((D_MODEL, 2 * D_FF), BF16),
            pltpu.VMEM((D_FF, D_MODEL), BF16),
            pltpu.VMEM((2, D_MODEL // FFN_WEIGHT_CHUNKS, 2 * D_FF), F32),
            pltpu.VMEM((2, D_FF // FFN_WEIGHT_CHUNKS, D_MODEL), F32),
            pltpu.SemaphoreType.DMA((2,)),
            pltpu.SemaphoreType.DMA((2,)),
        ],
        compiler_params=_params("arbitrary"),
        name="half_ffn",
    )(*lead_args, mod, w_in_all, w_out_all, g.reshape(1, D_MODEL), b.reshape(1, D_MODEL))


def _row_spec(width, tm=TM_FFN):
    return pl.BlockSpec((tm, width), lambda i: (i, 0))


def _ffn_plain(x, *args):
    return _ffn(_ffn_kernel, (x,), [_row_spec(D_MODEL)], *args)


def _ffn_split(x_lat, x_ctx, *args):
    n_lat = T_LAT // TM_FFN
    specs = [pl.BlockSpec((TM_FFN, D_MODEL), lambda i: (jnp.minimum(i, n_lat - 1), 0)),
             pl.BlockSpec((TM_FFN, D_MODEL), lambda i: (jnp.maximum(i - n_lat, 0), 0))]
    return _ffn(_ffn_split_kernel, (x_lat, x_ctx), specs, *args)


def _ffn_after_even(x, y_hy, h_sum, gate_in, w_mix, g1, b1, *args):
    lead = (x, y_hy, h_sum, gate_in, w_mix.astype(BF16), g1.reshape(1, D_MODEL), b1.reshape(1, D_MODEL))
    specs = [_row_spec(D_MODEL), _row_spec(HY_WIDTH), _row_spec(RG_WIDTH), _row_spec(RG_WIDTH),
             _resident((HY_WIDTH + RG_WIDTH, D_MODEL)), _resident((1, D_MODEL)), _resident((1, D_MODEL))]
    return _ffn(_ffn_after_even_kernel, lead, specs, *args)


def _ffn_after_mla(x, att, w_mix, g1, b1, *args):
    lead = (x, att, w_mix.astype(BF16), g1.reshape(1, D_MODEL), b1.reshape(1, D_MODEL))
    specs = [_row_spec(D_MODEL), _row_spec(MLA_HEADS * MLA_V),
             _resident((MLA_HEADS * MLA_V, D_MODEL)), _resident((1, D_MODEL)), _resident((1, D_MODEL))]
    return _ffn(_ffn_after_mla_kernel, lead, specs, *args)


def _evproj_kernel(x_ref, mod_ref, w_ref, hy_ref, rx_ref, gt_ref):
    shift, scale = mod_ref[0, 0], mod_ref[0, 1]
    h = (x_ref[...] * (1.0 + scale) + shift).astype(BF16)
    n_hy = 3 * HY_WIDTH
    hy_ref[...] = _dot(h, w_ref[:, :n_hy])
    rx_ref[...] = _dot(h, w_ref[:, n_hy:n_hy + RG_WIDTH])
    gt_ref[...] = _dot(h, w_ref[:, n_hy + RG_WIDTH:])


def _evproj(x, mod, w_in):
    tm = TM_EVPROJ
    row = lambda i: (i, 0)
    return pl.pallas_call(
        _evproj_kernel,
        grid=(T_ALL // tm,),
        in_specs=[
            pl.BlockSpec((tm, D_MODEL), row),
            pl.BlockSpec((1, 2, 1, D_MODEL), lambda i: (_group_of_tile(i, tm), 0, 0, 0)),
            _resident((D_MODEL, EV_IN)),
        ],
        out_specs=[
            pl.BlockSpec((tm, 3 * HY_WIDTH), row),
            pl.BlockSpec((tm, RG_WIDTH), row),
            pl.BlockSpec((tm, RG_WIDTH), row),
        ],
        out_shape=[
            jax.ShapeDtypeStruct((T_ALL, 3 * HY_WIDTH), F32),
            jax.ShapeDtypeStruct((T_ALL, RG_WIDTH), F32),
            jax.ShapeDtypeStruct((T_ALL, RG_WIDTH), F32),
        ],
        compiler_params=_params("arbitrary"),
        name="even_in_proj",
    )(x, mod, w_in.astype(BF16))


def _block_diag_tiles(w):
    per = RG_CT // RG_BLOCK_DIM
    w = w.reshape(RG_WIDTH // RG_CT, per, RG_BLOCK_DIM, RG_BLOCK_DIM)
    eye = jnp.eye(per, dtype=w.dtype)
    return jnp.einsum('cpde,pq->cpdqe', w, eye).reshape(RG_WIDTH // RG_CT, RG_CT, RG_CT)


def _rg_kernel(rx_ref, cw_ref, cb_ref, w_ref, bias_ref, lam_ref, out_ref,
                padl_ref, padc_ref, xt_f_ref, xt_b_ref, coef_a_ref, coef_b_ref, hbuf_f_ref, hbuf_b_ref):
    neg_lam = -lam_ref[...]
    softplus = jnp.maximum(neg_lam, 0.0) + jnp.log1p(jnp.exp(-jnp.abs(neg_lam)))
    neg_log_a_scale = RG_C * softplus
    a_exp2_scale = -LOG2_E * neg_log_a_scale
    cb = cb_ref[...]
    xts = (xt_f_ref, xt_b_ref)
    hbuf = (hbuf_f_ref, hbuf_b_ref)
    slab_rows = RG_TC + 2 * RG_HALO

    def step_rows(s):
        return slice(s * SUBLANES, (s + 1) * SUBLANES)

    halo = jnp.zeros((BATCH, RG_HALO, RG_CT), F32)
    for pad_ref, length in ((padl_ref, SEQ), (padc_ref, CTX_LEN)):
        pad_ref[:, 0:RG_HALO, :] = halo
        pad_ref[:, RG_HALO + length:, :] = halo
    padl_ref[:, RG_HALO:RG_HALO + SEQ, :] = rx_ref[0:BATCH]
    for b in range(BATCH):
        padc_ref[b, RG_HALO:RG_HALO + CTX_LEN, :] = rx_ref[BATCH, b * CTX_LEN:(b + 1) * CTX_LEN, :]

    def add_out(latent, t_start, d, first_touch):
        t_start = pl.multiple_of(t_start, SUBLANES)
        for b in range(BATCH):
            rows = hbuf[d][pl.ds(b, RG_TC, stride=SUBLANES), :]
            where = (b, pl.ds(t_start, RG_TC)) if latent else (BATCH, pl.ds(b * CTX_LEN + t_start, RG_TC))
            if first_touch:
                out_ref[where[0], where[1], :] = rows
            else:
                out_ref[where[0], where[1], :] += rows

    def sigmoid(z):
        return 1.0 / (1.0 + jnp.exp2(z * (-LOG2_E)))

    def coeffs(latent, length, t0, d, dst_ref):
        t0 = pl.multiple_of(jnp.asarray(t0, jnp.int32), SUBLANES)
        pad_ref = padl_ref if latent else padc_ref
        xt = xts[d]
        for b in range(BATCH):
            xt[pl.ds(b, slab_rows, stride=SUBLANES), :] = pad_ref[b, pl.ds(t0, slab_rows), :]
        x2 = cb
        for k in range(RG_CONV):
            start = (RG_HALO - RG_PAD_L + k) * SUBLANES
            x2 = x2 + xt[start:start + RG_TC * SUBLANES, :] * cw_ref[k:k + 1, :]
        xb = x2.astype(BF16)
        r = sigmoid(_dot(xb, w_ref[0, 2 * d]) + bias_ref[2 * d:2 * d + 1, :])
        i = sigmoid(_dot(xb, w_ref[0, 2 * d + 1]) + bias_ref[2 * d + 1:2 * d + 2, :])
        a = jnp.exp2(r * a_exp2_scale[d:d + 1, :])
        b = jnp.sqrt(jnp.tanh(r * neg_log_a_scale[d:d + 1, :]) * (a * a + 1.0)) * (i * x2)
        dst_ref[d, 0] = a
        dst_ref[d, 1] = b

    def scan(latent, first_touch, tf, tb, src_ref, hf, hb):
        tf = jnp.asarray(tf, jnp.int32)
        tb = jnp.asarray(tb, jnp.int32)
        for s in range(RG_TC):
            rows = step_rows(s)
            hf = src_ref[0, 0, rows, :] * hf + src_ref[0, 1, rows, :]
            hbuf_f_ref[rows, :] = hf
            rows = step_rows(RG_TC - 1 - s)
            hb = src_ref[1, 0, rows, :] * hb + src_ref[1, 1, rows, :]
            hbuf_b_ref[rows, :] = hb
        add_out(latent, tf, 0, first_touch)
        add_out(latent, tb, 1, first_touch)
        return hf, hb

    def sweep(latent, length, h_fwd, h_bwd):
        n_chunks = length // RG_TC
        assert n_chunks % 4 == 0
        last = n_chunks - 1

        def both(j, dst_ref):
            coeffs(latent, length, j * RG_TC, 0, dst_ref)
            coeffs(latent, length, (last - j) * RG_TC, 1, dst_ref)

        def body(first_touch, i, carry):
            j = 2 * i
            both(j + 1, coef_b_ref)
            carry = scan(latent, first_touch, j * RG_TC, (last - j) * RG_TC, coef_a_ref, *carry)
            both(jnp.minimum(j + 2, last), coef_a_ref)
            return scan(latent, first_touch, (j + 1) * RG_TC, (last - j - 1) * RG_TC, coef_b_ref, *carry)

        both(0, coef_a_ref)
        half = n_chunks // 4
        carry = lax.fori_loop(0, half, functools.partial(body, True), (h_fwd, h_bwd))
        return lax.fori_loop(half, 2 * half, functools.partial(body, False), carry)

    zero = jnp.zeros((SUBLANES, RG_CT), F32)
    hf, hb = sweep(False, CTX_LEN, zero, zero)
    sweep(True, SEQ, hf, hb)


def _rglru(rx, conv_w, conv_b, a_w, a_b, x_w, x_b, lam):
    assert T_CTX == SEQ, "the context rows must form exactly one latent-sized row block"
    w = jnp.stack([_block_diag_tiles(a_w[0]), _block_diag_tiles(x_w[0]),
                   _block_diag_tiles(a_w[1]), _block_diag_tiles(x_w[1])], axis=1).astype(BF16)
    bias = jnp.stack([a_b[0], x_b[0], a_b[1], x_b[1]], axis=0)
    n_ct = RG_WIDTH // RG_CT
    n_blocks = BATCH + 1
    chan = lambda c: (0, 0, c)
    lane = lambda c: (0, c)
    coef = pltpu.VMEM((2, 2, RG_TC * BATCH, RG_CT), F32)
    slab = pltpu.VMEM(((RG_TC + 2 * RG_HALO) * BATCH, RG_CT), F32)
    chunk = pltpu.VMEM((RG_TC * BATCH, RG_CT), F32)
    out = pl.pallas_call(
        _rg_kernel,
        grid=(n_ct,),
        in_specs=[
            pl.BlockSpec((n_blocks, SEQ, RG_CT), chan),
            pl.BlockSpec((4, RG_CT), lane),
            pl.BlockSpec((1, RG_CT), lane),
            pl.BlockSpec((1, 4, RG_CT, RG_CT), lambda c: (c, 0, 0, 0)),
            pl.BlockSpec((4, RG_CT), lane),
            pl.BlockSpec((2, RG_CT), lane),
        ],
        out_specs=pl.BlockSpec((n_blocks, SEQ, RG_CT), chan),
        out_shape=jax.ShapeDtypeStruct((n_blocks, SEQ, RG_WIDTH), F32),
        scratch_shapes=[
            pltpu.VMEM((BATCH, SEQ + 2 * RG_HALO, RG_CT), F32),
            pltpu.VMEM((BATCH, CTX_LEN + 2 * RG_HALO, RG_CT), F32),
            slab, slab,
            coef, coef,
            chunk, chunk,
        ],
        compiler_params=_params("arbitrary"),
        name="rglru_scan",
    )(rx.reshape(n_blocks, SEQ, RG_WIDTH), conv_w, conv_b.reshape(1, RG_WIDTH), w, bias, lam)
    return out.reshape(T_ALL, RG_WIDTH)


def _dft_tables(length):
    step = 1 << (length.bit_length() // 2)
    s = np.arange(length, dtype=np.int64)

    def thin(kvec):
        ang = ((kvec[:, None] * s[None, :]) % (2 * length)) * (math.pi / length)
        return (jnp.asarray(np.cos(ang)[:, None, :], dtype=F32),
                jnp.asarray(np.sin(ang)[:, None, :], dtype=F32))

    c_hi, s_hi = thin(np.arange(0, length, step, dtype=np.int64))
    c_lo, s_lo = thin(np.arange(step, dtype=np.int64))
    c_lo, s_lo = c_lo.reshape(1, step, length), s_lo.reshape(1, step, length)
    fc = (c_hi * c_lo - s_hi * s_lo).reshape(length, length)
    fs = (s_hi * c_lo + c_hi * s_lo).reshape(length, length)
    return fc.astype(BF16), fs.astype(BF16)


def _filter_features(length):
    f32 = np.float32
    pos = np.arange(length, dtype=f32)
    t = pos / f32(length)
    bands = np.linspace(1e-4, HY_BANDS - 1, HY_BANDS, dtype=f32)
    ang = (f32(2.0 * math.pi) * pos / f32(length))[:, None] * bands[None, :]
    feats = np.concatenate([t[:, None], np.cos(ang), -np.sin(ang)], axis=-1).astype(f32)
    feats = np.pad(feats, ((0, 0), (0, LANES - HY_EMB)))
    deltas = np.abs(np.linspace(HY_MIN_DECAY, HY_MAX_DECAY, HY_WIDTH, dtype=f32))
    window = (np.exp(-t[:, None] * deltas[None, :]) + f32(HY_SHIFT)).astype(f32)
    return jnp.asarray(feats), jnp.asarray(window)


def _filter_hidden(feat_ref, w1_ref, b1_ref, w2_ref, b2_ref, sf_ref, hid_ref):
    h = jnp.sin(sf_ref[0:1, :] * (_dot(feat_ref[...].astype(BF16), w1_ref[...]) + b1_ref[...]))
    h = jnp.sin(sf_ref[1:2, :] * (_dot(h.astype(BF16), w2_ref[...]) + b2_ref[...]))
    hid_ref[...] = h.astype(BF16)


def _filter_spectrum(hid_ref, win_ref, w3f_ref, w3b_ref, fc_ref, fs_ref, kc_ref, ks_ref, kn_ref, length):
    n = 2 * length
    hb16 = hid_ref[...]
    win = win_ref[...]
    row = lax.broadcasted_iota(jnp.int32, (length, HY_CT), 0)
    h_fwd = _dot(hb16, w3f_ref[...]) * win
    h_bwd0 = jnp.where(row == 0, 0.0, _dot(hb16, w3b_ref[...]) * win)
    h_sum = h_fwd + h_bwd0
    h_dif = h_fwd - h_bwd0
    weight = jnp.where(row == 0, 1.0 / n, 2.0 / n)
    kc_ref[...] = _dot(fc_ref[...], h_sum.astype(BF16)) * weight
    ks_ref[...] = _dot(fs_ref[...], h_dif.astype(BF16)) * weight
    sign = jnp.where((row & 1) == 0, 1.0, -1.0)
    kn_ref[...] = jnp.sum(h_sum * sign, axis=0, keepdims=True) * (1.0 / n)


def _filter_params(w1, b1, w2, b2, w3, sin_freq):
    hp = LANES - HY_FILT_HIDDEN
    w3p = jnp.pad(w3, ((0, hp), (0, 0))).astype(BF16)
    return (jnp.pad(w1, ((0, LANES - HY_EMB), (0, hp))).astype(BF16), jnp.pad(b1, (0, hp)).reshape(1, LANES),
            jnp.pad(w2, ((0, hp), (0, hp))).astype(BF16), jnp.pad(b2, (0, hp)).reshape(1, LANES),
            w3p[:, :HY_WIDTH], w3p[:, HY_WIDTH:], jnp.pad(sin_freq, ((0, 0), (0, hp))))


def _hyena_sequence(raw, cw_ref, cb_ref, bias_ref, fc_ref, fs_ref, kc_ref, ks_ref, kn_ref, length):
    row = lax.broadcasted_iota(jnp.int32, (length, HY_CT), 0)

    def short_conv(x, part):
        prev = jnp.where(row == 0, 0.0, pltpu.roll(x, 1, 0))
        nxt = jnp.where(row == length - 1, 0.0, pltpu.roll(x, length - 1, 0))
        return (prev * cw_ref[part, 0:1, :] + x * cw_ref[part, 1:2, :] + nxt * cw_ref[part, 2:3, :]
                + cb_ref[part:part + 1, :])

    x0 = short_conv(raw[0], 0)
    x1 = short_conv(raw[1], 1)
    v = short_conv(raw[2], 2)
    vx = v * x1
    vb = vx.astype(BF16)
    ft = min(length, HY_FREQ_TILE)
    y = None
    for f0 in range(0, length, ft):
        spec_c = _dot(fc_ref[f0:f0 + ft, :], vb)
        spec_s = _dot(fs_ref[f0:f0 + ft, :], vb)
        kc = kc_ref[f0:f0 + ft, :]
        ks = ks_ref[f0:f0 + ft, :]
        p = (spec_c * kc - spec_s * ks).astype(BF16)
        q = (spec_c * ks + spec_s * kc).astype(BF16)
        part = _dot(fc_ref[:, f0:f0 + ft], p) + _dot(fs_ref[:, f0:f0 + ft], q)
        y = part if y is None else y + part
    sign = jnp.where((row & 1) == 0, 1.0, -1.0)
    nyquist = jnp.sum(vx * sign, axis=0, keepdims=True) * kn_ref[...]
    return x0 * (y + sign * nyquist + vx * bias_ref[...])


def _hyena_kernel(x0_ref, x1_ref, v_ref, cw_ref, cb_ref, bias_ref,
                  featl_ref, winl_ref, featc_ref, winc_ref,
                  w1_ref, b1_ref, w2_ref, b2_ref, w3f_ref, w3b_ref, sf_ref,
                  fcl_ref, fsl_ref, fcc_ref, fsc_ref, o_ref,
                  hidl_ref, hidc_ref, kcl_ref, ksl_ref, knl_ref, kcc_ref, ksc_ref, knc_ref):
    step = pl.program_id(1)
    common = (cw_ref, cb_ref, bias_ref)

    @pl.when(jnp.logical_and(pl.program_id(0) == 0, step == 0))
    def _():
        sine_layers = (w1_ref, b1_ref, w2_ref, b2_ref, sf_ref)
        _filter_hidden(featl_ref, *sine_layers, hidl_ref)
        _filter_hidden(featc_ref, *sine_layers, hidc_ref)

    @pl.when(step == 0)
    def _():
        _filter_spectrum(hidl_ref, winl_ref, w3f_ref, w3b_ref, fcl_ref, fsl_ref, kcl_ref, ksl_ref, knl_ref, SEQ)
        _filter_spectrum(hidc_ref, winc_ref, w3f_ref, w3b_ref, fcc_ref, fsc_ref, kcc_ref, ksc_ref, knc_ref, CTX_LEN)

    @pl.when(step < BATCH)
    def _():
        raw = (x0_ref[...], x1_ref[...], v_ref[...])
        o_ref[...] = _hyena_sequence(raw, *common, fcl_ref, fsl_ref, kcl_ref, ksl_ref, knl_ref, SEQ)

    @pl.when(step == BATCH)
    def _():
        for r in range(BATCH):
            rows = slice(r * CTX_LEN, (r + 1) * CTX_LEN)
            raw = (x0_ref[rows, :], x1_ref[rows, :], v_ref[rows, :])
            o_ref[rows, :] = _hyena_sequence(raw, *common, fcc_ref, fsc_ref, kcc_ref, ksc_ref, knc_ref, CTX_LEN)


def _hyena(hy, filter_params, conv_w, conv_b, bias):
    assert T_CTX == SEQ, "the context rows must form exactly one latent-sized row block"
    n_ct = HY_WIDTH // HY_CT
    cw = conv_w.reshape(3, 3, HY_WIDTH).transpose(1, 0, 2)
    cb = conv_b.reshape(3, HY_WIDTH)
    feat_l, win_l = _filter_features(SEQ)
    feat_c, win_c = _filter_features(CTX_LEN)
    tab_l = _dft_tables(SEQ)
    tab_c = _dft_tables(CTX_LEN)

    def part_spec(part):
        return pl.BlockSpec((SEQ, HY_CT), lambda c, b: (b, part * n_ct + c))

    chan = lambda c, b: (0, c)
    once_per_tile = lambda rows: pl.BlockSpec((rows, HY_CT), chan, pipeline_mode=pl.Buffered(1))
    spectrum = lambda rows: pltpu.VMEM((rows, HY_CT), F32)
    return pl.pallas_call(
        _hyena_kernel,
        grid=(n_ct, BATCH + 1),
        in_specs=[
            part_spec(0), part_spec(1), part_spec(2),
            pl.BlockSpec((3, 3, HY_CT), lambda c, b: (0, 0, c)),
            pl.BlockSpec((3, HY_CT), chan),
            pl.BlockSpec((1, HY_CT), chan),
            _resident((SEQ, LANES)), once_per_tile(SEQ),
            _resident((CTX_LEN, LANES)), once_per_tile(CTX_LEN),
            _resident((LANES, LANES)), _resident((1, LANES)), _resident((LANES, LANES)), _resident((1, LANES)),
            once_per_tile(LANES), once_per_tile(LANES), _resident((2, LANES)),
            _resident((SEQ, SEQ)), _resident((SEQ, SEQ)),
            _resident((CTX_LEN, CTX_LEN)), _resident((CTX_LEN, CTX_LEN)),
        ],
        out_specs=pl.BlockSpec((SEQ, HY_CT), lambda c, b: (b, c)),
        out_shape=jax.ShapeDtypeStruct((T_ALL, HY_WIDTH), F32),
        scratch_shapes=[pltpu.VMEM((SEQ, LANES), BF16), pltpu.VMEM((CTX_LEN, LANES), BF16),
                        spectrum(SEQ), spectrum(SEQ), spectrum(1), spectrum(CTX_LEN), spectrum(CTX_LEN), spectrum(1)],
        compiler_params=_params("arbitrary", "arbitrary"),
        name="hyena_conv",
    )(hy, hy, hy, cw, cb, bias.reshape(1, HY_WIDTH), feat_l, win_l, feat_c, win_c, *filter_params, *tab_l, *tab_c)


def _rope_partner(n):
    idx = np.arange(n)
    return np.where((idx % (2 * ROPE_AXIS_PAIRS)) < ROPE_AXIS_PAIRS, idx + ROPE_AXIS_PAIRS, idx - ROPE_AXIS_PAIRS)


def _slot_lanes():
    d = np.arange(MLA_ROPE)
    axis, second, pair = d // (2 * ROPE_AXIS_PAIRS), (d // ROPE_AXIS_PAIRS) % 2, d % ROPE_AXIS_PAIRS
    rope_lane = axis * ROPE_AXIS_PAIRS + pair + (HEAD_SLOT // 2) * second
    free = np.setdiff1d(np.arange(HEAD_SLOT), rope_lane)
    return rope_lane, free[:MLA_NOPE]


def _to_slot(nope, rope):
    rope_lane, nope_lane = _slot_lanes()
    source = [(None, 0)] * HEAD_SLOT
    for arr, lanes in ((nope, nope_lane), (rope, rope_lane)):
        if arr is not None:
            for idx, lane in enumerate(lanes):
                source[lane] = (arr, idx)
    like = nope if nope is not None else rope
    pieces, lane = [], 0
    while lane < HEAD_SLOT:
        arr, start = source[lane]
        run = 1
        while (lane + run < HEAD_SLOT and source[lane + run][0] is arr
               and (arr is None or source[lane + run][1] == start + run)):
            run += 1
        pieces.append(jnp.zeros(like.shape[:-1] + (run,), like.dtype) if arr is None else arr[..., start:start + run])
        lane += run
    return jnp.concatenate(pieces, axis=-1)


def _rope_tables():
    f32 = np.float32
    rope_lane, nope_lane = _slot_lanes()
    rows = np.repeat(np.arange(SEQ // GRID_W, dtype=f32), GRID_W)
    cols = np.tile(np.arange(GRID_W, dtype=f32), SEQ // GRID_W)
    inv_freq = (f32(ROPE_BASE) ** (-np.arange(ROPE_AXIS_PAIRS, dtype=f32) / f32(ROPE_AXIS_PAIRS))).astype(f32)
    ang_r = rows[:, None] * inv_freq
    ang_c = cols[:, None] * inv_freq
    cos32 = np.concatenate([np.cos(ang_r)] * 2 + [np.cos(ang_c)] * 2, axis=1)
    sin32 = np.concatenate([-np.sin(ang_r), np.sin(ang_r), -np.sin(ang_c), np.sin(ang_c)], axis=1)

    def table(n_rows, rope_vals, nope_val):
        t = np.zeros((n_rows, HEAD_SLOT), f32)
        t[:, rope_lane] = rope_vals
        t[:, nope_lane] = nope_val
        return t

    q_cos = table(SEQ, cos32, 1.0) * f32(Q_SCALE)
    q_sin = table(SEQ, sin32, 0.0) * f32(Q_SCALE)
    k_cos = table(SEQ, cos32, 0.0)
    k_sin = table(SEQ, sin32, 0.0)
    ident = table(TM_PROJ, 1.0, 0.0)
    zero = np.zeros((TM_PROJ, HEAD_SLOT), f32)
    tables = ((q_cos, zero), (q_sin, zero), (k_cos, ident), (k_sin, zero))
    return tuple(jnp.asarray(np.concatenate(t, 0).astype(f32)) for t in tables)


def _value_ones():
    assert HEAD_SLOT == 2 * MLA_V
    lane = np.arange(MLA_HEADS * HEAD_SLOT)
    head_is_even = (lane // HEAD_SLOT) % 2 == 0
    upper_half = lane % HEAD_SLOT >= MLA_V
    return jnp.asarray((head_is_even == upper_half).astype(np.float32).reshape(1, -1))


def _mla_weights(w_in, w_q_up, w_kv_up):
    w_q = w_in[:, :MLA_Q_LORA]
    w_kv = w_in[:, MLA_Q_LORA:MLA_Q_LORA + MLA_KV_LORA]
    w_kr = w_in[:, MLA_Q_LORA + MLA_KV_LORA:]
    w_kr2 = jnp.concatenate([_to_slot(None, w_kr), _to_slot(None, w_kr[:, _rope_partner(MLA_ROPE)])], axis=1)

    qh = w_q_up.reshape(MLA_Q_LORA, MLA_HEADS, MLA_NOPE + MLA_ROPE)
    wq_slot = _to_slot(qh[..., :MLA_NOPE], qh[..., MLA_NOPE:]).reshape(MLA_Q_LORA, MLA_HEADS * HEAD_SLOT)

    kvh = w_kv_up.reshape(MLA_KV_LORA, MLA_HEADS, MLA_NOPE + MLA_V)
    wk_slot = _to_slot(kvh[..., :MLA_NOPE], None).reshape(MLA_KV_LORA, MLA_HEADS * HEAD_SLOT)
    even = (jnp.arange(MLA_HEADS) % 2 == 0).astype(F32)[None, :, None]
    wv = kvh[..., MLA_NOPE:]
    wv_slot = jnp.concatenate([wv * even, wv * (1.0 - even)], axis=-1).reshape(MLA_KV_LORA, MLA_HEADS * HEAD_SLOT)
    return [w.astype(BF16) for w in (w_q, w_kv, w_kr2, wq_slot, wk_slot, wv_slot)]


def _mlaproj_kernel(x_ref, mod_ref, wq_ref, wkv_ref, wkr_ref, wqs_ref, wks_ref, wv_ref,
                    vone_ref, qg_ref, kvg_ref, qcos_ref, qsin_ref, kcos_ref, ksin_ref,
                    q_ref, k_ref, v_ref):
    shift, scale = mod_ref[0, 0], mod_ref[0, 1]
    h = (x_ref[...] * (1.0 + scale) + shift).astype(BF16)

    def rmsnorm(y, g):
        return (y * lax.rsqrt(jnp.mean(y * y, axis=-1, keepdims=True) + RMS_EPS) * g).astype(BF16)

    qn = rmsnorm(_dot(h, wq_ref[...]), qg_ref[...])
    kvn = rmsnorm(_dot(h, wkv_ref[...]), kvg_ref[...])
    kr2 = _dot(h, wkr_ref[...])
    k_rope = kr2[:, :HEAD_SLOT] * kcos_ref[...] + kr2[:, HEAD_SLOT:] * ksin_ref[...]
    k_nope = _dot(kvn, wks_ref[...])
    v_ref[...] = (_dot(kvn, wv_ref[...]) + vone_ref[...]).astype(BF16)
    q_all = _dot(qn, wqs_ref[...])
    q_cos = qcos_ref[...]
    q_sin = qsin_ref[...]
    for hd in range(MLA_HEADS):
        sl = slice(hd * HEAD_SLOT, (hd + 1) * HEAD_SLOT)
        qh = q_all[:, sl]
        rotated = qh * q_cos + pltpu.roll(qh, HEAD_SLOT // 2, 1) * q_sin
        q_ref[:, sl] = rotated.astype(BF16)
        k_ref[:, sl] = (k_nope[:, sl] + k_rope).astype(BF16)


def _mlaproj(x, mod, w_in, q_norm, kv_norm, w_q_up, w_kv_up):
    tm = TM_PROJ
    weights = _mla_weights(w_in, w_q_up, w_kv_up)
    tables = _rope_tables()
    row = lambda i: (i, 0)
    tab = lambda i: (jnp.where(i < T_LAT // tm, i % (SEQ // tm), SEQ // tm), 0)
    wide = MLA_HEADS * HEAD_SLOT
    v_one = _value_ones()
    return pl.pallas_call(
        _mlaproj_kernel,
        grid=(T_ALL // tm,),
        in_specs=[
            pl.BlockSpec((tm, D_MODEL), row),
            pl.BlockSpec((1, 2, 1, D_MODEL), lambda i: (_group_of_tile(i, tm), 0, 0, 0)),
            *[_resident(w.shape) for w in weights],
            _resident((1, wide)),
            _resident((1, MLA_Q_LORA)),
            _resident((1, MLA_KV_LORA)),
            *[pl.BlockSpec((tm, HEAD_SLOT), tab) for _ in tables],
        ],
        out_specs=[pl.BlockSpec((tm, wide), row)] * 3,
        out_shape=[jax.ShapeDtypeStruct((T_ALL, wide), BF16)] * 3,
        compiler_params=_params("arbitrary"),
        name="mla_proj",
    )(x, mod, *weights, v_one, q_norm.reshape(1, MLA_Q_LORA), kv_norm.reshape(1, MLA_KV_LORA), *tables)


def _attn_kernel(q_ref, kl_ref, kc_ref, vl_ref, vc_ref, o_ref):
    low = lax.broadcasted_iota(jnp.int32, (TQ, HEAD_SLOT), 1) < MLA_V
    pair_out = []
    for hd in range(ATTN_HEADS):
        sl = slice(hd * HEAD_SLOT, (hd + 1) * HEAD_SLOT)
        q = q_ref[:, sl]
        s_lat = _dot_nt(q, kl_ref[:, sl])
        s_ctx = _dot_nt(q, kc_ref[:, sl])
        m = jnp.maximum(jnp.max(s_lat, axis=-1, keepdims=True), jnp.max(s_ctx, axis=-1, keepdims=True))
        p_lat = jnp.exp2(s_lat - m).astype(BF16)
        p_ctx = jnp.exp2(s_ctx - m).astype(BF16)
        pair_out.append(_dot(p_lat, vl_ref[:, sl]) + _dot(p_ctx, vc_ref[:, sl]))
        if hd % 2 == 1:
            even, odd = pair_out
            pair_out = []
            numer = jnp.where(low, even, odd)
            denom = pltpu.roll(jnp.where(low, odd, even), MLA_V, 1)
            pair = hd // 2
            o_ref[:, pair * HEAD_SLOT:(pair + 1) * HEAD_SLOT] = (numer / denom).astype(BF16)


def _attention(q, k, v):
    n_q = SEQ // TQ
    n_groups = MLA_HEADS // ATTN_HEADS
    ctx_block0 = T_LAT // CTX_LEN
    wide = ATTN_HEADS * HEAD_SLOT
    lat = lambda b, h, i: (b, h)
    ctx = lambda b, h, i: (ctx_block0 + b, h)
    return pl.pallas_call(
        _attn_kernel,
        grid=(BATCH, n_groups, n_q),
        in_specs=[
            pl.BlockSpec((TQ, wide), lambda b, h, i: (b * n_q + i, h)),
            pl.BlockSpec((SEQ, wide), lat),
            pl.BlockSpec((CTX_LEN, wide), ctx),
            pl.BlockSpec((SEQ, wide), lat),
            pl.BlockSpec((CTX_LEN, wide), ctx),
        ],
        out_specs=pl.BlockSpec((TQ, ATTN_HEADS * MLA_V), lambda b, h, i: (b * n_q + i, h)),
        out_shape=jax.ShapeDtypeStruct((T_LAT, MLA_HEADS * MLA_V), BF16),
        compiler_params=_params("arbitrary", "arbitrary", "arbitrary"),
        name="mla_attention",
    )(q, k, k, v, v)


def kernel(x, c, ctx, c_ctx, ada_w, ada_b, ln_g, ln_b, ffn_w_in, ffn_w_out, ev_w_in, ev_w_out, hy_conv_w, hy_conv_b, hy_filt_w1, hy_filt_b1, hy_filt_w2, hy_filt_b2, hy_filt_w3, hy_sin_freq, hy_bias, rg_conv_w, rg_conv_b, rg_a_w, rg_a_b, rg_x_w, rg_x_b, rg_lambda, mla_w_in, mla_q_norm, mla_kv_norm, mla_w_q_up, mla_w_kv_up, mla_w_out):
    assert x.shape == (BATCH, SEQ, D_MODEL) and ctx.shape == (BATCH, CTX_LEN, D_MODEL)
    cond = jnp.concatenate([c, c_ctx[None], jnp.zeros((2 * SUBLANES - N_GROUPS, D_MODEL), F32)], axis=0)
    mods = _adaln(cond, ada_w, ada_b)[:, :N_GROUPS].reshape(DEPTH, N_GROUPS, N_ADA, 1, D_MODEL)
    ffn_w = (ffn_w_in, ffn_w_out)

    m = mods[0]
    xs = _ffn_split(x.reshape(T_LAT, D_MODEL), ctx.reshape(T_CTX, D_MODEL),
                    m[:, 0:3], ffn_w, (0, 0), ln_g[0, 0], ln_b[0, 0], T_ALL)
    hy, rx, gate_in = _evproj(xs, m[:, 3:5], ev_w_in[0])
    h_sum = _rglru(rx, rg_conv_w[0], rg_conv_b[0], rg_a_w[0], rg_a_b[0], rg_x_w[0], rg_x_b[0], rg_lambda[0])
    filt = _filter_params(hy_filt_w1[0], hy_filt_b1[0], hy_filt_w2[0], hy_filt_b2[0], hy_filt_w3[0], hy_sin_freq[0])
    y_hy = _hyena(hy, filt, hy_conv_w[0], hy_conv_b[0], hy_bias[0])
    xs = _ffn_after_even(xs, y_hy, h_sum, gate_in, ev_w_out[0], ln_g[0, 1], ln_b[0, 1],
                         m[:, 5:9], ffn_w, (0, 1), ln_g[0, 2], ln_b[0, 2], T_ALL)

    m = mods[1]
    xs = _ffn_plain(xs, m[:, 0:3], ffn_w, (1, 0), ln_g[1, 0], ln_b[1, 0], T_ALL)
    q, k, v = _mlaproj(xs, m[:, 3:5], mla_w_in[0], mla_q_norm[0], mla_kv_norm[0], mla_w_q_up[0], mla_w_kv_up[0])
    att = _attention(q, k, v)
    xl = _ffn_after_mla(xs, att, mla_w_out[0], ln_g[1, 1], ln_b[1, 1],
                        m[:, 5:9], ffn_w, (1, 1), ln_g[1, 2], ln_b[1, 2], T_LAT)
    return xl.reshape(BATCH, SEQ, D_MODEL)
```

```python
import functools
import math

import jax
import jax.numpy as jnp
import numpy as np
from jax import lax
from jax.experimental import pallas as pl
from jax.experimental.pallas import tpu as pltpu

F32 = jnp.float32
BF16 = jnp.bfloat16

D_MODEL = 1024
BATCH = 8
SEQ = 2048
DEPTH = 2
CTX_LEN = 256
GRID_W = 64
N_ADA = 9
D_FF = 2816

HY_WIDTH = 512
RG_WIDTH = 512
EV_IN = 3 * HY_WIDTH + 2 * RG_WIDTH
HY_EMB = 33
HY_BANDS = (HY_EMB - 1) // 2
HY_FILT_HIDDEN = 64
HY_TARGET = 1e-2
HY_MAX_DECAY = math.log(HY_TARGET) / 0.3
HY_MIN_DECAY = math.log(HY_TARGET) / 1.5
HY_SHIFT = 0.05
RG_BLOCKS = 8
RG_BLOCK_DIM = RG_WIDTH // RG_BLOCKS
RG_C = 8.0
RG_CONV = 4
RG_PAD_L = 2

MLA_HEADS = 16
MLA_Q_LORA = 768
MLA_KV_LORA = 256
MLA_NOPE = 64
MLA_ROPE = 32
MLA_V = 64
MLA_SCALE = (MLA_NOPE + MLA_ROPE) ** -0.5
ROPE_AXIS_PAIRS = MLA_ROPE // 4
ROPE_BASE = 10000.0

LOG2_E = math.log2(math.e)
ALPHA = (2.0 * DEPTH) ** 0.25
LN_EPS = 1e-6
RMS_EPS = 1e-6

T_LAT = BATCH * SEQ
T_CTX = BATCH * CTX_LEN
T_ALL = T_LAT + T_CTX
N_GROUPS = BATCH + 1

LANES = 128
SUBLANES = 8
VMEM_LIMIT = 56 * 1024 * 1024

TM_FFN = 512
FF_CHUNK = 256
N_FF_CHUNKS = D_FF // FF_CHUNK
FFN_WEIGHT_CHUNKS = 8
TM_PROJ = 512
TM_EVPROJ = 1024
HY_CT = 256
HY_FREQ_TILE = 256
RG_CT = 128
RG_TC = 32
RG_HALO = SUBLANES
HEAD_SLOT = 128
TQ = 512
ATTN_HEADS = 16
Q_SCALE = MLA_SCALE * math.log2(math.e)
ADA_TN = 2304


def _dot(a, b):
    return jnp.dot(a, b, preferred_element_type=F32)


def _dot_nt(a, b):
    return lax.dot_general(a, b, (((1,), (1,)), ((), ())), preferred_element_type=F32)


def _resident(shape):
    nd = len(shape)
    return pl.BlockSpec(shape, lambda *_: (0,) * nd, pipeline_mode=pl.Buffered(1))


def _group_of_tile(i, tm):
    return jnp.where(i < T_LAT // tm, i // (SEQ // tm), BATCH)


def _params(*sem):
    return pltpu.CompilerParams(dimension_semantics=sem, vmem_limit_bytes=VMEM_LIMIT)


def _layernorm(z, g, b):
    mu = jnp.mean(z, axis=-1, keepdims=True)
    zc = z - mu
    var = jnp.mean(zc * zc, axis=-1, keepdims=True)
    return zc * lax.rsqrt(var + LN_EPS) * g + b


def _adaln_kernel(c_ref, w_ref, b_ref, o_ref):
    c = c_ref[...]
    s = (c * jax.nn.sigmoid(c)).astype(BF16)
    o_ref[0] = _dot(s, w_ref[0].astype(BF16)) + b_ref[0]


def _adaln(cond, ada_w, ada_b):
    n_out = N_ADA * D_MODEL
    rows = cond.shape[0]
    return pl.pallas_call(
        _adaln_kernel,
        grid=(DEPTH, n_out // ADA_TN),
        in_specs=[
            pl.BlockSpec((rows, D_MODEL), lambda l, j: (0, 0)),
            pl.BlockSpec((1, D_MODEL, ADA_TN), lambda l, j: (l, 0, j)),
            pl.BlockSpec((1, 1, ADA_TN), lambda l, j: (l, 0, j)),
        ],
        out_specs=pl.BlockSpec((1, rows, ADA_TN), lambda l, j: (l, 0, j)),
        out_shape=jax.ShapeDtypeStruct((DEPTH, rows, n_out), F32),
        compiler_params=_params("arbitrary", "arbitrary"),
        name="adaln",
    )(cond, ada_w, ada_b.reshape(DEPTH, 1, n_out))


def _ffn_body(x, mod_ref, wi_ref, wo_ref, g_ref, b_ref, o_ref, act_ref):
    n_slots = mod_ref.shape[1]
    shift, scale, gate = mod_ref[0, n_slots - 3], mod_ref[0, n_slots - 2], mod_ref[0, n_slots - 1]
    h = (x * (1.0 + scale) + shift).astype(BF16)
    for j in range(N_FF_CHUNKS):
        lo = j * FF_CHUNK
        a = _dot(h, wi_ref[:, lo:lo + FF_CHUNK])
        u = _dot(h, wi_ref[:, D_FF + lo:D_FF + lo + FF_CHUNK])
        act_ref[:, lo:lo + FF_CHUNK] = (a * jax.nn.sigmoid(a) * u).astype(BF16)
    y = _dot(act_ref[...], wo_ref[...])
    o_ref[...] = _layernorm(ALPHA * x + (0.5 * gate) * y, g_ref[...], b_ref[...])


def _load_weight_bf16(w_hbm, which, dst_ref, stage_ref, sem):
    layer, half = which
    rows = stage_ref.shape[1]
    n_chunks = dst_ref.shape[0] // rows

    def copy(c):
        return pltpu.make_async_copy(w_hbm.at[layer, half, pl.ds(c * rows, rows)], stage_ref.at[c % 2], sem.at[c % 2])

    copy(0).start()
    for c in range(n_chunks):
        if c + 1 < n_chunks:
            copy(c + 1).start()
        copy(c).wait()
        dst_ref[c * rows:(c + 1) * rows, :] = stage_ref[c % 2].astype(BF16)


def _ffn_steps(x_fn, which, mod_ref, wi_hbm, wo_hbm, g_ref, b_ref, o_ref,
               act_ref, wi_ref, wo_ref, stage_in_ref, stage_out_ref, sem_in, sem_out):
    @pl.when(pl.program_id(0) == 0)
    def _():
        _load_weight_bf16(wi_hbm, which, wi_ref, stage_in_ref, sem_in)
        _load_weight_bf16(wo_hbm, which, wo_ref, stage_out_ref, sem_out)

    _ffn_body(x_fn(), mod_ref, wi_ref, wo_ref, g_ref, b_ref, o_ref, act_ref)


def _ffn_kernel(x_ref, *rest, which):
    _ffn_steps(lambda: x_ref[...], which, *rest)


def _ffn_split_kernel(xl_ref, xc_ref, *rest, which):
    _ffn_steps(lambda: jnp.where(pl.program_id(0) < T_LAT // TM_FFN, xl_ref[...], xc_ref[...]), which, *rest)


def _ffn_after_even_kernel(x_ref, yh_ref, hs_ref, gt_ref, wmix_ref, g1_ref, b1_ref, mod_ref, *rest, which):
    def x_fn():
        y_rg = hs_ref[...] * jax.nn.gelu(gt_ref[...], approximate=True)
        y = (_dot(yh_ref[...].astype(BF16), wmix_ref[:HY_WIDTH, :])
             + _dot(y_rg.astype(BF16), wmix_ref[HY_WIDTH:, :]))
        return _layernorm(ALPHA * x_ref[...] + mod_ref[0, 0] * y, g1_ref[...], b1_ref[...])

    _ffn_steps(x_fn, which, mod_ref, *rest)


def _ffn_after_mla_kernel(x_ref, att_ref, wmix_ref, g1_ref, b1_ref, mod_ref, *rest, which):
    def x_fn():
        y = _dot(att_ref[...], wmix_ref[...])
        return _layernorm(ALPHA * x_ref[...] + mod_ref[0, 0] * y, g1_ref[...], b1_ref[...])

    _ffn_steps(x_fn, which, mod_ref, *rest)


def _ffn(body, lead_args, lead_specs, mod, ffn_w, which, g, b, n_rows):
    tm = TM_FFN
    w_in_all, w_out_all = ffn_w
    return pl.pallas_call(
        functools.partial(body, which=which),
        grid=(n_rows // tm,),
        in_specs=[
            *lead_specs,
            pl.BlockSpec((1, mod.shape[1], 1, D_MODEL), lambda i: (_group_of_tile(i, tm), 0, 0, 0)),
            pl.BlockSpec(memory_space=pl.ANY),
            pl.BlockSpec(memory_space=pl.ANY),
            _resident((1, D_MODEL)),
            _resident((1, D_MODEL)),
        ],
        out_specs=pl.BlockSpec((tm, D_MODEL), lambda i: (i, 0)),
        out_shape=jax.ShapeDtypeStruct((n_rows, D_MODEL), F32),
        scratch_shapes=[
            pltpu.VMEM((tm, D_FF), BF16),
            pltpu.VMEM((D_MODEL, 2 * D_FF), BF16),
            pltpu.VMEM((D_FF, D_MODEL), BF16),
            pltpu.VMEM((2, D_MODEL // FFN_WEIGHT_CHUNKS, 2 * D_FF), F32),
            pltpu.VMEM((2, D_FF // FFN_WEIGHT_CHUNKS, D_MODEL), F32),
            pltpu.SemaphoreType.DMA((2,)),
            pltpu.SemaphoreType.DMA((2,)),
        ],
        compiler_params=_params("arbitrary"),
        name="half_ffn",
    )(*lead_args, mod, w_in_all, w_out_all, g.reshape(1, D_MODEL), b.reshape(1, D_MODEL))


def _row_spec(width, tm=TM_FFN):
    return pl.BlockSpec((tm, width), lambda i: (i, 0))


def _ffn_plain(x, *args):
    return _ffn(_ffn_kernel, (x,), [_row_spec(D_MODEL)], *args)


def _ffn_split(x_lat, x_ctx, *args):
    n_lat = T_LAT // TM_FFN
    specs = [pl.BlockSpec((TM_FFN, D_MODEL), lambda i: (jnp.minimum(i, n_lat - 1), 0)),
             pl.BlockSpec((TM_FFN, D_MODEL), lambda i: (jnp.maximum(i - n_lat, 0), 0))]
    return _ffn(_ffn_split_kernel, (x_lat, x_ctx), specs, *args)


def _ffn_after_even(x, y_hy, h_sum, gate_in, w_mix, g1, b1, *args):
    lead = (x, y_hy, h_sum, gate_in, w_mix.astype(BF16), g1.reshape(1, D_MODEL), b1.reshape(1, D_MODEL))
    specs = [_row_spec(D_MODEL), _row_spec(HY_WIDTH), _row_spec(RG_WIDTH), _row_spec(RG_WIDTH),
             _resident((HY_WIDTH + RG_WIDTH, D_MODEL)), _resident((1, D_MODEL)), _resident((1, D_MODEL))]
    return _ffn(_ffn_after_even_kernel, lead, specs, *args)


def _ffn_after_mla(x, att, w_mix, g1, b1, *args):
    lead = (x, att, w_mix.astype(BF16), g1.reshape(1, D_MODEL), b1.reshape(1, D_MODEL))
    specs = [_row_spec(D_MODEL), _row_spec(MLA_HEADS * MLA_V),
             _resident((MLA_HEADS * MLA_V, D_MODEL)), _resident((1, D_MODEL)), _resident((1, D_MODEL))]
    return _ffn(_ffn_after_mla_kernel, lead, specs, *args)


def _evproj_kernel(x_ref, mod_ref, w_ref, hy_ref, rx_ref, gt_ref):
    shift, scale = mod_ref[0, 0], mod_ref[0, 1]
    h = (x_ref[...] * (1.0 + scale) + shift).astype(BF16)
    n_hy = 3 * HY_WIDTH
    hy_ref[...] = _dot(h, w_ref[:, :n_hy])
    rx_ref[...] = _dot(h, w_ref[:, n_hy:n_hy + RG_WIDTH])
    gt_ref[...] = _dot(h, w_ref[:, n_hy + RG_WIDTH:])


def _evproj(x, mod, w_in):
    tm = TM_EVPROJ
    row = lambda i: (i, 0)
    return pl.pallas_call(
        _evproj_kernel,
        grid=(T_ALL // tm,),
        in_specs=[
            pl.BlockSpec((tm, D_MODEL), row),
            pl.BlockSpec((1, 2, 1, D_MODEL), lambda i: (_group_of_tile(i, tm), 0, 0, 0)),
            _resident((D_MODEL, EV_IN)),
        ],
        out_specs=[
            pl.BlockSpec((tm, 3 * HY_WIDTH), row),
            pl.BlockSpec((tm, RG_WIDTH), row),
            pl.BlockSpec((tm, RG_WIDTH), row),
        ],
        out_shape=[
            jax.ShapeDtypeStruct((T_ALL, 3 * HY_WIDTH), F32),
            jax.ShapeDtypeStruct((T_ALL, RG_WIDTH), F32),
            jax.ShapeDtypeStruct((T_ALL, RG_WIDTH), F32),
        ],
        compiler_params=_params("arbitrary"),
        name="even_in_proj",
    )(x, mod, w_in.astype(BF16))


def _block_diag_tiles(w):
    per = RG_CT // RG_BLOCK_DIM
    w = w.reshape(RG_WIDTH // RG_CT, per, RG_BLOCK_DIM, RG_BLOCK_DIM)
    eye = jnp.eye(per, dtype=w.dtype)
    return jnp.einsum('cpde,pq->cpdqe', w, eye).reshape(RG_WIDTH // RG_CT, RG_CT, RG_CT)


def _rg_kernel(rx_ref, cw_ref, cb_ref, w_ref, bias_ref, lam_ref, out_ref,
                padl_ref, padc_ref, xt_f_ref, xt_b_ref, coef_a_ref, coef_b_ref, hbuf_f_ref, hbuf_b_ref):
    neg_lam = -lam_ref[...]
    softplus = jnp.maximum(neg_lam, 0.0) + jnp.log1p(jnp.exp(-jnp.abs(neg_lam)))
    neg_log_a_scale = RG_C * softplus
    a_exp2_scale = -LOG2_E * neg_log_a_scale
    cb = cb_ref[...]
    xts = (xt_f_ref, xt_b_ref)
    hbuf = (hbuf_f_ref, hbuf_b_ref)
    slab_rows = RG_TC + 2 * RG_HALO

    def step_rows(s):
        return slice(s * SUBLANES, (s + 1) * SUBLANES)

    halo = jnp.zeros((BATCH, RG_HALO, RG_CT), F32)
    for pad_ref, length in ((padl_ref, SEQ), (padc_ref, CTX_LEN)):
        pad_ref[:, 0:RG_HALO, :] = halo
        pad_ref[:, RG_HALO + length:, :] = halo
    padl_ref[:, RG_HALO:RG_HALO + SEQ, :] = rx_ref[0:BATCH]
    for b in range(BATCH):
        padc_ref[b, RG_HALO:RG_HALO + CTX_LEN, :] = rx_ref[BATCH, b * CTX_LEN:(b + 1) * CTX_LEN, :]

    def add_out(latent, t_start, d, first_touch):
        t_start = pl.multiple_of(t_start, SUBLANES)
        for b in range(BATCH):
            rows = hbuf[d][pl.ds(b, RG_TC, stride=SUBLANES), :]
            where = (b, pl.ds(t_start, RG_TC)) if latent else (BATCH, pl.ds(b * CTX_LEN + t_start, RG_TC))
            if first_touch:
                out_ref[where[0], where[1], :] = rows
            else:
                out_ref[where[0], where[1], :] += rows

    def sigmoid(z):
        return 1.0 / (1.0 + jnp.exp2(z * (-LOG2_E)))

    def coeffs(latent, length, t0, d, dst_ref):
        t0 = pl.multiple_of(jnp.asarray(t0, jnp.int32), SUBLANES)
        pad_ref = padl_ref if latent else padc_ref
        xt = xts[d]
        for b in range(BATCH):
            xt[pl.ds(b, slab_rows, stride=SUBLANES), :] = pad_ref[b, pl.ds(t0, slab_rows), :]
        x2 = cb
        for k in range(RG_CONV):
            start = (RG_HALO - RG_PAD_L + k) * SUBLANES
            x2 = x2 + xt[start:start + RG_TC * SUBLANES, :] * cw_ref[k:k + 1, :]
        xb = x2.astype(BF16)
        r = sigmoid(_dot(xb, w_ref[0, 2 * d]) + bias_ref[2 * d:2 * d + 1, :])
        i = sigmoid(_dot(xb, w_ref[0, 2 * d + 1]) + bias_ref[2 * d + 1:2 * d + 2, :])
        a = jnp.exp2(r * a_exp2_scale[d:d + 1, :])
        b = jnp.sqrt(jnp.tanh(r * neg_log_a_scale[d:d + 1, :]) * (a * a + 1.0)) * (i * x2)
        dst_ref[d, 0] = a
        dst_ref[d, 1] = b

    def scan(latent, first_touch, tf, tb, src_ref, hf, hb):
        tf = jnp.asarray(tf, jnp.int32)
        tb = jnp.asarray(tb, jnp.int32)
        for s in range(RG_TC):
            rows = step_rows(s)
            hf = src_ref[0, 0, rows, :] * hf + src_ref[0, 1, rows, :]
            hbuf_f_ref[rows, :] = hf
            rows = step_rows(RG_TC - 1 - s)
            hb = src_ref[1, 0, rows, :] * hb + src_ref[1, 1, rows, :]
            hbuf_b_ref[rows, :] = hb
        add_out(latent, tf, 0, first_touch)
        add_out(latent, tb, 1, first_touch)
        return hf, hb

    def sweep(latent, length, h_fwd, h_bwd):
        n_chunks = length // RG_TC
        assert n_chunks % 4 == 0
        last = n_chunks - 1

        def both(j, dst_ref):
            coeffs(latent, length, j * RG_TC, 0, dst_ref)
            coeffs(latent, length, (last - j) * RG_TC, 1, dst_ref)

        def body(first_touch, i, carry):
            j = 2 * i
            both(j + 1, coef_b_ref)
            carry = scan(latent, first_touch, j * RG_TC, (last - j) * RG_TC, coef_a_ref, *carry)
            both(jnp.minimum(j + 2, last), coef_a_ref)
            return scan(latent, first_touch, (j + 1) * RG_TC, (last - j - 1) * RG_TC, coef_b_ref, *carry)

        both(0, coef_a_ref)
        half = n_chunks // 4
        carry = lax.fori_loop(0, half, functools.partial(body, True), (h_fwd, h_bwd))
        return lax.fori_loop(half, 2 * half, functools.partial(body, False), carry)

    zero = jnp.zeros((SUBLANES, RG_CT), F32)
    hf, hb = sweep(False, CTX_LEN, zero, zero)
    sweep(True, SEQ, hf, hb)


def _rglru(rx, conv_w, conv_b, a_w, a_b, x_w, x_b, lam):
    assert T_CTX == SEQ, "the context rows must form exactly one latent-sized row block"
    w = jnp.stack([_block_diag_tiles(a_w[0]), _block_diag_tiles(x_w[0]),
                   _block_diag_tiles(a_w[1]), _block_diag_tiles(x_w[1])], axis=1).astype(BF16)
    bias = jnp.stack([a_b[0], x_b[0], a_b[1], x_b[1]], axis=0)
    n_ct = RG_WIDTH // RG_CT
    n_blocks = BATCH + 1
    chan = lambda c: (0, 0, c)
    lane = lambda c: (0, c)
    coef = pltpu.VMEM((2, 2, RG_TC * BATCH, RG_CT), F32)
    slab = pltpu.VMEM(((RG_TC + 2 * RG_HALO) * BATCH, RG_CT), F32)
    chunk = pltpu.VMEM((RG_TC * BATCH, RG_CT), F32)
    out = pl.pallas_call(
        _rg_kernel,
        grid=(n_ct,),
        in_specs=[
            pl.BlockSpec((n_blocks, SEQ, RG_CT), chan),
            pl.BlockSpec((4, RG_CT), lane),
            pl.BlockSpec((1, RG_CT), lane),
            pl.BlockSpec((1, 4, RG_CT, RG_CT), lambda c: (c, 0, 0, 0)),
            pl.BlockSpec((4, RG_CT), lane),
            pl.BlockSpec((2, RG_CT), lane),
        ],
        out_specs=pl.BlockSpec((n_blocks, SEQ, RG_CT), chan),
        out_shape=jax.ShapeDtypeStruct((n_blocks, SEQ, RG_WIDTH), F32),
        scratch_shapes=[
            pltpu.VMEM((BATCH, SEQ + 2 * RG_HALO, RG_CT), F32),
            pltpu.VMEM((BATCH, CTX_LEN + 2 * RG_HALO, RG_CT), F32),
            slab, slab,
            coef, coef,
            chunk, chunk,
        ],
        compiler_params=_params("arbitrary"),
        name="rglru_scan",
    )(rx.reshape(n_blocks, SEQ, RG_WIDTH), conv_w, conv_b.reshape(1, RG_WIDTH), w, bias, lam)
    return out.reshape(T_ALL, RG_WIDTH)


def _dft_tables(length):
    step = 1 << (length.bit_length() // 2)
    s = np.arange(length, dtype=np.int64)

    def thin(kvec):
        ang = ((kvec[:, None] * s[None, :]) % (2 * length)) * (math.pi / length)
        return (jnp.asarray(np.cos(ang)[:, None, :], dtype=F32),
                jnp.asarray(np.sin(ang)[:, None, :], dtype=F32))

    c_hi, s_hi = thin(np.arange(0, length, step, dtype=np.int64))
    c_lo, s_lo = thin(np.arange(step, dtype=np.int64))
    c_lo, s_lo = c_lo.reshape(1, step, length), s_lo.reshape(1, step, length)
    fc = (c_hi * c_lo - s_hi * s_lo).reshape(length, length)
    fs = (s_hi * c_lo + c_hi * s_lo).reshape(length, length)
    return fc.astype(BF16), fs.astype(BF16)


def _filter_features(length):
    f32 = np.float32
    pos = np.arange(length, dtype=f32)
    t = pos / f32(length)
    bands = np.linspace(1e-4, HY_BANDS - 1, HY_BANDS, dtype=f32)
    ang = (f32(2.0 * math.pi) * pos / f32(length))[:, None] * bands[None, :]
    feats = np.concatenate([t[:, None], np.cos(ang), -np.sin(ang)], axis=-1).astype(f32)
    feats = np.pad(feats, ((0, 0), (0, LANES - HY_EMB)))
    deltas = np.abs(np.linspace(HY_MIN_DECAY, HY_MAX_DECAY, HY_WIDTH, dtype=f32))
    window = (np.exp(-t[:, None] * deltas[None, :]) + f32(HY_SHIFT)).astype(f32)
    return jnp.asarray(feats), jnp.asarray(window)


def _filter_hidden(feat_ref, w1_ref, b1_ref, w2_ref, b2_ref, sf_ref, hid_ref):
    h = jnp.sin(sf_ref[0:1, :] * (_dot(feat_ref[...].astype(BF16), w1_ref[...]) + b1_ref[...]))
    h = jnp.sin(sf_ref[1:2, :] * (_dot(h.astype(BF16), w2_ref[...]) + b2_ref[...]))
    hid_ref[...] = h.astype(BF16)


def _filter_spectrum(hid_ref, win_ref, w3f_ref, w3b_ref, fc_ref, fs_ref, kc_ref, ks_ref, kn_ref, length):
    n = 2 * length
    hb16 = hid_ref[...]
    win = win_ref[...]
    row = lax.broadcasted_iota(jnp.int32, (length, HY_CT), 0)
    h_fwd = _dot(hb16, w3f_ref[...]) * win
    h_bwd0 = jnp.where(row == 0, 0.0, _dot(hb16, w3b_ref[...]) * win)
    h_sum = h_fwd + h_bwd0
    h_dif = h_fwd - h_bwd0
    weight = jnp.where(row == 0, 1.0 / n, 2.0 / n)
    kc_ref[...] = _dot(fc_ref[...], h_sum.astype(BF16)) * weight
    ks_ref[...] = _dot(fs_ref[...], h_dif.astype(BF16)) * weight
    sign = jnp.where((row & 1) == 0, 1.0, -1.0)
    kn_ref[...] = jnp.sum(h_sum * sign, axis=0, keepdims=True) * (1.0 / n)


def _filter_params(w1, b1, w2, b2, w3, sin_freq):
    hp = LANES - HY_FILT_HIDDEN
    w3p = jnp.pad(w3, ((0, hp), (0, 0))).astype(BF16)
    return (jnp.pad(w1, ((0, LANES - HY_EMB), (0, hp))).astype(BF16), jnp.pad(b1, (0, hp)).reshape(1, LANES),
            jnp.pad(w2, ((0, hp), (0, hp))).astype(BF16), jnp.pad(b2, (0, hp)).reshape(1, LANES),
            w3p[:, :HY_WIDTH], w3p[:, HY_WIDTH:], jnp.pad(sin_freq, ((0, 0), (0, hp))))


def _hyena_sequence(raw, cw_ref, cb_ref, bias_ref, fc_ref, fs_ref, kc_ref, ks_ref, kn_ref, length):
    row = lax.broadcasted_iota(jnp.int32, (length, HY_CT), 0)

    def short_conv(x, part):
        prev = jnp.where(row == 0, 0.0, pltpu.roll(x, 1, 0))
        nxt = jnp.where(row == length - 1, 0.0, pltpu.roll(x, length - 1, 0))
        return (prev * cw_ref[part, 0:1, :] + x * cw_ref[part, 1:2, :] + nxt * cw_ref[part, 2:3, :]
                + cb_ref[part:part + 1, :])

    x0 = short_conv(raw[0], 0)
    x1 = short_conv(raw[1], 1)
    v = short_conv(raw[2], 2)
    vx = v * x1
    vb = vx.astype(BF16)
    ft = min(length, HY_FREQ_TILE)
    ps, qs = [], []
    for f0 in range(0, length, ft):
        spec_c = _dot(fc_ref[f0:f0 + ft, :], vb)
        spec_s = _dot(fs_ref[f0:f0 + ft, :], vb)
        kc = kc_ref[f0:f0 + ft, :]
        ks = ks_ref[f0:f0 + ft, :]
        ps.append((spec_c * kc - spec_s * ks).astype(BF16))
        qs.append((spec_c * ks + spec_s * kc).astype(BF16))
    y = _dot(fc_ref[...], jnp.concatenate(ps, axis=0)) + _dot(fs_ref[...], jnp.concatenate(qs, axis=0))
    sign = jnp.where((row & 1) == 0, 1.0, -1.0)
    nyquist = jnp.sum(vx * sign, axis=0, keepdims=True) * kn_ref[...]
    return x0 * (y + sign * nyquist + vx * bias_ref[...])


def _hyena_kernel(x0_ref, x1_ref, v_ref, cw_ref, cb_ref, bias_ref,
                  featl_ref, winl_ref, featc_ref, winc_ref,
                  w1_ref, b1_ref, w2_ref, b2_ref, w3f_ref, w3b_ref, sf_ref,
                  fcl_ref, fsl_ref, fcc_ref, fsc_ref, o_ref,
                  hidl_ref, hidc_ref, kcl_ref, ksl_ref, knl_ref, kcc_ref, ksc_ref, knc_ref):
    step = pl.program_id(1)
    common = (cw_ref, cb_ref, bias_ref)

    @pl.when(jnp.logical_and(pl.program_id(0) == 0, step == 0))
    def _():
        sine_layers = (w1_ref, b1_ref, w2_ref, b2_ref, sf_ref)
        _filter_hidden(featl_ref, *sine_layers, hidl_ref)
        _filter_hidden(featc_ref, *sine_layers, hidc_ref)

    @pl.when(step == 0)
    def _():
        _filter_spectrum(hidl_ref, winl_ref, w3f_ref, w3b_ref, fcl_ref, fsl_ref, kcl_ref, ksl_ref, knl_ref, SEQ)
        _filter_spectrum(hidc_ref, winc_ref, w3f_ref, w3b_ref, fcc_ref, fsc_ref, kcc_ref, ksc_ref, knc_ref, CTX_LEN)

    @pl.when(step < BATCH)
    def _():
        raw = (x0_ref[...], x1_ref[...], v_ref[...])
        o_ref[...] = _hyena_sequence(raw, *common, fcl_ref, fsl_ref, kcl_ref, ksl_ref, knl_ref, SEQ)

    @pl.when(step == BATCH)
    def _():
        for r in range(BATCH):
            rows = slice(r * CTX_LEN, (r + 1) * CTX_LEN)
            raw = (x0_ref[rows, :], x1_ref[rows, :], v_ref[rows, :])
            o_ref[rows, :] = _hyena_sequence(raw, *common, fcc_ref, fsc_ref, kcc_ref, ksc_ref, knc_ref, CTX_LEN)


def _hyena(hy, filter_params, conv_w, conv_b, bias):
    assert T_CTX == SEQ, "the context rows must form exactly one latent-sized row block"
    n_ct = HY_WIDTH // HY_CT
    cw = conv_w.reshape(3, 3, HY_WIDTH).transpose(1, 0, 2)
    cb = conv_b.reshape(3, HY_WIDTH)
    feat_l, win_l = _filter_features(SEQ)
    feat_c, win_c = _filter_features(CTX_LEN)
    tab_l = _dft_tables(SEQ)
    tab_c = _dft_tables(CTX_LEN)

    def part_spec(part):
        return pl.BlockSpec((SEQ, HY_CT), lambda c, b: (b, part * n_ct + c))

    chan = lambda c, b: (0, c)
    once_per_tile = lambda rows: pl.BlockSpec((rows, HY_CT), chan, pipeline_mode=pl.Buffered(1))
    spectrum = lambda rows: pltpu.VMEM((rows, HY_CT), F32)
    return pl.pallas_call(
        _hyena_kernel,
        grid=(n_ct, BATCH + 1),
        in_specs=[
            part_spec(0), part_spec(1), part_spec(2),
            pl.BlockSpec((3, 3, HY_CT), lambda c, b: (0, 0, c)),
            pl.BlockSpec((3, HY_CT), chan),
            pl.BlockSpec((1, HY_CT), chan),
            _resident((SEQ, LANES)), once_per_tile(SEQ),
            _resident((CTX_LEN, LANES)), once_per_tile(CTX_LEN),
            _resident((LANES, LANES)), _resident((1, LANES)), _resident((LANES, LANES)), _resident((1, LANES)),
            once_per_tile(LANES), once_per_tile(LANES), _resident((2, LANES)),
            _resident((SEQ, SEQ)), _resident((SEQ, SEQ)),
            _resident((CTX_LEN, CTX_LEN)), _resident((CTX_LEN, CTX_LEN)),
        ],
        out_specs=pl.BlockSpec((SEQ, HY_CT), lambda c, b: (b, c)),
        out_shape=jax.ShapeDtypeStruct((T_ALL, HY_WIDTH), F32),
        scratch_shapes=[pltpu.VMEM((SEQ, LANES), BF16), pltpu.VMEM((CTX_LEN, LANES), BF16),
                        spectrum(SEQ), spectrum(SEQ), spectrum(1), spectrum(CTX_LEN), spectrum(CTX_LEN), spectrum(1)],
        compiler_params=_params("arbitrary", "arbitrary"),
        name="hyena_conv",
    )(hy, hy, hy, cw, cb, bias.reshape(1, HY_WIDTH), feat_l, win_l, feat_c, win_c, *filter_params, *tab_l, *tab_c)


def _rope_partner(n):
    idx = np.arange(n)
    return np.where((idx % (2 * ROPE_AXIS_PAIRS)) < ROPE_AXIS_PAIRS, idx + ROPE_AXIS_PAIRS, idx - ROPE_AXIS_PAIRS)


def _slot_lanes():
    d = np.arange(MLA_ROPE)
    axis, second, pair = d // (2 * ROPE_AXIS_PAIRS), (d // ROPE_AXIS_PAIRS) % 2, d % ROPE_AXIS_PAIRS
    rope_lane = axis * ROPE_AXIS_PAIRS + pair + (HEAD_SLOT // 2) * second
    free = np.setdiff1d(np.arange(HEAD_SLOT), rope_lane)
    return rope_lane, free[:MLA_NOPE]


def _to_slot(nope, rope):
    rope_lane, nope_lane = _slot_lanes()
    source = [(None, 0)] * HEAD_SLOT
    for arr, lanes in ((nope, nope_lane), (rope, rope_lane)):
        if arr is not None:
            for idx, lane in enumerate(lanes):
                source[lane] = (arr, idx)
    like = nope if nope is not None else rope
    pieces, lane = [], 0
    while lane < HEAD_SLOT:
        arr, start = source[lane]
        run = 1
        while (lane + run < HEAD_SLOT and source[lane + run][0] is arr
               and (arr is None or source[lane + run][1] == start + run)):
            run += 1
        pieces.append(jnp.zeros(like.shape[:-1] + (run,), like.dtype) if arr is None else arr[..., start:start + run])
        lane += run
    return jnp.concatenate(pieces, axis=-1)


def _rope_tables():
    f32 = np.float32
    rope_lane, nope_lane = _slot_lanes()
    rows = np.repeat(np.arange(SEQ // GRID_W, dtype=f32), GRID_W)
    cols = np.tile(np.arange(GRID_W, dtype=f32), SEQ // GRID_W)
    inv_freq = (f32(ROPE_BASE) ** (-np.arange(ROPE_AXIS_PAIRS, dtype=f32) / f32(ROPE_AXIS_PAIRS))).astype(f32)
    ang_r = rows[:, None] * inv_freq
    ang_c = cols[:, None] * inv_freq
    cos32 = np.concatenate([np.cos(ang_r)] * 2 + [np.cos(ang_c)] * 2, axis=1)
    sin32 = np.concatenate([-np.sin(ang_r), np.sin(ang_r), -np.sin(ang_c), np.sin(ang_c)], axis=1)

    def table(n_rows, rope_vals, nope_val):
        t = np.zeros((n_rows, HEAD_SLOT), f32)
        t[:, rope_lane] = rope_vals
        t[:, nope_lane] = nope_val
        return t

    q_cos = table(SEQ, cos32, 1.0) * f32(Q_SCALE)
    q_sin = table(SEQ, sin32, 0.0) * f32(Q_SCALE)
    k_cos = table(SEQ, cos32, 0.0)
    k_sin = table(SEQ, sin32, 0.0)
    ident = table(TM_PROJ, 1.0, 0.0)
    zero = np.zeros((TM_PROJ, HEAD_SLOT), f32)
    tables = ((q_cos, zero), (q_sin, zero), (k_cos, ident), (k_sin, zero))
    return tuple(jnp.asarray(np.concatenate(t, 0).astype(f32)) for t in tables)


def _value_ones():
    assert HEAD_SLOT == 2 * MLA_V
    lane = np.arange(MLA_HEADS * HEAD_SLOT)
    head_is_even = (lane // HEAD_SLOT) % 2 == 0
    upper_half = lane % HEAD_SLOT >= MLA_V
    return jnp.asarray((head_is_even == upper_half).astype(np.float32).reshape(1, -1))


def _mla_weights(w_in, w_q_up, w_kv_up):
    w_q = w_in[:, :MLA_Q_LORA]
    w_kv = w_in[:, MLA_Q_LORA:MLA_Q_LORA + MLA_KV_LORA]
    w_kr = w_in[:, MLA_Q_LORA + MLA_KV_LORA:]
    w_kr2 = jnp.concatenate([_to_slot(None, w_kr), _to_slot(None, w_kr[:, _rope_partner(MLA_ROPE)])], axis=1)

    qh = w_q_up.reshape(MLA_Q_LORA, MLA_HEADS, MLA_NOPE + MLA_ROPE)
    wq_slot = _to_slot(qh[..., :MLA_NOPE], qh[..., MLA_NOPE:]).reshape(MLA_Q_LORA, MLA_HEADS * HEAD_SLOT)

    kvh = w_kv_up.reshape(MLA_KV_LORA, MLA_HEADS, MLA_NOPE + MLA_V)
    wk_slot = _to_slot(kvh[..., :MLA_NOPE], None).reshape(MLA_KV_LORA, MLA_HEADS * HEAD_SLOT)
    even = (jnp.arange(MLA_HEADS) % 2 == 0).astype(F32)[None, :, None]
    wv = kvh[..., MLA_NOPE:]
    wv_slot = jnp.concatenate([wv * even, wv * (1.0 - even)], axis=-1).reshape(MLA_KV_LORA, MLA_HEADS * HEAD_SLOT)
    return [w.astype(BF16) for w in (w_q, w_kv, w_kr2, wq_slot, wk_slot, wv_slot)]


def _mlaproj_kernel(x_ref, mod_ref, wq_ref, wkv_ref, wkr_ref, wqs_ref, wks_ref, wv_ref,
                    vone_ref, qg_ref, kvg_ref, qcos_ref, qsin_ref, kcos_ref, ksin_ref,
                    q_ref, k_ref, v_ref):
    shift, scale = mod_ref[0, 0], mod_ref[0, 1]
    h = (x_ref[...] * (1.0 + scale) + shift).astype(BF16)

    def rmsnorm(y, g):
        return (y * lax.rsqrt(jnp.mean(y * y, axis=-1, keepdims=True) + RMS_EPS) * g).astype(BF16)

    qn = rmsnorm(_dot(h, wq_ref[...]), qg_ref[...])
    kvn = rmsnorm(_dot(h, wkv_ref[...]), kvg_ref[...])
    kr2 = _dot(h, wkr_ref[...])
    k_rope = kr2[:, :HEAD_SLOT] * kcos_ref[...] + kr2[:, HEAD_SLOT:] * ksin_ref[...]
    k_nope = _dot(kvn, wks_ref[...])
    v_ref[...] = (_dot(kvn, wv_ref[...]) + vone_ref[...]).astype(BF16)
    q_all = _dot(qn, wqs_ref[...])
    q_cos = qcos_ref[...]
    q_sin = qsin_ref[...]
    for hd in range(MLA_HEADS):
        sl = slice(hd * HEAD_SLOT, (hd + 1) * HEAD_SLOT)
        qh = q_all[:, sl]
        rotated = qh * q_cos + pltpu.roll(qh, HEAD_SLOT // 2, 1) * q_sin
        q_ref[:, sl] = rotated.astype(BF16)
        k_ref[:, sl] = (k_nope[:, sl] + k_rope).astype(BF16)


def _mlaproj(x, mod, w_in, q_norm, kv_norm, w_q_up, w_kv_up):
    tm = TM_PROJ
    weights = _mla_weights(w_in, w_q_up, w_kv_up)
    tables = _rope_tables()
    row = lambda i: (i, 0)
    tab = lambda i: (jnp.where(i < T_LAT // tm, i % (SEQ // tm), SEQ // tm), 0)
    wide = MLA_HEADS * HEAD_SLOT
    v_one = _value_ones()
    return pl.pallas_call(
        _mlaproj_kernel,
        grid=(T_ALL // tm,),
        in_specs=[
            pl.BlockSpec((tm, D_MODEL), row),
            pl.BlockSpec((1, 2, 1, D_MODEL), lambda i: (_group_of_tile(i, tm), 0, 0, 0)),
            *[_resident(w.shape) for w in weights],
            _resident((1, wide)),
            _resident((1, MLA_Q_LORA)),
            _resident((1, MLA_KV_LORA)),
            *[pl.BlockSpec((tm, HEAD_SLOT), tab) for _ in tables],
        ],
        out_specs=[pl.BlockSpec((tm, wide), row)] * 3,
        out_shape=[jax.ShapeDtypeStruct((T_ALL, wide), BF16)] * 3,
        compiler_params=_params("arbitrary"),
        name="mla_proj",
    )(x, mod, *weights, v_one, q_norm.reshape(1, MLA_Q_LORA), kv_norm.reshape(1, MLA_KV_LORA), *tables)


def _attn_kernel(q_ref, kl_ref, kc_ref, vl_ref, vc_ref, o_ref):
    low = lax.broadcasted_iota(jnp.int32, (TQ, HEAD_SLOT), 1) < MLA_V
    pair_out = []
    for hd in range(ATTN_HEADS):
        sl = slice(hd * HEAD_SLOT, (hd + 1) * HEAD_SLOT)
        q = q_ref[:, sl]
        s_lat = _dot_nt(q, kl_ref[:, sl])
        s_ctx = _dot_nt(q, kc_ref[:, sl])
        m = jnp.maximum(jnp.max(s_lat, axis=-1, keepdims=True), jnp.max(s_ctx, axis=-1, keepdims=True))
        p_lat = jnp.exp2(s_lat - m).astype(BF16)
        p_ctx = jnp.exp2(s_ctx - m).astype(BF16)
        pair_out.append(_dot(p_lat, vl_ref[:, sl]) + _dot(p_ctx, vc_ref[:, sl]))
        if hd % 2 == 1:
            even, odd = pair_out
            pair_out = []
            numer = jnp.where(low, even, odd)
            denom = pltpu.roll(jnp.where(low, odd, even), MLA_V, 1)
            pair = hd // 2
            o_ref[:, pair * HEAD_SLOT:(pair + 1) * HEAD_SLOT] = (numer / denom).astype(BF16)


def _attention(q, k, v):
    n_q = SEQ // TQ
    n_groups = MLA_HEADS // ATTN_HEADS
    ctx_block0 = T_LAT // CTX_LEN
    wide = ATTN_HEADS * HEAD_SLOT
    lat = lambda b, h, i: (b, h)
    ctx = lambda b, h, i: (ctx_block0 + b, h)
    return pl.pallas_call(
        _attn_kernel,
        grid=(BATCH, n_groups, n_q),
        in_specs=[
            pl.BlockSpec((TQ, wide), lambda b, h, i: (b * n_q + i, h)),
            pl.BlockSpec((SEQ, wide), lat),
            pl.BlockSpec((CTX_LEN, wide), ctx),
            pl.BlockSpec((SEQ, wide), lat),
            pl.BlockSpec((CTX_LEN, wide), ctx),
        ],
        out_specs=pl.BlockSpec((TQ, ATTN_HEADS * MLA_V), lambda b, h, i: (b * n_q + i, h)),
        out_shape=jax.ShapeDtypeStruct((T_LAT, MLA_HEADS * MLA_V), BF16),
        compiler_params=_params("arbitrary", "arbitrary", "arbitrary"),
        name="mla_attention",
    )(q, k, k, v, v)


def kernel(x, c, ctx, c_ctx, ada_w, ada_b, ln_g, ln_b, ffn_w_in, ffn_w_out, ev_w_in, ev_w_out, hy_conv_w, hy_conv_b, hy_filt_w1, hy_filt_b1, hy_filt_w2, hy_filt_b2, hy_filt_w3, hy_sin_freq, hy_bias, rg_conv_w, rg_conv_b, rg_a_w, rg_a_b, rg_x_w, rg_x_b, rg_lambda, mla_w_in, mla_q_norm, mla_kv_norm, mla_w_q_up, mla_w_kv_up, mla_w_out):
    assert x.shape == (BATCH, SEQ, D_MODEL) and ctx.shape == (BATCH, CTX_LEN, D_MODEL)
    cond = jnp.concatenate([c, c_ctx[None], jnp.zeros((2 * SUBLANES - N_GROUPS, D_MODEL), F32)], axis=0)
    mods = _adaln(cond, ada_w, ada_b)[:, :N_GROUPS].reshape(DEPTH, N_GROUPS, N_ADA, 1, D_MODEL)
    ffn_w = (ffn_w_in, ffn_w_out)

    m = mods[0]
    xs = _ffn_split(x.reshape(T_LAT, D_MODEL), ctx.reshape(T_CTX, D_MODEL),
                    m[:, 0:3], ffn_w, (0, 0), ln_g[0, 0], ln_b[0, 0], T_ALL)
    hy, rx, gate_in = _evproj(xs, m[:, 3:5], ev_w_in[0])
    h_sum = _rglru(rx, rg_conv_w[0], rg_conv_b[0], rg_a_w[0], rg_a_b[0], rg_x_w[0], rg_x_b[0], rg_lambda[0])
    filt = _filter_params(hy_filt_w1[0], hy_filt_b1[0], hy_filt_w2[0], hy_filt_b2[0], hy_filt_w3[0], hy_sin_freq[0])
    y_hy = _hyena(hy, filt, hy_conv_w[0], hy_conv_b[0], hy_bias[0])
    xs = _ffn_after_even(xs, y_hy, h_sum, gate_in, ev_w_out[0], ln_g[0, 1], ln_b[0, 1],
                         m[:, 5:9], ffn_w, (0, 1), ln_g[0, 2], ln_b[0, 2], T_ALL)

    m = mods[1]
    xs = _ffn_plain(xs, m[:, 0:3], ffn_w, (1, 0), ln_g[1, 0], ln_b[1, 0], T_ALL)
    q, k, v = _mlaproj(xs, m[:, 3:5], mla_w_in[0], mla_q_norm[0], mla_kv_norm[0], mla_w_q_up[0], mla_w_kv_up[0])
    att = _attention(q, k, v)
    xl = _ffn_after_mla(xs, att, mla_w_out[0], ln_g[1, 1], ln_b[1, 1],
                        m[:, 5:9], ffn_w, (1, 1), ln_g[1, 2], ln_b[1, 2], T_LAT)
    return xl.reshape(BATCH, SEQ, D_MODEL)
```

```python
import functools
import math

import jax
import jax.numpy as jnp
import numpy as np
from jax import lax
from jax.experimental import pallas as pl
from jax.experimental.pallas import tpu as pltpu

F32 = jnp.float32
BF16 = jnp.bfloat16

D_MODEL = 1024
BATCH = 8
SEQ = 2048
DEPTH = 2
CTX_LEN = 256
GRID_W = 64
N_ADA = 9
D_FF = 2816

HY_WIDTH = 512
RG_WIDTH = 512
EV_IN = 3 * HY_WIDTH + 2 * RG_WIDTH
HY_EMB = 33
HY_BANDS = (HY_EMB - 1) // 2
HY_FILT_HIDDEN = 64
HY_TARGET = 1e-2
HY_MAX_DECAY = math.log(HY_TARGET) / 0.3
HY_MIN_DECAY = math.log(HY_TARGET) / 1.5
HY_SHIFT = 0.05
RG_BLOCKS = 8
RG_BLOCK_DIM = RG_WIDTH // RG_BLOCKS
RG_C = 8.0
RG_CONV = 4
RG_PAD_L = 2

MLA_HEADS = 16
MLA_Q_LORA = 768
MLA_KV_LORA = 256
MLA_NOPE = 64
MLA_ROPE = 32
MLA_V = 64
MLA_SCALE = (MLA_NOPE + MLA_ROPE) ** -0.5
ROPE_AXIS_PAIRS = MLA_ROPE // 4
ROPE_BASE = 10000.0

LOG2_E = math.log2(math.e)
ALPHA = (2.0 * DEPTH) ** 0.25
LN_EPS = 1e-6
RMS_EPS = 1e-6

T_LAT = BATCH * SEQ
T_CTX = BATCH * CTX_LEN
T_ALL = T_LAT + T_CTX
N_GROUPS = BATCH + 1

LANES = 128
SUBLANES = 8
VMEM_LIMIT = 56 * 1024 * 1024

TM_FFN = 512
FF_CHUNK = 256
N_FF_CHUNKS = D_FF // FF_CHUNK
FFN_WEIGHT_CHUNKS = 8
TM_PROJ = 512
TM_EVPROJ = 1024
HY_CT = 256
HY_FREQ_TILE = 512
RG_CT = 128
RG_TC = 32
RG_HALO = SUBLANES
HEAD_SLOT = 128
TQ = 512
ATTN_HEADS = 16
Q_SCALE = MLA_SCALE * math.log2(math.e)
ADA_TN = 2304


def _dot(a, b):
    return jnp.dot(a, b, preferred_element_type=F32)


def _dot_nt(a, b):
    return lax.dot_general(a, b, (((1,), (1,)), ((), ())), preferred_element_type=F32)


def _resident(shape):
    nd = len(shape)
    return pl.BlockSpec(shape, lambda *_: (0,) * nd, pipeline_mode=pl.Buffered(1))


def _group_of_tile(i, tm):
    return jnp.where(i < T_LAT // tm, i // (SEQ // tm), BATCH)


def _params(*sem):
    return pltpu.CompilerParams(dimension_semantics=sem, vmem_limit_bytes=VMEM_LIMIT)


def _layernorm(z, g, b):
    mu = jnp.mean(z, axis=-1, keepdims=True)
    zc = z - mu
    var = jnp.mean(zc * zc, axis=-1, keepdims=True)
    return zc * lax.rsqrt(var + LN_EPS) * g + b


def _adaln_kernel(c_ref, w_ref, b_ref, o_ref):
    c = c_ref[...]
    s = (c * jax.nn.sigmoid(c)).astype(BF16)
    o_ref[0] = _dot(s, w_ref[0].astype(BF16)) + b_ref[0]


def _adaln(cond, ada_w, ada_b):
    n_out = N_ADA * D_MODEL
    rows = cond.shape[0]
    return pl.pallas_call(
        _adaln_kernel,
        grid=(DEPTH, n_out // ADA_TN),
        in_specs=[
            pl.BlockSpec((rows, D_MODEL), lambda l, j: (0, 0)),
            pl.BlockSpec((1, D_MODEL, ADA_TN), lambda l, j: (l, 0, j)),
            pl.BlockSpec((1, 1, ADA_TN), lambda l, j: (l, 0, j)),
        ],
        out_specs=pl.BlockSpec((1, rows, ADA_TN), lambda l, j: (l, 0, j)),
        out_shape=jax.ShapeDtypeStruct((DEPTH, rows, n_out), F32),
        compiler_params=_params("arbitrary", "arbitrary"),
        name="adaln",
    )(cond, ada_w, ada_b.reshape(DEPTH, 1, n_out))


def _ffn_body(x, mod_ref, wi_ref, wo_ref, g_ref, b_ref, o_ref, act_ref):
    n_slots = mod_ref.shape[1]
    shift, scale, gate = mod_ref[0, n_slots - 3], mod_ref[0, n_slots - 2], mod_ref[0, n_slots - 1]
    h = (x * (1.0 + scale) + shift).astype(BF16)
    for j in range(N_FF_CHUNKS):
        lo = j * FF_CHUNK
        a = _dot(h, wi_ref[:, lo:lo + FF_CHUNK])
        u = _dot(h, wi_ref[:, D_FF + lo:D_FF + lo + FF_CHUNK])
        act_ref[:, lo:lo + FF_CHUNK] = (a * jax.nn.sigmoid(a) * u).astype(BF16)
    y = _dot(act_ref[...], wo_ref[...])
    o_ref[...] = _layernorm(ALPHA * x + (0.5 * gate) * y, g_ref[...], b_ref[...])


def _load_weight_bf16(w_hbm, which, dst_ref, stage_ref, sem):
    layer, half = which
    rows = stage_ref.shape[1]
    n_chunks = dst_ref.shape[0] // rows

    def copy(c):
        return pltpu.make_async_copy(w_hbm.at[layer, half, pl.ds(c * rows, rows)], stage_ref.at[c % 2], sem.at[c % 2])

    copy(0).start()
    for c in range(n_chunks):
        if c + 1 < n_chunks:
            copy(c + 1).start()
        copy(c).wait()
        dst_ref[c * rows:(c + 1) * rows, :] = stage_ref[c % 2].astype(BF16)


def _ffn_steps(x_fn, which, mod_ref, wi_hbm, wo_hbm, g_ref, b_ref, o_ref,
               act_ref, wi_ref, wo_ref, stage_in_ref, stage_out_ref, sem_in, sem_out):
    @pl.when(pl.program_id(0) == 0)
    def _():
        _load_weight_bf16(wi_hbm, which, wi_ref, stage_in_ref, sem_in)
        _load_weight_bf16(wo_hbm, which, wo_ref, stage_out_ref, sem_out)

    _ffn_body(x_fn(), mod_ref, wi_ref, wo_ref, g_ref, b_ref, o_ref, act_ref)


def _ffn_kernel(x_ref, *rest, which):
    _ffn_steps(lambda: x_ref[...], which, *rest)


def _ffn_split_kernel(xl_ref, xc_ref, *rest, which):
    _ffn_steps(lambda: jnp.where(pl.program_id(0) < T_LAT // TM_FFN, xl_ref[...], xc_ref[...]), which, *rest)


def _ffn_after_even_kernel(x_ref, yh_ref, hs_ref, gt_ref, wmix_ref, g1_ref, b1_ref, mod_ref, *rest, which):
    def x_fn():
        y_rg = hs_ref[...] * jax.nn.gelu(gt_ref[...], approximate=True)
        y = (_dot(yh_ref[...].astype(BF16), wmix_ref[:HY_WIDTH, :])
             + _dot(y_rg.astype(BF16), wmix_ref[HY_WIDTH:, :]))
        return _layernorm(ALPHA * x_ref[...] + mod_ref[0, 0] * y, g1_ref[...], b1_ref[...])

    _ffn_steps(x_fn, which, mod_ref, *rest)


def _ffn_after_mla_kernel(x_ref, att_ref, wmix_ref, g1_ref, b1_ref, mod_ref, *rest, which):
    def x_fn():
        y = _dot(att_ref[...], wmix_ref[...])
        return _layernorm(ALPHA * x_ref[...] + mod_ref[0, 0] * y, g1_ref[...], b1_ref[...])

    _ffn_steps(x_fn, which, mod_ref, *rest)


def _ffn(body, lead_args, lead_specs, mod, ffn_w, which, g, b, n_rows):
    tm = TM_FFN
    w_in_all, w_out_all = ffn_w
    return pl.pallas_call(
        functools.partial(body, which=which),
        grid=(n_rows // tm,),
        in_specs=[
            *lead_specs,
            pl.BlockSpec((1, mod.shape[1], 1, D_MODEL), lambda i: (_group_of_tile(i, tm), 0, 0, 0)),
            pl.BlockSpec(memory_space=pl.ANY),
            pl.BlockSpec(memory_space=pl.ANY),
            _resident((1, D_MODEL)),
            _resident((1, D_MODEL)),
        ],
        out_specs=pl.BlockSpec((tm, D_MODEL), lambda i: (i, 0)),
        out_shape=jax.ShapeDtypeStruct((n_rows, D_MODEL), F32),
        scratch_shapes=[
            pltpu.VMEM((tm, D_FF), BF16),
            pltpu.VMEM((D_MODEL, 2 * D_FF), BF16),
            pltpu.VMEM((D_FF, D_MODEL), BF16),
            pltpu.VMEM((2, D_MODEL // FFN_WEIGHT_CHUNKS, 2 * D_FF), F32),
            pltpu.VMEM((2, D_FF // FFN_WEIGHT_CHUNKS, D_MODEL), F32),
            pltpu.SemaphoreType.DMA((2,)),
            pltpu.SemaphoreType.DMA((2,)),
        ],
        compiler_params=_params("arbitrary"),
        name="half_ffn",
    )(*lead_args, mod, w_in_all, w_out_all, g.reshape(1, D_MODEL), b.reshape(1, D_MODEL))


def _row_spec(width, tm=TM_FFN):
    return pl.BlockSpec((tm, width), lambda i: (i, 0))


def _ffn_plain(x, *args):
    return _ffn(_ffn_kernel, (x,), [_row_spec(D_MODEL)], *args)


def _ffn_split(x_lat, x_ctx, *args):
    n_lat = T_LAT // TM_FFN
    specs = [pl.BlockSpec((TM_FFN, D_MODEL), lambda i: (jnp.minimum(i, n_lat - 1), 0)),
             pl.BlockSpec((TM_FFN, D_MODEL), lambda i: (jnp.maximum(i - n_lat, 0), 0))]
    return _ffn(_ffn_split_kernel, (x_lat, x_ctx), specs, *args)


def _ffn_after_even(x, y_hy, h_sum, gate_in, w_mix, g1, b1, *args):
    lead = (x, y_hy, h_sum, gate_in, w_mix.astype(BF16), g1.reshape(1, D_MODEL), b1.reshape(1, D_MODEL))
    specs = [_row_spec(D_MODEL), _row_spec(HY_WIDTH), _row_spec(RG_WIDTH), _row_spec(RG_WIDTH),
             _resident((HY_WIDTH + RG_WIDTH, D_MODEL)), _resident((1, D_MODEL)), _resident((1, D_MODEL))]
    return _ffn(_ffn_after_even_kernel, lead, specs, *args)


def _ffn_after_mla(x, att, w_mix, g1, b1, *args):
    lead = (x, att, w_mix.astype(BF16), g1.reshape(1, D_MODEL), b1.reshape(1, D_MODEL))
    specs = [_row_spec(D_MODEL), _row_spec(MLA_HEADS * MLA_V),
             _resident((MLA_HEADS * MLA_V, D_MODEL)), _resident((1, D_MODEL)), _resident((1, D_MODEL))]
    return _ffn(_ffn_after_mla_kernel, lead, specs, *args)


def _evproj_kernel(x_ref, mod_ref, w_ref, hy_ref, rx_ref, gt_ref):
    shift, scale = mod_ref[0, 0], mod_ref[0, 1]
    h = (x_ref[...] * (1.0 + scale) + shift).astype(BF16)
    n_hy = 3 * HY_WIDTH
    hy_ref[...] = _dot(h, w_ref[:, :n_hy])
    rx_ref[...] = _dot(h, w_ref[:, n_hy:n_hy + RG_WIDTH])
    gt_ref[...] = _dot(h, w_ref[:, n_hy + RG_WIDTH:])


def _evproj(x, mod, w_in):
    tm = TM_EVPROJ
    row = lambda i: (i, 0)
    return pl.pallas_call(
        _evproj_kernel,
        grid=(T_ALL // tm,),
        in_specs=[
            pl.BlockSpec((tm, D_MODEL), row),
            pl.BlockSpec((1, 2, 1, D_MODEL), lambda i: (_group_of_tile(i, tm), 0, 0, 0)),
            _resident((D_MODEL, EV_IN)),
        ],
        out_specs=[
            pl.BlockSpec((tm, 3 * HY_WIDTH), row),
            pl.BlockSpec((tm, RG_WIDTH), row),
            pl.BlockSpec((tm, RG_WIDTH), row),
        ],
        out_shape=[
            jax.ShapeDtypeStruct((T_ALL, 3 * HY_WIDTH), F32),
            jax.ShapeDtypeStruct((T_ALL, RG_WIDTH), F32),
            jax.ShapeDtypeStruct((T_ALL, RG_WIDTH), F32),
        ],
        compiler_params=_params("arbitrary"),
        name="even_in_proj",
    )(x, mod, w_in.astype(BF16))


def _block_diag_tiles(w):
    per = RG_CT // RG_BLOCK_DIM
    w = w.reshape(RG_WIDTH // RG_CT, per, RG_BLOCK_DIM, RG_BLOCK_DIM)
    eye = jnp.eye(per, dtype=w.dtype)
    return jnp.einsum('cpde,pq->cpdqe', w, eye).reshape(RG_WIDTH // RG_CT, RG_CT, RG_CT)


def _rg_kernel(rx_ref, cw_ref, cb_ref, w_ref, bias_ref, lam_ref, out_ref,
                padl_ref, padc_ref, xt_f_ref, xt_b_ref, coef_a_ref, coef_b_ref, hbuf_f_ref, hbuf_b_ref):
    neg_lam = -lam_ref[...]
    softplus = jnp.maximum(neg_lam, 0.0) + jnp.log1p(jnp.exp(-jnp.abs(neg_lam)))
    neg_log_a_scale = RG_C * softplus
    a_exp2_scale = -LOG2_E * neg_log_a_scale
    cb = cb_ref[...]
    xts = (xt_f_ref, xt_b_ref)
    hbuf = (hbuf_f_ref, hbuf_b_ref)
    slab_rows = RG_TC + 2 * RG_HALO

    def step_rows(s):
        return slice(s * SUBLANES, (s + 1) * SUBLANES)

    halo = jnp.zeros((BATCH, RG_HALO, RG_CT), F32)
    for pad_ref, length in ((padl_ref, SEQ), (padc_ref, CTX_LEN)):
        pad_ref[:, 0:RG_HALO, :] = halo
        pad_ref[:, RG_HALO + length:, :] = halo
    padl_ref[:, RG_HALO:RG_HALO + SEQ, :] = rx_ref[0:BATCH]
    for b in range(BATCH):
        padc_ref[b, RG_HALO:RG_HALO + CTX_LEN, :] = rx_ref[BATCH, b * CTX_LEN:(b + 1) * CTX_LEN, :]

    def add_out(latent, t_start, d, first_touch):
        t_start = pl.multiple_of(t_start, SUBLANES)
        for b in range(BATCH):
            rows = hbuf[d][pl.ds(b, RG_TC, stride=SUBLANES), :]
            where = (b, pl.ds(t_start, RG_TC)) if latent else (BATCH, pl.ds(b * CTX_LEN + t_start, RG_TC))
            if first_touch:
                out_ref[where[0], where[1], :] = rows
            else:
                out_ref[where[0], where[1], :] += rows

    def sigmoid(z):
        return 1.0 / (1.0 + jnp.exp2(z * (-LOG2_E)))

    def coeffs(latent, length, t0, d, dst_ref):
        t0 = pl.multiple_of(jnp.asarray(t0, jnp.int32), SUBLANES)
        pad_ref = padl_ref if latent else padc_ref
        xt = xts[d]
        for b in range(BATCH):
            xt[pl.ds(b, slab_rows, stride=SUBLANES), :] = pad_ref[b, pl.ds(t0, slab_rows), :]
        x2 = cb
        for k in range(RG_CONV):
            start = (RG_HALO - RG_PAD_L + k) * SUBLANES
            x2 = x2 + xt[start:start + RG_TC * SUBLANES, :] * cw_ref[k:k + 1, :]
        xb = x2.astype(BF16)
        r = sigmoid(_dot(xb, w_ref[0, 2 * d]) + bias_ref[2 * d:2 * d + 1, :])
        i = sigmoid(_dot(xb, w_ref[0, 2 * d + 1]) + bias_ref[2 * d + 1:2 * d + 2, :])
        a = jnp.exp2(r * a_exp2_scale[d:d + 1, :])
        b = jnp.sqrt(jnp.tanh(r * neg_log_a_scale[d:d + 1, :]) * (a * a + 1.0)) * (i * x2)
        dst_ref[d, 0] = a
        dst_ref[d, 1] = b

    def scan(latent, first_touch, tf, tb, src_ref, hf, hb):
        tf = jnp.asarray(tf, jnp.int32)
        tb = jnp.asarray(tb, jnp.int32)
        for s in range(RG_TC):
            rows = step_rows(s)
            hf = src_ref[0, 0, rows, :] * hf + src_ref[0, 1, rows, :]
            hbuf_f_ref[rows, :] = hf
            rows = step_rows(RG_TC - 1 - s)
            hb = src_ref[1, 0, rows, :] * hb + src_ref[1, 1, rows, :]
            hbuf_b_ref[rows, :] = hb
        add_out(latent, tf, 0, first_touch)
        add_out(latent, tb, 1, first_touch)
        return hf, hb

    def sweep(latent, length, h_fwd, h_bwd):
        n_chunks = length // RG_TC
        assert n_chunks % 4 == 0
        last = n_chunks - 1

        def both(j, dst_ref):
            coeffs(latent, length, j * RG_TC, 0, dst_ref)
            coeffs(latent, length, (last - j) * RG_TC, 1, dst_ref)

        def body(first_touch, i, carry):
            j = 2 * i
            both(j + 1, coef_b_ref)
            carry = scan(latent, first_touch, j * RG_TC, (last - j) * RG_TC, coef_a_ref, *carry)
            both(jnp.minimum(j + 2, last), coef_a_ref)
            return scan(latent, first_touch, (j + 1) * RG_TC, (last - j - 1) * RG_TC, coef_b_ref, *carry)

        both(0, coef_a_ref)
        half = n_chunks // 4
        carry = lax.fori_loop(0, half, functools.partial(body, True), (h_fwd, h_bwd))
        return lax.fori_loop(half, 2 * half, functools.partial(body, False), carry)

    zero = jnp.zeros((SUBLANES, RG_CT), F32)
    hf, hb = sweep(False, CTX_LEN, zero, zero)
    sweep(True, SEQ, hf, hb)


def _rglru(rx, conv_w, conv_b, a_w, a_b, x_w, x_b, lam):
    assert T_CTX == SEQ, "the context rows must form exactly one latent-sized row block"
    w = jnp.stack([_block_diag_tiles(a_w[0]), _block_diag_tiles(x_w[0]),
                   _block_diag_tiles(a_w[1]), _block_diag_tiles(x_w[1])], axis=1).astype(BF16)
    bias = jnp.stack([a_b[0], x_b[0], a_b[1], x_b[1]], axis=0)
    n_ct = RG_WIDTH // RG_CT
    n_blocks = BATCH + 1
    chan = lambda c: (0, 0, c)
    lane = lambda c: (0, c)
    coef = pltpu.VMEM((2, 2, RG_TC * BATCH, RG_CT), F32)
    slab = pltpu.VMEM(((RG_TC + 2 * RG_HALO) * BATCH, RG_CT), F32)
    chunk = pltpu.VMEM((RG_TC * BATCH, RG_CT), F32)
    out = pl.pallas_call(
        _rg_kernel,
        grid=(n_ct,),
        in_specs=[
            pl.BlockSpec((n_blocks, SEQ, RG_CT), chan),
            pl.BlockSpec((4, RG_CT), lane),
            pl.BlockSpec((1, RG_CT), lane),
            pl.BlockSpec((1, 4, RG_CT, RG_CT), lambda c: (c, 0, 0, 0)),
            pl.BlockSpec((4, RG_CT), lane),
            pl.BlockSpec((2, RG_CT), lane),
        ],
        out_specs=pl.BlockSpec((n_blocks, SEQ, RG_CT), chan),
        out_shape=jax.ShapeDtypeStruct((n_blocks, SEQ, RG_WIDTH), F32),
        scratch_shapes=[
            pltpu.VMEM((BATCH, SEQ + 2 * RG_HALO, RG_CT), F32),
            pltpu.VMEM((BATCH, CTX_LEN + 2 * RG_HALO, RG_CT), F32),
            slab, slab,
            coef, coef,
            chunk, chunk,
        ],
        compiler_params=_params("arbitrary"),
        name="rglru_scan",
    )(rx.reshape(n_blocks, SEQ, RG_WIDTH), conv_w, conv_b.reshape(1, RG_WIDTH), w, bias, lam)
    return out.reshape(T_ALL, RG_WIDTH)


def _dft_tables(length):
    step = 1 << (length.bit_length() // 2)
    s = np.arange(length, dtype=np.int64)

    def thin(kvec):
        ang = ((kvec[:, None] * s[None, :]) % (2 * length)) * (math.pi / length)
        return (jnp.asarray(np.cos(ang)[:, None, :], dtype=F32),
                jnp.asarray(np.sin(ang)[:, None, :], dtype=F32))

    c_hi, s_hi = thin(np.arange(0, length, step, dtype=np.int64))
    c_lo, s_lo = thin(np.arange(step, dtype=np.int64))
    c_lo, s_lo = c_lo.reshape(1, step, length), s_lo.reshape(1, step, length)
    fc = (c_hi * c_lo - s_hi * s_lo).reshape(length, length)
    fs = (s_hi * c_lo + c_hi * s_lo).reshape(length, length)
    return fc.astype(BF16), fs.astype(BF16)


def _filter_features(length):
    f32 = np.float32
    pos = np.arange(length, dtype=f32)
    t = pos / f32(length)
    bands = np.linspace(1e-4, HY_BANDS - 1, HY_BANDS, dtype=f32)
    ang = (f32(2.0 * math.pi) * pos / f32(length))[:, None] * bands[None, :]
    feats = np.concatenate([t[:, None], np.cos(ang), -np.sin(ang)], axis=-1).astype(f32)
    feats = np.pad(feats, ((0, 0), (0, LANES - HY_EMB)))
    deltas = np.abs(np.linspace(HY_MIN_DECAY, HY_MAX_DECAY, HY_WIDTH, dtype=f32))
    window = (np.exp(-t[:, None] * deltas[None, :]) + f32(HY_SHIFT)).astype(f32)
    return jnp.asarray(feats), jnp.asarray(window)


def _filter_hidden(feat_ref, w1_ref, b1_ref, w2_ref, b2_ref, sf_ref, hid_ref):
    h = jnp.sin(sf_ref[0:1, :] * (_dot(feat_ref[...].astype(BF16), w1_ref[...]) + b1_ref[...]))
    h = jnp.sin(sf_ref[1:2, :] * (_dot(h.astype(BF16), w2_ref[...]) + b2_ref[...]))
    hid_ref[...] = h.astype(BF16)


def _filter_spectrum(hid_ref, win_ref, w3f_ref, w3b_ref, fc_ref, fs_ref, kc_ref, ks_ref, kn_ref, length):
    n = 2 * length
    hb16 = hid_ref[...]
    win = win_ref[...]
    row = lax.broadcasted_iota(jnp.int32, (length, HY_CT), 0)
    h_fwd = _dot(hb16, w3f_ref[...]) * win
    h_bwd0 = jnp.where(row == 0, 0.0, _dot(hb16, w3b_ref[...]) * win)
    h_sum = h_fwd + h_bwd0
    h_dif = h_fwd - h_bwd0
    weight = jnp.where(row == 0, 1.0 / n, 2.0 / n)
    kc_ref[...] = _dot(fc_ref[...], h_sum.astype(BF16)) * weight
    ks_ref[...] = _dot(fs_ref[...], h_dif.astype(BF16)) * weight
    sign = jnp.where((row & 1) == 0, 1.0, -1.0)
    kn_ref[...] = jnp.sum(h_sum * sign, axis=0, keepdims=True) * (1.0 / n)


def _filter_params(w1, b1, w2, b2, w3, sin_freq):
    hp = LANES - HY_FILT_HIDDEN
    w3p = jnp.pad(w3, ((0, hp), (0, 0))).astype(BF16)
    return (jnp.pad(w1, ((0, LANES - HY_EMB), (0, hp))).astype(BF16), jnp.pad(b1, (0, hp)).reshape(1, LANES),
            jnp.pad(w2, ((0, hp), (0, hp))).astype(BF16), jnp.pad(b2, (0, hp)).reshape(1, LANES),
            w3p[:, :HY_WIDTH], w3p[:, HY_WIDTH:], jnp.pad(sin_freq, ((0, 0), (0, hp))))


def _hyena_sequence(raw, cw_ref, cb_ref, bias_ref, fc_ref, fs_ref, kc_ref, ks_ref, kn_ref, length):
    row = lax.broadcasted_iota(jnp.int32, (length, HY_CT), 0)

    def short_conv(x, part):
        prev = jnp.where(row == 0, 0.0, pltpu.roll(x, 1, 0))
        nxt = jnp.where(row == length - 1, 0.0, pltpu.roll(x, length - 1, 0))
        return (prev * cw_ref[part, 0:1, :] + x * cw_ref[part, 1:2, :] + nxt * cw_ref[part, 2:3, :]
                + cb_ref[part:part + 1, :])

    x0 = short_conv(raw[0], 0)
    x1 = short_conv(raw[1], 1)
    v = short_conv(raw[2], 2)
    vx = v * x1
    vb = vx.astype(BF16)
    ft = min(length, HY_FREQ_TILE)
    ps, qs = [], []
    for f0 in range(0, length, ft):
        spec_c = _dot(fc_ref[f0:f0 + ft, :], vb)
        spec_s = _dot(fs_ref[f0:f0 + ft, :], vb)
        kc = kc_ref[f0:f0 + ft, :]
        ks = ks_ref[f0:f0 + ft, :]
        ps.append((spec_c * kc - spec_s * ks).astype(BF16))
        qs.append((spec_c * ks + spec_s * kc).astype(BF16))
    y = _dot(fc_ref[...], jnp.concatenate(ps, axis=0)) + _dot(fs_ref[...], jnp.concatenate(qs, axis=0))
    sign = jnp.where((row & 1) == 0, 1.0, -1.0)
    nyquist = jnp.sum(vx * sign, axis=0, keepdims=True) * kn_ref[...]
    return x0 * (y + sign * nyquist + vx * bias_ref[...])


def _hyena_kernel(x0_ref, x1_ref, v_ref, cw_ref, cb_ref, bias_ref,
                  featl_ref, winl_ref, featc_ref, winc_ref,
                  w1_ref, b1_ref, w2_ref, b2_ref, w3f_ref, w3b_ref, sf_ref,
                  fcl_ref, fsl_ref, fcc_ref, fsc_ref, o_ref,
                  hidl_ref, hidc_ref, kcl_ref, ksl_ref, knl_ref, kcc_ref, ksc_ref, knc_ref):
    step = pl.program_id(1)
    common = (cw_ref, cb_ref, bias_ref)

    @pl.when(jnp.logical_and(pl.program_id(0) == 0, step == 0))
    def _():
        sine_layers = (w1_ref, b1_ref, w2_ref, b2_ref, sf_ref)
        _filter_hidden(featl_ref, *sine_layers, hidl_ref)
        _filter_hidden(featc_ref, *sine_layers, hidc_ref)

    @pl.when(step == 0)
    def _():
        _filter_spectrum(hidl_ref, winl_ref, w3f_ref, w3b_ref, fcl_ref, fsl_ref, kcl_ref, ksl_ref, knl_ref, SEQ)
        _filter_spectrum(hidc_ref, winc_ref, w3f_ref, w3b_ref, fcc_ref, fsc_ref, kcc_ref, ksc_ref, knc_ref, CTX_LEN)

    @pl.when(step < BATCH)
    def _():
        raw = (x0_ref[...], x1_ref[...], v_ref[...])
        o_ref[...] = _hyena_sequence(raw, *common, fcl_ref, fsl_ref, kcl_ref, ksl_ref, knl_ref, SEQ)

    @pl.when(step == BATCH)
    def _():
        for r in range(BATCH):
            rows = slice(r * CTX_LEN, (r + 1) * CTX_LEN)
            raw = (x0_ref[rows, :], x1_ref[rows, :], v_ref[rows, :])
            o_ref[rows, :] = _hyena_sequence(raw, *common, fcc_ref, fsc_ref, kcc_ref, ksc_ref, knc_ref, CTX_LEN)


def _hyena(hy, filter_params, conv_w, conv_b, bias):
    assert T_CTX == SEQ, "the context rows must form exactly one latent-sized row block"
    n_ct = HY_WIDTH // HY_CT
    cw = conv_w.reshape(3, 3, HY_WIDTH).transpose(1, 0, 2)
    cb = conv_b.reshape(3, HY_WIDTH)
    feat_l, win_l = _filter_features(SEQ)
    feat_c, win_c = _filter_features(CTX_LEN)
    tab_l = _dft_tables(SEQ)
    tab_c = _dft_tables(CTX_LEN)

    def part_spec(part):
        return pl.BlockSpec((SEQ, HY_CT), lambda c, b: (b, part * n_ct + c))

    chan = lambda c, b: (0, c)
    once_per_tile = lambda rows: pl.BlockSpec((rows, HY_CT), chan, pipeline_mode=pl.Buffered(1))
    spectrum = lambda rows: pltpu.VMEM((rows, HY_CT), F32)
    return pl.pallas_call(
        _hyena_kernel,
        grid=(n_ct, BATCH + 1),
        in_specs=[
            part_spec(0), part_spec(1), part_spec(2),
            pl.BlockSpec((3, 3, HY_CT), lambda c, b: (0, 0, c)),
            pl.BlockSpec((3, HY_CT), chan),
            pl.BlockSpec((1, HY_CT), chan),
            _resident((SEQ, LANES)), once_per_tile(SEQ),
            _resident((CTX_LEN, LANES)), once_per_tile(CTX_LEN),
            _resident((LANES, LANES)), _resident((1, LANES)), _resident((LANES, LANES)), _resident((1, LANES)),
            once_per_tile(LANES), once_per_tile(LANES), _resident((2, LANES)),
            _resident((SEQ, SEQ)), _resident((SEQ, SEQ)),
            _resident((CTX_LEN, CTX_LEN)), _resident((CTX_LEN, CTX_LEN)),
        ],
        out_specs=pl.BlockSpec((SEQ, HY_CT), lambda c, b: (b, c)),
        out_shape=jax.ShapeDtypeStruct((T_ALL, HY_WIDTH), F32),
        scratch_shapes=[pltpu.VMEM((SEQ, LANES), BF16), pltpu.VMEM((CTX_LEN, LANES), BF16),
                        spectrum(SEQ), spectrum(SEQ), spectrum(1), spectrum(CTX_LEN), spectrum(CTX_LEN), spectrum(1)],
        compiler_params=_params("arbitrary", "arbitrary"),
        name="hyena_conv",
    )(hy, hy, hy, cw, cb, bias.reshape(1, HY_WIDTH), feat_l, win_l, feat_c, win_c, *filter_params, *tab_l, *tab_c)


def _rope_partner(n):
    idx = np.arange(n)
    return np.where((idx % (2 * ROPE_AXIS_PAIRS)) < ROPE_AXIS_PAIRS, idx + ROPE_AXIS_PAIRS, idx - ROPE_AXIS_PAIRS)


def _slot_lanes():
    d = np.arange(MLA_ROPE)
    axis, second, pair = d // (2 * ROPE_AXIS_PAIRS), (d // ROPE_AXIS_PAIRS) % 2, d % ROPE_AXIS_PAIRS
    rope_lane = axis * ROPE_AXIS_PAIRS + pair + (HEAD_SLOT // 2) * second
    free = np.setdiff1d(np.arange(HEAD_SLOT), rope_lane)
    return rope_lane, free[:MLA_NOPE]


def _to_slot(nope, rope):
    rope_lane, nope_lane = _slot_lanes()
    source = [(None, 0)] * HEAD_SLOT
    for arr, lanes in ((nope, nope_lane), (rope, rope_lane)):
        if arr is not None:
            for idx, lane in enumerate(lanes):
                source[lane] = (arr, idx)
    like = nope if nope is not None else rope
    pieces, lane = [], 0
    while lane < HEAD_SLOT:
        arr, start = source[lane]
        run = 1
        while (lane + run < HEAD_SLOT and source[lane + run][0] is arr
               and (arr is None or source[lane + run][1] == start + run)):
            run += 1
        pieces.append(jnp.zeros(like.shape[:-1] + (run,), like.dtype) if arr is None else arr[..., start:start + run])
        lane += run
    return jnp.concatenate(pieces, axis=-1)


def _rope_tables():
    f32 = np.float32
    rope_lane, nope_lane = _slot_lanes()
    rows = np.repeat(np.arange(SEQ // GRID_W, dtype=f32), GRID_W)
    cols = np.tile(np.arange(GRID_W, dtype=f32), SEQ // GRID_W)
    inv_freq = (f32(ROPE_BASE) ** (-np.arange(ROPE_AXIS_PAIRS, dtype=f32) / f32(ROPE_AXIS_PAIRS))).astype(f32)
    ang_r = rows[:, None] * inv_freq
    ang_c = cols[:, None] * inv_freq
    cos32 = np.concatenate([np.cos(ang_r)] * 2 + [np.cos(ang_c)] * 2, axis=1)
    sin32 = np.concatenate([-np.sin(ang_r), np.sin(ang_r), -np.sin(ang_c), np.sin(ang_c)], axis=1)

    def table(n_rows, rope_vals, nope_val):
        t = np.zeros((n_rows, HEAD_SLOT), f32)
        t[:, rope_lane] = rope_vals
        t[:, nope_lane] = nope_val
        return t

    q_cos = table(SEQ, cos32, 1.0) * f32(Q_SCALE)
    q_sin = table(SEQ, sin32, 0.0) * f32(Q_SCALE)
    k_cos = table(SEQ, cos32, 0.0)
    k_sin = table(SEQ, sin32, 0.0)
    ident = table(TM_PROJ, 1.0, 0.0)
    zero = np.zeros((TM_PROJ, HEAD_SLOT), f32)
    tables = ((q_cos, zero), (q_sin, zero), (k_cos, ident), (k_sin, zero))
    return tuple(jnp.asarray(np.concatenate(t, 0).astype(f32)) for t in tables)


def _value_ones():
    assert HEAD_SLOT == 2 * MLA_V
    lane = np.arange(MLA_HEADS * HEAD_SLOT)
    head_is_even = (lane // HEAD_SLOT) % 2 == 0
    upper_half = lane % HEAD_SLOT >= MLA_V
    return jnp.asarray((head_is_even == upper_half).astype(np.float32).reshape(1, -1))


def _mla_weights(w_in, w_q_up, w_kv_up):
    w_q = w_in[:, :MLA_Q_LORA]
    w_kv = w_in[:, MLA_Q_LORA:MLA_Q_LORA + MLA_KV_LORA]
    w_kr = w_in[:, MLA_Q_LORA + MLA_KV_LORA:]
    w_kr2 = jnp.concatenate([_to_slot(None, w_kr), _to_slot(None, w_kr[:, _rope_partner(MLA_ROPE)])], axis=1)

    qh = w_q_up.reshape(MLA_Q_LORA, MLA_HEADS, MLA_NOPE + MLA_ROPE)
    wq_slot = _to_slot(qh[..., :MLA_NOPE], qh[..., MLA_NOPE:]).reshape(MLA_Q_LORA, MLA_HEADS * HEAD_SLOT)

    kvh = w_kv_up.reshape(MLA_KV_LORA, MLA_HEADS, MLA_NOPE + MLA_V)
    wk_slot = _to_slot(kvh[..., :MLA_NOPE], None).reshape(MLA_KV_LORA, MLA_HEADS * HEAD_SLOT)
    even = (jnp.arange(MLA_HEADS) % 2 == 0).astype(F32)[None, :, None]
    wv = kvh[..., MLA_NOPE:]
    wv_slot = jnp.concatenate([wv * even, wv * (1.0 - even)], axis=-1).reshape(MLA_KV_LORA, MLA_HEADS * HEAD_SLOT)
    return [w.astype(BF16) for w in (w_q, w_kv, w_kr2, wq_slot, wk_slot, wv_slot)]


def _mlaproj_kernel(x_ref, mod_ref, wq_ref, wkv_ref, wkr_ref, wqs_ref, wks_ref, wv_ref,
                    vone_ref, qg_ref, kvg_ref, qcos_ref, qsin_ref, kcos_ref, ksin_ref,
                    q_ref, k_ref, v_ref):
    shift, scale = mod_ref[0, 0], mod_ref[0, 1]
    h = (x_ref[...] * (1.0 + scale) + shift).astype(BF16)

    def rmsnorm(y, g):
        return (y * lax.rsqrt(jnp.mean(y * y, axis=-1, keepdims=True) + RMS_EPS) * g).astype(BF16)

    qn = rmsnorm(_dot(h, wq_ref[...]), qg_ref[...])
    kvn = rmsnorm(_dot(h, wkv_ref[...]), kvg_ref[...])
    kr2 = _dot(h, wkr_ref[...])
    k_rope = kr2[:, :HEAD_SLOT] * kcos_ref[...] + kr2[:, HEAD_SLOT:] * ksin_ref[...]
    k_nope = _dot(kvn, wks_ref[...])
    v_ref[...] = (_dot(kvn, wv_ref[...]) + vone_ref[...]).astype(BF16)
    q_all = _dot(qn, wqs_ref[...])
    q_cos = qcos_ref[...]
    q_sin = qsin_ref[...]
    for hd in range(MLA_HEADS):
        sl = slice(hd * HEAD_SLOT, (hd + 1) * HEAD_SLOT)
        qh = q_all[:, sl]
        rotated = qh * q_cos + pltpu.roll(qh, HEAD_SLOT // 2, 1) * q_sin
        q_ref[:, sl] = rotated.astype(BF16)
        k_ref[:, sl] = (k_nope[:, sl] + k_rope).astype(BF16)


def _mlaproj(x, mod, w_in, q_norm, kv_norm, w_q_up, w_kv_up):
    tm = TM_PROJ
    weights = _mla_weights(w_in, w_q_up, w_kv_up)
    tables = _rope_tables()
    row = lambda i: (i, 0)
    tab = lambda i: (jnp.where(i < T_LAT // tm, i % (SEQ // tm), SEQ // tm), 0)
    wide = MLA_HEADS * HEAD_SLOT
    v_one = _value_ones()
    return pl.pallas_call(
        _mlaproj_kernel,
        grid=(T_ALL // tm,),
        in_specs=[
            pl.BlockSpec((tm, D_MODEL), row),
            pl.BlockSpec((1, 2, 1, D_MODEL), lambda i: (_group_of_tile(i, tm), 0, 0, 0)),
            *[_resident(w.shape) for w in weights],
            _resident((1, wide)),
            _resident((1, MLA_Q_LORA)),
            _resident((1, MLA_KV_LORA)),
            *[pl.BlockSpec((tm, HEAD_SLOT), tab) for _ in tables],
        ],
        out_specs=[pl.BlockSpec((tm, wide), row)] * 3,
        out_shape=[jax.ShapeDtypeStruct((T_ALL, wide), BF16)] * 3,
        compiler_params=_params("arbitrary"),
        name="mla_proj",
    )(x, mod, *weights, v_one, q_norm.reshape(1, MLA_Q_LORA), kv_norm.reshape(1, MLA_KV_LORA), *tables)


def _attn_kernel(q_ref, kl_ref, kc_ref, vl_ref, vc_ref, o_ref):
    low = lax.broadcasted_iota(jnp.int32, (TQ, HEAD_SLOT), 1) < MLA_V
    pair_out = []
    for hd in range(ATTN_HEADS):
        sl = slice(hd * HEAD_SLOT, (hd + 1) * HEAD_SLOT)
        q = q_ref[:, sl]
        s_lat = _dot_nt(q, kl_ref[:, sl])
        s_ctx = _dot_nt(q, kc_ref[:, sl])
        m = jnp.maximum(jnp.max(s_lat, axis=-1, keepdims=True), jnp.max(s_ctx, axis=-1, keepdims=True))
        p_lat = jnp.exp2(s_lat - m).astype(BF16)
        p_ctx = jnp.exp2(s_ctx - m).astype(BF16)
        pair_out.append(_dot(p_lat, vl_ref[:, sl]) + _dot(p_ctx, vc_ref[:, sl]))
        if hd % 2 == 1:
            even, odd = pair_out
            pair_out = []
            numer = jnp.where(low, even, odd)
            denom = pltpu.roll(jnp.where(low, odd, even), MLA_V, 1)
            pair = hd // 2
            o_ref[:, pair * HEAD_SLOT:(pair + 1) * HEAD_SLOT] = (numer / denom).astype(BF16)


def _attention(q, k, v):
    n_q = SEQ // TQ
    n_groups = MLA_HEADS // ATTN_HEADS
    ctx_block0 = T_LAT // CTX_LEN
    wide = ATTN_HEADS * HEAD_SLOT
    lat = lambda b, h, i: (b, h)
    ctx = lambda b, h, i: (ctx_block0 + b, h)
    return pl.pallas_call(
        _attn_kernel,
        grid=(BATCH, n_groups, n_q),
        in_specs=[
            pl.BlockSpec((TQ, wide), lambda b, h, i: (b * n_q + i, h)),
            pl.BlockSpec((SEQ, wide), lat),
            pl.BlockSpec((CTX_LEN, wide), ctx),
            pl.BlockSpec((SEQ, wide), lat),
            pl.BlockSpec((CTX_LEN, wide), ctx),
        ],
        out_specs=pl.BlockSpec((TQ, ATTN_HEADS * MLA_V), lambda b, h, i: (b * n_q + i, h)),
        out_shape=jax.ShapeDtypeStruct((T_LAT, MLA_HEADS * MLA_V), BF16),
        compiler_params=_params("arbitrary", "arbitrary", "arbitrary"),
        name="mla_attention",
    )(q, k, k, v, v)


def kernel(x, c, ctx, c_ctx, ada_w, ada_b, ln_g, ln_b, ffn_w_in, ffn_w_out, ev_w_in, ev_w_out, hy_conv_w, hy_conv_b, hy_filt_w1, hy_filt_b1, hy_filt_w2, hy_filt_b2, hy_filt_w3, hy_sin_freq, hy_bias, rg_conv_w, rg_conv_b, rg_a_w, rg_a_b, rg_x_w, rg_x_b, rg_lambda, mla_w_in, mla_q_norm, mla_kv_norm, mla_w_q_up, mla_w_kv_up, mla_w_out):
    assert x.shape == (BATCH, SEQ, D_MODEL) and ctx.shape == (BATCH, CTX_LEN, D_MODEL)
    cond = jnp.concatenate([c, c_ctx[None], jnp.zeros((2 * SUBLANES - N_GROUPS, D_MODEL), F32)], axis=0)
    mods = _adaln(cond, ada_w, ada_b)[:, :N_GROUPS].reshape(DEPTH, N_GROUPS, N_ADA, 1, D_MODEL)
    ffn_w = (ffn_w_in, ffn_w_out)

    m = mods[0]
    xs = _ffn_split(x.reshape(T_LAT, D_MODEL), ctx.reshape(T_CTX, D_MODEL),
                    m[:, 0:3], ffn_w, (0, 0), ln_g[0, 0], ln_b[0, 0], T_ALL)
    hy, rx, gate_in = _evproj(xs, m[:, 3:5], ev_w_in[0])
    h_sum = _rglru(rx, rg_conv_w[0], rg_conv_b[0], rg_a_w[0], rg_a_b[0], rg_x_w[0], rg_x_b[0], rg_lambda[0])
    filt = _filter_params(hy_filt_w1[0], hy_filt_b1[0], hy_filt_w2[0], hy_filt_b2[0], hy_filt_w3[0], hy_sin_freq[0])
    y_hy = _hyena(hy, filt, hy_conv_w[0], hy_conv_b[0], hy_bias[0])
    xs = _ffn_after_even(xs, y_hy, h_sum, gate_in, ev_w_out[0], ln_g[0, 1], ln_b[0, 1],
                         m[:, 5:9], ffn_w, (0, 1), ln_g[0, 2], ln_b[0, 2], T_ALL)

    m = mods[1]
    xs = _ffn_plain(xs, m[:, 0:3], ffn_w, (1, 0), ln_g[1, 0], ln_b[1, 0], T_ALL)
    q, k, v = _mlaproj(xs, m[:, 3:5], mla_w_in[0], mla_q_norm[0], mla_kv_norm[0], mla_w_q_up[0], mla_w_kv_up[0])
    att = _attention(q, k, v)
    xl = _ffn_after_mla(xs, att, mla_w_out[0], ln_g[1, 1], ln_b[1, 1],
                        m[:, 5:9], ffn_w, (1, 1), ln_g[1, 2], ln_b[1, 2], T_LAT)
    return xl.reshape(BATCH, SEQ, D_MODEL)
```

```python
import functools
import math

import jax
import jax.numpy as jnp
import numpy as np
from jax import lax
from jax.experimental import pallas as pl
from jax.experimental.pallas import tpu as pltpu

F32 = jnp.float32
BF16 = jnp.bfloat16

D_MODEL = 1024
BATCH = 8
SEQ = 2048
DEPTH = 2
CTX_LEN = 256
GRID_W = 64
N_ADA = 9
D_FF = 2816

HY_WIDTH = 512
RG_WIDTH = 512
EV_IN = 3 * HY_WIDTH + 2 * RG_WIDTH
HY_EMB = 33
HY_BANDS = (HY_EMB - 1) // 2
HY_FILT_HIDDEN = 64
HY_TARGET = 1e-2
HY_MAX_DECAY = math.log(HY_TARGET) / 0.3
HY_MIN_DECAY = math.log(HY_TARGET) / 1.5
HY_SHIFT = 0.05
RG_BLOCKS = 8
RG_BLOCK_DIM = RG_WIDTH // RG_BLOCKS
RG_C = 8.0
RG_CONV = 4
RG_PAD_L = 2

MLA_HEADS = 16
MLA_Q_LORA = 768
MLA_KV_LORA = 256
MLA_NOPE = 64
MLA_ROPE = 32
MLA_V = 64
MLA_SCALE = (MLA_NOPE + MLA_ROPE) ** -0.5
ROPE_AXIS_PAIRS = MLA_ROPE // 4
ROPE_BASE = 10000.0

LOG2_E = math.log2(math.e)
ALPHA = (2.0 * DEPTH) ** 0.25
LN_EPS = 1e-6
RMS_EPS = 1e-6

T_LAT = BATCH * SEQ
T_CTX = BATCH * CTX_LEN
T_ALL = T_LAT + T_CTX
N_GROUPS = BATCH + 1

LANES = 128
SUBLANES = 8
VMEM_LIMIT = 56 * 1024 * 1024

TM_FFN = 512
FF_CHUNK = 256
N_FF_CHUNKS = D_FF // FF_CHUNK
FFN_WEIGHT_CHUNKS = 8
TM_PROJ = 512
TM_EVPROJ = 1024
HY_CT = 256
HY_FREQ_TILE = 512
RG_CT = 128
RG_TC = 32
RG_HALO = SUBLANES
HEAD_SLOT = 128
TQ = 512
ATTN_HEADS = 16
Q_SCALE = MLA_SCALE * math.log2(math.e)
ADA_TN = 2304


def _dot(a, b):
    return jnp.dot(a, b, preferred_element_type=F32)


def _dot_nt(a, b):
    return lax.dot_general(a, b, (((1,), (1,)), ((), ())), preferred_element_type=F32)


def _resident(shape):
    nd = len(shape)
    return pl.BlockSpec(shape, lambda *_: (0,) * nd, pipeline_mode=pl.Buffered(1))


def _group_of_tile(i, tm):
    return jnp.where(i < T_LAT // tm, i // (SEQ // tm), BATCH)


def _params(*sem):
    return pltpu.CompilerParams(dimension_semantics=sem, vmem_limit_bytes=VMEM_LIMIT)


def _layernorm(z, g, b):
    mu = jnp.mean(z, axis=-1, keepdims=True)
    zc = z - mu
    var = jnp.mean(zc * zc, axis=-1, keepdims=True)
    return zc * lax.rsqrt(var + LN_EPS) * g + b


def _adaln_kernel(c_ref, w_ref, b_ref, o_ref):
    c = c_ref[...]
    s = (c * jax.nn.sigmoid(c)).astype(BF16)
    o_ref[0] = _dot(s, w_ref[0].astype(BF16)) + b_ref[0]


def _adaln(cond, ada_w, ada_b):
    n_out = N_ADA * D_MODEL
    rows = cond.shape[0]
    return pl.pallas_call(
        _adaln_kernel,
        grid=(DEPTH, n_out // ADA_TN),
        in_specs=[
            pl.BlockSpec((rows, D_MODEL), lambda l, j: (0, 0)),
            pl.BlockSpec((1, D_MODEL, ADA_TN), lambda l, j: (l, 0, j)),
            pl.BlockSpec((1, 1, ADA_TN), lambda l, j: (l, 0, j)),
        ],
        out_specs=pl.BlockSpec((1, rows, ADA_TN), lambda l, j: (l, 0, j)),
        out_shape=jax.ShapeDtypeStruct((DEPTH, rows, n_out), F32),
        compiler_params=_params("arbitrary", "arbitrary"),
        name="adaln",
    )(cond, ada_w, ada_b.reshape(DEPTH, 1, n_out))


def _ffn_body(x, mod_ref, wi_ref, wo_ref, g_ref, b_ref, o_ref, act_ref):
    n_slots = mod_ref.shape[1]
    shift, scale, gate = mod_ref[0, n_slots - 3], mod_ref[0, n_slots - 2], mod_ref[0, n_slots - 1]
    h = (x * (1.0 + scale) + shift).astype(BF16)
    for j in range(N_FF_CHUNKS):
        lo = j * FF_CHUNK
        a = _dot(h, wi_ref[:, lo:lo + FF_CHUNK])
        u = _dot(h, wi_ref[:, D_FF + lo:D_FF + lo + FF_CHUNK])
        act_ref[:, lo:lo + FF_CHUNK] = (a * jax.nn.sigmoid(a) * u).astype(BF16)
    y = _dot(act_ref[...], wo_ref[...])
    o_ref[...] = _layernorm(ALPHA * x + (0.5 * gate) * y, g_ref[...], b_ref[...])


def _load_weight_bf16(w_hbm, which, dst_ref, stage_ref, sem):
    layer, half = which
    rows = stage_ref.shape[1]
    n_chunks = dst_ref.shape[0] // rows

    def copy(c):
        return pltpu.make_async_copy(w_hbm.at[layer, half, pl.ds(c * rows, rows)], stage_ref.at[c % 2], sem.at[c % 2])

    copy(0).start()
    for c in range(n_chunks):
        if c + 1 < n_chunks:
            copy(c + 1).start()
        copy(c).wait()
        dst_ref[c * rows:(c + 1) * rows, :] = stage_ref[c % 2].astype(BF16)


def _ffn_steps(x_fn, which, mod_ref, wi_hbm, wo_hbm, g_ref, b_ref, o_ref,
               act_ref, wi_ref, wo_ref, stage_in_ref, stage_out_ref, sem_in, sem_out):
    @pl.when(pl.program_id(0) == 0)
    def _():
        _load_weight_bf16(wi_hbm, which, wi_ref, stage_in_ref, sem_in)
        _load_weight_bf16(wo_hbm, which, wo_ref, stage_out_ref, sem_out)

    _ffn_body(x_fn(), mod_ref, wi_ref, wo_ref, g_ref, b_ref, o_ref, act_ref)


def _ffn_kernel(x_ref, *rest, which):
    _ffn_steps(lambda: x_ref[...], which, *rest)


def _ffn_split_kernel(xl_ref, xc_ref, *rest, which):
    _ffn_steps(lambda: jnp.where(pl.program_id(0) < T_LAT // TM_FFN, xl_ref[...], xc_ref[...]), which, *rest)


def _ffn_after_even_kernel(x_ref, yh_ref, hs_ref, gt_ref, wmix_ref, g1_ref, b1_ref, mod_ref, *rest, which):
    def x_fn():
        y_rg = hs_ref[...] * jax.nn.gelu(gt_ref[...], approximate=True)
        y = (_dot(yh_ref[...].astype(BF16), wmix_ref[:HY_WIDTH, :])
             + _dot(y_rg.astype(BF16), wmix_ref[HY_WIDTH:, :]))
        return _layernorm(ALPHA * x_ref[...] + mod_ref[0, 0] * y, g1_ref[...], b1_ref[...])

    _ffn_steps(x_fn, which, mod_ref, *rest)


def _ffn_after_mla_kernel(x_ref, att_ref, wmix_ref, g1_ref, b1_ref, mod_ref, *rest, which):
    def x_fn():
        y = _dot(att_ref[...], wmix_ref[...])
        return _layernorm(ALPHA * x_ref[...] + mod_ref[0, 0] * y, g1_ref[...], b1_ref[...])

    _ffn_steps(x_fn, which, mod_ref, *rest)


def _ffn(body, lead_args, lead_specs, mod, ffn_w, which, g, b, n_rows):
    tm = TM_FFN
    w_in_all, w_out_all = ffn_w
    return pl.pallas_call(
        functools.partial(body, which=which),
        grid=(n_rows // tm,),
        in_specs=[
            *lead_specs,
            pl.BlockSpec((1, mod.shape[1], 1, D_MODEL), lambda i: (_group_of_tile(i, tm), 0, 0, 0)),
            pl.BlockSpec(memory_space=pl.ANY),
            pl.BlockSpec(memory_space=pl.ANY),
            _resident((1, D_MODEL)),
            _resident((1, D_MODEL)),
        ],
        out_specs=pl.BlockSpec((tm, D_MODEL), lambda i: (i, 0)),
        out_shape=jax.ShapeDtypeStruct((n_rows, D_MODEL), F32),
        scratch_shapes=[
            pltpu.VMEM((tm, D_FF), BF16),
            pltpu.VMEM((D_MODEL, 2 * D_FF), BF16),
            pltpu.VMEM((D_FF, D_MODEL), BF16),
            pltpu.VMEM((2, D_MODEL // FFN_WEIGHT_CHUNKS, 2 * D_FF), F32),
            pltpu.VMEM((2, D_FF // FFN_WEIGHT_CHUNKS, D_MODEL), F32),
            pltpu.SemaphoreType.DMA((2,)),
            pltpu.SemaphoreType.DMA((2,)),
        ],
        compiler_params=_params("arbitrary"),
        name="half_ffn",
    )(*lead_args, mod, w_in_all, w_out_all, g.reshape(1, D_MODEL), b.reshape(1, D_MODEL))


def _row_spec(width, tm=TM_FFN):
    return pl.BlockSpec((tm, width), lambda i: (i, 0))


def _ffn_plain(x, *args):
    return _ffn(_ffn_kernel, (x,), [_row_spec(D_MODEL)], *args)


def _ffn_split(x_lat, x_ctx, *args):
    n_lat = T_LAT // TM_FFN
    specs = [pl.BlockSpec((TM_FFN, D_MODEL), lambda i: (jnp.minimum(i, n_lat - 1), 0)),
             pl.BlockSpec((TM_FFN, D_MODEL), lambda i: (jnp.maximum(i - n_lat, 0), 0))]
    return _ffn(_ffn_split_kernel, (x_lat, x_ctx), specs, *args)


def _ffn_after_even(x, y_hy, h_sum, gate_in, w_mix, g1, b1, *args):
    lead = (x, y_hy, h_sum, gate_in, w_mix.astype(BF16), g1.reshape(1, D_MODEL), b1.reshape(1, D_MODEL))
    specs = [_row_spec(D_MODEL), _row_spec(HY_WIDTH), _row_spec(RG_WIDTH), _row_spec(RG_WIDTH),
             _resident((HY_WIDTH + RG_WIDTH, D_MODEL)), _resident((1, D_MODEL)), _resident((1, D_MODEL))]
    return _ffn(_ffn_after_even_kernel, lead, specs, *args)


def _ffn_after_mla(x, att, w_mix, g1, b1, *args):
    lead = (x, att, w_mix.astype(BF16), g1.reshape(1, D_MODEL), b1.reshape(1, D_MODEL))
    specs = [_row_spec(D_MODEL), _row_spec(MLA_HEADS * MLA_V),
             _resident((MLA_HEADS * MLA_V, D_MODEL)), _resident((1, D_MODEL)), _resident((1, D_MODEL))]
    return _ffn(_ffn_after_mla_kernel, lead, specs, *args)


def _evproj_kernel(x_ref, mod_ref, w_ref, hy_ref, rx_ref, gt_ref):
    shift, scale = mod_ref[0, 0], mod_ref[0, 1]
    h = (x_ref[...] * (1.0 + scale) + shift).astype(BF16)
    n_hy = 3 * HY_WIDTH
    hy_ref[...] = _dot(h, w_ref[:, :n_hy])
    rx_ref[...] = _dot(h, w_ref[:, n_hy:n_hy + RG_WIDTH])
    gt_ref[...] = _dot(h, w_ref[:, n_hy + RG_WIDTH:])


def _ffn_split_evproj_kernel(xl_ref, xc_ref, mod_ref, wi_hbm, wo_hbm, g_ref, b_ref, mod2_ref, wev_ref,
                             o_ref, hy_ref, rx_ref, gt_ref,
                             act_ref, wi_ref, wo_ref, stage_in_ref, stage_out_ref, sem_in, sem_out, *, which):
    x_fn = lambda: jnp.where(pl.program_id(0) < T_LAT // TM_FFN, xl_ref[...], xc_ref[...])
    _ffn_steps(x_fn, which, mod_ref, wi_hbm, wo_hbm, g_ref, b_ref, o_ref,
               act_ref, wi_ref, wo_ref, stage_in_ref, stage_out_ref, sem_in, sem_out)
    _evproj_kernel(o_ref, mod2_ref, wev_ref, hy_ref, rx_ref, gt_ref)


def _ffn_split_evproj(x_lat, x_ctx, mod_ffn, mod_proj, ffn_w, which, g, b, w_ev):
    tm = TM_FFN
    n_lat = T_LAT // tm
    row = lambda i: (i, 0)
    group = lambda i: (_group_of_tile(i, tm), 0, 0, 0)
    f32_rows = lambda width: jax.ShapeDtypeStruct((T_ALL, width), F32)
    return pl.pallas_call(
        functools.partial(_ffn_split_evproj_kernel, which=which),
        grid=(T_ALL // tm,),
        in_specs=[
            pl.BlockSpec((tm, D_MODEL), lambda i: (jnp.minimum(i, n_lat - 1), 0)),
            pl.BlockSpec((tm, D_MODEL), lambda i: (jnp.maximum(i - n_lat, 0), 0)),
            pl.BlockSpec((1, 3, 1, D_MODEL), group),
            pl.BlockSpec(memory_space=pl.ANY),
            pl.BlockSpec(memory_space=pl.ANY),
            _resident((1, D_MODEL)),
            _resident((1, D_MODEL)),
            pl.BlockSpec((1, 2, 1, D_MODEL), group),
            _resident((D_MODEL, EV_IN)),
        ],
        out_specs=[pl.BlockSpec((tm, D_MODEL), row), pl.BlockSpec((tm, 3 * HY_WIDTH), row),
                   pl.BlockSpec((tm, RG_WIDTH), row), pl.BlockSpec((tm, RG_WIDTH), row)],
        out_shape=[f32_rows(D_MODEL), f32_rows(3 * HY_WIDTH), f32_rows(RG_WIDTH), f32_rows(RG_WIDTH)],
        scratch_shapes=[
            pltpu.VMEM((tm, D_FF), BF16),
            pltpu.VMEM((D_MODEL, 2 * D_FF), BF16),
            pltpu.VMEM((D_FF, D_MODEL), BF16),
            pltpu.VMEM((2, D_MODEL // (2 * FFN_WEIGHT_CHUNKS), 2 * D_FF), F32),
            pltpu.VMEM((2, D_FF // (2 * FFN_WEIGHT_CHUNKS), D_MODEL), F32),
            pltpu.SemaphoreType.DMA((2,)),
            pltpu.SemaphoreType.DMA((2,)),
        ],
        compiler_params=_params("arbitrary"),
        name="half_ffn_even_in_proj",
    )(x_lat, x_ctx, mod_ffn, ffn_w[0], ffn_w[1], g.reshape(1, D_MODEL), b.reshape(1, D_MODEL),
      mod_proj, w_ev.astype(BF16))


def _evproj(x, mod, w_in):
    tm = TM_EVPROJ
    row = lambda i: (i, 0)
    return pl.pallas_call(
        _evproj_kernel,
        grid=(T_ALL // tm,),
        in_specs=[
            pl.BlockSpec((tm, D_MODEL), row),
            pl.BlockSpec((1, 2, 1, D_MODEL), lambda i: (_group_of_tile(i, tm), 0, 0, 0)),
            _resident((D_MODEL, EV_IN)),
        ],
        out_specs=[
            pl.BlockSpec((tm, 3 * HY_WIDTH), row),
            pl.BlockSpec((tm, RG_WIDTH), row),
            pl.BlockSpec((tm, RG_WIDTH), row),
        ],
        out_shape=[
            jax.ShapeDtypeStruct((T_ALL, 3 * HY_WIDTH), F32),
            jax.ShapeDtypeStruct((T_ALL, RG_WIDTH), F32),
            jax.ShapeDtypeStruct((T_ALL, RG_WIDTH), F32),
        ],
        compiler_params=_params("arbitrary"),
        name="even_in_proj",
    )(x, mod, w_in.astype(BF16))


def _block_diag_tiles(w):
    per = RG_CT // RG_BLOCK_DIM
    w = w.reshape(RG_WIDTH // RG_CT, per, RG_BLOCK_DIM, RG_BLOCK_DIM)
    eye = jnp.eye(per, dtype=w.dtype)
    return jnp.einsum('cpde,pq->cpdqe', w, eye).reshape(RG_WIDTH // RG_CT, RG_CT, RG_CT)


def _rg_kernel(rx_ref, cw_ref, cb_ref, w_ref, bias_ref, lam_ref, out_ref,
                padl_ref, padc_ref, xt_f_ref, xt_b_ref, coef_a_ref, coef_b_ref, hbuf_f_ref, hbuf_b_ref):
    neg_lam = -lam_ref[...]
    softplus = jnp.maximum(neg_lam, 0.0) + jnp.log1p(jnp.exp(-jnp.abs(neg_lam)))
    neg_log_a_scale = RG_C * softplus
    a_exp2_scale = -LOG2_E * neg_log_a_scale
    cb = cb_ref[...]
    xts = (xt_f_ref, xt_b_ref)
    hbuf = (hbuf_f_ref, hbuf_b_ref)
    slab_rows = RG_TC + 2 * RG_HALO

    def step_rows(s):
        return slice(s * SUBLANES, (s + 1) * SUBLANES)

    halo = jnp.zeros((BATCH, RG_HALO, RG_CT), F32)
    for pad_ref, length in ((padl_ref, SEQ), (padc_ref, CTX_LEN)):
        pad_ref[:, 0:RG_HALO, :] = halo
        pad_ref[:, RG_HALO + length:, :] = halo
    padl_ref[:, RG_HALO:RG_HALO + SEQ, :] = rx_ref[0:BATCH]
    for b in range(BATCH):
        padc_ref[b, RG_HALO:RG_HALO + CTX_LEN, :] = rx_ref[BATCH, b * CTX_LEN:(b + 1) * CTX_LEN, :]

    def add_out(latent, t_start, d, first_touch):
        t_start = pl.multiple_of(t_start, SUBLANES)
        for b in range(BATCH):
            rows = hbuf[d][pl.ds(b, RG_TC, stride=SUBLANES), :]
            where = (b, pl.ds(t_start, RG_TC)) if latent else (BATCH, pl.ds(b * CTX_LEN + t_start, RG_TC))
            if first_touch:
                out_ref[where[0], where[1], :] = rows
            else:
                out_ref[where[0], where[1], :] += rows

    def sigmoid(z):
        return 1.0 / (1.0 + jnp.exp2(z * (-LOG2_E)))

    def coeffs(latent, length, t0, d, dst_ref):
        t0 = pl.multiple_of(jnp.asarray(t0, jnp.int32), SUBLANES)
        pad_ref = padl_ref if latent else padc_ref
        xt = xts[d]
        for b in range(BATCH):
            xt[pl.ds(b, slab_rows, stride=SUBLANES), :] = pad_ref[b, pl.ds(t0, slab_rows), :]
        x2 = cb
        for k in range(RG_CONV):
            start = (RG_HALO - RG_PAD_L + k) * SUBLANES
            x2 = x2 + xt[start:start + RG_TC * SUBLANES, :] * cw_ref[k:k + 1, :]
        xb = x2.astype(BF16)
        r = sigmoid(_dot(xb, w_ref[0, 2 * d]) + bias_ref[2 * d:2 * d + 1, :])
        i = sigmoid(_dot(xb, w_ref[0, 2 * d + 1]) + bias_ref[2 * d + 1:2 * d + 2, :])
        a = jnp.exp2(r * a_exp2_scale[d:d + 1, :])
        b = jnp.sqrt(jnp.tanh(r * neg_log_a_scale[d:d + 1, :]) * (a * a + 1.0)) * (i * x2)
        dst_ref[d, 0] = a
        dst_ref[d, 1] = b

    def scan(latent, first_touch, tf, tb, src_ref, hf, hb):
        tf = jnp.asarray(tf, jnp.int32)
        tb = jnp.asarray(tb, jnp.int32)
        for s in range(RG_TC):
            rows = step_rows(s)
            hf = src_ref[0, 0, rows, :] * hf + src_ref[0, 1, rows, :]
            hbuf_f_ref[rows, :] = hf
            rows = step_rows(RG_TC - 1 - s)
            hb = src_ref[1, 0, rows, :] * hb + src_ref[1, 1, rows, :]
            hbuf_b_ref[rows, :] = hb
        add_out(latent, tf, 0, first_touch)
        add_out(latent, tb, 1, first_touch)
        return hf, hb

    def sweep(latent, length, h_fwd, h_bwd):
        n_chunks = length // RG_TC
        assert n_chunks % 4 == 0
        last = n_chunks - 1

        def both(j, dst_ref):
            coeffs(latent, length, j * RG_TC, 0, dst_ref)
            coeffs(latent, length, (last - j) * RG_TC, 1, dst_ref)

        def body(first_touch, i, carry):
            j = 2 * i
            both(j + 1, coef_b_ref)
            carry = scan(latent, first_touch, j * RG_TC, (last - j) * RG_TC, coef_a_ref, *carry)
            both(jnp.minimum(j + 2, last), coef_a_ref)
            return scan(latent, first_touch, (j + 1) * RG_TC, (last - j - 1) * RG_TC, coef_b_ref, *carry)

        both(0, coef_a_ref)
        half = n_chunks // 4
        carry = lax.fori_loop(0, half, functools.partial(body, True), (h_fwd, h_bwd))
        return lax.fori_loop(half, 2 * half, functools.partial(body, False), carry)

    zero = jnp.zeros((SUBLANES, RG_CT), F32)
    hf, hb = sweep(False, CTX_LEN, zero, zero)
    sweep(True, SEQ, hf, hb)


def _rglru(rx, conv_w, conv_b, a_w, a_b, x_w, x_b, lam):
    assert T_CTX == SEQ, "the context rows must form exactly one latent-sized row block"
    w = jnp.stack([_block_diag_tiles(a_w[0]), _block_diag_tiles(x_w[0]),
                   _block_diag_tiles(a_w[1]), _block_diag_tiles(x_w[1])], axis=1).astype(BF16)
    bias = jnp.stack([a_b[0], x_b[0], a_b[1], x_b[1]], axis=0)
    n_ct = RG_WIDTH // RG_CT
    n_blocks = BATCH + 1
    chan = lambda c: (0, 0, c)
    lane = lambda c: (0, c)
    coef = pltpu.VMEM((2, 2, RG_TC * BATCH, RG_CT), F32)
    slab = pltpu.VMEM(((RG_TC + 2 * RG_HALO) * BATCH, RG_CT), F32)
    chunk = pltpu.VMEM((RG_TC * BATCH, RG_CT), F32)
    out = pl.pallas_call(
        _rg_kernel,
        grid=(n_ct,),
        in_specs=[
            pl.BlockSpec((n_blocks, SEQ, RG_CT), chan),
            pl.BlockSpec((4, RG_CT), lane),
            pl.BlockSpec((1, RG_CT), lane),
            pl.BlockSpec((1, 4, RG_CT, RG_CT), lambda c: (c, 0, 0, 0)),
            pl.BlockSpec((4, RG_CT), lane),
            pl.BlockSpec((2, RG_CT), lane),
        ],
        out_specs=pl.BlockSpec((n_blocks, SEQ, RG_CT), chan),
        out_shape=jax.ShapeDtypeStruct((n_blocks, SEQ, RG_WIDTH), F32),
        scratch_shapes=[
            pltpu.VMEM((BATCH, SEQ + 2 * RG_HALO, RG_CT), F32),
            pltpu.VMEM((BATCH, CTX_LEN + 2 * RG_HALO, RG_CT), F32),
            slab, slab,
            coef, coef,
            chunk, chunk,
        ],
        compiler_params=_params("arbitrary"),
        name="rglru_scan",
    )(rx.reshape(n_blocks, SEQ, RG_WIDTH), conv_w, conv_b.reshape(1, RG_WIDTH), w, bias, lam)
    return out.reshape(T_ALL, RG_WIDTH)


def _dft_tables(length):
    step = 1 << (length.bit_length() // 2)
    s = np.arange(length, dtype=np.int64)

    def thin(kvec):
        ang = ((kvec[:, None] * s[None, :]) % (2 * length)) * (math.pi / length)
        return (jnp.asarray(np.cos(ang)[:, None, :], dtype=F32),
                jnp.asarray(np.sin(ang)[:, None, :], dtype=F32))

    c_hi, s_hi = thin(np.arange(0, length, step, dtype=np.int64))
    c_lo, s_lo = thin(np.arange(step, dtype=np.int64))
    c_lo, s_lo = c_lo.reshape(1, step, length), s_lo.reshape(1, step, length)
    fc = (c_hi * c_lo - s_hi * s_lo).reshape(length, length)
    fs = (s_hi * c_lo + c_hi * s_lo).reshape(length, length)
    return fc.astype(BF16), fs.astype(BF16)


def _filter_features(length):
    f32 = np.float32
    pos = np.arange(length, dtype=f32)
    t = pos / f32(length)
    bands = np.linspace(1e-4, HY_BANDS - 1, HY_BANDS, dtype=f32)
    ang = (f32(2.0 * math.pi) * pos / f32(length))[:, None] * bands[None, :]
    feats = np.concatenate([t[:, None], np.cos(ang), -np.sin(ang)], axis=-1).astype(f32)
    feats = np.pad(feats, ((0, 0), (0, LANES - HY_EMB)))
    deltas = np.abs(np.linspace(HY_MIN_DECAY, HY_MAX_DECAY, HY_WIDTH, dtype=f32))
    window = (np.exp(-t[:, None] * deltas[None, :]) + f32(HY_SHIFT)).astype(f32)
    return jnp.asarray(feats), jnp.asarray(window)


def _filter_hidden(feat_ref, w1_ref, b1_ref, w2_ref, b2_ref, sf_ref, hid_ref):
    h = jnp.sin(sf_ref[0:1, :] * (_dot(feat_ref[...].astype(BF16), w1_ref[...]) + b1_ref[...]))
    h = jnp.sin(sf_ref[1:2, :] * (_dot(h.astype(BF16), w2_ref[...]) + b2_ref[...]))
    hid_ref[...] = h.astype(BF16)


def _filter_spectrum(hid_ref, win_ref, w3f_ref, w3b_ref, fc_ref, fs_ref, kc_ref, ks_ref, kn_ref, length):
    n = 2 * length
    hb16 = hid_ref[...]
    win = win_ref[...]
    row = lax.broadcasted_iota(jnp.int32, (length, HY_CT), 0)
    h_fwd = _dot(hb16, w3f_ref[...]) * win
    h_bwd0 = jnp.where(row == 0, 0.0, _dot(hb16, w3b_ref[...]) * win)
    h_sum = h_fwd + h_bwd0
    h_dif = h_fwd - h_bwd0
    weight = jnp.where(row == 0, 1.0 / n, 2.0 / n)
    kc_ref[...] = _dot(fc_ref[...], h_sum.astype(BF16)) * weight
    ks_ref[...] = _dot(fs_ref[...], h_dif.astype(BF16)) * weight
    sign = jnp.where((row & 1) == 0, 1.0, -1.0)
    kn_ref[...] = jnp.sum(h_sum * sign, axis=0, keepdims=True) * (1.0 / n)


def _filter_params(w1, b1, w2, b2, w3, sin_freq):
    hp = LANES - HY_FILT_HIDDEN
    w3p = jnp.pad(w3, ((0, hp), (0, 0))).astype(BF16)
    return (jnp.pad(w1, ((0, LANES - HY_EMB), (0, hp))).astype(BF16), jnp.pad(b1, (0, hp)).reshape(1, LANES),
            jnp.pad(w2, ((0, hp), (0, hp))).astype(BF16), jnp.pad(b2, (0, hp)).reshape(1, LANES),
            w3p[:, :HY_WIDTH], w3p[:, HY_WIDTH:], jnp.pad(sin_freq, ((0, 0), (0, hp))))


def _hyena_sequence(raw, cw_ref, cb_ref, bias_ref, fc_ref, fs_ref, kc_ref, ks_ref, kn_ref, length):
    row = lax.broadcasted_iota(jnp.int32, (length, HY_CT), 0)

    def short_conv(x, part):
        prev = jnp.where(row == 0, 0.0, pltpu.roll(x, 1, 0))
        nxt = jnp.where(row == length - 1, 0.0, pltpu.roll(x, length - 1, 0))
        return (prev * cw_ref[part, 0:1, :] + x * cw_ref[part, 1:2, :] + nxt * cw_ref[part, 2:3, :]
                + cb_ref[part:part + 1, :])

    x0 = short_conv(raw[0], 0)
    x1 = short_conv(raw[1], 1)
    v = short_conv(raw[2], 2)
    vx = v * x1
    vb = vx.astype(BF16)
    ft = min(length, HY_FREQ_TILE)
    ps, qs = [], []
    for f0 in range(0, length, ft):
        spec_c = _dot(fc_ref[f0:f0 + ft, :], vb)
        spec_s = _dot(fs_ref[f0:f0 + ft, :], vb)
        kc = kc_ref[f0:f0 + ft, :]
        ks = ks_ref[f0:f0 + ft, :]
        ps.append((spec_c * kc - spec_s * ks).astype(BF16))
        qs.append((spec_c * ks + spec_s * kc).astype(BF16))
    y = _dot(fc_ref[...], jnp.concatenate(ps, axis=0)) + _dot(fs_ref[...], jnp.concatenate(qs, axis=0))
    sign = jnp.where((row & 1) == 0, 1.0, -1.0)
    nyquist = jnp.sum(vx * sign, axis=0, keepdims=True) * kn_ref[...]
    return x0 * (y + sign * nyquist + vx * bias_ref[...])


def _hyena_kernel(x0_ref, x1_ref, v_ref, cw_ref, cb_ref, bias_ref,
                  featl_ref, winl_ref, featc_ref, winc_ref,
                  w1_ref, b1_ref, w2_ref, b2_ref, w3f_ref, w3b_ref, sf_ref,
                  fcl_ref, fsl_ref, fcc_ref, fsc_ref, o_ref,
                  hidl_ref, hidc_ref, kcl_ref, ksl_ref, knl_ref, kcc_ref, ksc_ref, knc_ref):
    step = pl.program_id(1)
    common = (cw_ref, cb_ref, bias_ref)

    @pl.when(jnp.logical_and(pl.program_id(0) == 0, step == 0))
    def _():
        sine_layers = (w1_ref, b1_ref, w2_ref, b2_ref, sf_ref)
        _filter_hidden(featl_ref, *sine_layers, hidl_ref)
        _filter_hidden(featc_ref, *sine_layers, hidc_ref)

    @pl.when(step == 0)
    def _():
        _filter_spectrum(hidl_ref, winl_ref, w3f_ref, w3b_ref, fcl_ref, fsl_ref, kcl_ref, ksl_ref, knl_ref, SEQ)
        _filter_spectrum(hidc_ref, winc_ref, w3f_ref, w3b_ref, fcc_ref, fsc_ref, kcc_ref, ksc_ref, knc_ref, CTX_LEN)

    @pl.when(step < BATCH)
    def _():
        raw = (x0_ref[...], x1_ref[...], v_ref[...])
        o_ref[...] = _hyena_sequence(raw, *common, fcl_ref, fsl_ref, kcl_ref, ksl_ref, knl_ref, SEQ)

    @pl.when(step == BATCH)
    def _():
        for r in range(BATCH):
            rows = slice(r * CTX_LEN, (r + 1) * CTX_LEN)
            raw = (x0_ref[rows, :], x1_ref[rows, :], v_ref[rows, :])
            o_ref[rows, :] = _hyena_sequence(raw, *common, fcc_ref, fsc_ref, kcc_ref, ksc_ref, knc_ref, CTX_LEN)


def _hyena(hy, filter_params, conv_w, conv_b, bias):
    assert T_CTX == SEQ, "the context rows must form exactly one latent-sized row block"
    n_ct = HY_WIDTH // HY_CT
    cw = conv_w.reshape(3, 3, HY_WIDTH).transpose(1, 0, 2)
    cb = conv_b.reshape(3, HY_WIDTH)
    feat_l, win_l = _filter_features(SEQ)
    feat_c, win_c = _filter_features(CTX_LEN)
    tab_l = _dft_tables(SEQ)
    tab_c = _dft_tables(CTX_LEN)

    def part_spec(part):
        return pl.BlockSpec((SEQ, HY_CT), lambda c, b: (b, part * n_ct + c))

    chan = lambda c, b: (0, c)
    once_per_tile = lambda rows: pl.BlockSpec((rows, HY_CT), chan, pipeline_mode=pl.Buffered(1))
    spectrum = lambda rows: pltpu.VMEM((rows, HY_CT), F32)
    return pl.pallas_call(
        _hyena_kernel,
        grid=(n_ct, BATCH + 1),
        in_specs=[
            part_spec(0), part_spec(1), part_spec(2),
            pl.BlockSpec((3, 3, HY_CT), lambda c, b: (0, 0, c)),
            pl.BlockSpec((3, HY_CT), chan),
            pl.BlockSpec((1, HY_CT), chan),
            _resident((SEQ, LANES)), once_per_tile(SEQ),
            _resident((CTX_LEN, LANES)), once_per_tile(CTX_LEN),
            _resident((LANES, LANES)), _resident((1, LANES)), _resident((LANES, LANES)), _resident((1, LANES)),
            once_per_tile(LANES), once_per_tile(LANES), _resident((2, LANES)),
            _resident((SEQ, SEQ)), _resident((SEQ, SEQ)),
            _resident((CTX_LEN, CTX_LEN)), _resident((CTX_LEN, CTX_LEN)),
        ],
        out_specs=pl.BlockSpec((SEQ, HY_CT), lambda c, b: (b, c)),
        out_shape=jax.ShapeDtypeStruct((T_ALL, HY_WIDTH), F32),
        scratch_shapes=[pltpu.VMEM((SEQ, LANES), BF16), pltpu.VMEM((CTX_LEN, LANES), BF16),
                        spectrum(SEQ), spectrum(SEQ), spectrum(1), spectrum(CTX_LEN), spectrum(CTX_LEN), spectrum(1)],
        compiler_params=_params("arbitrary", "arbitrary"),
        name="hyena_conv",
    )(hy, hy, hy, cw, cb, bias.reshape(1, HY_WIDTH), feat_l, win_l, feat_c, win_c, *filter_params, *tab_l, *tab_c)


def _rope_partner(n):
    idx = np.arange(n)
    return np.where((idx % (2 * ROPE_AXIS_PAIRS)) < ROPE_AXIS_PAIRS, idx + ROPE_AXIS_PAIRS, idx - ROPE_AXIS_PAIRS)


def _slot_lanes():
    d = np.arange(MLA_ROPE)
    axis, second, pair = d // (2 * ROPE_AXIS_PAIRS), (d // ROPE_AXIS_PAIRS) % 2, d % ROPE_AXIS_PAIRS
    rope_lane = axis * ROPE_AXIS_PAIRS + pair + (HEAD_SLOT // 2) * second
    free = np.setdiff1d(np.arange(HEAD_SLOT), rope_lane)
    return rope_lane, free[:MLA_NOPE]


def _to_slot(nope, rope):
    rope_lane, nope_lane = _slot_lanes()
    source = [(None, 0)] * HEAD_SLOT
    for arr, lanes in ((nope, nope_lane), (rope, rope_lane)):
        if arr is not None:
            for idx, lane in enumerate(lanes):
                source[lane] = (arr, idx)
    like = nope if nope is not None else rope
    pieces, lane = [], 0
    while lane < HEAD_SLOT:
        arr, start = source[lane]
        run = 1
        while (lane + run < HEAD_SLOT and source[lane + run][0] is arr
               and (arr is None or source[lane + run][1] == start + run)):
            run += 1
        pieces.append(jnp.zeros(like.shape[:-1] + (run,), like.dtype) if arr is None else arr[..., start:start + run])
        lane += run
    return jnp.concatenate(pieces, axis=-1)


def _rope_tables():
    f32 = np.float32
    rope_lane, nope_lane = _slot_lanes()
    rows = np.repeat(np.arange(SEQ // GRID_W, dtype=f32), GRID_W)
    cols = np.tile(np.arange(GRID_W, dtype=f32), SEQ // GRID_W)
    inv_freq = (f32(ROPE_BASE) ** (-np.arange(ROPE_AXIS_PAIRS, dtype=f32) / f32(ROPE_AXIS_PAIRS))).astype(f32)
    ang_r = rows[:, None] * inv_freq
    ang_c = cols[:, None] * inv_freq
    cos32 = np.concatenate([np.cos(ang_r)] * 2 + [np.cos(ang_c)] * 2, axis=1)
    sin32 = np.concatenate([-np.sin(ang_r), np.sin(ang_r), -np.sin(ang_c), np.sin(ang_c)], axis=1)

    def table(n_rows, rope_vals, nope_val):
        t = np.zeros((n_rows, HEAD_SLOT), f32)
        t[:, rope_lane] = rope_vals
        t[:, nope_lane] = nope_val
        return t

    q_cos = table(SEQ, cos32, 1.0) * f32(Q_SCALE)
    q_sin = table(SEQ, sin32, 0.0) * f32(Q_SCALE)
    k_cos = table(SEQ, cos32, 0.0)
    k_sin = table(SEQ, sin32, 0.0)
    ident = table(TM_PROJ, 1.0, 0.0)
    zero = np.zeros((TM_PROJ, HEAD_SLOT), f32)
    tables = ((q_cos, zero), (q_sin, zero), (k_cos, ident), (k_sin, zero))
    return tuple(jnp.asarray(np.concatenate(t, 0).astype(f32)) for t in tables)


def _value_ones():
    assert HEAD_SLOT == 2 * MLA_V
    lane = np.arange(MLA_HEADS * HEAD_SLOT)
    head_is_even = (lane // HEAD_SLOT) % 2 == 0
    upper_half = lane % HEAD_SLOT >= MLA_V
    return jnp.asarray((head_is_even == upper_half).astype(np.float32).reshape(1, -1))


def _mla_weights(w_in, w_q_up, w_kv_up):
    w_q = w_in[:, :MLA_Q_LORA]
    w_kv = w_in[:, MLA_Q_LORA:MLA_Q_LORA + MLA_KV_LORA]
    w_kr = w_in[:, MLA_Q_LORA + MLA_KV_LORA:]
    w_kr2 = jnp.concatenate([_to_slot(None, w_kr), _to_slot(None, w_kr[:, _rope_partner(MLA_ROPE)])], axis=1)

    qh = w_q_up.reshape(MLA_Q_LORA, MLA_HEADS, MLA_NOPE + MLA_ROPE)
    wq_slot = _to_slot(qh[..., :MLA_NOPE], qh[..., MLA_NOPE:]).reshape(MLA_Q_LORA, MLA_HEADS * HEAD_SLOT)

    kvh = w_kv_up.reshape(MLA_KV_LORA, MLA_HEADS, MLA_NOPE + MLA_V)
    wk_slot = _to_slot(kvh[..., :MLA_NOPE], None).reshape(MLA_KV_LORA, MLA_HEADS * HEAD_SLOT)
    even = (jnp.arange(MLA_HEADS) % 2 == 0).astype(F32)[None, :, None]
    wv = kvh[..., MLA_NOPE:]
    wv_slot = jnp.concatenate([wv * even, wv * (1.0 - even)], axis=-1).reshape(MLA_KV_LORA, MLA_HEADS * HEAD_SLOT)
    return [w.astype(BF16) for w in (w_q, w_kv, w_kr2, wq_slot, wk_slot, wv_slot)]


def _mlaproj_kernel(x_ref, mod_ref, wq_ref, wkv_ref, wkr_ref, wqs_ref, wks_ref, wv_ref,
                    vone_ref, qg_ref, kvg_ref, qcos_ref, qsin_ref, kcos_ref, ksin_ref,
                    q_ref, k_ref, v_ref):
    shift, scale = mod_ref[0, 0], mod_ref[0, 1]
    h = (x_ref[...] * (1.0 + scale) + shift).astype(BF16)

    def rmsnorm(y, g):
        return (y * lax.rsqrt(jnp.mean(y * y, axis=-1, keepdims=True) + RMS_EPS) * g).astype(BF16)

    qn = rmsnorm(_dot(h, wq_ref[...]), qg_ref[...])
    kvn = rmsnorm(_dot(h, wkv_ref[...]), kvg_ref[...])
    kr2 = _dot(h, wkr_ref[...])
    k_rope = kr2[:, :HEAD_SLOT] * kcos_ref[...] + kr2[:, HEAD_SLOT:] * ksin_ref[...]
    k_nope = _dot(kvn, wks_ref[...])
    v_ref[...] = (_dot(kvn, wv_ref[...]) + vone_ref[...]).astype(BF16)
    q_all = _dot(qn, wqs_ref[...])
    q_cos = qcos_ref[...]
    q_sin = qsin_ref[...]
    for hd in range(MLA_HEADS):
        sl = slice(hd * HEAD_SLOT, (hd + 1) * HEAD_SLOT)
        qh = q_all[:, sl]
        rotated = qh * q_cos + pltpu.roll(qh, HEAD_SLOT // 2, 1) * q_sin
        q_ref[:, sl] = rotated.astype(BF16)
        k_ref[:, sl] = (k_nope[:, sl] + k_rope).astype(BF16)


def _mlaproj(x, mod, w_in, q_norm, kv_norm, w_q_up, w_kv_up):
    tm = TM_PROJ
    weights = _mla_weights(w_in, w_q_up, w_kv_up)
    tables = _rope_tables()
    row = lambda i: (i, 0)
    tab = lambda i: (jnp.where(i < T_LAT // tm, i % (SEQ // tm), SEQ // tm), 0)
    wide = MLA_HEADS * HEAD_SLOT
    v_one = _value_ones()
    return pl.pallas_call(
        _mlaproj_kernel,
        grid=(T_ALL // tm,),
        in_specs=[
            pl.BlockSpec((tm, D_MODEL), row),
            pl.BlockSpec((1, 2, 1, D_MODEL), lambda i: (_group_of_tile(i, tm), 0, 0, 0)),
            *[_resident(w.shape) for w in weights],
            _resident((1, wide)),
            _resident((1, MLA_Q_LORA)),
            _resident((1, MLA_KV_LORA)),
            *[pl.BlockSpec((tm, HEAD_SLOT), tab) for _ in tables],
        ],
        out_specs=[pl.BlockSpec((tm, wide), row)] * 3,
        out_shape=[jax.ShapeDtypeStruct((T_ALL, wide), BF16)] * 3,
        compiler_params=_params("arbitrary"),
        name="mla_proj",
    )(x, mod, *weights, v_one, q_norm.reshape(1, MLA_Q_LORA), kv_norm.reshape(1, MLA_KV_LORA), *tables)


def _attn_kernel(q_ref, kl_ref, kc_ref, vl_ref, vc_ref, o_ref):
    low = lax.broadcasted_iota(jnp.int32, (TQ, HEAD_SLOT), 1) < MLA_V
    pair_out = []
    for hd in range(ATTN_HEADS):
        sl = slice(hd * HEAD_SLOT, (hd + 1) * HEAD_SLOT)
        q = q_ref[:, sl]
        s_lat = _dot_nt(q, kl_ref[:, sl])
        s_ctx = _dot_nt(q, kc_ref[:, sl])
        m = jnp.maximum(jnp.max(s_lat, axis=-1, keepdims=True), jnp.max(s_ctx, axis=-1, keepdims=True))
        p_lat = jnp.exp2(s_lat - m).astype(BF16)
        p_ctx = jnp.exp2(s_ctx - m).astype(BF16)
        pair_out.append(_dot(p_lat, vl_ref[:, sl]) + _dot(p_ctx, vc_ref[:, sl]))
        if hd % 2 == 1:
            even, odd = pair_out
            pair_out = []
            numer = jnp.where(low, even, odd)
            denom = pltpu.roll(jnp.where(low, odd, even), MLA_V, 1)
            pair = hd // 2
            o_ref[:, pair * HEAD_SLOT:(pair + 1) * HEAD_SLOT] = (numer / denom).astype(BF16)


def _attention(q, k, v):
    n_q = SEQ // TQ
    n_groups = MLA_HEADS // ATTN_HEADS
    ctx_block0 = T_LAT // CTX_LEN
    wide = ATTN_HEADS * HEAD_SLOT
    lat = lambda b, h, i: (b, h)
    ctx = lambda b, h, i: (ctx_block0 + b, h)
    return pl.pallas_call(
        _attn_kernel,
        grid=(BATCH, n_groups, n_q),
        in_specs=[
            pl.BlockSpec((TQ, wide), lambda b, h, i: (b * n_q + i, h)),
            pl.BlockSpec((SEQ, wide), lat),
            pl.BlockSpec((CTX_LEN, wide), ctx),
            pl.BlockSpec((SEQ, wide), lat),
            pl.BlockSpec((CTX_LEN, wide), ctx),
        ],
        out_specs=pl.BlockSpec((TQ, ATTN_HEADS * MLA_V), lambda b, h, i: (b * n_q + i, h)),
        out_shape=jax.ShapeDtypeStruct((T_LAT, MLA_HEADS * MLA_V), BF16),
        compiler_params=_params("arbitrary", "arbitrary", "arbitrary"),
        name="mla_attention",
    )(q, k, k, v, v)


def kernel(x, c, ctx, c_ctx, ada_w, ada_b, ln_g, ln_b, ffn_w_in, ffn_w_out, ev_w_in, ev_w_out, hy_conv_w, hy_conv_b, hy_filt_w1, hy_filt_b1, hy_filt_w2, hy_filt_b2, hy_filt_w3, hy_sin_freq, hy_bias, rg_conv_w, rg_conv_b, rg_a_w, rg_a_b, rg_x_w, rg_x_b, rg_lambda, mla_w_in, mla_q_norm, mla_kv_norm, mla_w_q_up, mla_w_kv_up, mla_w_out):
    assert x.shape == (BATCH, SEQ, D_MODEL) and ctx.shape == (BATCH, CTX_LEN, D_MODEL)
    cond = jnp.concatenate([c, c_ctx[None], jnp.zeros((2 * SUBLANES - N_GROUPS, D_MODEL), F32)], axis=0)
    mods = _adaln(cond, ada_w, ada_b)[:, :N_GROUPS].reshape(DEPTH, N_GROUPS, N_ADA, 1, D_MODEL)
    ffn_w = (ffn_w_in, ffn_w_out)

    m = mods[0]
    xs, hy, rx, gate_in = _ffn_split_evproj(x.reshape(T_LAT, D_MODEL), ctx.reshape(T_CTX, D_MODEL),
                                            m[:, 0:3], m[:, 3:5], ffn_w, (0, 0), ln_g[0, 0], ln_b[0, 0], ev_w_in[0])
    h_sum = _rglru(rx, rg_conv_w[0], rg_conv_b[0], rg_a_w[0], rg_a_b[0], rg_x_w[0], rg_x_b[0], rg_lambda[0])
    filt = _filter_params(hy_filt_w1[0], hy_filt_b1[0], hy_filt_w2[0], hy_filt_b2[0], hy_filt_w3[0], hy_sin_freq[0])
    y_hy = _hyena(hy, filt, hy_conv_w[0], hy_conv_b[0], hy_bias[0])
    xs = _ffn_after_even(xs, y_hy, h_sum, gate_in, ev_w_out[0], ln_g[0, 1], ln_b[0, 1],
                         m[:, 5:9], ffn_w, (0, 1), ln_g[0, 2], ln_b[0, 2], T_ALL)

    m = mods[1]
    xs = _ffn_plain(xs, m[:, 0:3], ffn_w, (1, 0), ln_g[1, 0], ln_b[1, 0], T_ALL)
    q, k, v = _mlaproj(xs, m[:, 3:5], mla_w_in[0], mla_q_norm[0], mla_kv_norm[0], mla_w_q_up[0], mla_w_kv_up[0])
    att = _attention(q, k, v)
    xl = _ffn_after_mla(xs, att, mla_w_out[0], ln_g[1, 1], ln_b[1, 1],
                        m[:, 5:9], ffn_w, (1, 1), ln_g[1, 2], ln_b[1, 2], T_LAT)
    return xl.reshape(BATCH, SEQ, D_MODEL)
```

```python
import functools
import math

import jax
import jax.numpy as jnp
import numpy as np
from jax import lax
from jax.experimental import pallas as pl
from jax.experimental.pallas import tpu as pltpu

F32 = jnp.float32
BF16 = jnp.bfloat16

D_MODEL = 1024
BATCH = 8
SEQ = 2048
DEPTH = 2
CTX_LEN = 256
GRID_W = 64
N_ADA = 9
D_FF = 2816

HY_WIDTH = 512
RG_WIDTH = 512
EV_IN = 3 * HY_WIDTH + 2 * RG_WIDTH
HY_EMB = 33
HY_BANDS = (HY_EMB - 1) // 2
HY_FILT_HIDDEN = 64
HY_TARGET = 1e-2
HY_MAX_DECAY = math.log(HY_TARGET) / 0.3
HY_MIN_DECAY = math.log(HY_TARGET) / 1.5
HY_SHIFT = 0.05
RG_BLOCKS = 8
RG_BLOCK_DIM = RG_WIDTH // RG_BLOCKS
RG_C = 8.0
RG_CONV = 4
RG_PAD_L = 2

MLA_HEADS = 16
MLA_Q_LORA = 768
MLA_KV_LORA = 256
MLA_NOPE = 64
MLA_ROPE = 32
MLA_V = 64
MLA_SCALE = (MLA_NOPE + MLA_ROPE) ** -0.5
ROPE_AXIS_PAIRS = MLA_ROPE // 4
ROPE_BASE = 10000.0

LOG2_E = math.log2(math.e)
ALPHA = (2.0 * DEPTH) ** 0.25
LN_EPS = 1e-6
RMS_EPS = 1e-6

T_LAT = BATCH * SEQ
T_CTX = BATCH * CTX_LEN
T_ALL = T_LAT + T_CTX
N_GROUPS = BATCH + 1

LANES = 128
SUBLANES = 8
VMEM_LIMIT = 56 * 1024 * 1024

TM_FFN = 512
FF_CHUNK = 256
N_FF_CHUNKS = D_FF // FF_CHUNK
FFN_WEIGHT_CHUNKS = 16
TM_PROJ = 512
TM_EVPROJ = 1024
HY_CT = 256
HY_FREQ_TILE = 512
RG_CT = 128
RG_TC = 32
RG_HALO = SUBLANES
HEAD_SLOT = 128
TQ = 512
ATTN_HEADS = 16
Q_SCALE = MLA_SCALE * math.log2(math.e)
ADA_TN = 2304


def _dot(a, b):
    return jnp.dot(a, b, preferred_element_type=F32)


def _dot_nt(a, b):
    return lax.dot_general(a, b, (((1,), (1,)), ((), ())), preferred_element_type=F32)


def _resident(shape):
    nd = len(shape)
    return pl.BlockSpec(shape, lambda *_: (0,) * nd, pipeline_mode=pl.Buffered(1))


def _group_of_tile(i, tm):
    return jnp.where(i < T_LAT // tm, i // (SEQ // tm), BATCH)


def _params(*sem):
    return pltpu.CompilerParams(dimension_semantics=sem, vmem_limit_bytes=VMEM_LIMIT)


def _layernorm(z, g, b):
    mu = jnp.mean(z, axis=-1, keepdims=True)
    zc = z - mu
    var = jnp.mean(zc * zc, axis=-1, keepdims=True)
    return zc * lax.rsqrt(var + LN_EPS) * g + b


def _adaln_kernel(c_ref, w_ref, b_ref, o_ref):
    c = c_ref[...]
    s = (c * jax.nn.sigmoid(c)).astype(BF16)
    o_ref[0] = _dot(s, w_ref[0].astype(BF16)) + b_ref[0]


def _adaln(cond, ada_w, ada_b):
    n_out = N_ADA * D_MODEL
    rows = cond.shape[0]
    return pl.pallas_call(
        _adaln_kernel,
        grid=(DEPTH, n_out // ADA_TN),
        in_specs=[
            pl.BlockSpec((rows, D_MODEL), lambda l, j: (0, 0)),
            pl.BlockSpec((1, D_MODEL, ADA_TN), lambda l, j: (l, 0, j)),
            pl.BlockSpec((1, 1, ADA_TN), lambda l, j: (l, 0, j)),
        ],
        out_specs=pl.BlockSpec((1, rows, ADA_TN), lambda l, j: (l, 0, j)),
        out_shape=jax.ShapeDtypeStruct((DEPTH, rows, n_out), F32),
        compiler_params=_params("arbitrary", "arbitrary"),
        name="adaln",
    )(cond, ada_w, ada_b.reshape(DEPTH, 1, n_out))


def _ffn_body(x, mod_ref, wi_ref, wo_ref, g_ref, b_ref, o_ref, act_ref):
    n_slots = mod_ref.shape[1]
    shift, scale, gate = mod_ref[0, n_slots - 3], mod_ref[0, n_slots - 2], mod_ref[0, n_slots - 1]
    h = (x * (1.0 + scale) + shift).astype(BF16)
    for j in range(N_FF_CHUNKS):
        lo = j * FF_CHUNK
        a = _dot(h, wi_ref[:, lo:lo + FF_CHUNK])
        u = _dot(h, wi_ref[:, D_FF + lo:D_FF + lo + FF_CHUNK])
        act_ref[:, lo:lo + FF_CHUNK] = (a * jax.nn.sigmoid(a) * u).astype(BF16)
    y = _dot(act_ref[...], wo_ref[...])
    o_ref[...] = _layernorm(ALPHA * x + (0.5 * gate) * y, g_ref[...], b_ref[...])


def _load_weight_bf16(w_hbm, which, dst_ref, stage_ref, sem):
    layer, half = which
    rows = stage_ref.shape[1]
    n_chunks = dst_ref.shape[0] // rows

    def copy(c):
        return pltpu.make_async_copy(w_hbm.at[layer, half, pl.ds(c * rows, rows)], stage_ref.at[c % 2], sem.at[c % 2])

    copy(0).start()
    for c in range(n_chunks):
        if c + 1 < n_chunks:
            copy(c + 1).start()
        copy(c).wait()
        dst_ref[c * rows:(c + 1) * rows, :] = stage_ref[c % 2].astype(BF16)


def _ffn_steps(x_fn, which, mod_ref, wi_hbm, wo_hbm, g_ref, b_ref, o_ref,
               act_ref, wi_ref, wo_ref, stage_in_ref, stage_out_ref, sem_in, sem_out):
    @pl.when(pl.program_id(0) == 0)
    def _():
        _load_weight_bf16(wi_hbm, which, wi_ref, stage_in_ref, sem_in)
        _load_weight_bf16(wo_hbm, which, wo_ref, stage_out_ref, sem_out)

    _ffn_body(x_fn(), mod_ref, wi_ref, wo_ref, g_ref, b_ref, o_ref, act_ref)


def _ffn_kernel(x_ref, *rest, which):
    _ffn_steps(lambda: x_ref[...], which, *rest)


def _ffn_split_kernel(xl_ref, xc_ref, *rest, which):
    _ffn_steps(lambda: jnp.where(pl.program_id(0) < T_LAT // TM_FFN, xl_ref[...], xc_ref[...]), which, *rest)


def _ffn_after_even_kernel(x_ref, yh_ref, hs_ref, gt_ref, wmix_ref, g1_ref, b1_ref, mod_ref, *rest, which):
    def x_fn():
        y_rg = hs_ref[...] * jax.nn.gelu(gt_ref[...], approximate=True)
        y = (_dot(yh_ref[...].astype(BF16), wmix_ref[:HY_WIDTH, :])
             + _dot(y_rg.astype(BF16), wmix_ref[HY_WIDTH:, :]))
        return _layernorm(ALPHA * x_ref[...] + mod_ref[0, 0] * y, g1_ref[...], b1_ref[...])

    _ffn_steps(x_fn, which, mod_ref, *rest)


def _ffn_after_mla_kernel(x_ref, att_ref, wmix_ref, g1_ref, b1_ref, mod_ref, *rest, which):
    def x_fn():
        y = _dot(att_ref[...], wmix_ref[...])
        return _layernorm(ALPHA * x_ref[...] + mod_ref[0, 0] * y, g1_ref[...], b1_ref[...])

    _ffn_steps(x_fn, which, mod_ref, *rest)


def _ffn(body, lead_args, lead_specs, mod, ffn_w, which, g, b, n_rows):
    tm = TM_FFN
    w_in_all, w_out_all = ffn_w
    return pl.pallas_call(
        functools.partial(body, which=which),
        grid=(n_rows // tm,),
        in_specs=[
            *lead_specs,
            pl.BlockSpec((1, mod.shape[1], 1, D_MODEL), lambda i: (_group_of_tile(i, tm), 0, 0, 0)),
            pl.BlockSpec(memory_space=pl.ANY),
            pl.BlockSpec(memory_space=pl.ANY),
            _resident((1, D_MODEL)),
            _resident((1, D_MODEL)),
        ],
        out_specs=pl.BlockSpec((tm, D_MODEL), lambda i: (i, 0)),
        out_shape=jax.ShapeDtypeStruct((n_rows, D_MODEL), F32),
        scratch_shapes=[
            pltpu.VMEM((tm, D_FF), BF16),
            pltpu.VMEM((D_MODEL, 2 * D_FF), BF16),
            pltpu.VMEM((D_FF, D_MODEL), BF16),
            pltpu.VMEM((2, D_MODEL // FFN_WEIGHT_CHUNKS, 2 * D_FF), F32),
            pltpu.VMEM((2, D_FF // FFN_WEIGHT_CHUNKS, D_MODEL), F32),
            pltpu.SemaphoreType.DMA((2,)),
            pltpu.SemaphoreType.DMA((2,)),
        ],
        compiler_params=_params("arbitrary"),
        name="half_ffn",
    )(*lead_args, mod, w_in_all, w_out_all, g.reshape(1, D_MODEL), b.reshape(1, D_MODEL))


def _row_spec(width, tm=TM_FFN):
    return pl.BlockSpec((tm, width), lambda i: (i, 0))


def _ffn_plain(x, *args):
    return _ffn(_ffn_kernel, (x,), [_row_spec(D_MODEL)], *args)


def _ffn_split(x_lat, x_ctx, *args):
    n_lat = T_LAT // TM_FFN
    specs = [pl.BlockSpec((TM_FFN, D_MODEL), lambda i: (jnp.minimum(i, n_lat - 1), 0)),
             pl.BlockSpec((TM_FFN, D_MODEL), lambda i: (jnp.maximum(i - n_lat, 0), 0))]
    return _ffn(_ffn_split_kernel, (x_lat, x_ctx), specs, *args)


def _ffn_after_even(x, y_hy, h_sum, gate_in, w_mix, g1, b1, *args):
    lead = (x, y_hy, h_sum, gate_in, w_mix.astype(BF16), g1.reshape(1, D_MODEL), b1.reshape(1, D_MODEL))
    specs = [_row_spec(D_MODEL), _row_spec(HY_WIDTH), _row_spec(RG_WIDTH), _row_spec(RG_WIDTH),
             _resident((HY_WIDTH + RG_WIDTH, D_MODEL)), _resident((1, D_MODEL)), _resident((1, D_MODEL))]
    return _ffn(_ffn_after_even_kernel, lead, specs, *args)


def _ffn_after_mla(x, att, w_mix, g1, b1, *args):
    lead = (x, att, w_mix.astype(BF16), g1.reshape(1, D_MODEL), b1.reshape(1, D_MODEL))
    specs = [_row_spec(D_MODEL), _row_spec(MLA_HEADS * MLA_V),
             _resident((MLA_HEADS * MLA_V, D_MODEL)), _resident((1, D_MODEL)), _resident((1, D_MODEL))]
    return _ffn(_ffn_after_mla_kernel, lead, specs, *args)


def _evproj_kernel(x_ref, mod_ref, w_ref, hy_ref, rx_ref, gt_ref):
    shift, scale = mod_ref[0, 0], mod_ref[0, 1]
    h = (x_ref[...] * (1.0 + scale) + shift).astype(BF16)
    n_hy = 3 * HY_WIDTH
    hy_ref[...] = _dot(h, w_ref[:, :n_hy])
    rx_ref[...] = _dot(h, w_ref[:, n_hy:n_hy + RG_WIDTH])
    gt_ref[...] = _dot(h, w_ref[:, n_hy + RG_WIDTH:])


def _ffn_split_evproj_kernel(xl_ref, xc_ref, mod_ref, wi_hbm, wo_hbm, g_ref, b_ref, mod2_ref, wev_ref,
                             o_ref, hy_ref, rx_ref, gt_ref,
                             act_ref, wi_ref, wo_ref, stage_in_ref, stage_out_ref, sem_in, sem_out, *, which):
    x_fn = lambda: jnp.where(pl.program_id(0) < T_LAT // TM_FFN, xl_ref[...], xc_ref[...])
    _ffn_steps(x_fn, which, mod_ref, wi_hbm, wo_hbm, g_ref, b_ref, o_ref,
               act_ref, wi_ref, wo_ref, stage_in_ref, stage_out_ref, sem_in, sem_out)
    _evproj_kernel(o_ref, mod2_ref, wev_ref, hy_ref, rx_ref, gt_ref)


def _ffn_split_evproj(x_lat, x_ctx, mod_ffn, mod_proj, ffn_w, which, g, b, w_ev):
    tm = TM_FFN
    n_lat = T_LAT // tm
    row = lambda i: (i, 0)
    group = lambda i: (_group_of_tile(i, tm), 0, 0, 0)
    f32_rows = lambda width: jax.ShapeDtypeStruct((T_ALL, width), F32)
    return pl.pallas_call(
        functools.partial(_ffn_split_evproj_kernel, which=which),
        grid=(T_ALL // tm,),
        in_specs=[
            pl.BlockSpec((tm, D_MODEL), lambda i: (jnp.minimum(i, n_lat - 1), 0)),
            pl.BlockSpec((tm, D_MODEL), lambda i: (jnp.maximum(i - n_lat, 0), 0)),
            pl.BlockSpec((1, 3, 1, D_MODEL), group),
            pl.BlockSpec(memory_space=pl.ANY),
            pl.BlockSpec(memory_space=pl.ANY),
            _resident((1, D_MODEL)),
            _resident((1, D_MODEL)),
            pl.BlockSpec((1, 2, 1, D_MODEL), group),
            _resident((D_MODEL, EV_IN)),
        ],
        out_specs=[pl.BlockSpec((tm, D_MODEL), row), pl.BlockSpec((tm, 3 * HY_WIDTH), row),
                   pl.BlockSpec((tm, RG_WIDTH), row), pl.BlockSpec((tm, RG_WIDTH), row)],
        out_shape=[f32_rows(D_MODEL), f32_rows(3 * HY_WIDTH), f32_rows(RG_WIDTH), f32_rows(RG_WIDTH)],
        scratch_shapes=[
            pltpu.VMEM((tm, D_FF), BF16),
            pltpu.VMEM((D_MODEL, 2 * D_FF), BF16),
            pltpu.VMEM((D_FF, D_MODEL), BF16),
            pltpu.VMEM((2, D_MODEL // FFN_WEIGHT_CHUNKS, 2 * D_FF), F32),
            pltpu.VMEM((2, D_FF // FFN_WEIGHT_CHUNKS, D_MODEL), F32),
            pltpu.SemaphoreType.DMA((2,)),
            pltpu.SemaphoreType.DMA((2,)),
        ],
        compiler_params=_params("arbitrary"),
        name="half_ffn_even_in_proj",
    )(x_lat, x_ctx, mod_ffn, ffn_w[0], ffn_w[1], g.reshape(1, D_MODEL), b.reshape(1, D_MODEL),
      mod_proj, w_ev.astype(BF16))


def _evproj(x, mod, w_in):
    tm = TM_EVPROJ
    row = lambda i: (i, 0)
    return pl.pallas_call(
        _evproj_kernel,
        grid=(T_ALL // tm,),
        in_specs=[
            pl.BlockSpec((tm, D_MODEL), row),
            pl.BlockSpec((1, 2, 1, D_MODEL), lambda i: (_group_of_tile(i, tm), 0, 0, 0)),
            _resident((D_MODEL, EV_IN)),
        ],
        out_specs=[
            pl.BlockSpec((tm, 3 * HY_WIDTH), row),
            pl.BlockSpec((tm, RG_WIDTH), row),
            pl.BlockSpec((tm, RG_WIDTH), row),
        ],
        out_shape=[
            jax.ShapeDtypeStruct((T_ALL, 3 * HY_WIDTH), F32),
            jax.ShapeDtypeStruct((T_ALL, RG_WIDTH), F32),
            jax.ShapeDtypeStruct((T_ALL, RG_WIDTH), F32),
        ],
        compiler_params=_params("arbitrary"),
        name="even_in_proj",
    )(x, mod, w_in.astype(BF16))


def _block_diag_tiles(w):
    per = RG_CT // RG_BLOCK_DIM
    w = w.reshape(RG_WIDTH // RG_CT, per, RG_BLOCK_DIM, RG_BLOCK_DIM)
    eye = jnp.eye(per, dtype=w.dtype)
    return jnp.einsum('cpde,pq->cpdqe', w, eye).reshape(RG_WIDTH // RG_CT, RG_CT, RG_CT)


def _rg_kernel(rx_ref, cw_ref, cb_ref, w_ref, bias_ref, lam_ref, out_ref,
                padl_ref, padc_ref, xt_f_ref, xt_b_ref, coef_a_ref, coef_b_ref, hbuf_f_ref, hbuf_b_ref):
    neg_lam = -lam_ref[...]
    softplus = jnp.maximum(neg_lam, 0.0) + jnp.log1p(jnp.exp(-jnp.abs(neg_lam)))
    neg_log_a_scale = RG_C * softplus
    a_exp2_scale = -LOG2_E * neg_log_a_scale
    cb = cb_ref[...]
    xts = (xt_f_ref, xt_b_ref)
    hbuf = (hbuf_f_ref, hbuf_b_ref)
    slab_rows = RG_TC + 2 * RG_HALO

    def step_rows(s):
        return slice(s * SUBLANES, (s + 1) * SUBLANES)

    halo = jnp.zeros((BATCH, RG_HALO, RG_CT), F32)
    for pad_ref, length in ((padl_ref, SEQ), (padc_ref, CTX_LEN)):
        pad_ref[:, 0:RG_HALO, :] = halo
        pad_ref[:, RG_HALO + length:, :] = halo
    padl_ref[:, RG_HALO:RG_HALO + SEQ, :] = rx_ref[0:BATCH]
    for b in range(BATCH):
        padc_ref[b, RG_HALO:RG_HALO + CTX_LEN, :] = rx_ref[BATCH, b * CTX_LEN:(b + 1) * CTX_LEN, :]

    def add_out(latent, t_start, d, first_touch):
        t_start = pl.multiple_of(t_start, SUBLANES)
        for b in range(BATCH):
            rows = hbuf[d][pl.ds(b, RG_TC, stride=SUBLANES), :]
            where = (b, pl.ds(t_start, RG_TC)) if latent else (BATCH, pl.ds(b * CTX_LEN + t_start, RG_TC))
            if first_touch:
                out_ref[where[0], where[1], :] = rows
            else:
                out_ref[where[0], where[1], :] += rows

    def sigmoid(z):
        return 1.0 / (1.0 + jnp.exp2(z * (-LOG2_E)))

    def coeffs(latent, length, t0, d, dst_ref):
        t0 = pl.multiple_of(jnp.asarray(t0, jnp.int32), SUBLANES)
        pad_ref = padl_ref if latent else padc_ref
        xt = xts[d]
        for b in range(BATCH):
            xt[pl.ds(b, slab_rows, stride=SUBLANES), :] = pad_ref[b, pl.ds(t0, slab_rows), :]
        x2 = cb
        for k in range(RG_CONV):
            start = (RG_HALO - RG_PAD_L + k) * SUBLANES
            x2 = x2 + xt[start:start + RG_TC * SUBLANES, :] * cw_ref[k:k + 1, :]
        xb = x2.astype(BF16)
        r = sigmoid(_dot(xb, w_ref[0, 2 * d]) + bias_ref[2 * d:2 * d + 1, :])
        i = sigmoid(_dot(xb, w_ref[0, 2 * d + 1]) + bias_ref[2 * d + 1:2 * d + 2, :])
        a = jnp.exp2(r * a_exp2_scale[d:d + 1, :])
        b = jnp.sqrt(jnp.tanh(r * neg_log_a_scale[d:d + 1, :]) * (a * a + 1.0)) * (i * x2)
        dst_ref[d, 0] = a
        dst_ref[d, 1] = b

    def scan(latent, first_touch, tf, tb, src_ref, hf, hb):
        tf = jnp.asarray(tf, jnp.int32)
        tb = jnp.asarray(tb, jnp.int32)
        for s in range(RG_TC):
            rows = step_rows(s)
            hf = src_ref[0, 0, rows, :] * hf + src_ref[0, 1, rows, :]
            hbuf_f_ref[rows, :] = hf
            rows = step_rows(RG_TC - 1 - s)
            hb = src_ref[1, 0, rows, :] * hb + src_ref[1, 1, rows, :]
            hbuf_b_ref[rows, :] = hb
        add_out(latent, tf, 0, first_touch)
        add_out(latent, tb, 1, first_touch)
        return hf, hb

    def sweep(latent, length, h_fwd, h_bwd):
        n_chunks = length // RG_TC
        assert n_chunks % 4 == 0
        last = n_chunks - 1

        def both(j, dst_ref):
            coeffs(latent, length, j * RG_TC, 0, dst_ref)
            coeffs(latent, length, (last - j) * RG_TC, 1, dst_ref)

        def body(first_touch, i, carry):
            j = 2 * i
            both(j + 1, coef_b_ref)
            carry = scan(latent, first_touch, j * RG_TC, (last - j) * RG_TC, coef_a_ref, *carry)
            both(jnp.minimum(j + 2, last), coef_a_ref)
            return scan(latent, first_touch, (j + 1) * RG_TC, (last - j - 1) * RG_TC, coef_b_ref, *carry)

        both(0, coef_a_ref)
        half = n_chunks // 4
        carry = lax.fori_loop(0, half, functools.partial(body, True), (h_fwd, h_bwd))
        return lax.fori_loop(half, 2 * half, functools.partial(body, False), carry)

    zero = jnp.zeros((SUBLANES, RG_CT), F32)
    hf, hb = sweep(False, CTX_LEN, zero, zero)
    sweep(True, SEQ, hf, hb)


def _rglru(rx, conv_w, conv_b, a_w, a_b, x_w, x_b, lam):
    assert T_CTX == SEQ, "the context rows must form exactly one latent-sized row block"
    w = jnp.stack([_block_diag_tiles(a_w[0]), _block_diag_tiles(x_w[0]),
                   _block_diag_tiles(a_w[1]), _block_diag_tiles(x_w[1])], axis=1).astype(BF16)
    bias = jnp.stack([a_b[0], x_b[0], a_b[1], x_b[1]], axis=0)
    n_ct = RG_WIDTH // RG_CT
    n_blocks = BATCH + 1
    chan = lambda c: (0, 0, c)
    lane = lambda c: (0, c)
    coef = pltpu.VMEM((2, 2, RG_TC * BATCH, RG_CT), F32)
    slab = pltpu.VMEM(((RG_TC + 2 * RG_HALO) * BATCH, RG_CT), F32)
    chunk = pltpu.VMEM((RG_TC * BATCH, RG_CT), F32)
    out = pl.pallas_call(
        _rg_kernel,
        grid=(n_ct,),
        in_specs=[
            pl.BlockSpec((n_blocks, SEQ, RG_CT), chan),
            pl.BlockSpec((4, RG_CT), lane),
            pl.BlockSpec((1, RG_CT), lane),
            pl.BlockSpec((1, 4, RG_CT, RG_CT), lambda c: (c, 0, 0, 0)),
            pl.BlockSpec((4, RG_CT), lane),
            pl.BlockSpec((2, RG_CT), lane),
        ],
        out_specs=pl.BlockSpec((n_blocks, SEQ, RG_CT), chan),
        out_shape=jax.ShapeDtypeStruct((n_blocks, SEQ, RG_WIDTH), F32),
        scratch_shapes=[
            pltpu.VMEM((BATCH, SEQ + 2 * RG_HALO, RG_CT), F32),
            pltpu.VMEM((BATCH, CTX_LEN + 2 * RG_HALO, RG_CT), F32),
            slab, slab,
            coef, coef,
            chunk, chunk,
        ],
        compiler_params=_params("arbitrary"),
        name="rglru_scan",
    )(rx.reshape(n_blocks, SEQ, RG_WIDTH), conv_w, conv_b.reshape(1, RG_WIDTH), w, bias, lam)
    return out.reshape(T_ALL, RG_WIDTH)


def _dft_tables(length):
    step = 1 << (length.bit_length() // 2)
    s = np.arange(length, dtype=np.int64)

    def thin(kvec):
        ang = ((kvec[:, None] * s[None, :]) % (2 * length)) * (math.pi / length)
        return (jnp.asarray(np.cos(ang)[:, None, :], dtype=F32),
                jnp.asarray(np.sin(ang)[:, None, :], dtype=F32))

    c_hi, s_hi = thin(np.arange(0, length, step, dtype=np.int64))
    c_lo, s_lo = thin(np.arange(step, dtype=np.int64))
    c_lo, s_lo = c_lo.reshape(1, step, length), s_lo.reshape(1, step, length)
    fc = (c_hi * c_lo - s_hi * s_lo).reshape(length, length)
    fs = (s_hi * c_lo + c_hi * s_lo).reshape(length, length)
    return fc.astype(BF16), fs.astype(BF16)


def _filter_features(length):
    f32 = np.float32
    pos = np.arange(length, dtype=f32)
    t = pos / f32(length)
    bands = np.linspace(1e-4, HY_BANDS - 1, HY_BANDS, dtype=f32)
    ang = (f32(2.0 * math.pi) * pos / f32(length))[:, None] * bands[None, :]
    feats = np.concatenate([t[:, None], np.cos(ang), -np.sin(ang)], axis=-1).astype(f32)
    feats = np.pad(feats, ((0, 0), (0, LANES - HY_EMB)))
    deltas = np.abs(np.linspace(HY_MIN_DECAY, HY_MAX_DECAY, HY_WIDTH, dtype=f32))
    window = (np.exp(-t[:, None] * deltas[None, :]) + f32(HY_SHIFT)).astype(f32)
    return jnp.asarray(feats), jnp.asarray(window)


def _filter_hidden(feat_ref, w1_ref, b1_ref, w2_ref, b2_ref, sf_ref, hid_ref):
    h = jnp.sin(sf_ref[0:1, :] * (_dot(feat_ref[...].astype(BF16), w1_ref[...]) + b1_ref[...]))
    h = jnp.sin(sf_ref[1:2, :] * (_dot(h.astype(BF16), w2_ref[...]) + b2_ref[...]))
    hid_ref[...] = h.astype(BF16)


def _filter_spectrum(hid_ref, win_ref, w3f_ref, w3b_ref, fc_ref, fs_ref, kc_ref, ks_ref, kn_ref, length):
    n = 2 * length
    hb16 = hid_ref[...]
    win = win_ref[...]
    row = lax.broadcasted_iota(jnp.int32, (length, HY_CT), 0)
    h_fwd = _dot(hb16, w3f_ref[...]) * win
    h_bwd0 = jnp.where(row == 0, 0.0, _dot(hb16, w3b_ref[...]) * win)
    h_sum = h_fwd + h_bwd0
    h_dif = h_fwd - h_bwd0
    weight = jnp.where(row == 0, 1.0 / n, 2.0 / n)
    kc_ref[...] = _dot(fc_ref[...], h_sum.astype(BF16)) * weight
    ks_ref[...] = _dot(fs_ref[...], h_dif.astype(BF16)) * weight
    sign = jnp.where((row & 1) == 0, 1.0, -1.0)
    kn_ref[...] = jnp.sum(h_sum * sign, axis=0, keepdims=True) * (1.0 / n)


def _filter_params(w1, b1, w2, b2, w3, sin_freq):
    hp = LANES - HY_FILT_HIDDEN
    w3p = jnp.pad(w3, ((0, hp), (0, 0))).astype(BF16)
    return (jnp.pad(w1, ((0, LANES - HY_EMB), (0, hp))).astype(BF16), jnp.pad(b1, (0, hp)).reshape(1, LANES),
            jnp.pad(w2, ((0, hp), (0, hp))).astype(BF16), jnp.pad(b2, (0, hp)).reshape(1, LANES),
            w3p[:, :HY_WIDTH], w3p[:, HY_WIDTH:], jnp.pad(sin_freq, ((0, 0), (0, hp))))


def _hyena_sequence(raw, cw_ref, cb_ref, bias_ref, fc_ref, fs_ref, kc_ref, ks_ref, kn_ref, length):
    row = lax.broadcasted_iota(jnp.int32, (length, HY_CT), 0)

    def short_conv(x, part):
        prev = jnp.where(row == 0, 0.0, pltpu.roll(x, 1, 0))
        nxt = jnp.where(row == length - 1, 0.0, pltpu.roll(x, length - 1, 0))
        return (prev * cw_ref[part, 0:1, :] + x * cw_ref[part, 1:2, :] + nxt * cw_ref[part, 2:3, :]
                + cb_ref[part:part + 1, :])

    x0 = short_conv(raw[0], 0)
    x1 = short_conv(raw[1], 1)
    v = short_conv(raw[2], 2)
    vx = v * x1
    vb = vx.astype(BF16)
    ft = min(length, HY_FREQ_TILE)
    ps, qs = [], []
    for f0 in range(0, length, ft):
        spec_c = _dot(fc_ref[f0:f0 + ft, :], vb)
        spec_s = _dot(fs_ref[f0:f0 + ft, :], vb)
        kc = kc_ref[f0:f0 + ft, :]
        ks = ks_ref[f0:f0 + ft, :]
        ps.append((spec_c * kc - spec_s * ks).astype(BF16))
        qs.append((spec_c * ks + spec_s * kc).astype(BF16))
    y = _dot(fc_ref[...], jnp.concatenate(ps, axis=0)) + _dot(fs_ref[...], jnp.concatenate(qs, axis=0))
    sign = jnp.where((row & 1) == 0, 1.0, -1.0)
    nyquist = jnp.sum(vx * sign, axis=0, keepdims=True) * kn_ref[...]
    return x0 * (y + sign * nyquist + vx * bias_ref[...])


def _hyena_kernel(x0_ref, x1_ref, v_ref, cw_ref, cb_ref, bias_ref,
                  featl_ref, winl_ref, featc_ref, winc_ref,
                  w1_ref, b1_ref, w2_ref, b2_ref, w3f_ref, w3b_ref, sf_ref,
                  fcl_ref, fsl_ref, fcc_ref, fsc_ref, o_ref,
                  hidl_ref, hidc_ref, kcl_ref, ksl_ref, knl_ref, kcc_ref, ksc_ref, knc_ref):
    step = pl.program_id(1)
    common = (cw_ref, cb_ref, bias_ref)

    @pl.when(jnp.logical_and(pl.program_id(0) == 0, step == 0))
    def _():
        sine_layers = (w1_ref, b1_ref, w2_ref, b2_ref, sf_ref)
        _filter_hidden(featl_ref, *sine_layers, hidl_ref)
        _filter_hidden(featc_ref, *sine_layers, hidc_ref)

    @pl.when(step == 0)
    def _():
        _filter_spectrum(hidl_ref, winl_ref, w3f_ref, w3b_ref, fcl_ref, fsl_ref, kcl_ref, ksl_ref, knl_ref, SEQ)
        _filter_spectrum(hidc_ref, winc_ref, w3f_ref, w3b_ref, fcc_ref, fsc_ref, kcc_ref, ksc_ref, knc_ref, CTX_LEN)

    @pl.when(step < BATCH)
    def _():
        raw = (x0_ref[...], x1_ref[...], v_ref[...])
        o_ref[...] = _hyena_sequence(raw, *common, fcl_ref, fsl_ref, kcl_ref, ksl_ref, knl_ref, SEQ)

    @pl.when(step == BATCH)
    def _():
        for r in range(BATCH):
            rows = slice(r * CTX_LEN, (r + 1) * CTX_LEN)
            raw = (x0_ref[rows, :], x1_ref[rows, :], v_ref[rows, :])
            o_ref[rows, :] = _hyena_sequence(raw, *common, fcc_ref, fsc_ref, kcc_ref, ksc_ref, knc_ref, CTX_LEN)


def _hyena(hy, filter_params, conv_w, conv_b, bias):
    assert T_CTX == SEQ, "the context rows must form exactly one latent-sized row block"
    n_ct = HY_WIDTH // HY_CT
    cw = conv_w.reshape(3, 3, HY_WIDTH).transpose(1, 0, 2)
    cb = conv_b.reshape(3, HY_WIDTH)
    feat_l, win_l = _filter_features(SEQ)
    feat_c, win_c = _filter_features(CTX_LEN)
    tab_l = _dft_tables(SEQ)
    tab_c = _dft_tables(CTX_LEN)

    def part_spec(part):
        return pl.BlockSpec((SEQ, HY_CT), lambda c, b: (b, part * n_ct + c))

    chan = lambda c, b: (0, c)
    once_per_tile = lambda rows: pl.BlockSpec((rows, HY_CT), chan, pipeline_mode=pl.Buffered(1))
    spectrum = lambda rows: pltpu.VMEM((rows, HY_CT), F32)
    return pl.pallas_call(
        _hyena_kernel,
        grid=(n_ct, BATCH + 1),
        in_specs=[
            part_spec(0), part_spec(1), part_spec(2),
            pl.BlockSpec((3, 3, HY_CT), lambda c, b: (0, 0, c)),
            pl.BlockSpec((3, HY_CT), chan),
            pl.BlockSpec((1, HY_CT), chan),
            _resident((SEQ, LANES)), once_per_tile(SEQ),
            _resident((CTX_LEN, LANES)), once_per_tile(CTX_LEN),
            _resident((LANES, LANES)), _resident((1, LANES)), _resident((LANES, LANES)), _resident((1, LANES)),
            once_per_tile(LANES), once_per_tile(LANES), _resident((2, LANES)),
            _resident((SEQ, SEQ)), _resident((SEQ, SEQ)),
            _resident((CTX_LEN, CTX_LEN)), _resident((CTX_LEN, CTX_LEN)),
        ],
        out_specs=pl.BlockSpec((SEQ, HY_CT), lambda c, b: (b, c)),
        out_shape=jax.ShapeDtypeStruct((T_ALL, HY_WIDTH), F32),
        scratch_shapes=[pltpu.VMEM((SEQ, LANES), BF16), pltpu.VMEM((CTX_LEN, LANES), BF16),
                        spectrum(SEQ), spectrum(SEQ), spectrum(1), spectrum(CTX_LEN), spectrum(CTX_LEN), spectrum(1)],
        compiler_params=_params("arbitrary", "arbitrary"),
        name="hyena_conv",
    )(hy, hy, hy, cw, cb, bias.reshape(1, HY_WIDTH), feat_l, win_l, feat_c, win_c, *filter_params, *tab_l, *tab_c)


def _rope_partner(n):
    idx = np.arange(n)
    return np.where((idx % (2 * ROPE_AXIS_PAIRS)) < ROPE_AXIS_PAIRS, idx + ROPE_AXIS_PAIRS, idx - ROPE_AXIS_PAIRS)


def _slot_lanes():
    d = np.arange(MLA_ROPE)
    axis, second, pair = d // (2 * ROPE_AXIS_PAIRS), (d // ROPE_AXIS_PAIRS) % 2, d % ROPE_AXIS_PAIRS
    rope_lane = axis * ROPE_AXIS_PAIRS + pair + (HEAD_SLOT // 2) * second
    free = np.setdiff1d(np.arange(HEAD_SLOT), rope_lane)
    return rope_lane, free[:MLA_NOPE]


def _to_slot(nope, rope):
    rope_lane, nope_lane = _slot_lanes()
    source = [(None, 0)] * HEAD_SLOT
    for arr, lanes in ((nope, nope_lane), (rope, rope_lane)):
        if arr is not None:
            for idx, lane in enumerate(lanes):
                source[lane] = (arr, idx)
    like = nope if nope is not None else rope
    pieces, lane = [], 0
    while lane < HEAD_SLOT:
        arr, start = source[lane]
        run = 1
        while (lane + run < HEAD_SLOT and source[lane + run][0] is arr
               and (arr is None or source[lane + run][1] == start + run)):
            run += 1
        pieces.append(jnp.zeros(like.shape[:-1] + (run,), like.dtype) if arr is None else arr[..., start:start + run])
        lane += run
    return jnp.concatenate(pieces, axis=-1)


def _rope_tables():
    f32 = np.float32
    rope_lane, nope_lane = _slot_lanes()
    rows = np.repeat(np.arange(SEQ // GRID_W, dtype=f32), GRID_W)
    cols = np.tile(np.arange(GRID_W, dtype=f32), SEQ // GRID_W)
    inv_freq = (f32(ROPE_BASE) ** (-np.arange(ROPE_AXIS_PAIRS, dtype=f32) / f32(ROPE_AXIS_PAIRS))).astype(f32)
    ang_r = rows[:, None] * inv_freq
    ang_c = cols[:, None] * inv_freq
    cos32 = np.concatenate([np.cos(ang_r)] * 2 + [np.cos(ang_c)] * 2, axis=1)
    sin32 = np.concatenate([-np.sin(ang_r), np.sin(ang_r), -np.sin(ang_c), np.sin(ang_c)], axis=1)

    def table(n_rows, rope_vals, nope_val):
        t = np.zeros((n_rows, HEAD_SLOT), f32)
        t[:, rope_lane] = rope_vals
        t[:, nope_lane] = nope_val
        return t

    q_cos = table(SEQ, cos32, 1.0) * f32(Q_SCALE)
    q_sin = table(SEQ, sin32, 0.0) * f32(Q_SCALE)
    k_cos = table(SEQ, cos32, 0.0)
    k_sin = table(SEQ, sin32, 0.0)
    ident = table(TM_PROJ, 1.0, 0.0)
    zero = np.zeros((TM_PROJ, HEAD_SLOT), f32)
    tables = ((q_cos, zero), (q_sin, zero), (k_cos, ident), (k_sin, zero))
    return tuple(jnp.asarray(np.concatenate(t, 0).astype(f32)) for t in tables)


def _value_ones():
    assert HEAD_SLOT == 2 * MLA_V
    lane = np.arange(MLA_HEADS * HEAD_SLOT)
    head_is_even = (lane // HEAD_SLOT) % 2 == 0
    upper_half = lane % HEAD_SLOT >= MLA_V
    return jnp.asarray((head_is_even == upper_half).astype(np.float32).reshape(1, -1))


def _mla_weights(w_in, w_q_up, w_kv_up):
    w_q = w_in[:, :MLA_Q_LORA]
    w_kv = w_in[:, MLA_Q_LORA:MLA_Q_LORA + MLA_KV_LORA]
    w_kr = w_in[:, MLA_Q_LORA + MLA_KV_LORA:]
    w_kr2 = jnp.concatenate([_to_slot(None, w_kr), _to_slot(None, w_kr[:, _rope_partner(MLA_ROPE)])], axis=1)

    qh = w_q_up.reshape(MLA_Q_LORA, MLA_HEADS, MLA_NOPE + MLA_ROPE)
    wq_slot = _to_slot(qh[..., :MLA_NOPE], qh[..., MLA_NOPE:]).reshape(MLA_Q_LORA, MLA_HEADS * HEAD_SLOT)

    kvh = w_kv_up.reshape(MLA_KV_LORA, MLA_HEADS, MLA_NOPE + MLA_V)
    wk_slot = _to_slot(kvh[..., :MLA_NOPE], None).reshape(MLA_KV_LORA, MLA_HEADS * HEAD_SLOT)
    even = (jnp.arange(MLA_HEADS) % 2 == 0).astype(F32)[None, :, None]
    wv = kvh[..., MLA_NOPE:]
    wv_slot = jnp.concatenate([wv * even, wv * (1.0 - even)], axis=-1).reshape(MLA_KV_LORA, MLA_HEADS * HEAD_SLOT)
    return [w.astype(BF16) for w in (w_q, w_kv, w_kr2, wq_slot, wk_slot, wv_slot)]


def _mlaproj_kernel(x_ref, mod_ref, wq_ref, wkv_ref, wkr_ref, wqs_ref, wks_ref, wv_ref,
                    vone_ref, qg_ref, kvg_ref, qcos_ref, qsin_ref, kcos_ref, ksin_ref,
                    q_ref, k_ref, v_ref):
    shift, scale = mod_ref[0, 0], mod_ref[0, 1]
    h = (x_ref[...] * (1.0 + scale) + shift).astype(BF16)

    def rmsnorm(y, g):
        return (y * lax.rsqrt(jnp.mean(y * y, axis=-1, keepdims=True) + RMS_EPS) * g).astype(BF16)

    qn = rmsnorm(_dot(h, wq_ref[...]), qg_ref[...])
    kvn = rmsnorm(_dot(h, wkv_ref[...]), kvg_ref[...])
    kr2 = _dot(h, wkr_ref[...])
    k_rope = kr2[:, :HEAD_SLOT] * kcos_ref[...] + kr2[:, HEAD_SLOT:] * ksin_ref[...]
    k_nope = _dot(kvn, wks_ref[...])
    v_ref[...] = (_dot(kvn, wv_ref[...]) + vone_ref[...]).astype(BF16)
    q_all = _dot(qn, wqs_ref[...])
    q_cos = qcos_ref[...]
    q_sin = qsin_ref[...]
    for hd in range(MLA_HEADS):
        sl = slice(hd * HEAD_SLOT, (hd + 1) * HEAD_SLOT)
        qh = q_all[:, sl]
        rotated = qh * q_cos + pltpu.roll(qh, HEAD_SLOT // 2, 1) * q_sin
        q_ref[:, sl] = rotated.astype(BF16)
        k_ref[:, sl] = (k_nope[:, sl] + k_rope).astype(BF16)


def _mlaproj(x, mod, w_in, q_norm, kv_norm, w_q_up, w_kv_up):
    tm = TM_PROJ
    weights = _mla_weights(w_in, w_q_up, w_kv_up)
    tables = _rope_tables()
    row = lambda i: (i, 0)
    tab = lambda i: (jnp.where(i < T_LAT // tm, i % (SEQ // tm), SEQ // tm), 0)
    wide = MLA_HEADS * HEAD_SLOT
    v_one = _value_ones()
    return pl.pallas_call(
        _mlaproj_kernel,
        grid=(T_ALL // tm,),
        in_specs=[
            pl.BlockSpec((tm, D_MODEL), row),
            pl.BlockSpec((1, 2, 1, D_MODEL), lambda i: (_group_of_tile(i, tm), 0, 0, 0)),
            *[_resident(w.shape) for w in weights],
            _resident((1, wide)),
            _resident((1, MLA_Q_LORA)),
            _resident((1, MLA_KV_LORA)),
            *[pl.BlockSpec((tm, HEAD_SLOT), tab) for _ in tables],
        ],
        out_specs=[pl.BlockSpec((tm, wide), row)] * 3,
        out_shape=[jax.ShapeDtypeStruct((T_ALL, wide), BF16)] * 3,
        compiler_params=_params("arbitrary"),
        name="mla_proj",
    )(x, mod, *weights, v_one, q_norm.reshape(1, MLA_Q_LORA), kv_norm.reshape(1, MLA_KV_LORA), *tables)


def _attn_kernel(q_ref, kl_ref, kc_ref, vl_ref, vc_ref, o_ref):
    low = lax.broadcasted_iota(jnp.int32, (TQ, HEAD_SLOT), 1) < MLA_V
    pair_out = []
    for hd in range(ATTN_HEADS):
        sl = slice(hd * HEAD_SLOT, (hd + 1) * HEAD_SLOT)
        q = q_ref[:, sl]
        s_lat = _dot_nt(q, kl_ref[:, sl])
        s_ctx = _dot_nt(q, kc_ref[:, sl])
        m = jnp.maximum(jnp.max(s_lat, axis=-1, keepdims=True), jnp.max(s_ctx, axis=-1, keepdims=True))
        p_lat = jnp.exp2(s_lat - m).astype(BF16)
        p_ctx = jnp.exp2(s_ctx - m).astype(BF16)
        pair_out.append(_dot(p_lat, vl_ref[:, sl]) + _dot(p_ctx, vc_ref[:, sl]))
        if hd % 2 == 1:
            even, odd = pair_out
            pair_out = []
            numer = jnp.where(low, even, odd)
            denom = pltpu.roll(jnp.where(low, odd, even), MLA_V, 1)
            pair = hd // 2
            o_ref[:, pair * HEAD_SLOT:(pair + 1) * HEAD_SLOT] = (numer / denom).astype(BF16)


def _attention(q, k, v):
    n_q = SEQ // TQ
    n_groups = MLA_HEADS // ATTN_HEADS
    ctx_block0 = T_LAT // CTX_LEN
    wide = ATTN_HEADS * HEAD_SLOT
    lat = lambda b, h, i: (b, h)
    ctx = lambda b, h, i: (ctx_block0 + b, h)
    return pl.pallas_call(
        _attn_kernel,
        grid=(BATCH, n_groups, n_q),
        in_specs=[
            pl.BlockSpec((TQ, wide), lambda b, h, i: (b * n_q + i, h)),
            pl.BlockSpec((SEQ, wide), lat),
            pl.BlockSpec((CTX_LEN, wide), ctx),
            pl.BlockSpec((SEQ, wide), lat),
            pl.BlockSpec((CTX_LEN, wide), ctx),
        ],
        out_specs=pl.BlockSpec((TQ, ATTN_HEADS * MLA_V), lambda b, h, i: (b * n_q + i, h)),
        out_shape=jax.ShapeDtypeStruct((T_LAT, MLA_HEADS * MLA_V), BF16),
        compiler_params=_params("arbitrary", "arbitrary", "arbitrary"),
        name="mla_attention",
    )(q, k, k, v, v)


def kernel(x, c, ctx, c_ctx, ada_w, ada_b, ln_g, ln_b, ffn_w_in, ffn_w_out, ev_w_in, ev_w_out, hy_conv_w, hy_conv_b, hy_filt_w1, hy_filt_b1, hy_filt_w2, hy_filt_b2, hy_filt_w3, hy_sin_freq, hy_bias, rg_conv_w, rg_conv_b, rg_a_w, rg_a_b, rg_x_w, rg_x_b, rg_lambda, mla_w_in, mla_q_norm, mla_kv_norm, mla_w_q_up, mla_w_kv_up, mla_w_out):
    assert x.shape == (BATCH, SEQ, D_MODEL) and ctx.shape == (BATCH, CTX_LEN, D_MODEL)
    cond = jnp.concatenate([c, c_ctx[None], jnp.zeros((2 * SUBLANES - N_GROUPS, D_MODEL), F32)], axis=0)
    mods = _adaln(cond, ada_w, ada_b)[:, :N_GROUPS].reshape(DEPTH, N_GROUPS, N_ADA, 1, D_MODEL)
    ffn_w = (ffn_w_in, ffn_w_out)

    m = mods[0]
    xs, hy, rx, gate_in = _ffn_split_evproj(x.reshape(T_LAT, D_MODEL), ctx.reshape(T_CTX, D_MODEL),
                                            m[:, 0:3], m[:, 3:5], ffn_w, (0, 0), ln_g[0, 0], ln_b[0, 0], ev_w_in[0])
    h_sum = _rglru(rx, rg_conv_w[0], rg_conv_b[0], rg_a_w[0], rg_a_b[0], rg_x_w[0], rg_x_b[0], rg_lambda[0])
    filt = _filter_params(hy_filt_w1[0], hy_filt_b1[0], hy_filt_w2[0], hy_filt_b2[0], hy_filt_w3[0], hy_sin_freq[0])
    y_hy = _hyena(hy, filt, hy_conv_w[0], hy_conv_b[0], hy_bias[0])
    xs = _ffn_after_even(xs, y_hy, h_sum, gate_in, ev_w_out[0], ln_g[0, 1], ln_b[0, 1],
                         m[:, 5:9], ffn_w, (0, 1), ln_g[0, 2], ln_b[0, 2], T_ALL)

    m = mods[1]
    xs = _ffn_plain(xs, m[:, 0:3], ffn_w, (1, 0), ln_g[1, 0], ln_b[1, 0], T_ALL)
    q, k, v = _mlaproj(xs, m[:, 3:5], mla_w_in[0], mla_q_norm[0], mla_kv_norm[0], mla_w_q_up[0], mla_w_kv_up[0])
    att = _attention(q, k, v)
    xl = _ffn_after_mla(xs, att, mla_w_out[0], ln_g[1, 1], ln_b[1, 1],
                        m[:, 5:9], ffn_w, (1, 1), ln_g[1, 2], ln_b[1, 2], T_LAT)
    return xl.reshape(BATCH, SEQ, D_MODEL)
```
